```python
import math
import jax
import jax.numpy as jnp
from jax import lax
import numpy as np

D_MODEL = 4096
BATCH = 2
SEQ = 4096
DEPTH = 1
DEC_BATCH = 8
DEC_SEQ = 64
PAST_LEN = 1024

CHUNK = 64
WINDOW = 128
WINDOW_CHUNKS = WINDOW // CHUNK
HEAD_DIM = 64
ATTN_WIDTH = D_MODEL // 2
N_Q_HEADS = ATTN_WIDTH // HEAD_DIM
N_KV_HEADS = max(1, N_Q_HEADS // 8)
GQA_REP = N_Q_HEADS // N_KV_HEADS
KV_WIDTH = N_KV_HEADS * HEAD_DIM
SSM_WIDTH = D_MODEL // 2
SSM_GROUP = 16
N_GROUPS = SSM_WIDTH // SSM_GROUP
STATE_DIM = 64
ROPE_THETA = 10000.0
NORM_EPS = 1e-5
DT_MIN = 1e-3
DT_MAX = 1e-1
IN_WIDTH = 2 * ATTN_WIDTH + 2 * KV_WIDTH + 2 * SSM_WIDTH + 2 * D_MODEL

kernel_name = 'chunk_streaming_swa_sink_s5_hybrid_step'


def _rmsnorm(x, g):
    xf = x.astype(jnp.float32)
    y = xf * lax.rsqrt(jnp.mean(xf * xf, axis=-1, keepdims=True) + NORM_EPS)
    return (y * g.astype(jnp.float32)).astype(x.dtype)


def _rope(x, pos):
    half = HEAD_DIM // 2
    inv_freq = ROPE_THETA ** (-jnp.arange(half, dtype=jnp.float32) / half)
    ang = pos.astype(jnp.float32)[:, None] * inv_freq[None, :]
    cos = jnp.cos(ang)[None, :, None, :]
    sin = jnp.sin(ang)[None, :, None, :]
    xf = x.astype(jnp.float32)
    x1, x2 = xf[..., :half], xf[..., half:]
    return jnp.concatenate([x1 * cos - x2 * sin, x2 * cos + x1 * sin], axis=-1).astype(x.dtype)


def _attend(q, k, v, sink, mask):
    s = jnp.einsum('...qgrd,...kgd->...grqk', q, k, preferred_element_type=jnp.float32)
    s = s * (HEAD_DIM ** -0.5)
    if mask is not None:
        s = jnp.where(mask, s, -jnp.inf)
    sk = jnp.broadcast_to(sink.astype(jnp.float32)[:, :, None, None], s.shape[:-1] + (1,))
    p = jax.nn.softmax(jnp.concatenate([s, sk], axis=-1), axis=-1)[..., :-1]
    return jnp.einsum('...grqk,...kgd->...qgrd', p.astype(v.dtype), v)


def _swa_prompt(q, k, v, sink):
    b, t = q.shape[0], q.shape[1]
    nc = t // CHUNK
    qc = q.reshape(b, nc, CHUNK, N_KV_HEADS, GQA_REP, HEAD_DIM)
    kc = k.reshape(b, nc, CHUNK, N_KV_HEADS, HEAD_DIM)
    vc = v.reshape(b, nc, CHUNK, N_KV_HEADS, HEAD_DIM)
    pad = ((0, 0), (WINDOW_CHUNKS, 0), (0, 0), (0, 0), (0, 0))
    kp = jnp.pad(kc, pad)
    vp = jnp.pad(vc, pad)
    kb = jnp.concatenate([kp[:, j:j + nc] for j in range(WINDOW_CHUNKS + 1)], axis=2)
    vb = jnp.concatenate([vp[:, j:j + nc] for j in range(WINDOW_CHUNKS + 1)], axis=2)
    key_chunk = (jnp.arange(nc)[:, None] - WINDOW_CHUNKS
                 + (jnp.arange((WINDOW_CHUNKS + 1) * CHUNK) // CHUNK)[None, :])
    mask = (key_chunk >= 0)[:, None, None, None, :]
    o = _attend(qc, kb, vb, sink, mask)
    return o.reshape(b, t, ATTN_WIDTH)


def _swa_cached(q, k, v, cache_k, cache_v, sink):
    b, t = q.shape[0], q.shape[1]
    kb = jnp.concatenate([cache_k.astype(k.dtype), k], axis=1)
    vb = jnp.concatenate([cache_v.astype(v.dtype), v], axis=1)
    qg = q.reshape(b, t, N_KV_HEADS, GQA_REP, HEAD_DIM)
    o = _attend(qg, kb, vb, sink, None)
    return o.reshape(b, t, ATTN_WIDTH)


def _s5_discretise(lambda_re, lambda_im, log_dt, b_re, b_im):
    lr = jnp.minimum(lambda_re.astype(jnp.float32), -1e-4)
    li = lambda_im.astype(jnp.float32)
    dt = jnp.exp(log_dt.astype(jnp.float32))[:, None]
    mag = jnp.exp(lr * dt)
    a_re = mag * jnp.cos(li * dt)
    a_im = mag * jnp.sin(li * dt)
    den = lr * lr + li * li
    nr = a_re - 1.0
    f_re = (nr * lr + a_im * li) / den
    f_im = (a_im * lr - nr * li) / den
    br = b_re.astype(jnp.float32)
    bi = b_im.astype(jnp.float32)
    bb_re = f_re[..., None] * br - f_im[..., None] * bi
    bb_im = f_re[..., None] * bi + f_im[..., None] * br
    return a_re, a_im, bb_re, bb_im


def _cmul_combine(left, right):
    al_re, al_im, bl_re, bl_im = left
    ar_re, ar_im, br_re, br_im = right
    return (al_re * ar_re - al_im * ar_im,
            al_re * ar_im + al_im * ar_re,
            ar_re * bl_re - ar_im * bl_im + br_re,
            ar_re * bl_im + ar_im * bl_re + br_im)


def _s5_scan(u, h0_re, h0_im, a_re, a_im, bb_re, bb_im, c_re, c_im, d_skip):
    b, t = u.shape[0], u.shape[1]
    uf = u.astype(jnp.float32).reshape(b, t, N_GROUPS, SSM_GROUP)
    x_re = jnp.einsum('btgi,gpi->btgp', uf, bb_re)
    x_im = jnp.einsum('btgi,gpi->btgp', uf, bb_im)
    h0r = h0_re.astype(jnp.float32)
    h0i = h0_im.astype(jnp.float32)
    x_re = x_re.at[:, 0].add(a_re * h0r - a_im * h0i)
    x_im = x_im.at[:, 0].add(a_re * h0i + a_im * h0r)
    a_re_t = jnp.broadcast_to(a_re, (1, t) + a_re.shape)
    a_im_t = jnp.broadcast_to(a_im, (1, t) + a_im.shape)
    _, _, h_re, h_im = lax.associative_scan(_cmul_combine, (a_re_t, a_im_t, x_re, x_im), axis=1)
    y = (jnp.einsum('btgp,gip->btgi', h_re, c_re.astype(jnp.float32))
         - jnp.einsum('btgp,gip->btgi', h_im, c_im.astype(jnp.float32))
         + d_skip.astype(jnp.float32) * uf)
    return y.reshape(b, t, SSM_WIDTH).astype(u.dtype), h_re[:, -1], h_im[:, -1]


def _layer(x, pos, kv_cache_k, kv_cache_v, h0_re, h0_im, disc, norm_g, w_in, sink,
           c_re, c_im, d_skip, w_glu, b_glu, w_pa, w_ps, w_out):
    b, t, _ = x.shape
    h = _rmsnorm(x, norm_g)
    proj = jnp.einsum('btd,de->bte', h, w_in)
    sizes = (ATTN_WIDTH, KV_WIDTH, KV_WIDTH, ATTN_WIDTH, SSM_WIDTH, SSM_WIDTH, D_MODEL, D_MODEL)
    points = [int(p) for p in np.cumsum(sizes)[:-1]]
    q, k, v, z_a, u, z_s, g_a, g_s = jnp.split(proj, points, axis=-1)
    q = _rope(q.reshape(b, t, N_Q_HEADS, HEAD_DIM), pos)
    k = _rope(k.reshape(b, t, N_KV_HEADS, HEAD_DIM), pos)
    v = v.reshape(b, t, N_KV_HEADS, HEAD_DIM)
    sink_g = sink.reshape(N_KV_HEADS, GQA_REP)
    if kv_cache_k is None:
        attn = _swa_prompt(q, k, v, sink_g)
    else:
        attn = _swa_cached(q, k, v, kv_cache_k, kv_cache_v, sink_g)
    a_re, a_im, bb_re, bb_im = disc
    y_ssm, hl_re, hl_im = _s5_scan(u, h0_re, h0_im, a_re, a_im, bb_re, bb_im, c_re, c_im, d_skip)
    glu = jnp.einsum('btc,ce->bte', y_ssm, w_glu) + b_glu
    glu_a, glu_g = jnp.split(glu, 2, axis=-1)
    ssm_out = glu_a * jax.nn.sigmoid(glu_g)
    br_a = jnp.einsum('btc,cd->btd', attn * jax.nn.silu(z_a), w_pa)
    br_s = jnp.einsum('btc,cd->btd', ssm_out * jax.nn.silu(z_s), w_ps)
    merged = jax.nn.sigmoid(g_a) * br_a + jax.nn.sigmoid(g_s) * br_s
    out = x + jnp.einsum('btd,de->bte', merged, w_out)
    return out, k, v, hl_re, hl_im


def setup_inputs(seed: int = 0) -> dict:
    key = jax.random.key(seed)
    ks = jax.random.split(key, 23)
    f32 = jnp.float32
    rows = min(WINDOW, PAST_LEN)

    def nrm(k, shape, scale):
        return scale * jax.random.normal(k, shape, f32)

    return {
        'x_prompt': nrm(ks[0], (BATCH, SEQ, D_MODEL), 1.0),
        'x_sample': nrm(ks[1], (DEC_BATCH, DEC_SEQ, D_MODEL), 1.0),
        'cache_k': nrm(ks[2], (DEPTH, DEC_BATCH, rows, N_KV_HEADS, HEAD_DIM), 1.0),
        'cache_v': nrm(ks[3], (DEPTH, DEC_BATCH, rows, N_KV_HEADS, HEAD_DIM), 1.0),
        'state_ssm_re': nrm(ks[4], (DEPTH, DEC_BATCH, N_GROUPS, STATE_DIM), 0.3),
        'state_ssm_im': nrm(ks[5], (DEPTH, DEC_BATCH, N_GROUPS, STATE_DIM), 0.3),
        'norm_g': 1.0 + nrm(ks[6], (DEPTH, D_MODEL), 0.02),
        'w_in': nrm(ks[7], (DEPTH, D_MODEL, IN_WIDTH), D_MODEL ** -0.5),
        'sink': nrm(ks[8], (DEPTH, N_Q_HEADS), 1.0),
        'lambda_re': -0.5 + nrm(ks[9], (DEPTH, N_GROUPS, STATE_DIM), 0.01),
        'lambda_im': math.pi * jnp.arange(STATE_DIM, dtype=f32) + nrm(ks[10], (DEPTH, N_GROUPS, STATE_DIM), 0.01),
        'log_dt': jax.random.uniform(ks[11], (DEPTH, N_GROUPS), f32, math.log(DT_MIN), math.log(DT_MAX)),
        'b_re': nrm(ks[12], (DEPTH, N_GROUPS, STATE_DIM, SSM_GROUP), (2 * SSM_GROUP) ** -0.5),
        'b_im': nrm(ks[13], (DEPTH, N_GROUPS, STATE_DIM, SSM_GROUP), (2 * SSM_GROUP) ** -0.5),
        'c_re': nrm(ks[14], (DEPTH, N_GROUPS, SSM_GROUP, STATE_DIM), STATE_DIM ** -0.5),
        'c_im': nrm(ks[15], (DEPTH, N_GROUPS, SSM_GROUP, STATE_DIM), STATE_DIM ** -0.5),
        'd_skip': nrm(ks[16], (DEPTH, N_GROUPS, SSM_GROUP), 1.0),
        'w_glu': nrm(ks[17], (DEPTH, SSM_WIDTH, 2 * SSM_WIDTH), SSM_WIDTH ** -0.5),
        'b_glu': nrm(ks[18], (DEPTH, 2 * SSM_WIDTH), 0.01),
        'w_pa': nrm(ks[19], (DEPTH, ATTN_WIDTH, D_MODEL), ATTN_WIDTH ** -0.5),
        'w_ps': nrm(ks[20], (DEPTH, SSM_WIDTH, D_MODEL), SSM_WIDTH ** -0.5),
        'w_out': nrm(ks[21], (DEPTH, D_MODEL, D_MODEL), D_MODEL ** -0.5),
        'final_g': 1.0 + nrm(ks[22], (D_MODEL,), 0.02),
    }


def reference(x_prompt, x_sample, cache_k, cache_v, state_ssm_re, state_ssm_im,
              norm_g, w_in, sink, lambda_re, lambda_im, log_dt, b_re, b_im, c_re, c_im,
              d_skip, w_glu, b_glu, w_pa, w_ps, w_out, final_g):
    bp, sp = x_prompt.shape[0], x_prompt.shape[1]
    ts = x_sample.shape[1]
    pos_p = jnp.arange(sp, dtype=jnp.int32)
    pos_s = PAST_LEN + jnp.arange(ts, dtype=jnp.int32)
    zeros = jnp.zeros((bp, N_GROUPS, STATE_DIM), jnp.float32)
    keep = min(WINDOW, sp)
    xp, xs = x_prompt, x_sample
    kp_l, vp_l, hpr_l, hpi_l = [], [], [], []
    ks_l, vs_l, hsr_l, hsi_l = [], [], [], []
    for l in range(DEPTH):
        disc = _s5_discretise(lambda_re[l], lambda_im[l], log_dt[l], b_re[l], b_im[l])
        shared = (norm_g[l], w_in[l], sink[l], c_re[l], c_im[l], d_skip[l],
                  w_glu[l], b_glu[l], w_pa[l], w_ps[l], w_out[l])
        xp, kp, vp, hpr, hpi = _layer(xp, pos_p, None, None, zeros, zeros, disc, *shared)
        xs, kn, vn, hsr, hsi = _layer(xs, pos_s, cache_k[l], cache_v[l],
                                      state_ssm_re[l], state_ssm_im[l], disc, *shared)
        kp_l.append(kp[:, sp - keep:])
        vp_l.append(vp[:, sp - keep:])
        hpr_l.append(hpr)
        hpi_l.append(hpi)
        ks_l.append(kn)
        vs_l.append(vn)
        hsr_l.append(hsr)
        hsi_l.append(hsi)
    y_prompt = _rmsnorm(xp, final_g)
    y_sample = _rmsnorm(xs, final_g)
    return (y_prompt, y_sample,
            jnp.stack(kp_l), jnp.stack(vp_l), jnp.stack(hpr_l), jnp.stack(hpi_l),
            jnp.stack(ks_l), jnp.stack(vs_l), jnp.stack(hsr_l), jnp.stack(hsi_l))
```

```python
import functools
import math

import jax
import jax.numpy as jnp
from jax import lax
from jax.experimental import pallas as pl
from jax.experimental.pallas import tpu as pltpu

CHUNK = 64
WINDOW = 128
HEAD_DIM = 64
GQA_GROUPING = 8
SSM_GROUP = 16
PAST_LEN = 1024
ROPE_THETA = 10000.0
NORM_EPS = 1e-5
LAMBDA_RE_MAX = -1e-4

SSM_CHUNK = 16
SSM_POW_BITS = 4
LANES = 128
V7X_VMEM_BYTES = 64 * 1024 * 1024
BF16 = jnp.bfloat16
F32 = jnp.float32


def _pick(n, prefs):
    for p in prefs:
        if n % p == 0:
            return p
    raise ValueError(f"no tile in {prefs} divides {n}")


def _params(sem, vmem_bytes):
    limit = min(int(vmem_bytes), V7X_VMEM_BYTES - 4 * 1024 * 1024)
    return pltpu.CompilerParams(dimension_semantics=sem, vmem_limit_bytes=limit)


def _nbytes(shape, dtype):
    return math.prod(shape) * jnp.dtype(dtype).itemsize


def _vmem_estimate(blocks, temps=()):
    return 2 * (2 * sum(_nbytes(s, d) for s, d in blocks) + sum(_nbytes(s, d) for s, d in temps))


def _rmsnorm_kernel(xp_ref, xs_ref, g_ref, o_ref, *, n_prompt_blocks):
    i = pl.program_id(0)

    def norm(x_ref):
        x = x_ref[...]
        y = x * lax.rsqrt(jnp.mean(x * x, axis=-1, keepdims=True) + NORM_EPS)
        o_ref[...] = (y * g_ref[...]).astype(o_ref.dtype)

    @pl.when(i < n_prompt_blocks)
    def _():
        norm(xp_ref)

    @pl.when(i >= n_prompt_blocks)
    def _():
        norm(xs_ref)


def _rmsnorm(xp, xs, g, tm):
    mp, d = xp.shape
    ms = xs.shape[0]
    npb, nsb = mp // tm, ms // tm
    return pl.pallas_call(
        functools.partial(_rmsnorm_kernel, n_prompt_blocks=npb),
        grid=(npb + nsb,),
        in_specs=[
            pl.BlockSpec((tm, d), lambda i: (jnp.minimum(i, npb - 1), 0)),
            pl.BlockSpec((tm, d), lambda i: (jnp.maximum(i - npb, 0), 0)),
            pl.BlockSpec((1, d), lambda i: (0, 0)),
        ],
        out_specs=pl.BlockSpec((tm, d), lambda i: (i, 0)),
        out_shape=jax.ShapeDtypeStruct((mp + ms, d), BF16),
        compiler_params=_params(("arbitrary",), _vmem_estimate([((tm, d), F32)] * 2 + [((tm, d), BF16)], [((tm, d), F32)])),
        name="rmsnorm_in",
    )(xp, xs, g.reshape(1, d))


def _rope(acc, cos_ref, sin_ref):
    tm, tn = acc.shape
    reps = tn // LANES
    cos = jnp.tile(cos_ref[...], (1, reps))
    sin = jnp.tile(sin_ref[...], (1, reps))
    lane = lax.broadcasted_iota(jnp.int32, (tm, tn), 1)
    low = (lane % HEAD_DIM) < (HEAD_DIM // 2)
    partner = jnp.where(low, pltpu.roll(acc, tn - HEAD_DIM // 2, 1), pltpu.roll(acc, HEAD_DIM // 2, 1))
    return acc * cos + partner * sin


def _store_heads(o_ref, val):
    for h in range(val.shape[1] // HEAD_DIM):
        o_ref[h] = val[:, h * HEAD_DIM:(h + 1) * HEAD_DIM].astype(o_ref.dtype)


def _proj_q_kernel(h_ref, w_ref, cos_ref, sin_ref, q_ref):
    acc = jnp.dot(h_ref[...], w_ref[...], preferred_element_type=F32)
    _store_heads(q_ref, _rope(acc, cos_ref, sin_ref) * (HEAD_DIM ** -0.5))


def _proj_kv_kernel(h_ref, w_ref, cos_ref, sin_ref, kf_ref, vf_ref, kh_ref, vh_ref, *, kvw):
    acc = jnp.dot(h_ref[...], w_ref[...], preferred_element_type=F32)
    k = _rope(acc[:, :kvw], cos_ref, sin_ref)
    v = acc[:, kvw:]
    kf_ref[...] = k
    vf_ref[...] = v
    _store_heads(kh_ref, k)
    _store_heads(vh_ref, v)


def _proj_act_kernel(h_ref, w_ref, o_ref, *, act):
    acc = jnp.dot(h_ref[...], w_ref[...], preferred_element_type=F32)
    if act == "silu":
        acc = acc * jax.nn.sigmoid(acc)
    elif act == "sigmoid":
        acc = jax.nn.sigmoid(acc)
    o_ref[...] = acc.astype(o_ref.dtype)


def _proj_specs(d, tm, tn):
    return [pl.BlockSpec((tm, d), lambda j, i: (i, 0)), pl.BlockSpec((d, tn), lambda j, i: (0, j))]


def _proj_vmem(d, tm, tn, outs):
    return _vmem_estimate([((tm, d), BF16), ((d, tn), BF16)] + outs, [((tm, tn), F32)] * 3)


def _proj_q(h, w, cos, sin, aw, tm, tn):
    m, d = h.shape
    nh = aw // HEAD_DIM
    tab = pl.BlockSpec((tm, LANES), lambda j, i: (i, 0))
    return pl.pallas_call(
        _proj_q_kernel,
        grid=(aw // tn, m // tm),
        in_specs=_proj_specs(d, tm, tn) + [tab, tab],
        out_specs=pl.BlockSpec((tn // HEAD_DIM, tm, HEAD_DIM), lambda j, i: (j, i, 0)),
        out_shape=jax.ShapeDtypeStruct((nh, m, HEAD_DIM), BF16),
        compiler_params=_params(("arbitrary", "arbitrary"), _proj_vmem(d, tm, tn, [((tm, 2 * tn), BF16), ((tm, 2 * LANES), F32)])),
        name="proj_q",
    )(h, w, cos, sin)


def _proj_kv(h, w, cos, sin, kvw, tm):
    m, d = h.shape
    g = kvw // HEAD_DIM
    tn = 2 * kvw
    tab = pl.BlockSpec((tm, LANES), lambda j, i: (i, 0))
    flat = pl.BlockSpec((tm, kvw), lambda j, i: (i, 0))
    heads = pl.BlockSpec((g, tm, HEAD_DIM), lambda j, i: (0, i, 0))
    return pl.pallas_call(
        functools.partial(_proj_kv_kernel, kvw=kvw),
        grid=(1, m // tm),
        in_specs=_proj_specs(d, tm, tn) + [tab, tab],
        out_specs=[flat, flat, heads, heads],
        out_shape=[jax.ShapeDtypeStruct((m, kvw), F32)] * 2 + [jax.ShapeDtypeStruct((g, m, HEAD_DIM), BF16)] * 2,
        compiler_params=_params(("arbitrary", "arbitrary"), _proj_vmem(d, tm, tn, [((tm, 2 * tn), F32), ((tm, 2 * tn), BF16), ((tm, 2 * LANES), F32)])),
        name="proj_kv",
    )(h, w, cos, sin)


def _proj_act(h, w, act, tm, tn, name):
    m, d = h.shape
    ncols = w.shape[1]
    return pl.pallas_call(
        functools.partial(_proj_act_kernel, act=act),
        grid=(ncols // tn, m // tm),
        in_specs=_proj_specs(d, tm, tn),
        out_specs=pl.BlockSpec((tm, tn), lambda j, i: (i, j)),
        out_shape=jax.ShapeDtypeStruct((m, ncols), BF16),
        compiler_params=_params(("arbitrary", "arbitrary"), _proj_vmem(d, tm, tn, [((tm, tn), BF16)])),
        name=name,
    )(h, w)


def _attend_unit(q, kw, vw, sink, n_valid):
    s = lax.dot_general(q, kw, (((1,), (1,)), ((), ())), preferred_element_type=F32)
    if n_valid is not None:
        col = lax.broadcasted_iota(jnp.int32, s.shape, 1)
        s = jnp.where(col < n_valid, s, -jnp.inf)
    m = jnp.maximum(jnp.max(s, axis=1, keepdims=True), sink)
    e = jnp.exp(s - m)
    denom = jnp.sum(e, axis=1, keepdims=True) + jnp.exp(sink - m)
    return jnp.dot(e.astype(BF16), vw, preferred_element_type=F32) / denom


def _store_unit(o_ref, row0, o, rep):
    for r in range(0, rep, 2):
        pair = jnp.concatenate([o[r * CHUNK:(r + 1) * CHUNK], o[(r + 1) * CHUNK:(r + 2) * CHUNK]], axis=1)
        o_ref[row0:row0 + CHUNK, r * HEAD_DIM:(r + 2) * HEAD_DIM] = pair.astype(o_ref.dtype)


def _attn_prompt_kernel(q_ref, k_ref, v_ref, sink_ref, o_ref, *, n_chunks, rep):
    cb = pl.program_id(2)
    win = WINDOW + CHUNK
    sink = sink_ref[0]
    for c in range(n_chunks):
        cg = cb * n_chunks + c
        start = pl.multiple_of(jnp.maximum(cg - WINDOW // CHUNK, 0) * CHUNK, CHUNK)
        q = q_ref[:, c * CHUNK:(c + 1) * CHUNK, :].reshape(rep * CHUNK, HEAD_DIM)
        kw = k_ref[0, pl.ds(start, win), :]
        vw = v_ref[0, pl.ds(start, win), :]
        n_valid = jnp.where(cb == 0, (c + 1) * CHUNK, win) if c < WINDOW // CHUNK else None
        _store_unit(o_ref, c * CHUNK, _attend_unit(q, kw, vw, sink, n_valid), rep)


def _attn_sample_kernel(q_ref, k_ref, v_ref, sink_ref, prev_ref, o_ref, *, rep):
    del prev_ref
    q = q_ref[...].reshape(rep * CHUNK, HEAD_DIM)
    _store_unit(o_ref, 0, _attend_unit(q, k_ref[0], v_ref[0], sink_ref[0], None), rep)


def _attention(q_hm, k_hm, v_hm, ks, vs, sink_rows, batch, seq, dec_batch, aw):
    nh, m, _ = q_hm.shape
    g = k_hm.shape[0]
    rep = nh // g
    win = WINDOW + CHUNK
    n_chunks = _pick(seq // CHUNK, (8, 4, 2))
    ncb = seq // (CHUNK * n_chunks)
    blocks = [((rep, CHUNK * n_chunks, HEAD_DIM), BF16), ((seq, LANES), BF16), ((seq, LANES), BF16), ((CHUNK * n_chunks, rep * HEAD_DIM), BF16)]
    temps = [((rep * CHUNK, 2 * LANES), F32)] * (4 * n_chunks)
    attn = pl.pallas_call(
        functools.partial(_attn_prompt_kernel, n_chunks=n_chunks, rep=rep),
        grid=(batch, g, ncb),
        in_specs=[
            pl.BlockSpec((rep, CHUNK * n_chunks, HEAD_DIM), lambda b, gi, cb: (gi, b * ncb + cb, 0)),
            pl.BlockSpec((1, seq, HEAD_DIM), lambda b, gi, cb: (gi, b, 0)),
            pl.BlockSpec((1, seq, HEAD_DIM), lambda b, gi, cb: (gi, b, 0)),
            pl.BlockSpec((1, rep * CHUNK, 1), lambda b, gi, cb: (gi, 0, 0)),
        ],
        out_specs=pl.BlockSpec((CHUNK * n_chunks, rep * HEAD_DIM), lambda b, gi, cb: (b * ncb + cb, gi)),
        out_shape=jax.ShapeDtypeStruct((m, aw), BF16),
        compiler_params=_params(("arbitrary",) * 3, _vmem_estimate(blocks, temps)),
        name="attn_prompt",
    )(q_hm, k_hm, v_hm, sink_rows)
    row0 = batch * seq // CHUNK
    return pl.pallas_call(
        functools.partial(_attn_sample_kernel, rep=rep),
        grid=(dec_batch, g),
        in_specs=[
            pl.BlockSpec((rep, CHUNK, HEAD_DIM), lambda b, gi: (gi, row0 + b, 0)),
            pl.BlockSpec((1, win, HEAD_DIM), lambda b, gi: (gi, b, 0)),
            pl.BlockSpec((1, win, HEAD_DIM), lambda b, gi: (gi, b, 0)),
            pl.BlockSpec((1, rep * CHUNK, 1), lambda b, gi: (gi, 0, 0)),
            pl.BlockSpec(memory_space=pl.ANY),
        ],
        out_specs=pl.BlockSpec((CHUNK, rep * HEAD_DIM), lambda b, gi: (row0 + b, gi)),
        out_shape=jax.ShapeDtypeStruct((m, aw), BF16),
        input_output_aliases={4: 0},
        compiler_params=_params(("arbitrary",) * 2, _vmem_estimate(blocks[:1] + blocks[3:], temps[:4])),
        name="attn_sample",
    )(q_hm, ks, vs, sink_rows, attn)


def _ssm_disc_kernel(lre_ref, lim_ref, ldt_ref, pre_ref, pim_ref, fre_ref, fim_ref):
    lr = jnp.minimum(lre_ref[...], LAMBDA_RE_MAX)
    li = lim_ref[...]
    dt = jnp.exp(ldt_ref[...])
    mag = jnp.exp(lr * dt)
    a_re = mag * jnp.cos(li * dt)
    a_im = mag * jnp.sin(li * dt)
    den = lr * lr + li * li
    nr = a_re - 1.0
    fre_ref[...] = (nr * lr + a_im * li) / den
    fim_ref[...] = (a_im * lr - nr * li) / den
    pre_ref[0] = a_re
    pim_ref[0] = a_im
    for b in range(1, SSM_POW_BITS + 1):
        a_re, a_im = a_re * a_re - a_im * a_im, 2.0 * a_re * a_im
        pre_ref[b] = a_re
        pim_ref[b] = a_im


def _cmul(ar, ai, br, bi):
    return ar * br - ai * bi, ar * bi + ai * br


def _power_by_bits(exponent, base_re, base_im, shape):
    p_re = p_im = None
    for b in range(SSM_POW_BITS):
        on = jnp.broadcast_to(((exponent >> b) & 1) == 1, shape)
        f_re = jnp.where(on, jnp.broadcast_to(base_re[b], shape), 1.0)
        f_im = jnp.where(on, jnp.broadcast_to(base_im[b], shape), 0.0)
        if p_re is None:
            p_re, p_im = f_re, f_im
        else:
            p_re, p_im = _cmul(p_re, p_im, f_re, f_im)
    return p_re, p_im


def _ssm_build_kernel(pcr_ref, pci_ref, prr_ref, pri_ref, frr_ref, fri_ref, btr_ref, bti_ref, ctr_ref, cti_ref, dv_ref,
                      me_ref, f_ref, a1_ref, a2_ref, a2s_ref, *, groups, p):
    lc = SSM_CHUNK * SSM_GROUP
    lag = lax.broadcasted_iota(jnp.int32, (1, lc), 1) // SSM_GROUP
    tail = (SSM_CHUNK - 1) - lax.broadcasted_iota(jnp.int32, (lc, 1), 0) // SSM_GROUP
    sub = lax.broadcasted_iota(jnp.int32, (SSM_GROUP, lc), 0)
    lane = lax.broadcasted_iota(jnp.int32, (SSM_GROUP, lc), 1)
    for gi in range(groups):
        col_re = [pcr_ref[b, gi] for b in range(SSM_POW_BITS)]
        col_im = [pci_ref[b, gi] for b in range(SSM_POW_BITS)]
        row_re = [prr_ref[b, gi] for b in range(SSM_POW_BITS)]
        row_im = [pri_ref[b, gi] for b in range(SSM_POW_BITS)]
        ctr, cti = ctr_ref[gi], cti_ref[gi]
        pw_re, pw_im = _power_by_bits(lag, col_re, col_im, (p, lc))
        w_re, w_im = _cmul(pw_re, pw_im, ctr, cti)
        pn_re, pn_im = _cmul(pw_re, pw_im, col_re[0], col_im[0])
        e_re, e_im = _cmul(pn_re, pn_im, ctr, cti)
        bb_re, bb_im = _cmul(frr_ref[gi], fri_ref[gi], btr_ref[gi], bti_ref[gi])
        r0 = jnp.dot(jnp.concatenate([bb_re, bb_im], axis=1), jnp.concatenate([w_re, -w_im], axis=0),
                     preferred_element_type=F32, precision=lax.Precision.HIGHEST)
        r0 = r0 + jnp.where(sub == lane, dv_ref[gi], 0.0)
        rows = [r0] + [jnp.where(lane >= s * SSM_GROUP, pltpu.roll(r0, s * SSM_GROUP, 1), 0.0) for s in range(1, SSM_CHUNK)]
        me_ref[gi] = jnp.concatenate(rows + [e_re, -e_im], axis=0).astype(me_ref.dtype)
        q_re, q_im = _power_by_bits(tail, row_re, row_im, (lc, p))
        f_re, f_im = _cmul(q_re, q_im, jnp.tile(bb_re, (SSM_CHUNK, 1)), jnp.tile(bb_im, (SSM_CHUNK, 1)))
        f_ref[gi] = jnp.concatenate([f_re, f_im], axis=1).astype(f_ref.dtype)
        d_re, d_im = prr_ref[SSM_POW_BITS, gi], pri_ref[SSM_POW_BITS, gi]
        a1_ref[gi] = jnp.concatenate([d_re, d_re], axis=1)
        a2_ref[gi] = jnp.concatenate([-d_im, d_im], axis=1)
        a2s_ref[gi] = jnp.concatenate([d_im, -d_im], axis=1)


def _ssm_params(lambda_re, lambda_im, log_dt, b_re, b_im, c_re, c_im, d_skip):
    ng, p = lambda_re.shape
    lc = SSM_CHUNK * SSM_GROUP
    nb = SSM_POW_BITS + 1
    full = pl.BlockSpec((ng, p), lambda: (0, 0))
    pows = pl.BlockSpec((nb, ng, p), lambda: (0, 0, 0))
    pre, pim, fre, fim = pl.pallas_call(
        _ssm_disc_kernel,
        in_specs=[full, full, pl.BlockSpec((ng, 1), lambda: (0, 0))],
        out_specs=[pows, pows, full, full],
        out_shape=[jax.ShapeDtypeStruct((nb, ng, p), F32)] * 2 + [jax.ShapeDtypeStruct((ng, p), F32)] * 2,
        name="ssm_disc",
    )(lambda_re, lambda_im, log_dt.reshape(ng, 1))
    gb = _pick(ng, (8, 4, 2, 1))
    bt_re = jnp.swapaxes(b_re, 1, 2)
    bt_im = jnp.swapaxes(b_im, 1, 2)
    ct_re = jnp.tile(jnp.swapaxes(c_re, 1, 2), (1, 1, SSM_CHUNK))
    ct_im = jnp.tile(jnp.swapaxes(c_im, 1, 2), (1, 1, SSM_CHUNK))
    dvec = jnp.pad(d_skip, ((0, 0), (0, lc - SSM_GROUP))).reshape(ng, 1, lc)
    col = pl.BlockSpec((nb, gb, p, 1), lambda i: (0, i, 0, 0))
    row = pl.BlockSpec((nb, gb, 1, p), lambda i: (0, i, 0, 0))
    frow = pl.BlockSpec((gb, 1, p), lambda i: (i, 0, 0))
    bt = pl.BlockSpec((gb, SSM_GROUP, p), lambda i: (i, 0, 0))
    ct = pl.BlockSpec((gb, p, lc), lambda i: (i, 0, 0))
    dec = pl.BlockSpec((gb, 1, 2 * p), lambda i: (i, 0, 0))
    blocks = ([((nb, gb, p, LANES), F32)] * 2 + [((nb, gb, 8, LANES), F32)] * 2 + [((gb, SSM_GROUP, LANES), F32)] * 2
              + [((gb, p, lc), F32)] * 2 + [((gb, lc + 2 * p, lc), BF16), ((gb, lc, 2 * p), BF16)])
    me, f, a1, a2, a2s = pl.pallas_call(
        functools.partial(_ssm_build_kernel, groups=gb, p=p),
        grid=(ng // gb,),
        in_specs=[col, col, row, row, frow, frow, bt, bt, ct, ct, pl.BlockSpec((gb, 1, lc), lambda i: (i, 0, 0))],
        out_specs=[pl.BlockSpec((gb, lc + 2 * p, lc), lambda i: (i, 0, 0)), pl.BlockSpec((gb, lc, 2 * p), lambda i: (i, 0, 0)), dec, dec, dec],
        out_shape=[jax.ShapeDtypeStruct((ng, lc + 2 * p, lc), BF16), jax.ShapeDtypeStruct((ng, lc, 2 * p), BF16)]
        + [jax.ShapeDtypeStruct((ng, 1, 2 * p), F32)] * 3,
        compiler_params=_params(("arbitrary",), _vmem_estimate(blocks, [((lc, lc), F32)] * 8)),
        name="ssm_build",
    )(pre.reshape(nb, ng, p, 1), pim.reshape(nb, ng, p, 1), pre.reshape(nb, ng, 1, p), pim.reshape(nb, ng, 1, p),
      fre.reshape(ng, 1, p), fim.reshape(ng, 1, p), bt_re, bt_im, ct_re, ct_im, dvec)
    flat = lambda a: a.reshape(1, ng * 2 * p)
    return me, f, flat(a1), flat(a2), flat(a2s)


def _ssm_state_in_kernel(u_ref, f_ref, s_ref, *, groups, p):
    lc = SSM_CHUNK * SSM_GROUP
    for gi in range(groups):
        s_ref[:, gi * 2 * p:(gi + 1) * 2 * p] = jnp.dot(u_ref[:, gi * lc:(gi + 1) * lc], f_ref[gi], preferred_element_type=F32)


def _ssm_out_kernel(u_ref, h_ref, me_ref, y_ref, *, groups, p):
    lc = SSM_CHUNK * SSM_GROUP
    for gi in range(groups):
        lhs = jnp.concatenate([u_ref[:, gi * lc:(gi + 1) * lc], h_ref[:, gi * 2 * p:(gi + 1) * 2 * p].astype(BF16)], axis=1)
        y_ref[:, gi * lc:(gi + 1) * lc] = jnp.dot(lhs, me_ref[gi], preferred_element_type=F32).astype(y_ref.dtype)


def _ssm_scan_kernel(s_ref, h0_ref, a1_ref, a2_ref, a2s_ref, h_ref, fin_ref, ss_ref, *, batch, kp, dec_batch, ks, p):
    def swap_halves(x):
        lane = lax.broadcasted_iota(jnp.int32, x.shape, 1)
        return jnp.where(lane % (2 * p) < p, pltpu.roll(x, x.shape[1] - p, 1), pltpu.roll(x, p, 1))

    ss_ref[...] = swap_halves(s_ref[...])
    a1, a2, a2s = a1_ref[...], a2_ref[...], a2s_ref[...]
    zero = jnp.zeros_like(a1)

    def step(k, h, hs):
        h_ref[pl.ds(k, 1), :] = h
        s = s_ref[pl.ds(k, 1), :]
        ss = ss_ref[pl.ds(k, 1), :]
        return a1 * h + a2 * hs + s, a1 * hs + a2s * h + ss

    def body(k, carry):
        out = []
        for b in range(batch):
            out.extend(step(b * kp + k, carry[2 * b], carry[2 * b + 1]))
        return tuple(out)

    fin = lax.fori_loop(0, kp, body, (zero,) * (2 * batch))
    for b in range(batch):
        fin_ref[b:b + 1, :] = fin[2 * b]
    h0s_all = swap_halves(h0_ref[...])
    for b in range(dec_batch):
        h, hs = h0_ref[b:b + 1, :], h0s_all[b:b + 1, :]
        for k in range(ks):
            h, hs = step(batch * kp + b * ks + k, h, hs)
        fin_ref[batch + b:batch + b + 1, :] = h


def _ssm(u2, me, f, a1, a2, a2s, h0, batch, kp, dec_batch, ks):
    nk, width = u2.shape
    ng, lc, p2 = f.shape
    p = p2 // 2
    gb = _pick(ng, (8, 4, 2, 1))
    u_spec = pl.BlockSpec((nk, gb * lc), lambda i: (0, i))
    st_spec = pl.BlockSpec((nk, gb * p2), lambda i: (0, i))
    s = pl.pallas_call(
        functools.partial(_ssm_state_in_kernel, groups=gb, p=p),
        grid=(ng // gb,),
        in_specs=[u_spec, pl.BlockSpec((gb, lc, p2), lambda i: (i, 0, 0))],
        out_specs=st_spec,
        out_shape=jax.ShapeDtypeStruct((nk, ng * p2), F32),
        compiler_params=_params(("arbitrary",), _vmem_estimate([((nk, gb * lc), BF16), ((gb, lc, p2), BF16), ((nk, gb * p2), F32)])),
        name="ssm_state_in",
    )(u2, f)
    lb = _pick(ng * p2, (1024, 512, 256, 128))
    nseq = batch + dec_batch
    rowb = pl.BlockSpec((1, lb), lambda i: (0, i))
    h, fin = pl.pallas_call(
        functools.partial(_ssm_scan_kernel, batch=batch, kp=kp, dec_batch=dec_batch, ks=ks, p=p),
        grid=(ng * p2 // lb,),
        in_specs=[pl.BlockSpec((nk, lb), lambda i: (0, i)), pl.BlockSpec((dec_batch, lb), lambda i: (0, i)), rowb, rowb, rowb],
        out_specs=[pl.BlockSpec((nk, lb), lambda i: (0, i)), pl.BlockSpec((nseq, lb), lambda i: (0, i))],
        out_shape=[jax.ShapeDtypeStruct((nk, ng * p2), F32), jax.ShapeDtypeStruct((nseq, ng * p2), F32)],
        scratch_shapes=[pltpu.VMEM((nk, lb), F32)],
        compiler_params=_params(("arbitrary",), _vmem_estimate([((nk, lb), F32)] * 2, [((nk, lb), F32)])),
        name="ssm_scan",
    )(s, h0, a1, a2, a2s)
    y2 = pl.pallas_call(
        functools.partial(_ssm_out_kernel, groups=gb, p=p),
        grid=(ng // gb,),
        in_specs=[u_spec, st_spec, pl.BlockSpec((gb, lc + p2, lc), lambda i: (i, 0, 0))],
        out_specs=u_spec,
        out_shape=jax.ShapeDtypeStruct((nk, width), BF16),
        compiler_params=_params(("arbitrary",), _vmem_estimate([((nk, gb * lc), BF16)] * 2 + [((nk, gb * p2), F32), ((gb, lc + p2, lc), BF16)])),
        name="ssm_out",
    )(u2, h, me)
    return y2, fin


def _glu_kernel(y_ref, wa_ref, wg_ref, ba_ref, bg_ref, zs_ref, o_ref):
    y = y_ref[...]
    a = jnp.dot(y, wa_ref[...], preferred_element_type=F32) + ba_ref[...]
    g = jnp.dot(y, wg_ref[...], preferred_element_type=F32) + bg_ref[...]
    o_ref[...] = (a * jax.nn.sigmoid(g) * zs_ref[...].astype(F32)).astype(o_ref.dtype)


def _glu(y, w_glu, b_glu, zs, tm, tn):
    m, sw = y.shape
    nb = sw // tn
    blocks = [((tm, sw), BF16), ((sw, tn), BF16), ((sw, tn), BF16), ((tm, tn), BF16), ((tm, tn), BF16)]
    return pl.pallas_call(
        _glu_kernel,
        grid=(nb, m // tm),
        in_specs=[
            pl.BlockSpec((tm, sw), lambda j, i: (i, 0)),
            pl.BlockSpec((sw, tn), lambda j, i: (0, j)),
            pl.BlockSpec((sw, tn), lambda j, i: (0, nb + j)),
            pl.BlockSpec((1, tn), lambda j, i: (0, j)),
            pl.BlockSpec((1, tn), lambda j, i: (0, nb + j)),
            pl.BlockSpec((tm, tn), lambda j, i: (i, j)),
        ],
        out_specs=pl.BlockSpec((tm, tn), lambda j, i: (i, j)),
        out_shape=jax.ShapeDtypeStruct((m, sw), BF16),
        compiler_params=_params(("arbitrary", "arbitrary"), _vmem_estimate(blocks, [((tm, tn), F32)] * 4)),
        name="glu",
    )(y, w_glu, w_glu, b_glu, b_glu, zs)


def _merge_kernel(attn_ref, za_ref, xs_ref, wpa_ref, wps_ref, ga_ref, gs_ref, o_ref):
    xa = (attn_ref[...].astype(F32) * za_ref[...].astype(F32)).astype(BF16)
    br_a = jnp.dot(xa, wpa_ref[...], preferred_element_type=F32)
    br_s = jnp.dot(xs_ref[...], wps_ref[...], preferred_element_type=F32)
    o_ref[...] = (ga_ref[...].astype(F32) * br_a + gs_ref[...].astype(F32) * br_s).astype(o_ref.dtype)


def _merge(attn, za, xs, w_pa, w_ps, gate, tm, tn):
    m, aw = attn.shape
    sw = xs.shape[1]
    d = w_pa.shape[1]
    nb = d // tn
    blocks = [((tm, aw), BF16)] * 2 + [((tm, sw), BF16), ((aw, tn), BF16), ((sw, tn), BF16)] + [((tm, tn), BF16)] * 3
    return pl.pallas_call(
        _merge_kernel,
        grid=(nb, m // tm),
        in_specs=[
            pl.BlockSpec((tm, aw), lambda j, i: (i, 0)),
            pl.BlockSpec((tm, aw), lambda j, i: (i, 0)),
            pl.BlockSpec((tm, sw), lambda j, i: (i, 0)),
            pl.BlockSpec((aw, tn), lambda j, i: (0, j)),
            pl.BlockSpec((sw, tn), lambda j, i: (0, j)),
            pl.BlockSpec((tm, tn), lambda j, i: (i, j)),
            pl.BlockSpec((tm, tn), lambda j, i: (i, nb + j)),
        ],
        out_specs=pl.BlockSpec((tm, tn), lambda j, i: (i, j)),
        out_shape=jax.ShapeDtypeStruct((m, d), BF16),
        compiler_params=_params(("arbitrary", "arbitrary"), _vmem_estimate(blocks, [((tm, tn), F32)] * 4 + [((tm, aw), F32)])),
        name="merge",
    )(attn, za, xs, w_pa, w_ps, gate, gate)


def _out_kernel(mg_ref, w_ref, x_ref, g_ref, o_ref, acc_ref, *, n_col_blocks, tn):
    j = pl.program_id(1)
    acc_ref[j] = x_ref[...] + jnp.dot(mg_ref[...], w_ref[...], preferred_element_type=F32)

    @pl.when(j == n_col_blocks - 1)
    def _():
        ssq = acc_ref[0] * acc_ref[0]
        tot = jnp.sum(ssq, axis=-1, keepdims=True)
        for c in range(1, n_col_blocks):
            blk = acc_ref[c]
            tot = tot + jnp.sum(blk * blk, axis=-1, keepdims=True)
        scale = lax.rsqrt(tot / (n_col_blocks * tn) + NORM_EPS)
        for c in range(n_col_blocks):
            o_ref[:, c * tn:(c + 1) * tn] = acc_ref[c] * scale * g_ref[:, c * tn:(c + 1) * tn]


def _out(merged, row_block0, w_out, x, final_g, tm, tn):
    mx, d = x.shape
    nb = d // tn
    blocks = [((tm, d), BF16), ((d, tn), BF16), ((tm, tn), F32), ((tm, d), F32)]
    return pl.pallas_call(
        functools.partial(_out_kernel, n_col_blocks=nb, tn=tn),
        grid=(mx // tm, nb),
        in_specs=[
            pl.BlockSpec((tm, d), lambda i, j: (row_block0 + i, 0)),
            pl.BlockSpec((d, tn), lambda i, j: (0, j)),
            pl.BlockSpec((tm, tn), lambda i, j: (i, j)),
            pl.BlockSpec((1, d), lambda i, j: (0, 0)),
        ],
        out_specs=pl.BlockSpec((tm, d), lambda i, j: (i, 0)),
        out_shape=jax.ShapeDtypeStruct((mx, d), F32),
        scratch_shapes=[pltpu.VMEM((nb, tm, tn), F32)],
        compiler_params=_params(("arbitrary", "arbitrary"), _vmem_estimate(blocks, [((tm, d), F32), ((tm, tn), F32)])),
        name="out_norm",
    )(merged, w_out, x, final_g.reshape(1, d))


def _rope_tables(positions):
    half = HEAD_DIM // 2
    inv_freq = ROPE_THETA ** (-jnp.arange(half, dtype=F32) / half)
    ang = positions.astype(F32)[:, None] * inv_freq[None, :]
    cos, sin = jnp.cos(ang), jnp.sin(ang)
    reps = LANES // HEAD_DIM
    return jnp.tile(jnp.concatenate([cos, cos], axis=1), (1, reps)), jnp.tile(jnp.concatenate([-sin, sin], axis=1), (1, reps))


def kernel(x_prompt, x_sample, cache_k, cache_v, state_ssm_re, state_ssm_im, norm_g, w_in, sink, lambda_re, lambda_im,
           log_dt, b_re, b_im, c_re, c_im, d_skip, w_glu, b_glu, w_pa, w_ps, w_out, final_g):
    depth = norm_g.shape[0]
    assert depth == 1, "one trunk layer"
    batch, seq, d = x_prompt.shape
    dec_batch, dec_seq, _ = x_sample.shape
    aw = w_pa.shape[1]
    sw = w_ps.shape[1]
    nh = aw // HEAD_DIM
    g = max(1, nh // GQA_GROUPING)
    rep = nh // g
    kvw = g * HEAD_DIM
    ng, p = lambda_re.shape[1:]
    assert dec_seq == CHUNK and cache_k.shape[2] == WINDOW and seq % (2 * CHUNK) == 0
    assert kvw % LANES == 0 and rep % 2 == 0 and sw == ng * SSM_GROUP and (1 << SSM_POW_BITS) == SSM_CHUNK
    mp, ms = batch * seq, dec_batch * dec_seq
    m = mp + ms
    tm = _pick(math.gcd(mp, ms), (512, 256, 128))
    tn = _pick(math.gcd(aw, sw), (1024, 512, 256))

    xp = x_prompt.reshape(mp, d)
    xs = x_sample.reshape(ms, d)
    widths = (aw, 2 * kvw, aw, sw, sw, 2 * d)
    assert sum(widths) == w_in.shape[2]
    edges = [sum(widths[:n]) for n in range(len(widths) + 1)]
    w_q, w_kv, w_za, w_u, w_zs, w_gate = (w_in[0, :, a:b].astype(BF16) for a, b in zip(edges[:-1], edges[1:]))
    positions = jnp.concatenate([jnp.tile(jnp.arange(seq, dtype=jnp.int32), batch),
                                 jnp.tile(PAST_LEN + jnp.arange(dec_seq, dtype=jnp.int32), dec_batch)])
    cos, sin = _rope_tables(positions)

    h = _rmsnorm(xp, xs, norm_g[0], tm)
    q_hm = _proj_q(h, w_q, cos, sin, aw, tm, tn)
    k_f, v_f, k_hm, v_hm = _proj_kv(h, w_kv, cos, sin, kvw, tm)
    za = _proj_act(h, w_za, "silu", tm, tn, "proj_za")
    u = _proj_act(h, w_u, "none", tm, tn, "proj_u")
    zs = _proj_act(h, w_zs, "silu", tm, tn, "proj_zs")
    gate = _proj_act(h, w_gate, "sigmoid", tm, tn, "proj_gate")

    to_heads = lambda c: jnp.transpose(c[0], (2, 0, 1, 3)).astype(BF16)
    new_rows = lambda a: a[:, mp:].reshape(g, dec_batch, dec_seq, HEAD_DIM)
    ks = jnp.concatenate([to_heads(cache_k), new_rows(k_hm)], axis=2).reshape(g, dec_batch * (WINDOW + CHUNK), HEAD_DIM)
    vs = jnp.concatenate([to_heads(cache_v), new_rows(v_hm)], axis=2).reshape(g, dec_batch * (WINDOW + CHUNK), HEAD_DIM)
    sink_rows = jnp.repeat(sink[0].reshape(g, rep), CHUNK, axis=1).reshape(g, rep * CHUNK, 1)
    attn = _attention(q_hm, k_hm, v_hm, ks, vs, sink_rows, batch, seq, dec_batch, aw)

    me, f, a1, a2, a2s = _ssm_params(lambda_re[0], lambda_im[0], log_dt[0], b_re[0], b_im[0], c_re[0], c_im[0], d_skip[0])
    nk = m // SSM_CHUNK
    u2 = u.reshape(nk, SSM_CHUNK, ng, SSM_GROUP).transpose(0, 2, 1, 3).reshape(nk, ng * SSM_CHUNK * SSM_GROUP)
    h0 = jnp.concatenate([state_ssm_re[0], state_ssm_im[0]], axis=-1).reshape(dec_batch, ng * 2 * p)
    y2, fin = _ssm(u2, me, f, a1, a2, a2s, h0, batch, seq // SSM_CHUNK, dec_batch, dec_seq // SSM_CHUNK)
    y = y2.reshape(nk, ng, SSM_CHUNK, SSM_GROUP).transpose(0, 2, 1, 3).reshape(m, sw)

    x_ssm = _glu(y, w_glu[0].astype(BF16), b_glu[0].reshape(1, 2 * sw), zs, tm, tn)
    merged = _merge(attn, za, x_ssm, w_pa[0].astype(BF16), w_ps[0].astype(BF16), gate, tm, tn)
    w_out_b = w_out[0].astype(BF16)
    tno = _pick(d, (512, 256))
    y_prompt = _out(merged, 0, w_out_b, xp, final_g, tm, tno).reshape(batch, seq, d)
    y_sample = _out(merged, mp // tm, w_out_b, xs, final_g, tm, tno).reshape(dec_batch, dec_seq, d)

    keep = min(WINDOW, seq)
    last_rows = lambda a: a[:mp].reshape(batch, seq, g, HEAD_DIM)[:, seq - keep:][None]
    dec_rows = lambda a: a[mp:].reshape(dec_batch, dec_seq, g, HEAD_DIM)[None]
    fin = fin.reshape(batch + dec_batch, ng, 2, p)
    return (y_prompt, y_sample, last_rows(k_f), last_rows(v_f), fin[:batch, :, 0][None], fin[:batch, :, 1][None],
            dec_rows(k_f), dec_rows(v_f), fin[batch:, :, 0][None], fin[batch:, :, 1][None])
```

```python
import functools
import math

import jax
import jax.numpy as jnp
from jax import lax
from jax.experimental import pallas as pl
from jax.experimental.pallas import tpu as pltpu

CHUNK = 64
WINDOW = 128
HEAD_DIM = 64
GQA_GROUPING = 8
SSM_GROUP = 16
PAST_LEN = 1024
ROPE_THETA = 10000.0
NORM_EPS = 1e-5
LAMBDA_RE_MAX = -1e-4

SSM_CHUNK = 16
SSM_POW_BITS = 4
LANES = 128
V7X_VMEM_BYTES = 64 * 1024 * 1024
BF16 = jnp.bfloat16
F32 = jnp.float32


def _pick(n, prefs):
    for p in prefs:
        if n % p == 0:
            return p
    raise ValueError(f"no tile in {prefs} divides {n}")


def _params(sem, vmem_bytes):
    limit = min(int(vmem_bytes), V7X_VMEM_BYTES - 4 * 1024 * 1024)
    return pltpu.CompilerParams(dimension_semantics=sem, vmem_limit_bytes=limit)


def _nbytes(shape, dtype):
    return math.prod(shape) * jnp.dtype(dtype).itemsize


def _vmem_estimate(blocks, temps=()):
    return 2 * (2 * sum(_nbytes(s, d) for s, d in blocks) + sum(_nbytes(s, d) for s, d in temps))


def _rmsnorm_kernel(xp_ref, xs_ref, g_ref, o_ref, *, n_prompt_blocks):
    i = pl.program_id(0)

    def norm(x_ref):
        x = x_ref[...]
        y = x * lax.rsqrt(jnp.mean(x * x, axis=-1, keepdims=True) + NORM_EPS)
        o_ref[...] = (y * g_ref[...]).astype(o_ref.dtype)

    @pl.when(i < n_prompt_blocks)
    def _():
        norm(xp_ref)

    @pl.when(i >= n_prompt_blocks)
    def _():
        norm(xs_ref)


def _rmsnorm(xp, xs, g, tm):
    mp, d = xp.shape
    ms = xs.shape[0]
    npb, nsb = mp // tm, ms // tm
    return pl.pallas_call(
        functools.partial(_rmsnorm_kernel, n_prompt_blocks=npb),
        grid=(npb + nsb,),
        in_specs=[
            pl.BlockSpec((tm, d), lambda i: (jnp.minimum(i, npb - 1), 0)),
            pl.BlockSpec((tm, d), lambda i: (jnp.maximum(i - npb, 0), 0)),
            pl.BlockSpec((1, d), lambda i: (0, 0)),
        ],
        out_specs=pl.BlockSpec((tm, d), lambda i: (i, 0)),
        out_shape=jax.ShapeDtypeStruct((mp + ms, d), BF16),
        compiler_params=_params(("arbitrary",), _vmem_estimate([((tm, d), F32)] * 2 + [((tm, d), BF16)], [((tm, d), F32)])),
        name="rmsnorm_in",
    )(xp, xs, g.reshape(1, d))


def _rope(acc, cos_ref, sin_ref):
    tm, tn = acc.shape
    reps = tn // LANES
    cos = jnp.tile(cos_ref[...], (1, reps))
    sin = jnp.tile(sin_ref[...], (1, reps))
    lane = lax.broadcasted_iota(jnp.int32, (tm, tn), 1)
    low = (lane % HEAD_DIM) < (HEAD_DIM // 2)
    partner = jnp.where(low, pltpu.roll(acc, tn - HEAD_DIM // 2, 1), pltpu.roll(acc, HEAD_DIM // 2, 1))
    return acc * cos + partner * sin


def _store_heads(o_ref, val):
    for h in range(val.shape[1] // HEAD_DIM):
        o_ref[h] = val[:, h * HEAD_DIM:(h + 1) * HEAD_DIM].astype(o_ref.dtype)


def _cast_weight(w_ref, wb_ref):
    @pl.when(pl.program_id(1) == 0)
    def _():
        wb_ref[...] = w_ref[...].astype(wb_ref.dtype)


def _proj_q_kernel(h_ref, w_ref, cos_ref, sin_ref, q_ref, wb_ref):
    _cast_weight(w_ref, wb_ref)
    acc = jnp.dot(h_ref[...], wb_ref[...], preferred_element_type=F32)
    _store_heads(q_ref, _rope(acc, cos_ref, sin_ref) * (HEAD_DIM ** -0.5))


def _proj_kv_kernel(h_ref, w_ref, cos_ref, sin_ref, kf_ref, vf_ref, kh_ref, vh_ref, wb_ref, *, kvw):
    _cast_weight(w_ref, wb_ref)
    acc = jnp.dot(h_ref[...], wb_ref[...], preferred_element_type=F32)
    k = _rope(acc[:, :kvw], cos_ref, sin_ref)
    v = acc[:, kvw:]
    kf_ref[...] = k
    vf_ref[...] = v
    _store_heads(kh_ref, k)
    _store_heads(vh_ref, v)


def _proj_act_kernel(h_ref, w_ref, o_ref, wb_ref, *, act):
    _cast_weight(w_ref, wb_ref)
    acc = jnp.dot(h_ref[...], wb_ref[...], preferred_element_type=F32)
    if act == "silu":
        acc = acc * jax.nn.sigmoid(acc)
    elif act == "sigmoid":
        acc = jax.nn.sigmoid(acc)
    o_ref[...] = acc.astype(o_ref.dtype)


def _proj_specs(d, tm, tn, col0):
    assert col0 % tn == 0
    cb0 = col0 // tn
    return [pl.BlockSpec((tm, d), lambda j, i: (i, 0)), pl.BlockSpec((d, tn), lambda j, i: (0, cb0 + j))]


def _proj_vmem(d, tm, tn, outs):
    return _vmem_estimate([((tm, d), BF16), ((d, tn), F32)] + outs, [((tm, tn), F32)] * 3 + [((d, tn), BF16)])


def _proj_q(h, w, cos, sin, aw, tm, tn):
    m, d = h.shape
    nh = aw // HEAD_DIM
    tab = pl.BlockSpec((tm, LANES), lambda j, i: (i, 0))
    return pl.pallas_call(
        _proj_q_kernel,
        grid=(aw // tn, m // tm),
        in_specs=_proj_specs(d, tm, tn, 0) + [tab, tab],
        out_specs=pl.BlockSpec((tn // HEAD_DIM, tm, HEAD_DIM), lambda j, i: (j, i, 0)),
        out_shape=jax.ShapeDtypeStruct((nh, m, HEAD_DIM), BF16),
        scratch_shapes=[pltpu.VMEM((d, tn), BF16)],
        compiler_params=_params(("arbitrary", "arbitrary"), _proj_vmem(d, tm, tn, [((tm, 2 * tn), BF16), ((tm, 2 * LANES), F32)])),
        name="proj_q",
    )(h, w, cos, sin)


def _proj_kv(h, w, cos, sin, col0, kvw, tm):
    m, d = h.shape
    g = kvw // HEAD_DIM
    tn = 2 * kvw
    tab = pl.BlockSpec((tm, LANES), lambda j, i: (i, 0))
    flat = pl.BlockSpec((tm, kvw), lambda j, i: (i, 0))
    heads = pl.BlockSpec((g, tm, HEAD_DIM), lambda j, i: (0, i, 0))
    return pl.pallas_call(
        functools.partial(_proj_kv_kernel, kvw=kvw),
        grid=(1, m // tm),
        in_specs=_proj_specs(d, tm, tn, col0) + [tab, tab],
        out_specs=[flat, flat, heads, heads],
        out_shape=[jax.ShapeDtypeStruct((m, kvw), F32)] * 2 + [jax.ShapeDtypeStruct((g, m, HEAD_DIM), BF16)] * 2,
        scratch_shapes=[pltpu.VMEM((d, tn), BF16)],
        compiler_params=_params(("arbitrary", "arbitrary"), _proj_vmem(d, tm, tn, [((tm, 2 * tn), F32), ((tm, 2 * tn), BF16), ((tm, 2 * LANES), F32)])),
        name="proj_kv",
    )(h, w, cos, sin)


def _proj_act(h, w, col0, ncols, act, tm, tn, name):
    m, d = h.shape
    return pl.pallas_call(
        functools.partial(_proj_act_kernel, act=act),
        grid=(ncols // tn, m // tm),
        in_specs=_proj_specs(d, tm, tn, col0),
        out_specs=pl.BlockSpec((tm, tn), lambda j, i: (i, j)),
        out_shape=jax.ShapeDtypeStruct((m, ncols), BF16),
        scratch_shapes=[pltpu.VMEM((d, tn), BF16)],
        compiler_params=_params(("arbitrary", "arbitrary"), _proj_vmem(d, tm, tn, [((tm, tn), BF16)])),
        name=name,
    )(h, w)


def _attend_unit(q, kw, vw, sink, n_valid):
    st = lax.dot_general(kw, q, (((1,), (1,)), ((), ())), preferred_element_type=F32)
    if n_valid is not None:
        row = lax.broadcasted_iota(jnp.int32, st.shape, 0)
        st = jnp.where(row < n_valid, st, -jnp.inf)
    m = jnp.maximum(jnp.max(st, axis=0, keepdims=True), sink)
    e = jnp.exp(st - m)
    denom = jnp.sum(e, axis=0, keepdims=True) + jnp.exp(sink - m)
    ot = jnp.dot(vw.astype(F32).T.astype(BF16), e.astype(BF16), preferred_element_type=F32) / denom
    return ot.T


def _store_unit(o_ref, row0, o, rep):
    for r in range(0, rep, 2):
        pair = jnp.concatenate([o[r * CHUNK:(r + 1) * CHUNK], o[(r + 1) * CHUNK:(r + 2) * CHUNK]], axis=1)
        o_ref[row0:row0 + CHUNK, r * HEAD_DIM:(r + 2) * HEAD_DIM] = pair.astype(o_ref.dtype)


def _attn_prompt_kernel(q_ref, k_ref, v_ref, sink_ref, o_ref, *, n_chunks, rep):
    cb = pl.program_id(2)
    win = WINDOW + CHUNK
    sink = sink_ref[0]
    for c in range(n_chunks):
        cg = cb * n_chunks + c
        start = pl.multiple_of(jnp.maximum(cg - WINDOW // CHUNK, 0) * CHUNK, CHUNK)
        q = q_ref[:, c * CHUNK:(c + 1) * CHUNK, :].reshape(rep * CHUNK, HEAD_DIM)
        kw = k_ref[0, pl.ds(start, win), :]
        vw = v_ref[0, pl.ds(start, win), :]
        n_valid = jnp.where(cb == 0, (c + 1) * CHUNK, win) if c < WINDOW // CHUNK else None
        _store_unit(o_ref, c * CHUNK, _attend_unit(q, kw, vw, sink, n_valid), rep)


def _attn_sample_kernel(q_ref, k_ref, v_ref, sink_ref, prev_ref, o_ref, *, rep):
    del prev_ref
    q = q_ref[...].reshape(rep * CHUNK, HEAD_DIM)
    _store_unit(o_ref, 0, _attend_unit(q, k_ref[0], v_ref[0], sink_ref[0], None), rep)


def _attention(q_hm, k_hm, v_hm, ks, vs, sink_rows, batch, seq, dec_batch, aw):
    nh, m, _ = q_hm.shape
    g = k_hm.shape[0]
    rep = nh // g
    win = WINDOW + CHUNK
    n_chunks = _pick(seq // CHUNK, (8, 4, 2))
    ncb = seq // (CHUNK * n_chunks)
    blocks = [((rep, CHUNK * n_chunks, HEAD_DIM), BF16), ((seq, LANES), BF16), ((seq, LANES), BF16), ((CHUNK * n_chunks, rep * HEAD_DIM), BF16)]
    temps = [((rep * CHUNK, 2 * LANES), F32)] * (4 * n_chunks)
    attn = pl.pallas_call(
        functools.partial(_attn_prompt_kernel, n_chunks=n_chunks, rep=rep),
        grid=(batch, g, ncb),
        in_specs=[
            pl.BlockSpec((rep, CHUNK * n_chunks, HEAD_DIM), lambda b, gi, cb: (gi, b * ncb + cb, 0)),
            pl.BlockSpec((1, seq, HEAD_DIM), lambda b, gi, cb: (gi, b, 0)),
            pl.BlockSpec((1, seq, HEAD_DIM), lambda b, gi, cb: (gi, b, 0)),
            pl.BlockSpec((1, 1, rep * CHUNK), lambda b, gi, cb: (gi, 0, 0)),
        ],
        out_specs=pl.BlockSpec((CHUNK * n_chunks, rep * HEAD_DIM), lambda b, gi, cb: (b * ncb + cb, gi)),
        out_shape=jax.ShapeDtypeStruct((m, aw), BF16),
        compiler_params=_params(("arbitrary",) * 3, _vmem_estimate(blocks, temps)),
        name="attn_prompt",
    )(q_hm, k_hm, v_hm, sink_rows)
    row0 = batch * seq // CHUNK
    return pl.pallas_call(
        functools.partial(_attn_sample_kernel, rep=rep),
        grid=(dec_batch, g),
        in_specs=[
            pl.BlockSpec((rep, CHUNK, HEAD_DIM), lambda b, gi: (gi, row0 + b, 0)),
            pl.BlockSpec((1, win, HEAD_DIM), lambda b, gi: (gi, b, 0)),
            pl.BlockSpec((1, win, HEAD_DIM), lambda b, gi: (gi, b, 0)),
            pl.BlockSpec((1, 1, rep * CHUNK), lambda b, gi: (gi, 0, 0)),
            pl.BlockSpec(memory_space=pl.ANY),
        ],
        out_specs=pl.BlockSpec((CHUNK, rep * HEAD_DIM), lambda b, gi: (row0 + b, gi)),
        out_shape=jax.ShapeDtypeStruct((m, aw), BF16),
        input_output_aliases={4: 0},
        compiler_params=_params(("arbitrary",) * 2, _vmem_estimate(blocks[:1] + blocks[3:], temps[:4])),
        name="attn_sample",
    )(q_hm, ks, vs, sink_rows, attn)


def _ssm_disc_kernel(lre_ref, lim_ref, ldt_ref, pre_ref, pim_ref, fre_ref, fim_ref):
    lr = jnp.minimum(lre_ref[...], LAMBDA_RE_MAX)
    li = lim_ref[...]
    dt = jnp.exp(ldt_ref[...])
    mag = jnp.exp(lr * dt)
    a_re = mag * jnp.cos(li * dt)
    a_im = mag * jnp.sin(li * dt)
    den = lr * lr + li * li
    nr = a_re - 1.0
    fre_ref[...] = (nr * lr + a_im * li) / den
    fim_ref[...] = (a_im * lr - nr * li) / den
    pre_ref[0] = a_re
    pim_ref[0] = a_im
    for b in range(1, SSM_POW_BITS + 1):
        a_re, a_im = a_re * a_re - a_im * a_im, 2.0 * a_re * a_im
        pre_ref[b] = a_re
        pim_ref[b] = a_im


def _cmul(ar, ai, br, bi):
    return ar * br - ai * bi, ar * bi + ai * br


def _power_by_bits(exponent, base_re, base_im, shape):
    p_re = p_im = None
    for b in range(SSM_POW_BITS):
        on = jnp.broadcast_to(((exponent >> b) & 1) == 1, shape)
        f_re = jnp.where(on, jnp.broadcast_to(base_re[b], shape), 1.0)
        f_im = jnp.where(on, jnp.broadcast_to(base_im[b], shape), 0.0)
        if p_re is None:
            p_re, p_im = f_re, f_im
        else:
            p_re, p_im = _cmul(p_re, p_im, f_re, f_im)
    return p_re, p_im


def _ssm_build_kernel(pcr_ref, pci_ref, prr_ref, pri_ref, frr_ref, fri_ref, btr_ref, bti_ref, ctr_ref, cti_ref, dv_ref,
                      me_ref, f_ref, a1_ref, a2_ref, a2s_ref, *, groups, p):
    lc = SSM_CHUNK * SSM_GROUP
    lag = lax.broadcasted_iota(jnp.int32, (1, lc), 1) // SSM_GROUP
    tail = (SSM_CHUNK - 1) - lax.broadcasted_iota(jnp.int32, (lc, 1), 0) // SSM_GROUP
    sub = lax.broadcasted_iota(jnp.int32, (SSM_GROUP, lc), 0)
    lane = lax.broadcasted_iota(jnp.int32, (SSM_GROUP, lc), 1)
    for gi in range(groups):
        col_re = [pcr_ref[b, gi] for b in range(SSM_POW_BITS)]
        col_im = [pci_ref[b, gi] for b in range(SSM_POW_BITS)]
        row_re = [prr_ref[b, gi] for b in range(SSM_POW_BITS)]
        row_im = [pri_ref[b, gi] for b in range(SSM_POW_BITS)]
        ctr, cti = ctr_ref[gi], cti_ref[gi]
        pw_re, pw_im = _power_by_bits(lag, col_re, col_im, (p, lc))
        w_re, w_im = _cmul(pw_re, pw_im, ctr, cti)
        pn_re, pn_im = _cmul(pw_re, pw_im, col_re[0], col_im[0])
        e_re, e_im = _cmul(pn_re, pn_im, ctr, cti)
        bb_re, bb_im = _cmul(frr_ref[gi], fri_ref[gi], btr_ref[gi], bti_ref[gi])
        r0 = jnp.dot(jnp.concatenate([bb_re, bb_im], axis=1), jnp.concatenate([w_re, -w_im], axis=0),
                     preferred_element_type=F32, precision=lax.Precision.HIGHEST)
        r0 = r0 + jnp.where(sub == lane, dv_ref[gi], 0.0)
        rows = [r0] + [jnp.where(lane >= s * SSM_GROUP, pltpu.roll(r0, s * SSM_GROUP, 1), 0.0) for s in range(1, SSM_CHUNK)]
        me_ref[gi] = jnp.concatenate(rows + [e_re, -e_im], axis=0).astype(me_ref.dtype)
        q_re, q_im = _power_by_bits(tail, row_re, row_im, (lc, p))
        f_re, f_im = _cmul(q_re, q_im, jnp.tile(bb_re, (SSM_CHUNK, 1)), jnp.tile(bb_im, (SSM_CHUNK, 1)))
        f_ref[gi] = jnp.concatenate([f_re, f_im], axis=1).astype(f_ref.dtype)
        d_re, d_im = prr_ref[SSM_POW_BITS, gi], pri_ref[SSM_POW_BITS, gi]
        a1_ref[gi] = jnp.concatenate([d_re, d_re], axis=1)
        a2_ref[gi] = jnp.concatenate([-d_im, d_im], axis=1)
        a2s_ref[gi] = jnp.concatenate([d_im, -d_im], axis=1)


def _ssm_params(lambda_re, lambda_im, log_dt, b_re, b_im, c_re, c_im, d_skip):
    ng, p = lambda_re.shape
    lc = SSM_CHUNK * SSM_GROUP
    nb = SSM_POW_BITS + 1
    full = pl.BlockSpec((ng, p), lambda: (0, 0))
    pows = pl.BlockSpec((nb, ng, p), lambda: (0, 0, 0))
    pre, pim, fre, fim = pl.pallas_call(
        _ssm_disc_kernel,
        in_specs=[full, full, pl.BlockSpec((ng, 1), lambda: (0, 0))],
        out_specs=[pows, pows, full, full],
        out_shape=[jax.ShapeDtypeStruct((nb, ng, p), F32)] * 2 + [jax.ShapeDtypeStruct((ng, p), F32)] * 2,
        name="ssm_disc",
    )(lambda_re, lambda_im, log_dt.reshape(ng, 1))
    gb = _pick(ng, (8, 4, 2, 1))
    bt_re = jnp.swapaxes(b_re, 1, 2)
    bt_im = jnp.swapaxes(b_im, 1, 2)
    ct_re = jnp.tile(jnp.swapaxes(c_re, 1, 2), (1, 1, SSM_CHUNK))
    ct_im = jnp.tile(jnp.swapaxes(c_im, 1, 2), (1, 1, SSM_CHUNK))
    dvec = jnp.pad(d_skip, ((0, 0), (0, lc - SSM_GROUP))).reshape(ng, 1, lc)
    col = pl.BlockSpec((nb, gb, p, 1), lambda i: (0, i, 0, 0))
    row = pl.BlockSpec((nb, gb, 1, p), lambda i: (0, i, 0, 0))
    frow = pl.BlockSpec((gb, 1, p), lambda i: (i, 0, 0))
    bt = pl.BlockSpec((gb, SSM_GROUP, p), lambda i: (i, 0, 0))
    ct = pl.BlockSpec((gb, p, lc), lambda i: (i, 0, 0))
    dec = pl.BlockSpec((gb, 1, 2 * p), lambda i: (i, 0, 0))
    blocks = ([((nb, gb, p, LANES), F32)] * 2 + [((nb, gb, 8, LANES), F32)] * 2 + [((gb, SSM_GROUP, LANES), F32)] * 2
              + [((gb, p, lc), F32)] * 2 + [((gb, lc + 2 * p, lc), BF16), ((gb, lc, 2 * p), BF16)])
    me, f, a1, a2, a2s = pl.pallas_call(
        functools.partial(_ssm_build_kernel, groups=gb, p=p),
        grid=(ng // gb,),
        in_specs=[col, col, row, row, frow, frow, bt, bt, ct, ct, pl.BlockSpec((gb, 1, lc), lambda i: (i, 0, 0))],
        out_specs=[pl.BlockSpec((gb, lc + 2 * p, lc), lambda i: (i, 0, 0)), pl.BlockSpec((gb, lc, 2 * p), lambda i: (i, 0, 0)), dec, dec, dec],
        out_shape=[jax.ShapeDtypeStruct((ng, lc + 2 * p, lc), BF16), jax.ShapeDtypeStruct((ng, lc, 2 * p), BF16)]
        + [jax.ShapeDtypeStruct((ng, 1, 2 * p), F32)] * 3,
        compiler_params=_params(("arbitrary",), _vmem_estimate(blocks, [((lc, lc), F32)] * 8)),
        name="ssm_build",
    )(pre.reshape(nb, ng, p, 1), pim.reshape(nb, ng, p, 1), pre.reshape(nb, ng, 1, p), pim.reshape(nb, ng, 1, p),
      fre.reshape(ng, 1, p), fim.reshape(ng, 1, p), bt_re, bt_im, ct_re, ct_im, dvec)
    flat = lambda a: a.reshape(1, ng * 2 * p)
    return me, f, flat(a1), flat(a2), flat(a2s)


def _ssm_state_in_kernel(u_ref, f_ref, s_ref, *, groups, p):
    lc = SSM_CHUNK * SSM_GROUP
    for gi in range(groups):
        s_ref[:, gi * 2 * p:(gi + 1) * 2 * p] = jnp.dot(u_ref[:, gi * lc:(gi + 1) * lc], f_ref[gi], preferred_element_type=F32)


def _ssm_out_kernel(u_ref, h_ref, me_ref, y_ref, *, groups, p):
    lc = SSM_CHUNK * SSM_GROUP
    for gi in range(groups):
        lhs = jnp.concatenate([u_ref[:, gi * lc:(gi + 1) * lc], h_ref[:, gi * 2 * p:(gi + 1) * 2 * p].astype(BF16)], axis=1)
        y_ref[:, gi * lc:(gi + 1) * lc] = jnp.dot(lhs, me_ref[gi], preferred_element_type=F32).astype(y_ref.dtype)


def _ssm_scan_kernel(s_ref, h0_ref, a1_ref, a2_ref, a2s_ref, h_ref, fin_ref, ss_ref, *, batch, kp, dec_batch, ks, p):
    def swap_halves(x):
        lane = lax.broadcasted_iota(jnp.int32, x.shape, 1)
        return jnp.where(lane % (2 * p) < p, pltpu.roll(x, x.shape[1] - p, 1), pltpu.roll(x, p, 1))

    ss_ref[...] = swap_halves(s_ref[...])
    a1, a2, a2s = a1_ref[...], a2_ref[...], a2s_ref[...]
    zero = jnp.zeros_like(a1)

    def step(k, h, hs):
        h_ref[pl.ds(k, 1), :] = h
        s = s_ref[pl.ds(k, 1), :]
        ss = ss_ref[pl.ds(k, 1), :]
        return a1 * h + a2 * hs + s, a1 * hs + a2s * h + ss

    def body(k, carry):
        out = []
        for b in range(batch):
            out.extend(step(b * kp + k, carry[2 * b], carry[2 * b + 1]))
        return tuple(out)

    fin = lax.fori_loop(0, kp, body, (zero,) * (2 * batch))
    for b in range(batch):
        fin_ref[b:b + 1, :] = fin[2 * b]
    h0s_all = swap_halves(h0_ref[...])
    for b in range(dec_batch):
        h, hs = h0_ref[b:b + 1, :], h0s_all[b:b + 1, :]
        for k in range(ks):
            h, hs = step(batch * kp + b * ks + k, h, hs)
        fin_ref[batch + b:batch + b + 1, :] = h


def _ssm(u2, me, f, a1, a2, a2s, h0, batch, kp, dec_batch, ks):
    nk, width = u2.shape
    ng, lc, p2 = f.shape
    p = p2 // 2
    gb = _pick(ng, (8, 4, 2, 1))
    u_spec = pl.BlockSpec((nk, gb * lc), lambda i: (0, i))
    st_spec = pl.BlockSpec((nk, gb * p2), lambda i: (0, i))
    s = pl.pallas_call(
        functools.partial(_ssm_state_in_kernel, groups=gb, p=p),
        grid=(ng // gb,),
        in_specs=[u_spec, pl.BlockSpec((gb, lc, p2), lambda i: (i, 0, 0))],
        out_specs=st_spec,
        out_shape=jax.ShapeDtypeStruct((nk, ng * p2), F32),
        compiler_params=_params(("arbitrary",), _vmem_estimate([((nk, gb * lc), BF16), ((gb, lc, p2), BF16), ((nk, gb * p2), F32)])),
        name="ssm_state_in",
    )(u2, f)
    lb = _pick(ng * p2, (1024, 512, 256, 128))
    nseq = batch + dec_batch
    rowb = pl.BlockSpec((1, lb), lambda i: (0, i))
    h, fin = pl.pallas_call(
        functools.partial(_ssm_scan_kernel, batch=batch, kp=kp, dec_batch=dec_batch, ks=ks, p=p),
        grid=(ng * p2 // lb,),
        in_specs=[pl.BlockSpec((nk, lb), lambda i: (0, i)), pl.BlockSpec((dec_batch, lb), lambda i: (0, i)), rowb, rowb, rowb],
        out_specs=[pl.BlockSpec((nk, lb), lambda i: (0, i)), pl.BlockSpec((nseq, lb), lambda i: (0, i))],
        out_shape=[jax.ShapeDtypeStruct((nk, ng * p2), F32), jax.ShapeDtypeStruct((nseq, ng * p2), F32)],
        scratch_shapes=[pltpu.VMEM((nk, lb), F32)],
        compiler_params=_params(("arbitrary",), _vmem_estimate([((nk, lb), F32)] * 2, [((nk, lb), F32)])),
        name="ssm_scan",
    )(s, h0, a1, a2, a2s)
    y2 = pl.pallas_call(
        functools.partial(_ssm_out_kernel, groups=gb, p=p),
        grid=(ng // gb,),
        in_specs=[u_spec, st_spec, pl.BlockSpec((gb, lc + p2, lc), lambda i: (i, 0, 0))],
        out_specs=u_spec,
        out_shape=jax.ShapeDtypeStruct((nk, width), BF16),
        compiler_params=_params(("arbitrary",), _vmem_estimate([((nk, gb * lc), BF16)] * 2 + [((nk, gb * p2), F32), ((gb, lc + p2, lc), BF16)])),
        name="ssm_out",
    )(u2, h, me)
    return y2, fin


def _glu_kernel(y_ref, wa_ref, wg_ref, ba_ref, bg_ref, zs_ref, o_ref, wab_ref, wgb_ref):
    _cast_weight(wa_ref, wab_ref)
    _cast_weight(wg_ref, wgb_ref)
    y = y_ref[...]
    a = jnp.dot(y, wab_ref[...], preferred_element_type=F32) + ba_ref[...]
    g = jnp.dot(y, wgb_ref[...], preferred_element_type=F32) + bg_ref[...]
    o_ref[...] = (a * jax.nn.sigmoid(g) * zs_ref[...].astype(F32)).astype(o_ref.dtype)


def _glu(y, w_glu, b_glu, zs, tm, tn):
    m, sw = y.shape
    nb = sw // tn
    blocks = [((tm, sw), BF16), ((sw, tn), F32), ((sw, tn), F32), ((tm, tn), BF16), ((tm, tn), BF16)]
    return pl.pallas_call(
        _glu_kernel,
        grid=(nb, m // tm),
        in_specs=[
            pl.BlockSpec((tm, sw), lambda j, i: (i, 0)),
            pl.BlockSpec((sw, tn), lambda j, i: (0, j)),
            pl.BlockSpec((sw, tn), lambda j, i: (0, nb + j)),
            pl.BlockSpec((1, tn), lambda j, i: (0, j)),
            pl.BlockSpec((1, tn), lambda j, i: (0, nb + j)),
            pl.BlockSpec((tm, tn), lambda j, i: (i, j)),
        ],
        out_specs=pl.BlockSpec((tm, tn), lambda j, i: (i, j)),
        out_shape=jax.ShapeDtypeStruct((m, sw), BF16),
        scratch_shapes=[pltpu.VMEM((sw, tn), BF16)] * 2,
        compiler_params=_params(("arbitrary", "arbitrary"), _vmem_estimate(blocks, [((tm, tn), F32)] * 4 + [((sw, tn), BF16)] * 2)),
        name="glu",
    )(y, w_glu, w_glu, b_glu, b_glu, zs)


def _merge_kernel(attn_ref, za_ref, xs_ref, wpa_ref, wps_ref, ga_ref, gs_ref, o_ref, wpab_ref, wpsb_ref):
    _cast_weight(wpa_ref, wpab_ref)
    _cast_weight(wps_ref, wpsb_ref)
    xa = (attn_ref[...].astype(F32) * za_ref[...].astype(F32)).astype(BF16)
    br_a = jnp.dot(xa, wpab_ref[...], preferred_element_type=F32)
    br_s = jnp.dot(xs_ref[...], wpsb_ref[...], preferred_element_type=F32)
    o_ref[...] = (ga_ref[...].astype(F32) * br_a + gs_ref[...].astype(F32) * br_s).astype(o_ref.dtype)


def _merge(attn, za, xs, w_pa, w_ps, gate, tm, tn):
    m, aw = attn.shape
    sw = xs.shape[1]
    d = w_pa.shape[1]
    nb = d // tn
    blocks = [((tm, aw), BF16)] * 2 + [((tm, sw), BF16), ((aw, tn), F32), ((sw, tn), F32)] + [((tm, tn), BF16)] * 3
    return pl.pallas_call(
        _merge_kernel,
        grid=(nb, m // tm),
        in_specs=[
            pl.BlockSpec((tm, aw), lambda j, i: (i, 0)),
            pl.BlockSpec((tm, aw), lambda j, i: (i, 0)),
            pl.BlockSpec((tm, sw), lambda j, i: (i, 0)),
            pl.BlockSpec((aw, tn), lambda j, i: (0, j)),
            pl.BlockSpec((sw, tn), lambda j, i: (0, j)),
            pl.BlockSpec((tm, tn), lambda j, i: (i, j)),
            pl.BlockSpec((tm, tn), lambda j, i: (i, nb + j)),
        ],
        out_specs=pl.BlockSpec((tm, tn), lambda j, i: (i, j)),
        out_shape=jax.ShapeDtypeStruct((m, d), BF16),
        scratch_shapes=[pltpu.VMEM((aw, tn), BF16), pltpu.VMEM((sw, tn), BF16)],
        compiler_params=_params(("arbitrary", "arbitrary"),
                                _vmem_estimate(blocks, [((tm, tn), F32)] * 4 + [((tm, aw), F32), ((aw, tn), BF16), ((sw, tn), BF16)])),
        name="merge",
    )(attn, za, xs, w_pa, w_ps, gate, gate)


def _out_kernel(mg_ref, w_ref, x_ref, g_ref, o_ref, acc_ref, *, n_col_blocks, tn):
    j = pl.program_id(1)
    acc_ref[j] = x_ref[...] + jnp.dot(mg_ref[...], w_ref[...], preferred_element_type=F32)

    @pl.when(j == n_col_blocks - 1)
    def _():
        ssq = acc_ref[0] * acc_ref[0]
        tot = jnp.sum(ssq, axis=-1, keepdims=True)
        for c in range(1, n_col_blocks):
            blk = acc_ref[c]
            tot = tot + jnp.sum(blk * blk, axis=-1, keepdims=True)
        scale = lax.rsqrt(tot / (n_col_blocks * tn) + NORM_EPS)
        for c in range(n_col_blocks):
            o_ref[:, c * tn:(c + 1) * tn] = acc_ref[c] * scale * g_ref[:, c * tn:(c + 1) * tn]


def _out(merged, row_block0, w_out, x, final_g, tm, tn):
    mx, d = x.shape
    nb = d // tn
    blocks = [((tm, d), BF16), ((d, tn), BF16), ((tm, tn), F32), ((tm, d), F32)]
    return pl.pallas_call(
        functools.partial(_out_kernel, n_col_blocks=nb, tn=tn),
        grid=(mx // tm, nb),
        in_specs=[
            pl.BlockSpec((tm, d), lambda i, j: (row_block0 + i, 0)),
            pl.BlockSpec((d, tn), lambda i, j: (0, j)),
            pl.BlockSpec((tm, tn), lambda i, j: (i, j)),
            pl.BlockSpec((1, d), lambda i, j: (0, 0)),
        ],
        out_specs=pl.BlockSpec((tm, d), lambda i, j: (i, 0)),
        out_shape=jax.ShapeDtypeStruct((mx, d), F32),
        scratch_shapes=[pltpu.VMEM((nb, tm, tn), F32)],
        compiler_params=_params(("arbitrary", "arbitrary"), _vmem_estimate(blocks, [((tm, d), F32), ((tm, tn), F32)])),
        name="out_norm",
    )(merged, w_out, x, final_g.reshape(1, d))


def _rope_tables(positions):
    half = HEAD_DIM // 2
    inv_freq = ROPE_THETA ** (-jnp.arange(half, dtype=F32) / half)
    ang = positions.astype(F32)[:, None] * inv_freq[None, :]
    cos, sin = jnp.cos(ang), jnp.sin(ang)
    reps = LANES // HEAD_DIM
    return jnp.tile(jnp.concatenate([cos, cos], axis=1), (1, reps)), jnp.tile(jnp.concatenate([-sin, sin], axis=1), (1, reps))


def kernel(x_prompt, x_sample, cache_k, cache_v, state_ssm_re, state_ssm_im, norm_g, w_in, sink, lambda_re, lambda_im,
           log_dt, b_re, b_im, c_re, c_im, d_skip, w_glu, b_glu, w_pa, w_ps, w_out, final_g):
    depth = norm_g.shape[0]
    assert depth == 1, "one trunk layer"
    batch, seq, d = x_prompt.shape
    dec_batch, dec_seq, _ = x_sample.shape
    aw = w_pa.shape[1]
    sw = w_ps.shape[1]
    nh = aw // HEAD_DIM
    g = max(1, nh // GQA_GROUPING)
    rep = nh // g
    kvw = g * HEAD_DIM
    ng, p = lambda_re.shape[1:]
    assert dec_seq == CHUNK and cache_k.shape[2] == WINDOW and seq % (2 * CHUNK) == 0
    assert kvw % LANES == 0 and rep % 2 == 0 and sw == ng * SSM_GROUP and (1 << SSM_POW_BITS) == SSM_CHUNK
    mp, ms = batch * seq, dec_batch * dec_seq
    m = mp + ms
    widths = (aw, 2 * kvw, aw, sw, sw, 2 * d)
    assert sum(widths) == w_in.shape[2]
    c_q, c_kv, c_za, c_u, c_zs, c_gate = (sum(widths[:n]) for n in range(len(widths)))
    tm = _pick(math.gcd(mp, ms), (512, 256, 128))
    tmp = max(t for t in range(16, 1153, 16) if m % t == 0)
    tn = _pick(math.gcd(c_kv, c_za, c_u, c_zs, c_gate, 2 * d), (512, 256, 128))

    xp = x_prompt.reshape(mp, d)
    xs = x_sample.reshape(ms, d)
    w_in2 = w_in.reshape(d, w_in.shape[2])
    positions = jnp.concatenate([jnp.tile(jnp.arange(seq, dtype=jnp.int32), batch),
                                 jnp.tile(PAST_LEN + jnp.arange(dec_seq, dtype=jnp.int32), dec_batch)])
    cos, sin = _rope_tables(positions)

    h = _rmsnorm(xp, xs, norm_g[0], tm)
    q_hm = _proj_q(h, w_in2, cos, sin, aw, tmp, tn)
    k_f, v_f, k_hm, v_hm = _proj_kv(h, w_in2, cos, sin, c_kv, kvw, tmp)
    za = _proj_act(h, w_in2, c_za, aw, "silu", tmp, tn, "proj_za")
    u = _proj_act(h, w_in2, c_u, sw, "none", tmp, tn, "proj_u")
    zs = _proj_act(h, w_in2, c_zs, sw, "silu", tmp, tn, "proj_zs")
    gate = _proj_act(h, w_in2, c_gate, 2 * d, "sigmoid", tmp, tn, "proj_gate")

    to_heads = lambda c: jnp.transpose(c[0], (2, 0, 1, 3)).astype(BF16)
    new_rows = lambda a: a[:, mp:].reshape(g, dec_batch, dec_seq, HEAD_DIM)
    ks = jnp.concatenate([to_heads(cache_k), new_rows(k_hm)], axis=2).reshape(g, dec_batch * (WINDOW + CHUNK), HEAD_DIM)
    vs = jnp.concatenate([to_heads(cache_v), new_rows(v_hm)], axis=2).reshape(g, dec_batch * (WINDOW + CHUNK), HEAD_DIM)
    sink_rows = jnp.repeat(sink[0].reshape(g, rep), CHUNK, axis=1).reshape(g, 1, rep * CHUNK)
    attn = _attention(q_hm, k_hm, v_hm, ks, vs, sink_rows, batch, seq, dec_batch, aw)

    me, f, a1, a2, a2s = _ssm_params(lambda_re[0], lambda_im[0], log_dt[0], b_re[0], b_im[0], c_re[0], c_im[0], d_skip[0])
    nk = m // SSM_CHUNK
    u2 = u.reshape(nk, SSM_CHUNK, ng, SSM_GROUP).transpose(0, 2, 1, 3).reshape(nk, ng * SSM_CHUNK * SSM_GROUP)
    h0 = jnp.concatenate([state_ssm_re[0], state_ssm_im[0]], axis=-1).reshape(dec_batch, ng * 2 * p)
    y2, fin = _ssm(u2, me, f, a1, a2, a2s, h0, batch, seq // SSM_CHUNK, dec_batch, dec_seq // SSM_CHUNK)
    y = y2.reshape(nk, ng, SSM_CHUNK, SSM_GROUP).transpose(0, 2, 1, 3).reshape(m, sw)

    x_ssm = _glu(y, w_glu.reshape(sw, 2 * sw), b_glu.reshape(1, 2 * sw), zs, tm, tn)
    merged = _merge(attn, za, x_ssm, w_pa.reshape(aw, d), w_ps.reshape(sw, d), gate, tm, tn)
    w_out_b = w_out[0].astype(BF16)
    tno = _pick(d, (512, 256))
    y_prompt = _out(merged, 0, w_out_b, xp, final_g, tm, tno).reshape(batch, seq, d)
    y_sample = _out(merged, mp // tm, w_out_b, xs, final_g, tm, tno).reshape(dec_batch, dec_seq, d)

    keep = min(WINDOW, seq)
    last_rows = lambda a: a[:mp].reshape(batch, seq, g, HEAD_DIM)[:, seq - keep:][None]
    dec_rows = lambda a: a[mp:].reshape(dec_batch, dec_seq, g, HEAD_DIM)[None]
    fin = fin.reshape(batch + dec_batch, ng, 2, p)
    return (y_prompt, y_sample, last_rows(k_f), last_rows(v_f), fin[:batch, :, 0][None], fin[:batch, :, 1][None],
            dec_rows(k_f), dec_rows(v_f), fin[batch:, :, 0][None], fin[batch:, :, 1][None])
```

```python
import functools
import math

import jax
import jax.numpy as jnp
from jax import lax
from jax.experimental import pallas as pl
from jax.experimental.pallas import tpu as pltpu

CHUNK = 64
WINDOW = 128
HEAD_DIM = 64
GQA_GROUPING = 8
SSM_GROUP = 16
PAST_LEN = 1024
ROPE_THETA = 10000.0
NORM_EPS = 1e-5
LAMBDA_RE_MAX = -1e-4

SSM_CHUNK = 16
SSM_POW_BITS = 4
LANES = 128
V7X_VMEM_BYTES = 64 * 1024 * 1024
BF16 = jnp.bfloat16
F32 = jnp.float32


def _pick(n, prefs):
    for p in prefs:
        if n % p == 0:
            return p
    raise ValueError(f"no tile in {prefs} divides {n}")


def _params(sem, vmem_bytes):
    limit = min(int(vmem_bytes), V7X_VMEM_BYTES - 4 * 1024 * 1024)
    return pltpu.CompilerParams(dimension_semantics=sem, vmem_limit_bytes=limit)


def _nbytes(shape, dtype):
    return math.prod(shape) * jnp.dtype(dtype).itemsize


def _vmem_estimate(blocks, temps=()):
    return 2 * (2 * sum(_nbytes(s, d) for s, d in blocks) + sum(_nbytes(s, d) for s, d in temps))


def _rmsnorm_kernel(xp_ref, xs_ref, g_ref, o_ref, *, n_prompt_blocks):
    i = pl.program_id(0)

    def norm(x_ref):
        x = x_ref[...]
        y = x * lax.rsqrt(jnp.mean(x * x, axis=-1, keepdims=True) + NORM_EPS)
        o_ref[...] = (y * g_ref[...]).astype(o_ref.dtype)

    @pl.when(i < n_prompt_blocks)
    def _():
        norm(xp_ref)

    @pl.when(i >= n_prompt_blocks)
    def _():
        norm(xs_ref)


def _rmsnorm(xp, xs, g, tm):
    mp, d = xp.shape
    ms = xs.shape[0]
    npb, nsb = mp // tm, ms // tm
    return pl.pallas_call(
        functools.partial(_rmsnorm_kernel, n_prompt_blocks=npb),
        grid=(npb + nsb,),
        in_specs=[
            pl.BlockSpec((tm, d), lambda i: (jnp.minimum(i, npb - 1), 0)),
            pl.BlockSpec((tm, d), lambda i: (jnp.maximum(i - npb, 0), 0)),
            pl.BlockSpec((1, d), lambda i: (0, 0)),
        ],
        out_specs=pl.BlockSpec((tm, d), lambda i: (i, 0)),
        out_shape=jax.ShapeDtypeStruct((mp + ms, d), BF16),
        compiler_params=_params(("arbitrary",), _vmem_estimate([((tm, d), F32)] * 2 + [((tm, d), BF16)], [((tm, d), F32)])),
        name="rmsnorm_in",
    )(xp, xs, g.reshape(1, d))


def _rope(acc, cos_ref, sin_ref):
    tm, tn = acc.shape
    reps = tn // LANES
    cos = jnp.tile(cos_ref[...], (1, reps))
    sin = jnp.tile(sin_ref[...], (1, reps))
    lane = lax.broadcasted_iota(jnp.int32, (tm, tn), 1)
    low = (lane % HEAD_DIM) < (HEAD_DIM // 2)
    partner = jnp.where(low, pltpu.roll(acc, tn - HEAD_DIM // 2, 1), pltpu.roll(acc, HEAD_DIM // 2, 1))
    return acc * cos + partner * sin


def _store_heads(o_ref, val):
    for h in range(val.shape[1] // HEAD_DIM):
        o_ref[h] = val[:, h * HEAD_DIM:(h + 1) * HEAD_DIM].astype(o_ref.dtype)


def _cast_weight(w_ref, wb_ref):
    @pl.when(pl.program_id(1) == 0)
    def _():
        wb_ref[...] = w_ref[...].astype(wb_ref.dtype)


def _proj_q_kernel(h_ref, w_ref, cos_ref, sin_ref, q_ref, wb_ref):
    _cast_weight(w_ref, wb_ref)
    acc = jnp.dot(h_ref[...], wb_ref[...], preferred_element_type=F32)
    _store_heads(q_ref, _rope(acc, cos_ref, sin_ref) * (HEAD_DIM ** -0.5))


def _proj_kv_kernel(h_ref, w_ref, cos_ref, sin_ref, kf_ref, vf_ref, kh_ref, vh_ref, wb_ref, *, kvw):
    _cast_weight(w_ref, wb_ref)
    acc = jnp.dot(h_ref[...], wb_ref[...], preferred_element_type=F32)
    k = _rope(acc[:, :kvw], cos_ref, sin_ref)
    v = acc[:, kvw:]
    kf_ref[...] = k
    vf_ref[...] = v
    _store_heads(kh_ref, k)
    _store_heads(vh_ref, v)


def _proj_act_kernel(h_ref, w_ref, o_ref, wb_ref, *, act):
    _cast_weight(w_ref, wb_ref)
    acc = jnp.dot(h_ref[...], wb_ref[...], preferred_element_type=F32)
    if act == "silu":
        acc = acc * jax.nn.sigmoid(acc)
    elif act == "sigmoid":
        acc = jax.nn.sigmoid(acc)
    o_ref[...] = acc.astype(o_ref.dtype)


def _proj_u_kernel(h_ref, w_ref, o_ref, wb_ref, *, n_real):
    _cast_weight(w_ref, wb_ref)
    i = pl.program_id(1)

    @pl.when(i < n_real)
    def _():
        o_ref[...] = jnp.dot(h_ref[...], wb_ref[...], preferred_element_type=F32)

    @pl.when(i >= n_real)
    def _():
        o_ref[...] = jnp.zeros(o_ref.shape, o_ref.dtype)


def _proj_specs(d, tm, tn, col0):
    assert col0 % tn == 0
    cb0 = col0 // tn
    return [pl.BlockSpec((tm, d), lambda j, i: (i, 0)), pl.BlockSpec((d, tn), lambda j, i: (0, cb0 + j))]


def _proj_vmem(d, tm, tn, outs):
    return _vmem_estimate([((tm, d), BF16), ((d, tn), F32)] + outs, [((tm, tn), F32)] * 3 + [((d, tn), BF16)])


def _proj_q(h, w, cos, sin, aw, tm, tn):
    m, d = h.shape
    nh = aw // HEAD_DIM
    tab = pl.BlockSpec((tm, LANES), lambda j, i: (i, 0))
    return pl.pallas_call(
        _proj_q_kernel,
        grid=(aw // tn, m // tm),
        in_specs=_proj_specs(d, tm, tn, 0) + [tab, tab],
        out_specs=pl.BlockSpec((tn // HEAD_DIM, tm, HEAD_DIM), lambda j, i: (j, i, 0)),
        out_shape=jax.ShapeDtypeStruct((nh, m, HEAD_DIM), BF16),
        scratch_shapes=[pltpu.VMEM((d, tn), BF16)],
        compiler_params=_params(("arbitrary", "arbitrary"), _proj_vmem(d, tm, tn, [((tm, 2 * tn), BF16), ((tm, 2 * LANES), F32)])),
        name="proj_q",
    )(h, w, cos, sin)


def _proj_kv(h, w, cos, sin, col0, kvw, tm):
    m, d = h.shape
    g = kvw // HEAD_DIM
    tn = 2 * kvw
    tab = pl.BlockSpec((tm, LANES), lambda j, i: (i, 0))
    flat = pl.BlockSpec((tm, kvw), lambda j, i: (i, 0))
    heads = pl.BlockSpec((g, tm, HEAD_DIM), lambda j, i: (0, i, 0))
    return pl.pallas_call(
        functools.partial(_proj_kv_kernel, kvw=kvw),
        grid=(1, m // tm),
        in_specs=_proj_specs(d, tm, tn, col0) + [tab, tab],
        out_specs=[flat, flat, heads, heads],
        out_shape=[jax.ShapeDtypeStruct((m, kvw), F32)] * 2 + [jax.ShapeDtypeStruct((g, m, HEAD_DIM), BF16)] * 2,
        scratch_shapes=[pltpu.VMEM((d, tn), BF16)],
        compiler_params=_params(("arbitrary", "arbitrary"), _proj_vmem(d, tm, tn, [((tm, 2 * tn), F32), ((tm, 2 * tn), BF16), ((tm, 2 * LANES), F32)])),
        name="proj_kv",
    )(h, w, cos, sin)


def _proj_act(h, w, col0, ncols, act, tm, tn, name):
    m, d = h.shape
    return pl.pallas_call(
        functools.partial(_proj_act_kernel, act=act),
        grid=(ncols // tn, m // tm),
        in_specs=_proj_specs(d, tm, tn, col0),
        out_specs=pl.BlockSpec((tm, tn), lambda j, i: (i, j)),
        out_shape=jax.ShapeDtypeStruct((m, ncols), BF16),
        scratch_shapes=[pltpu.VMEM((d, tn), BF16)],
        compiler_params=_params(("arbitrary", "arbitrary"), _proj_vmem(d, tm, tn, [((tm, tn), BF16)])),
        name=name,
    )(h, w)


def _proj_u(h, w, col0, ncols, m_pad, tm, tn):
    m, d = h.shape
    n_real = m // tm
    cb0 = col0 // tn
    return pl.pallas_call(
        functools.partial(_proj_u_kernel, n_real=n_real),
        grid=(ncols // tn, m_pad // tm),
        in_specs=[pl.BlockSpec((tm, d), lambda j, i: (jnp.minimum(i, n_real - 1), 0)), pl.BlockSpec((d, tn), lambda j, i: (0, cb0 + j))],
        out_specs=pl.BlockSpec((tm, tn), lambda j, i: (i, j)),
        out_shape=jax.ShapeDtypeStruct((m_pad, ncols), F32),
        scratch_shapes=[pltpu.VMEM((d, tn), BF16)],
        compiler_params=_params(("arbitrary", "arbitrary"), _proj_vmem(d, tm, tn, [((tm, tn), F32)])),
        name="proj_u",
    )(h, w)


def _attend_unit(q, kw, vw, sink, n_valid):
    st = lax.dot_general(kw, q, (((1,), (1,)), ((), ())), preferred_element_type=F32)
    if n_valid is not None:
        row = lax.broadcasted_iota(jnp.int32, st.shape, 0)
        st = jnp.where(row < n_valid, st, -jnp.inf)
    m = jnp.maximum(jnp.max(st, axis=0, keepdims=True), sink)
    e = jnp.exp(st - m)
    denom = jnp.sum(e, axis=0, keepdims=True) + jnp.exp(sink - m)
    ot = jnp.dot(vw.astype(F32).T.astype(BF16), e.astype(BF16), preferred_element_type=F32) / denom
    return ot.T


def _store_unit(o_ref, za_ref, row0, o, rep):
    for r in range(0, rep, 2):
        pair = jnp.concatenate([o[r * CHUNK:(r + 1) * CHUNK], o[(r + 1) * CHUNK:(r + 2) * CHUNK]], axis=1)
        gate = za_ref[row0:row0 + CHUNK, r * HEAD_DIM:(r + 2) * HEAD_DIM].astype(F32)
        o_ref[row0:row0 + CHUNK, r * HEAD_DIM:(r + 2) * HEAD_DIM] = (pair * gate).astype(o_ref.dtype)


def _attn_prompt_kernel(q_ref, k_ref, v_ref, sink_ref, za_ref, o_ref, *, n_chunks, rep):
    cb = pl.program_id(2)
    win = WINDOW + CHUNK
    sink = sink_ref[0]
    for c in range(n_chunks):
        cg = cb * n_chunks + c
        start = pl.multiple_of(jnp.maximum(cg - WINDOW // CHUNK, 0) * CHUNK, CHUNK)
        q = q_ref[:, c * CHUNK:(c + 1) * CHUNK, :].reshape(rep * CHUNK, HEAD_DIM)
        kw = k_ref[0, pl.ds(start, win), :]
        vw = v_ref[0, pl.ds(start, win), :]
        n_valid = jnp.where(cb == 0, (c + 1) * CHUNK, win) if c < WINDOW // CHUNK else None
        _store_unit(o_ref, za_ref, c * CHUNK, _attend_unit(q, kw, vw, sink, n_valid), rep)


def _attn_sample_kernel(q_ref, k_ref, v_ref, sink_ref, za_ref, prev_ref, o_ref, *, rep):
    del prev_ref
    q = q_ref[...].reshape(rep * CHUNK, HEAD_DIM)
    _store_unit(o_ref, za_ref, 0, _attend_unit(q, k_ref[0], v_ref[0], sink_ref[0], None), rep)


def _attention(q_hm, k_hm, v_hm, ks, vs, sink_rows, za, batch, seq, dec_batch, aw):
    nh, m, _ = q_hm.shape
    g = k_hm.shape[0]
    rep = nh // g
    win = WINDOW + CHUNK
    n_chunks = _pick(seq // CHUNK, (8, 4, 2))
    ncb = seq // (CHUNK * n_chunks)
    blocks = [((rep, CHUNK * n_chunks, HEAD_DIM), BF16), ((seq, LANES), BF16), ((seq, LANES), BF16), ((CHUNK * n_chunks, rep * HEAD_DIM), BF16)]
    temps = [((rep * CHUNK, 2 * LANES), F32)] * (4 * n_chunks)
    attn = pl.pallas_call(
        functools.partial(_attn_prompt_kernel, n_chunks=n_chunks, rep=rep),
        grid=(batch, g, ncb),
        in_specs=[
            pl.BlockSpec((rep, CHUNK * n_chunks, HEAD_DIM), lambda b, gi, cb: (gi, b * ncb + cb, 0)),
            pl.BlockSpec((1, seq, HEAD_DIM), lambda b, gi, cb: (gi, b, 0)),
            pl.BlockSpec((1, seq, HEAD_DIM), lambda b, gi, cb: (gi, b, 0)),
            pl.BlockSpec((1, 1, rep * CHUNK), lambda b, gi, cb: (gi, 0, 0)),
            pl.BlockSpec((CHUNK * n_chunks, rep * HEAD_DIM), lambda b, gi, cb: (b * ncb + cb, gi)),
        ],
        out_specs=pl.BlockSpec((CHUNK * n_chunks, rep * HEAD_DIM), lambda b, gi, cb: (b * ncb + cb, gi)),
        out_shape=jax.ShapeDtypeStruct((m, aw), BF16),
        compiler_params=_params(("arbitrary",) * 3, _vmem_estimate(blocks + blocks[3:], temps)),
        name="attn_prompt",
    )(q_hm, k_hm, v_hm, sink_rows, za)
    row0 = batch * seq // CHUNK
    return pl.pallas_call(
        functools.partial(_attn_sample_kernel, rep=rep),
        grid=(dec_batch, g),
        in_specs=[
            pl.BlockSpec((rep, CHUNK, HEAD_DIM), lambda b, gi: (gi, row0 + b, 0)),
            pl.BlockSpec((1, win, HEAD_DIM), lambda b, gi: (gi, b, 0)),
            pl.BlockSpec((1, win, HEAD_DIM), lambda b, gi: (gi, b, 0)),
            pl.BlockSpec((1, 1, rep * CHUNK), lambda b, gi: (gi, 0, 0)),
            pl.BlockSpec((CHUNK, rep * HEAD_DIM), lambda b, gi: (row0 + b, gi)),
            pl.BlockSpec(memory_space=pl.ANY),
        ],
        out_specs=pl.BlockSpec((CHUNK, rep * HEAD_DIM), lambda b, gi: (row0 + b, gi)),
        out_shape=jax.ShapeDtypeStruct((m, aw), BF16),
        input_output_aliases={5: 0},
        compiler_params=_params(("arbitrary",) * 2, _vmem_estimate(blocks[:1] + blocks[3:] * 2, temps[:4])),
        name="attn_sample",
    )(q_hm, ks, vs, sink_rows, za, attn)


def _ssm_disc_kernel(lre_ref, lim_ref, ldt_ref, pre_ref, pim_ref, fre_ref, fim_ref):
    lr = jnp.minimum(lre_ref[...], LAMBDA_RE_MAX)
    li = lim_ref[...]
    dt = jnp.exp(ldt_ref[...])
    mag = jnp.exp(lr * dt)
    a_re = mag * jnp.cos(li * dt)
    a_im = mag * jnp.sin(li * dt)
    den = lr * lr + li * li
    nr = a_re - 1.0
    fre_ref[...] = (nr * lr + a_im * li) / den
    fim_ref[...] = (a_im * lr - nr * li) / den
    pre_ref[0] = a_re
    pim_ref[0] = a_im
    for b in range(1, SSM_POW_BITS + 1):
        a_re, a_im = a_re * a_re - a_im * a_im, 2.0 * a_re * a_im
        pre_ref[b] = a_re
        pim_ref[b] = a_im


def _cmul(ar, ai, br, bi):
    return ar * br - ai * bi, ar * bi + ai * br


def _power_by_bits(exponent, base_re, base_im, shape):
    p_re = p_im = None
    for b in range(SSM_POW_BITS):
        on = jnp.broadcast_to(((exponent >> b) & 1) == 1, shape)
        f_re = jnp.where(on, jnp.broadcast_to(base_re[b], shape), 1.0)
        f_im = jnp.where(on, jnp.broadcast_to(base_im[b], shape), 0.0)
        if p_re is None:
            p_re, p_im = f_re, f_im
        else:
            p_re, p_im = _cmul(p_re, p_im, f_re, f_im)
    return p_re, p_im


def _ssm_build_kernel(pcr_ref, pci_ref, prr_ref, pri_ref, frr_ref, fri_ref, btr_ref, bti_ref, ctr_ref, cti_ref, dv_ref,
                      me_ref, f_ref, a1_ref, a2_ref, a2s_ref, *, groups, p):
    lc = SSM_CHUNK * SSM_GROUP
    lag = lax.broadcasted_iota(jnp.int32, (1, lc), 1) // SSM_GROUP
    tail = (SSM_CHUNK - 1) - lax.broadcasted_iota(jnp.int32, (lc, 1), 0) // SSM_GROUP
    sub = lax.broadcasted_iota(jnp.int32, (SSM_GROUP, lc), 0)
    lane = lax.broadcasted_iota(jnp.int32, (SSM_GROUP, lc), 1)
    for gi in range(groups):
        col_re = [pcr_ref[b, gi] for b in range(SSM_POW_BITS)]
        col_im = [pci_ref[b, gi] for b in range(SSM_POW_BITS)]
        row_re = [prr_ref[b, gi] for b in range(SSM_POW_BITS)]
        row_im = [pri_ref[b, gi] for b in range(SSM_POW_BITS)]
        ctr, cti = ctr_ref[gi], cti_ref[gi]
        pw_re, pw_im = _power_by_bits(lag, col_re, col_im, (p, lc))
        w_re, w_im = _cmul(pw_re, pw_im, ctr, cti)
        pn_re, pn_im = _cmul(pw_re, pw_im, col_re[0], col_im[0])
        e_re, e_im = _cmul(pn_re, pn_im, ctr, cti)
        bb_re, bb_im = _cmul(frr_ref[gi], fri_ref[gi], btr_ref[gi], bti_ref[gi])
        r0 = jnp.dot(jnp.concatenate([bb_re, bb_im], axis=1), jnp.concatenate([w_re, -w_im], axis=0),
                     preferred_element_type=F32, precision=lax.Precision.HIGHEST)
        r0 = r0 + jnp.where(sub == lane, dv_ref[gi], 0.0)
        rows = [r0] + [jnp.where(lane >= s * SSM_GROUP, pltpu.roll(r0, s * SSM_GROUP, 1), 0.0) for s in range(1, SSM_CHUNK)]
        me_ref[gi] = jnp.concatenate(rows + [e_re, -e_im], axis=0).T.astype(me_ref.dtype)
        q_re, q_im = _power_by_bits(tail, row_re, row_im, (lc, p))
        f_re, f_im = _cmul(q_re, q_im, jnp.tile(bb_re, (SSM_CHUNK, 1)), jnp.tile(bb_im, (SSM_CHUNK, 1)))
        f_ref[gi] = jnp.concatenate([f_re, f_im], axis=1).T.astype(f_ref.dtype)
        d_re, d_im = prr_ref[SSM_POW_BITS, gi], pri_ref[SSM_POW_BITS, gi]
        a1_ref[gi] = jnp.concatenate([d_re, d_re], axis=1)
        a2_ref[gi] = jnp.concatenate([-d_im, d_im], axis=1)
        a2s_ref[gi] = jnp.concatenate([d_im, -d_im], axis=1)


def _ssm_params(lambda_re, lambda_im, log_dt, b_re, b_im, c_re, c_im, d_skip):
    ng, p = lambda_re.shape
    lc = SSM_CHUNK * SSM_GROUP
    nb = SSM_POW_BITS + 1
    full = pl.BlockSpec((ng, p), lambda: (0, 0))
    pows = pl.BlockSpec((nb, ng, p), lambda: (0, 0, 0))
    pre, pim, fre, fim = pl.pallas_call(
        _ssm_disc_kernel,
        in_specs=[full, full, pl.BlockSpec((ng, 1), lambda: (0, 0))],
        out_specs=[pows, pows, full, full],
        out_shape=[jax.ShapeDtypeStruct((nb, ng, p), F32)] * 2 + [jax.ShapeDtypeStruct((ng, p), F32)] * 2,
        name="ssm_disc",
    )(lambda_re, lambda_im, log_dt.reshape(ng, 1))
    gb = _pick(ng, (8, 4, 2, 1))
    bt_re = jnp.swapaxes(b_re, 1, 2)
    bt_im = jnp.swapaxes(b_im, 1, 2)
    ct_re = jnp.tile(jnp.swapaxes(c_re, 1, 2), (1, 1, SSM_CHUNK))
    ct_im = jnp.tile(jnp.swapaxes(c_im, 1, 2), (1, 1, SSM_CHUNK))
    dvec = jnp.pad(d_skip, ((0, 0), (0, lc - SSM_GROUP))).reshape(ng, 1, lc)
    col = pl.BlockSpec((nb, gb, p, 1), lambda i: (0, i, 0, 0))
    row = pl.BlockSpec((nb, gb, 1, p), lambda i: (0, i, 0, 0))
    frow = pl.BlockSpec((gb, 1, p), lambda i: (i, 0, 0))
    bt = pl.BlockSpec((gb, SSM_GROUP, p), lambda i: (i, 0, 0))
    ct = pl.BlockSpec((gb, p, lc), lambda i: (i, 0, 0))
    dec = pl.BlockSpec((gb, 1, 2 * p), lambda i: (i, 0, 0))
    blocks = ([((nb, gb, p, LANES), F32)] * 2 + [((nb, gb, 8, LANES), F32)] * 2 + [((gb, SSM_GROUP, LANES), F32)] * 2
              + [((gb, p, lc), F32)] * 2 + [((gb, lc + 2 * p, lc), BF16), ((gb, lc, 2 * p), BF16)])
    me, f, a1, a2, a2s = pl.pallas_call(
        functools.partial(_ssm_build_kernel, groups=gb, p=p),
        grid=(ng // gb,),
        in_specs=[col, col, row, row, frow, frow, bt, bt, ct, ct, pl.BlockSpec((gb, 1, lc), lambda i: (i, 0, 0))],
        out_specs=[pl.BlockSpec((gb, lc, lc + 2 * p), lambda i: (i, 0, 0)), pl.BlockSpec((gb, 2 * p, lc), lambda i: (i, 0, 0)), dec, dec, dec],
        out_shape=[jax.ShapeDtypeStruct((ng, lc, lc + 2 * p), BF16), jax.ShapeDtypeStruct((ng, 2 * p, lc), BF16)]
        + [jax.ShapeDtypeStruct((ng, 1, 2 * p), F32)] * 3,
        compiler_params=_params(("arbitrary",), _vmem_estimate(blocks, [((lc, lc), F32)] * 8)),
        name="ssm_build",
    )(pre.reshape(nb, ng, p, 1), pim.reshape(nb, ng, p, 1), pre.reshape(nb, ng, 1, p), pim.reshape(nb, ng, 1, p),
      fre.reshape(ng, 1, p), fim.reshape(ng, 1, p), bt_re, bt_im, ct_re, ct_im, dvec)
    flat = lambda a: a.reshape(1, ng * 2 * p)
    return me, f, flat(a1), flat(a2), flat(a2s)


def _chunk_major_inputs(u_ref, n_blocks):
    rows = LANES * SSM_CHUNK
    return [[u_ref[pl.ds(kb * rows + s, LANES, stride=SSM_CHUNK), :].T.astype(BF16) for kb in range(n_blocks)]
            for s in range(SSM_CHUNK)]


def _group_operand(xt, gi, n_blocks):
    return [jnp.concatenate([xt[s][kb][gi * SSM_GROUP:(gi + 1) * SSM_GROUP, :] for kb in range(n_blocks)], axis=1)
            for s in range(SSM_CHUNK)]


def _ssm_state_in_kernel(u_ref, ft_ref, s_ref, *, n_blocks, p):
    groups = LANES // SSM_GROUP
    xt = _chunk_major_inputs(u_ref, n_blocks)
    for gi in range(groups):
        rhs = jnp.concatenate(_group_operand(xt, gi, n_blocks), axis=0)
        st = jnp.dot(ft_ref[gi], rhs, preferred_element_type=F32)
        for kb in range(n_blocks):
            s_ref[kb * LANES:(kb + 1) * LANES, gi * 2 * p:(gi + 1) * 2 * p] = st[:, kb * LANES:(kb + 1) * LANES].T


def _ssm_out_kernel(u_ref, h_ref, met_ref, y_ref, *, n_blocks, p):
    groups = LANES // SSM_GROUP
    xt = _chunk_major_inputs(u_ref, n_blocks)
    zt = [[None] * groups for _ in range(SSM_CHUNK)]
    for gi in range(groups):
        ht = jnp.concatenate([h_ref[kb * LANES:(kb + 1) * LANES, gi * 2 * p:(gi + 1) * 2 * p].T for kb in range(n_blocks)], axis=1)
        rhs = jnp.concatenate(_group_operand(xt, gi, n_blocks) + [ht.astype(BF16)], axis=0)
        yt = jnp.dot(met_ref[gi], rhs, preferred_element_type=F32)
        for t in range(SSM_CHUNK):
            zt[t][gi] = yt[t * SSM_GROUP:(t + 1) * SSM_GROUP, :]
    rows = LANES * SSM_CHUNK
    for t in range(SSM_CHUNK):
        z = jnp.concatenate(zt[t], axis=0)
        for kb in range(n_blocks):
            y_ref[pl.ds(kb * rows + t, LANES, stride=SSM_CHUNK), :] = z[:, kb * LANES:(kb + 1) * LANES].T


def _ssm_scan_kernel(s_ref, h0_ref, a1_ref, a2_ref, a2s_ref, h_ref, fin_ref, ss_ref, *, batch, kp, dec_batch, ks, p):
    def swap_halves(x):
        lane = lax.broadcasted_iota(jnp.int32, x.shape, 1)
        return jnp.where(lane % (2 * p) < p, pltpu.roll(x, x.shape[1] - p, 1), pltpu.roll(x, p, 1))

    ss_ref[...] = swap_halves(s_ref[...])
    a1, a2, a2s = a1_ref[...], a2_ref[...], a2s_ref[...]
    zero = jnp.zeros_like(a1)

    def step(k, h, hs):
        h_ref[pl.ds(k, 1), :] = h
        s = s_ref[pl.ds(k, 1), :]
        ss = ss_ref[pl.ds(k, 1), :]
        return a1 * h + a2 * hs + s, a1 * hs + a2s * h + ss

    def body(k, carry):
        out = []
        for b in range(batch):
            out.extend(step(b * kp + k, carry[2 * b], carry[2 * b + 1]))
        return tuple(out)

    fin = lax.fori_loop(0, kp, body, (zero,) * (2 * batch))
    for b in range(batch):
        fin_ref[b:b + 1, :] = fin[2 * b]
    h0s_all = swap_halves(h0_ref[...])
    for b in range(dec_batch):
        h, hs = h0_ref[b:b + 1, :], h0s_all[b:b + 1, :]
        for k in range(ks):
            h, hs = step(batch * kp + b * ks + k, h, hs)
        fin_ref[batch + b:batch + b + 1, :] = h
    n_real = batch * kp + dec_batch * ks
    if n_real < h_ref.shape[0]:
        h_ref[n_real:, :] = jnp.zeros((h_ref.shape[0] - n_real, h_ref.shape[1]), F32)


def _ssm(u, met, ft, a1, a2, a2s, h0, batch, kp, dec_batch, ks):
    m_pad, sw = u.shape
    ng, p2, lc = ft.shape
    p = p2 // 2
    nk_pad = m_pad // SSM_CHUNK
    n_blocks = nk_pad // LANES
    groups = LANES // SSM_GROUP
    u_spec = pl.BlockSpec((m_pad, LANES), lambda i: (0, i))
    st_spec = pl.BlockSpec((nk_pad, groups * p2), lambda i: (0, i))
    temps = [((lc + p2, nk_pad), BF16), ((lc, nk_pad), F32), ((m_pad, LANES), BF16)]
    s = pl.pallas_call(
        functools.partial(_ssm_state_in_kernel, n_blocks=n_blocks, p=p),
        grid=(ng // groups,),
        in_specs=[u_spec, pl.BlockSpec((groups, p2, lc), lambda i: (i, 0, 0))],
        out_specs=st_spec,
        out_shape=jax.ShapeDtypeStruct((nk_pad, ng * p2), F32),
        compiler_params=_params(("arbitrary",), _vmem_estimate([((m_pad, LANES), F32), ((groups, p2, lc), BF16), ((nk_pad, groups * p2), F32)], temps)),
        name="ssm_state_in",
    )(u, ft)
    lb = _pick(ng * p2, (1024, 512, 256, 128))
    nseq = batch + dec_batch
    rowb = pl.BlockSpec((1, lb), lambda i: (0, i))
    h, fin = pl.pallas_call(
        functools.partial(_ssm_scan_kernel, batch=batch, kp=kp, dec_batch=dec_batch, ks=ks, p=p),
        grid=(ng * p2 // lb,),
        in_specs=[pl.BlockSpec((nk_pad, lb), lambda i: (0, i)), pl.BlockSpec((dec_batch, lb), lambda i: (0, i)), rowb, rowb, rowb],
        out_specs=[pl.BlockSpec((nk_pad, lb), lambda i: (0, i)), pl.BlockSpec((nseq, lb), lambda i: (0, i))],
        out_shape=[jax.ShapeDtypeStruct((nk_pad, ng * p2), F32), jax.ShapeDtypeStruct((nseq, ng * p2), F32)],
        scratch_shapes=[pltpu.VMEM((nk_pad, lb), F32)],
        compiler_params=_params(("arbitrary",), _vmem_estimate([((nk_pad, lb), F32)] * 2, [((nk_pad, lb), F32)])),
        name="ssm_scan",
    )(s, h0, a1, a2, a2s)
    y = pl.pallas_call(
        functools.partial(_ssm_out_kernel, n_blocks=n_blocks, p=p),
        grid=(ng // groups,),
        in_specs=[u_spec, st_spec, pl.BlockSpec((groups, lc, lc + p2), lambda i: (i, 0, 0))],
        out_specs=u_spec,
        out_shape=jax.ShapeDtypeStruct((m_pad, sw), F32),
        compiler_params=_params(("arbitrary",), _vmem_estimate(
            [((m_pad, LANES), F32)] * 2 + [((nk_pad, groups * p2), F32), ((groups, lc, lc + p2), BF16)], temps)),
        name="ssm_out",
    )(u, h, met)
    return y, fin


def _glu_kernel(y_ref, wa_ref, wg_ref, ba_ref, bg_ref, zs_ref, o_ref, wab_ref, wgb_ref):
    _cast_weight(wa_ref, wab_ref)
    _cast_weight(wg_ref, wgb_ref)
    y = y_ref[...].astype(BF16)
    a = jnp.dot(y, wab_ref[...], preferred_element_type=F32) + ba_ref[...]
    g = jnp.dot(y, wgb_ref[...], preferred_element_type=F32) + bg_ref[...]
    o_ref[...] = (a * jax.nn.sigmoid(g) * zs_ref[...].astype(F32)).astype(o_ref.dtype)


def _glu(y, w_glu, b_glu, zs, tm, tn):
    m, sw = zs.shape
    nb = sw // tn
    blocks = [((tm, sw), F32), ((sw, tn), F32), ((sw, tn), F32), ((tm, tn), BF16), ((tm, tn), BF16)]
    return pl.pallas_call(
        _glu_kernel,
        grid=(nb, m // tm),
        in_specs=[
            pl.BlockSpec((tm, sw), lambda j, i: (i, 0)),
            pl.BlockSpec((sw, tn), lambda j, i: (0, j)),
            pl.BlockSpec((sw, tn), lambda j, i: (0, nb + j)),
            pl.BlockSpec((1, tn), lambda j, i: (0, j)),
            pl.BlockSpec((1, tn), lambda j, i: (0, nb + j)),
            pl.BlockSpec((tm, tn), lambda j, i: (i, j)),
        ],
        out_specs=pl.BlockSpec((tm, tn), lambda j, i: (i, j)),
        out_shape=jax.ShapeDtypeStruct((m, sw), BF16),
        scratch_shapes=[pltpu.VMEM((sw, tn), BF16)] * 2,
        compiler_params=_params(("arbitrary", "arbitrary"), _vmem_estimate(blocks, [((tm, tn), F32)] * 4 + [((sw, tn), BF16)] * 2)),
        name="glu",
    )(y, w_glu, w_glu, b_glu, b_glu, zs)


def _merge_kernel(xa_ref, xs_ref, wpa_ref, wps_ref, ga_ref, gs_ref, o_ref, wpab_ref, wpsb_ref):
    _cast_weight(wpa_ref, wpab_ref)
    _cast_weight(wps_ref, wpsb_ref)
    br_a = jnp.dot(xa_ref[...], wpab_ref[...], preferred_element_type=F32)
    br_s = jnp.dot(xs_ref[...], wpsb_ref[...], preferred_element_type=F32)
    o_ref[...] = (ga_ref[...].astype(F32) * br_a + gs_ref[...].astype(F32) * br_s).astype(o_ref.dtype)


def _merge(xa, xs, w_pa, w_ps, gate, tm, tn):
    m, aw = xa.shape
    sw = xs.shape[1]
    d = w_pa.shape[1]
    nb = d // tn
    blocks = [((tm, aw), BF16), ((tm, sw), BF16), ((aw, tn), F32), ((sw, tn), F32)] + [((tm, tn), BF16)] * 3
    return pl.pallas_call(
        _merge_kernel,
        grid=(nb, m // tm),
        in_specs=[
            pl.BlockSpec((tm, aw), lambda j, i: (i, 0)),
            pl.BlockSpec((tm, sw), lambda j, i: (i, 0)),
            pl.BlockSpec((aw, tn), lambda j, i: (0, j)),
            pl.BlockSpec((sw, tn), lambda j, i: (0, j)),
            pl.BlockSpec((tm, tn), lambda j, i: (i, j)),
            pl.BlockSpec((tm, tn), lambda j, i: (i, nb + j)),
        ],
        out_specs=pl.BlockSpec((tm, tn), lambda j, i: (i, j)),
        out_shape=jax.ShapeDtypeStruct((m, d), BF16),
        scratch_shapes=[pltpu.VMEM((aw, tn), BF16), pltpu.VMEM((sw, tn), BF16)],
        compiler_params=_params(("arbitrary", "arbitrary"),
                                _vmem_estimate(blocks, [((tm, tn), F32)] * 4 + [((aw, tn), BF16), ((sw, tn), BF16)])),
        name="merge",
    )(xa, xs, w_pa, w_ps, gate, gate)


def _out_kernel(mg_ref, w_ref, x_ref, g_ref, o_ref, acc_ref, *, n_col_blocks, tn):
    j = pl.program_id(1)
    acc_ref[j] = x_ref[...] + jnp.dot(mg_ref[...], w_ref[...], preferred_element_type=F32)

    @pl.when(j == n_col_blocks - 1)
    def _():
        ssq = acc_ref[0] * acc_ref[0]
        tot = jnp.sum(ssq, axis=-1, keepdims=True)
        for c in range(1, n_col_blocks):
            blk = acc_ref[c]
            tot = tot + jnp.sum(blk * blk, axis=-1, keepdims=True)
        scale = lax.rsqrt(tot / (n_col_blocks * tn) + NORM_EPS)
        for c in range(n_col_blocks):
            o_ref[:, c * tn:(c + 1) * tn] = acc_ref[c] * scale * g_ref[:, c * tn:(c + 1) * tn]


def _out(merged, row_block0, w_out, x, final_g, tm, tn):
    mx, d = x.shape
    nb = d // tn
    blocks = [((tm, d), BF16), ((d, tn), BF16), ((tm, tn), F32), ((tm, d), F32)]
    return pl.pallas_call(
        functools.partial(_out_kernel, n_col_blocks=nb, tn=tn),
        grid=(mx // tm, nb),
        in_specs=[
            pl.BlockSpec((tm, d), lambda i, j: (row_block0 + i, 0)),
            pl.BlockSpec((d, tn), lambda i, j: (0, j)),
            pl.BlockSpec((tm, tn), lambda i, j: (i, j)),
            pl.BlockSpec((1, d), lambda i, j: (0, 0)),
        ],
        out_specs=pl.BlockSpec((tm, d), lambda i, j: (i, 0)),
        out_shape=jax.ShapeDtypeStruct((mx, d), F32),
        scratch_shapes=[pltpu.VMEM((nb, tm, tn), F32)],
        compiler_params=_params(("arbitrary", "arbitrary"), _vmem_estimate(blocks, [((tm, d), F32), ((tm, tn), F32)])),
        name="out_norm",
    )(merged, w_out, x, final_g.reshape(1, d))


def _rope_tables(positions):
    half = HEAD_DIM // 2
    inv_freq = ROPE_THETA ** (-jnp.arange(half, dtype=F32) / half)
    ang = positions.astype(F32)[:, None] * inv_freq[None, :]
    cos, sin = jnp.cos(ang), jnp.sin(ang)
    reps = LANES // HEAD_DIM
    return jnp.tile(jnp.concatenate([cos, cos], axis=1), (1, reps)), jnp.tile(jnp.concatenate([-sin, sin], axis=1), (1, reps))


def kernel(x_prompt, x_sample, cache_k, cache_v, state_ssm_re, state_ssm_im, norm_g, w_in, sink, lambda_re, lambda_im,
           log_dt, b_re, b_im, c_re, c_im, d_skip, w_glu, b_glu, w_pa, w_ps, w_out, final_g):
    depth = norm_g.shape[0]
    assert depth == 1, "one trunk layer"
    batch, seq, d = x_prompt.shape
    dec_batch, dec_seq, _ = x_sample.shape
    aw = w_pa.shape[1]
    sw = w_ps.shape[1]
    nh = aw // HEAD_DIM
    g = max(1, nh // GQA_GROUPING)
    rep = nh // g
    kvw = g * HEAD_DIM
    ng, p = lambda_re.shape[1:]
    assert dec_seq == CHUNK and cache_k.shape[2] == WINDOW and seq % (2 * CHUNK) == 0
    assert kvw % LANES == 0 and rep % 2 == 0 and sw == ng * SSM_GROUP and (1 << SSM_POW_BITS) == SSM_CHUNK
    assert 2 * p == LANES and sw % LANES == 0
    mp, ms = batch * seq, dec_batch * dec_seq
    m = mp + ms
    widths = (aw, 2 * kvw, aw, sw, sw, 2 * d)
    assert sum(widths) == w_in.shape[2]
    c_q, c_kv, c_za, c_u, c_zs, c_gate = (sum(widths[:n]) for n in range(len(widths)))
    tm = _pick(math.gcd(mp, ms), (512, 256, 128))
    tmp = max(t for t in range(16, 1153, 16) if m % t == 0)
    tn = _pick(math.gcd(c_kv, c_za, c_u, c_zs, c_gate, 2 * d), (512, 256, 128))

    xp = x_prompt.reshape(mp, d)
    xs = x_sample.reshape(ms, d)
    w_in2 = w_in.reshape(d, w_in.shape[2])
    positions = jnp.concatenate([jnp.tile(jnp.arange(seq, dtype=jnp.int32), batch),
                                 jnp.tile(PAST_LEN + jnp.arange(dec_seq, dtype=jnp.int32), dec_batch)])
    cos, sin = _rope_tables(positions)

    h = _rmsnorm(xp, xs, norm_g[0], tm)
    q_hm = _proj_q(h, w_in2, cos, sin, aw, tmp, tn)
    k_f, v_f, k_hm, v_hm = _proj_kv(h, w_in2, cos, sin, c_kv, kvw, tmp)
    za = _proj_act(h, w_in2, c_za, aw, "silu", tmp, tn, "proj_za")
    nk = m // SSM_CHUNK
    m_pad = -(-nk // LANES) * LANES * SSM_CHUNK
    u = _proj_u(h, w_in2, c_u, sw, m_pad, _pick(math.gcd(m, m_pad), (512, 256, 128)), tn)
    zs = _proj_act(h, w_in2, c_zs, sw, "silu", tmp, tn, "proj_zs")
    gate = _proj_act(h, w_in2, c_gate, 2 * d, "sigmoid", tmp, tn, "proj_gate")

    to_heads = lambda c: jnp.transpose(c[0], (2, 0, 1, 3)).astype(BF16)
    new_rows = lambda a: a[:, mp:].reshape(g, dec_batch, dec_seq, HEAD_DIM)
    ks = jnp.concatenate([to_heads(cache_k), new_rows(k_hm)], axis=2).reshape(g, dec_batch * (WINDOW + CHUNK), HEAD_DIM)
    vs = jnp.concatenate([to_heads(cache_v), new_rows(v_hm)], axis=2).reshape(g, dec_batch * (WINDOW + CHUNK), HEAD_DIM)
    sink_rows = jnp.repeat(sink[0].reshape(g, rep), CHUNK, axis=1).reshape(g, 1, rep * CHUNK)
    xa = _attention(q_hm, k_hm, v_hm, ks, vs, sink_rows, za, batch, seq, dec_batch, aw)

    met, ft, a1, a2, a2s = _ssm_params(lambda_re[0], lambda_im[0], log_dt[0], b_re[0], b_im[0], c_re[0], c_im[0], d_skip[0])
    h0 = jnp.concatenate([state_ssm_re[0], state_ssm_im[0]], axis=-1).reshape(dec_batch, ng * 2 * p)
    y, fin = _ssm(u, met, ft, a1, a2, a2s, h0, batch, seq // SSM_CHUNK, dec_batch, dec_seq // SSM_CHUNK)

    tmm = max(t for t in range(16, 577, 16) if m % t == 0)
    x_ssm = _glu(y, w_glu.reshape(sw, 2 * sw), b_glu.reshape(1, 2 * sw), zs, tmm, tn)
    merged = _merge(xa, x_ssm, w_pa.reshape(aw, d), w_ps.reshape(sw, d), gate, tmm, tn)
    w_out_b = w_out[0].astype(BF16)
    tno = _pick(d, (512, 256))
    y_prompt = _out(merged, 0, w_out_b, xp, final_g, tm, tno).reshape(batch, seq, d)
    y_sample = _out(merged, mp // tm, w_out_b, xs, final_g, tm, tno).reshape(dec_batch, dec_seq, d)

    keep = min(WINDOW, seq)
    last_rows = lambda a: a[:mp].reshape(batch, seq, g, HEAD_DIM)[:, seq - keep:][None]
    dec_rows = lambda a: a[mp:].reshape(dec_batch, dec_seq, g, HEAD_DIM)[None]
    fin = fin.reshape(batch + dec_batch, ng, 2, p)
    return (y_prompt, y_sample, last_rows(k_f), last_rows(v_f), fin[:batch, :, 0][None], fin[:batch, :, 1][None],
            dec_rows(k_f), dec_rows(v_f), fin[batch:, :, 0][None], fin[batch:, :, 1][None])
```

```python
import functools
import math

import jax
import jax.numpy as jnp
from jax import lax
from jax.experimental import pallas as pl
from jax.experimental.pallas import tpu as pltpu

CHUNK = 64
WINDOW = 128
HEAD_DIM = 64
GQA_GROUPING = 8
SSM_GROUP = 16
PAST_LEN = 1024
ROPE_THETA = 10000.0
NORM_EPS = 1e-5
LAMBDA_RE_MAX = -1e-4
LOG2_E = math.log2(math.e)

SSM_CHUNK = 16
SSM_POW_BITS = 4
LANES = 128
V7X_VMEM_BYTES = 64 * 1024 * 1024
BF16 = jnp.bfloat16
F32 = jnp.float32


def _pick(n, prefs):
    for p in prefs:
        if n % p == 0:
            return p
    raise ValueError(f"no tile in {prefs} divides {n}")


def _params(sem, vmem_bytes):
    limit = min(int(vmem_bytes), V7X_VMEM_BYTES - 4 * 1024 * 1024)
    return pltpu.CompilerParams(dimension_semantics=sem, vmem_limit_bytes=limit)


def _sigmoid(x):
    return 0.5 * jnp.tanh(0.5 * x) + 0.5


def _nbytes(shape, dtype):
    return math.prod(shape) * jnp.dtype(dtype).itemsize


def _vmem_estimate(blocks, temps=()):
    return 2 * (2 * sum(_nbytes(s, d) for s, d in blocks) + sum(_nbytes(s, d) for s, d in temps))


def _rmsnorm_kernel(xp_ref, xs_ref, g_ref, o_ref, *, n_prompt_blocks):
    i = pl.program_id(0)

    def norm(x_ref):
        x = x_ref[...]
        y = x * lax.rsqrt(jnp.mean(x * x, axis=-1, keepdims=True) + NORM_EPS)
        o_ref[...] = (y * g_ref[...]).astype(o_ref.dtype)

    @pl.when(i < n_prompt_blocks)
    def _():
        norm(xp_ref)

    @pl.when(i >= n_prompt_blocks)
    def _():
        norm(xs_ref)


def _rmsnorm(xp, xs, g, tm):
    mp, d = xp.shape
    ms = xs.shape[0]
    npb, nsb = mp // tm, ms // tm
    return pl.pallas_call(
        functools.partial(_rmsnorm_kernel, n_prompt_blocks=npb),
        grid=(npb + nsb,),
        in_specs=[
            pl.BlockSpec((tm, d), lambda i: (jnp.minimum(i, npb - 1), 0)),
            pl.BlockSpec((tm, d), lambda i: (jnp.maximum(i - npb, 0), 0)),
            pl.BlockSpec((1, d), lambda i: (0, 0)),
        ],
        out_specs=pl.BlockSpec((tm, d), lambda i: (i, 0)),
        out_shape=jax.ShapeDtypeStruct((mp + ms, d), BF16),
        compiler_params=_params(("arbitrary",), _vmem_estimate([((tm, d), F32)] * 2 + [((tm, d), BF16)], [((tm, d), F32)])),
        name="rmsnorm_in",
    )(xp, xs, g.reshape(1, d))


def _rope(acc, cos_ref, sin_ref):
    tm, tn = acc.shape
    reps = tn // LANES
    cos = jnp.tile(cos_ref[...], (1, reps))
    sin = jnp.tile(sin_ref[...], (1, reps))
    lane = lax.broadcasted_iota(jnp.int32, (tm, tn), 1)
    low = (lane % HEAD_DIM) < (HEAD_DIM // 2)
    partner = jnp.where(low, pltpu.roll(acc, tn - HEAD_DIM // 2, 1), pltpu.roll(acc, HEAD_DIM // 2, 1))
    return acc * cos + partner * sin


def _store_heads(o_ref, val):
    for h in range(val.shape[1] // HEAD_DIM):
        o_ref[h] = val[:, h * HEAD_DIM:(h + 1) * HEAD_DIM].astype(o_ref.dtype)


def _cast_weight(w_ref, wb_ref):
    @pl.when(pl.program_id(1) == 0)
    def _():
        wb_ref[...] = w_ref[...].astype(wb_ref.dtype)


def _proj_q_kernel(h_ref, w_ref, cos_ref, sin_ref, q_ref, wb_ref):
    _cast_weight(w_ref, wb_ref)
    acc = jnp.dot(h_ref[...], wb_ref[...], preferred_element_type=F32)
    _store_heads(q_ref, _rope(acc, cos_ref, sin_ref) * (HEAD_DIM ** -0.5 * LOG2_E))


def _proj_kv_kernel(h_ref, w_ref, cos_ref, sin_ref, kf_ref, vf_ref, kh_ref, vh_ref, wb_ref, *, kvw):
    _cast_weight(w_ref, wb_ref)
    acc = jnp.dot(h_ref[...], wb_ref[...], preferred_element_type=F32)
    k = _rope(acc[:, :kvw], cos_ref, sin_ref)
    v = acc[:, kvw:]
    kf_ref[...] = k
    vf_ref[...] = v
    _store_heads(kh_ref, k)
    _store_heads(vh_ref, v)


def _proj_act_kernel(h_ref, w_ref, o_ref, wb_ref, *, act):
    _cast_weight(w_ref, wb_ref)
    acc = jnp.dot(h_ref[...], wb_ref[...], preferred_element_type=F32)
    if act == "silu":
        acc = acc * _sigmoid(acc)
    elif act == "sigmoid":
        acc = _sigmoid(acc)
    o_ref[...] = acc.astype(o_ref.dtype)


def _proj_u_kernel(h_ref, w_ref, o_ref, wb_ref, *, n_real):
    _cast_weight(w_ref, wb_ref)
    i = pl.program_id(1)

    @pl.when(i < n_real)
    def _():
        o_ref[...] = jnp.dot(h_ref[...], wb_ref[...], preferred_element_type=F32)

    @pl.when(i >= n_real)
    def _():
        o_ref[...] = jnp.zeros(o_ref.shape, o_ref.dtype)


def _proj_specs(d, tm, tn, col0):
    assert col0 % tn == 0
    cb0 = col0 // tn
    return [pl.BlockSpec((tm, d), lambda j, i: (i, 0)), pl.BlockSpec((d, tn), lambda j, i: (0, cb0 + j))]


def _proj_vmem(d, tm, tn, outs):
    return _vmem_estimate([((tm, d), BF16), ((d, tn), F32)] + outs, [((tm, tn), F32)] * 3 + [((d, tn), BF16)])


def _proj_q(h, w, cos, sin, aw, tm, tn):
    m, d = h.shape
    nh = aw // HEAD_DIM
    tab = pl.BlockSpec((tm, LANES), lambda j, i: (i, 0))
    return pl.pallas_call(
        _proj_q_kernel,
        grid=(aw // tn, m // tm),
        in_specs=_proj_specs(d, tm, tn, 0) + [tab, tab],
        out_specs=pl.BlockSpec((tn // HEAD_DIM, tm, HEAD_DIM), lambda j, i: (j, i, 0)),
        out_shape=jax.ShapeDtypeStruct((nh, m, HEAD_DIM), BF16),
        scratch_shapes=[pltpu.VMEM((d, tn), BF16)],
        compiler_params=_params(("arbitrary", "arbitrary"), _proj_vmem(d, tm, tn, [((tm, 2 * tn), BF16), ((tm, 2 * LANES), F32)])),
        name="proj_q",
    )(h, w, cos, sin)


def _proj_kv(h, w, cos, sin, col0, kvw, tm):
    m, d = h.shape
    g = kvw // HEAD_DIM
    tn = 2 * kvw
    tab = pl.BlockSpec((tm, LANES), lambda j, i: (i, 0))
    flat = pl.BlockSpec((tm, kvw), lambda j, i: (i, 0))
    heads = pl.BlockSpec((g, tm, HEAD_DIM), lambda j, i: (0, i, 0))
    return pl.pallas_call(
        functools.partial(_proj_kv_kernel, kvw=kvw),
        grid=(1, m // tm),
        in_specs=_proj_specs(d, tm, tn, col0) + [tab, tab],
        out_specs=[flat, flat, heads, heads],
        out_shape=[jax.ShapeDtypeStruct((m, kvw), F32)] * 2 + [jax.ShapeDtypeStruct((g, m, HEAD_DIM), BF16)] * 2,
        scratch_shapes=[pltpu.VMEM((d, tn), BF16)],
        compiler_params=_params(("arbitrary", "arbitrary"), _proj_vmem(d, tm, tn, [((tm, 2 * tn), F32), ((tm, 2 * tn), BF16), ((tm, 2 * LANES), F32)])),
        name="proj_kv",
    )(h, w, cos, sin)


def _proj_act(h, w, col0, ncols, act, tm, tn, name):
    m, d = h.shape
    return pl.pallas_call(
        functools.partial(_proj_act_kernel, act=act),
        grid=(ncols // tn, m // tm),
        in_specs=_proj_specs(d, tm, tn, col0),
        out_specs=pl.BlockSpec((tm, tn), lambda j, i: (i, j)),
        out_shape=jax.ShapeDtypeStruct((m, ncols), BF16),
        scratch_shapes=[pltpu.VMEM((d, tn), BF16)],
        compiler_params=_params(("arbitrary", "arbitrary"), _proj_vmem(d, tm, tn, [((tm, tn), BF16)])),
        name=name,
    )(h, w)


def _proj_u(h, w, col0, ncols, m_pad, tm, tn):
    m, d = h.shape
    n_real = m // tm
    cb0 = col0 // tn
    return pl.pallas_call(
        functools.partial(_proj_u_kernel, n_real=n_real),
        grid=(ncols // tn, m_pad // tm),
        in_specs=[pl.BlockSpec((tm, d), lambda j, i: (jnp.minimum(i, n_real - 1), 0)), pl.BlockSpec((d, tn), lambda j, i: (0, cb0 + j))],
        out_specs=pl.BlockSpec((tm, tn), lambda j, i: (i, j)),
        out_shape=jax.ShapeDtypeStruct((m_pad, ncols), F32),
        scratch_shapes=[pltpu.VMEM((d, tn), BF16)],
        compiler_params=_params(("arbitrary", "arbitrary"), _proj_vmem(d, tm, tn, [((tm, tn), F32)])),
        name="proj_u",
    )(h, w)


def _attn_weights(q, kw, sink, n_valid):
    st = lax.dot_general(kw, q, (((1,), (1,)), ((), ())), preferred_element_type=F32)
    if n_valid is not None:
        row = lax.broadcasted_iota(jnp.int32, st.shape, 0)
        st = jnp.where(row < n_valid, st, -jnp.inf)
    m = jnp.maximum(jnp.max(st, axis=0, keepdims=True), sink)
    e = jnp.exp2(st - m)
    return e.astype(BF16), jnp.sum(e, axis=0, keepdims=True) + jnp.exp2(sink - m)


def _attn_values(vw, e, denom):
    ot = jnp.dot(vw.astype(F32).T.astype(BF16), e, preferred_element_type=F32) / denom
    return ot.T


def _store_unit(o_ref, za_ref, row0, o, rep):
    for r in range(0, rep, 2):
        pair = jnp.concatenate([o[r * CHUNK:(r + 1) * CHUNK], o[(r + 1) * CHUNK:(r + 2) * CHUNK]], axis=1)
        gate = za_ref[row0:row0 + CHUNK, r * HEAD_DIM:(r + 2) * HEAD_DIM].astype(F32)
        o_ref[row0:row0 + CHUNK, r * HEAD_DIM:(r + 2) * HEAD_DIM] = (pair * gate).astype(o_ref.dtype)


def _attn_prompt_kernel(q_ref, k_ref, v_ref, sink_ref, za_ref, o_ref, *, n_chunks, rep):
    cb = pl.program_id(2)
    win = WINDOW + CHUNK
    sink = sink_ref[0] * LOG2_E
    staged = []
    for c in range(n_chunks):
        cg = cb * n_chunks + c
        start = pl.multiple_of(jnp.maximum(cg - WINDOW // CHUNK, 0) * CHUNK, CHUNK)
        q = q_ref[:, c * CHUNK:(c + 1) * CHUNK, :].reshape(rep * CHUNK, HEAD_DIM)
        n_valid = jnp.where(cb == 0, (c + 1) * CHUNK, win) if c < WINDOW // CHUNK else None
        staged.append((v_ref[0, pl.ds(start, win), :],) + _attn_weights(q, k_ref[0, pl.ds(start, win), :], sink, n_valid))
    for c in range(n_chunks):
        _store_unit(o_ref, za_ref, c * CHUNK, _attn_values(*staged[c]), rep)


def _attn_sample_kernel(q_ref, k_ref, v_ref, sink_ref, za_ref, prev_ref, o_ref, *, n_streams, rep):
    del prev_ref
    win = WINDOW + CHUNK
    sink = sink_ref[0] * LOG2_E
    staged = []
    for b in range(n_streams):
        q = q_ref[:, b * CHUNK:(b + 1) * CHUNK, :].reshape(rep * CHUNK, HEAD_DIM)
        staged.append((v_ref[0, b * win:(b + 1) * win, :],) + _attn_weights(q, k_ref[0, b * win:(b + 1) * win, :], sink, None))
    for b in range(n_streams):
        _store_unit(o_ref, za_ref, b * CHUNK, _attn_values(*staged[b]), rep)


def _attention(q_hm, k_hm, v_hm, ks, vs, sink_rows, za, batch, seq, dec_batch, aw):
    nh, m, _ = q_hm.shape
    g = k_hm.shape[0]
    rep = nh // g
    win = WINDOW + CHUNK
    n_chunks = _pick(seq // CHUNK, (16, 8, 4, 2))
    ncb = seq // (CHUNK * n_chunks)
    blocks = [((rep, CHUNK * n_chunks, HEAD_DIM), BF16), ((seq, LANES), BF16), ((seq, LANES), BF16), ((CHUNK * n_chunks, rep * HEAD_DIM), BF16)]
    temps = [((rep * CHUNK, 2 * LANES), F32)] * (4 * n_chunks)
    attn = pl.pallas_call(
        functools.partial(_attn_prompt_kernel, n_chunks=n_chunks, rep=rep),
        grid=(batch, g, ncb),
        in_specs=[
            pl.BlockSpec((rep, CHUNK * n_chunks, HEAD_DIM), lambda b, gi, cb: (gi, b * ncb + cb, 0)),
            pl.BlockSpec((1, seq, HEAD_DIM), lambda b, gi, cb: (gi, b, 0)),
            pl.BlockSpec((1, seq, HEAD_DIM), lambda b, gi, cb: (gi, b, 0)),
            pl.BlockSpec((1, 1, rep * CHUNK), lambda b, gi, cb: (gi, 0, 0)),
            pl.BlockSpec((CHUNK * n_chunks, rep * HEAD_DIM), lambda b, gi, cb: (b * ncb + cb, gi)),
        ],
        out_specs=pl.BlockSpec((CHUNK * n_chunks, rep * HEAD_DIM), lambda b, gi, cb: (b * ncb + cb, gi)),
        out_shape=jax.ShapeDtypeStruct((m, aw), BF16),
        compiler_params=_params(("arbitrary",) * 3, _vmem_estimate(blocks + blocks[3:], temps)),
        name="attn_prompt",
    )(q_hm, k_hm, v_hm, sink_rows, za)
    rows = dec_batch * CHUNK
    assert (batch * seq) % rows == 0
    row0 = batch * seq // rows
    sblocks = [((rep, rows, LANES), BF16), ((dec_batch * win, LANES), BF16), ((dec_batch * win, LANES), BF16)] + [((rows, rep * HEAD_DIM), BF16)] * 2
    return pl.pallas_call(
        functools.partial(_attn_sample_kernel, n_streams=dec_batch, rep=rep),
        grid=(g,),
        in_specs=[
            pl.BlockSpec((rep, rows, HEAD_DIM), lambda gi: (gi, row0, 0)),
            pl.BlockSpec((1, dec_batch * win, HEAD_DIM), lambda gi: (gi, 0, 0)),
            pl.BlockSpec((1, dec_batch * win, HEAD_DIM), lambda gi: (gi, 0, 0)),
            pl.BlockSpec((1, 1, rep * CHUNK), lambda gi: (gi, 0, 0)),
            pl.BlockSpec((rows, rep * HEAD_DIM), lambda gi: (row0, gi)),
            pl.BlockSpec(memory_space=pl.ANY),
        ],
        out_specs=pl.BlockSpec((rows, rep * HEAD_DIM), lambda gi: (row0, gi)),
        out_shape=jax.ShapeDtypeStruct((m, aw), BF16),
        input_output_aliases={5: 0},
        compiler_params=_params(("arbitrary",), _vmem_estimate(sblocks, temps[:4 * dec_batch])),
        name="attn_sample",
    )(q_hm, ks, vs, sink_rows, za, attn)


def _ssm_disc_kernel(lre_ref, lim_ref, ldt_ref, are_ref, aim_ref, dre_ref, dim_ref, fre_ref, fim_ref):
    lr = jnp.minimum(lre_ref[...], LAMBDA_RE_MAX)
    li = lim_ref[...]
    dt = jnp.exp(ldt_ref[...])
    mag = jnp.exp(lr * dt)
    a_re = mag * jnp.cos(li * dt)
    a_im = mag * jnp.sin(li * dt)
    den = lr * lr + li * li
    nr = a_re - 1.0
    fre_ref[...] = (nr * lr + a_im * li) / den
    fim_ref[...] = (a_im * lr - nr * li) / den
    are_ref[...] = a_re
    aim_ref[...] = a_im
    for _ in range(SSM_POW_BITS):
        a_re, a_im = a_re * a_re - a_im * a_im, 2.0 * a_re * a_im
    dre_ref[...] = a_re
    dim_ref[...] = a_im


def _cmul(ar, ai, br, bi):
    return ar * br - ai * bi, ar * bi + ai * br


def _ssm_build_kernel(are_ref, aim_ref, dre_ref, dim_ref, fre_ref, fim_ref, btr_ref, bti_ref, cr_ref, ci_ref, dv_ref,
                      met_ref, ft_ref, a1_ref, a2_ref, a2s_ref, *, groups):
    lc = SSM_CHUNK * SSM_GROUP
    sub = lax.broadcasted_iota(jnp.int32, (SSM_GROUP, lc), 0)
    lane = lax.broadcasted_iota(jnp.int32, (SSM_GROUP, lc), 1)
    for gi in range(groups):
        row = slice(gi, gi + 1)
        a_re, a_im = are_ref[row, :], aim_ref[row, :]
        pw = [(jnp.ones_like(a_re), jnp.zeros_like(a_re))]
        for _ in range(SSM_CHUNK):
            pw.append(_cmul(pw[-1][0], pw[-1][1], a_re, a_im))
        c_re, c_im = cr_ref[gi], ci_ref[gi]
        wt = [_cmul(pr, pi, c_re, c_im) for pr, pi in pw]
        wt_re = jnp.concatenate([w[0] for w in wt[:SSM_CHUNK]], axis=0)
        wt_im = jnp.concatenate([w[1] for w in wt[:SSM_CHUNK]], axis=0)
        et_re = jnp.concatenate([w[0] for w in wt[1:]], axis=0)
        et_im = jnp.concatenate([w[1] for w in wt[1:]], axis=0)
        bb_re, bb_im = _cmul(fre_ref[row, :], fim_ref[row, :], btr_ref[gi], bti_ref[gi])
        r0 = lax.dot_general(jnp.concatenate([bb_re, bb_im], axis=1), jnp.concatenate([wt_re, -wt_im], axis=1),
                             (((1,), (1,)), ((), ())), preferred_element_type=F32, precision=lax.Precision.HIGHEST)
        r0 = r0 + jnp.where(sub == lane, dv_ref[gi], 0.0)
        rows = [r0] + [jnp.where(lane >= s * SSM_GROUP, pltpu.roll(r0, s * SSM_GROUP, 1), 0.0) for s in range(1, SSM_CHUNK)]
        mt = jnp.concatenate(rows, axis=0).T
        met_ref[gi] = jnp.concatenate([mt, et_re, -et_im], axis=1).astype(met_ref.dtype)
        fb = [_cmul(pw[SSM_CHUNK - 1 - s][0], pw[SSM_CHUNK - 1 - s][1], bb_re, bb_im) for s in range(SSM_CHUNK)]
        f_all = jnp.concatenate([jnp.concatenate([x[0] for x in fb], axis=0), jnp.concatenate([x[1] for x in fb], axis=0)], axis=1)
        ft_ref[gi] = f_all.T.astype(ft_ref.dtype)
        d_re, d_im = dre_ref[row, :], dim_ref[row, :]
        a1_ref[gi] = jnp.concatenate([d_re, d_re], axis=1)
        a2_ref[gi] = jnp.concatenate([-d_im, d_im], axis=1)
        a2s_ref[gi] = jnp.concatenate([d_im, -d_im], axis=1)


def _ssm_params(lambda_re, lambda_im, log_dt, b_re, b_im, c_re, c_im, d_skip):
    ng, p = lambda_re.shape
    lc = SSM_CHUNK * SSM_GROUP
    full = pl.BlockSpec((ng, p), lambda: (0, 0))
    disc = pl.pallas_call(
        _ssm_disc_kernel,
        in_specs=[full, full, pl.BlockSpec((ng, 1), lambda: (0, 0))],
        out_specs=[full] * 6,
        out_shape=[jax.ShapeDtypeStruct((ng, p), F32)] * 6,
        name="ssm_disc",
    )(lambda_re, lambda_im, log_dt.reshape(ng, 1))
    gb = _pick(ng, (8,))
    bt_re = jnp.swapaxes(b_re, 1, 2)
    bt_im = jnp.swapaxes(b_im, 1, 2)
    dvec = jnp.pad(d_skip, ((0, 0), (0, lc - SSM_GROUP))).reshape(ng, 1, lc)
    rows = pl.BlockSpec((gb, p), lambda i: (i, 0))
    mats = pl.BlockSpec((gb, SSM_GROUP, p), lambda i: (i, 0, 0))
    dec = pl.BlockSpec((gb, 1, 2 * p), lambda i: (i, 0, 0))
    blocks = ([((gb, LANES), F32)] * 6 + [((gb, SSM_GROUP, LANES), F32)] * 4 + [((gb, 8, lc), F32)]
              + [((gb, lc, lc + 2 * p), BF16), ((gb, 2 * p, lc), BF16)] + [((gb, 8, LANES), F32)] * 3)
    met, ft, a1, a2, a2s = pl.pallas_call(
        functools.partial(_ssm_build_kernel, groups=gb),
        grid=(ng // gb,),
        in_specs=[rows] * 6 + [mats] * 4 + [pl.BlockSpec((gb, 1, lc), lambda i: (i, 0, 0))],
        out_specs=[pl.BlockSpec((gb, lc, lc + 2 * p), lambda i: (i, 0, 0)), pl.BlockSpec((gb, 2 * p, lc), lambda i: (i, 0, 0)), dec, dec, dec],
        out_shape=[jax.ShapeDtypeStruct((ng, lc, lc + 2 * p), BF16), jax.ShapeDtypeStruct((ng, 2 * p, lc), BF16)]
        + [jax.ShapeDtypeStruct((ng, 1, 2 * p), F32)] * 3,
        compiler_params=_params(("arbitrary",), _vmem_estimate(blocks, [((lc, lc + 2 * p), F32)] * 8)),
        name="ssm_build",
    )(*disc, bt_re, bt_im, c_re, c_im, dvec)
    flat = lambda a: a.reshape(1, ng * 2 * p)
    return met, ft, flat(a1), flat(a2), flat(a2s)


def _chunk_major_inputs(u_ref, n_blocks):
    rows = LANES * SSM_CHUNK
    return [[u_ref[pl.ds(kb * rows + s, LANES, stride=SSM_CHUNK), :].T.astype(BF16) for kb in range(n_blocks)]
            for s in range(SSM_CHUNK)]


def _group_operand(xt, gi, n_blocks):
    return [jnp.concatenate([xt[s][kb][gi * SSM_GROUP:(gi + 1) * SSM_GROUP, :] for kb in range(n_blocks)], axis=1)
            for s in range(SSM_CHUNK)]


def _ssm_state_in_kernel(u_ref, ft_ref, s_ref, *, n_blocks, p):
    groups = LANES // SSM_GROUP
    xt = _chunk_major_inputs(u_ref, n_blocks)
    for gi in range(groups):
        rhs = jnp.concatenate(_group_operand(xt, gi, n_blocks), axis=0)
        st = jnp.dot(ft_ref[gi], rhs, preferred_element_type=F32)
        for kb in range(n_blocks):
            s_ref[kb * LANES:(kb + 1) * LANES, gi * 2 * p:(gi + 1) * 2 * p] = st[:, kb * LANES:(kb + 1) * LANES].T


def _ssm_out_kernel(u_ref, h_ref, met_ref, y_ref, *, n_blocks, p):
    groups = LANES // SSM_GROUP
    xt = _chunk_major_inputs(u_ref, n_blocks)
    zt = [[None] * groups for _ in range(SSM_CHUNK)]
    for gi in range(groups):
        ht = jnp.concatenate([h_ref[kb * LANES:(kb + 1) * LANES, gi * 2 * p:(gi + 1) * 2 * p].T for kb in range(n_blocks)], axis=1)
        rhs = jnp.concatenate(_group_operand(xt, gi, n_blocks) + [ht.astype(BF16)], axis=0)
        yt = jnp.dot(met_ref[gi], rhs, preferred_element_type=F32)
        for t in range(SSM_CHUNK):
            zt[t][gi] = yt[t * SSM_GROUP:(t + 1) * SSM_GROUP, :]
    rows = LANES * SSM_CHUNK
    for t in range(SSM_CHUNK):
        z = jnp.concatenate(zt[t], axis=0)
        for kb in range(n_blocks):
            y_ref[pl.ds(kb * rows + t, LANES, stride=SSM_CHUNK), :] = z[:, kb * LANES:(kb + 1) * LANES].T


def _ssm_scan_kernel(s_ref, h0_ref, a1_ref, a2_ref, a2s_ref, h_ref, fin_ref, ss_ref, *, batch, kp, dec_batch, ks, p):
    def swap_halves(x):
        lane = lax.broadcasted_iota(jnp.int32, x.shape, 1)
        return jnp.where(lane % (2 * p) < p, pltpu.roll(x, x.shape[1] - p, 1), pltpu.roll(x, p, 1))

    ss_ref[...] = swap_halves(s_ref[...])
    a1, a2, a2s = a1_ref[...], a2_ref[...], a2s_ref[...]
    zero = jnp.zeros_like(a1)

    def step(k, h, hs):
        h_ref[pl.ds(k, 1), :] = h
        s = s_ref[pl.ds(k, 1), :]
        ss = ss_ref[pl.ds(k, 1), :]
        return a1 * h + a2 * hs + s, a1 * hs + a2s * h + ss

    def body(k, carry):
        out = []
        for b in range(batch):
            out.extend(step(b * kp + k, carry[2 * b], carry[2 * b + 1]))
        return tuple(out)

    fin = lax.fori_loop(0, kp, body, (zero,) * (2 * batch))
    for b in range(batch):
        fin_ref[b:b + 1, :] = fin[2 * b]
    h0s_all = swap_halves(h0_ref[...])
    for b in range(dec_batch):
        h, hs = h0_ref[b:b + 1, :], h0s_all[b:b + 1, :]
        for k in range(ks):
            h, hs = step(batch * kp + b * ks + k, h, hs)
        fin_ref[batch + b:batch + b + 1, :] = h
    n_real = batch * kp + dec_batch * ks
    if n_real < h_ref.shape[0]:
        h_ref[n_real:, :] = jnp.zeros((h_ref.shape[0] - n_real, h_ref.shape[1]), F32)


def _ssm(u, met, ft, a1, a2, a2s, h0, batch, kp, dec_batch, ks):
    m_pad, sw = u.shape
    ng, p2, lc = ft.shape
    p = p2 // 2
    nk_pad = m_pad // SSM_CHUNK
    n_blocks = nk_pad // LANES
    groups = LANES // SSM_GROUP
    u_spec = pl.BlockSpec((m_pad, LANES), lambda i: (0, i))
    st_spec = pl.BlockSpec((nk_pad, groups * p2), lambda i: (0, i))
    temps = [((lc + p2, nk_pad), BF16), ((lc, nk_pad), F32), ((m_pad, LANES), BF16)]
    s = pl.pallas_call(
        functools.partial(_ssm_state_in_kernel, n_blocks=n_blocks, p=p),
        grid=(ng // groups,),
        in_specs=[u_spec, pl.BlockSpec((groups, p2, lc), lambda i: (i, 0, 0))],
        out_specs=st_spec,
        out_shape=jax.ShapeDtypeStruct((nk_pad, ng * p2), F32),
        compiler_params=_params(("arbitrary",), _vmem_estimate([((m_pad, LANES), F32), ((groups, p2, lc), BF16), ((nk_pad, groups * p2), F32)], temps)),
        name="ssm_state_in",
    )(u, ft)
    lb = _pick(ng * p2, (1024, 512, 256, 128))
    nseq = batch + dec_batch
    rowb = pl.BlockSpec((1, lb), lambda i: (0, i))
    h, fin = pl.pallas_call(
        functools.partial(_ssm_scan_kernel, batch=batch, kp=kp, dec_batch=dec_batch, ks=ks, p=p),
        grid=(ng * p2 // lb,),
        in_specs=[pl.BlockSpec((nk_pad, lb), lambda i: (0, i)), pl.BlockSpec((dec_batch, lb), lambda i: (0, i)), rowb, rowb, rowb],
        out_specs=[pl.BlockSpec((nk_pad, lb), lambda i: (0, i)), pl.BlockSpec((nseq, lb), lambda i: (0, i))],
        out_shape=[jax.ShapeDtypeStruct((nk_pad, ng * p2), F32), jax.ShapeDtypeStruct((nseq, ng * p2), F32)],
        scratch_shapes=[pltpu.VMEM((nk_pad, lb), F32)],
        compiler_params=_params(("arbitrary",), _vmem_estimate([((nk_pad, lb), F32)] * 2, [((nk_pad, lb), F32)])),
        name="ssm_scan",
    )(s, h0, a1, a2, a2s)
    y = pl.pallas_call(
        functools.partial(_ssm_out_kernel, n_blocks=n_blocks, p=p),
        grid=(ng // groups,),
        in_specs=[u_spec, st_spec, pl.BlockSpec((groups, lc, lc + p2), lambda i: (i, 0, 0))],
        out_specs=u_spec,
        out_shape=jax.ShapeDtypeStruct((m_pad, sw), F32),
        compiler_params=_params(("arbitrary",), _vmem_estimate(
            [((m_pad, LANES), F32)] * 2 + [((nk_pad, groups * p2), F32), ((groups, lc, lc + p2), BF16)], temps)),
        name="ssm_out",
    )(u, h, met)
    return y, fin


def _glu_kernel(y_ref, wa_ref, wg_ref, ba_ref, bg_ref, zs_ref, o_ref, wab_ref, wgb_ref):
    _cast_weight(wa_ref, wab_ref)
    _cast_weight(wg_ref, wgb_ref)
    y = y_ref[...].astype(BF16)
    a = jnp.dot(y, wab_ref[...], preferred_element_type=F32) + ba_ref[...]
    g = jnp.dot(y, wgb_ref[...], preferred_element_type=F32) + bg_ref[...]
    o_ref[...] = (a * _sigmoid(g) * zs_ref[...].astype(F32)).astype(o_ref.dtype)


def _glu(y, w_glu, b_glu, zs, tm, tn):
    m, sw = zs.shape
    nb = sw // tn
    blocks = [((tm, sw), F32), ((sw, tn), F32), ((sw, tn), F32), ((tm, tn), BF16), ((tm, tn), BF16)]
    return pl.pallas_call(
        _glu_kernel,
        grid=(nb, m // tm),
        in_specs=[
            pl.BlockSpec((tm, sw), lambda j, i: (i, 0)),
            pl.BlockSpec((sw, tn), lambda j, i: (0, j)),
            pl.BlockSpec((sw, tn), lambda j, i: (0, nb + j)),
            pl.BlockSpec((1, tn), lambda j, i: (0, j)),
            pl.BlockSpec((1, tn), lambda j, i: (0, nb + j)),
            pl.BlockSpec((tm, tn), lambda j, i: (i, j)),
        ],
        out_specs=pl.BlockSpec((tm, tn), lambda j, i: (i, j)),
        out_shape=jax.ShapeDtypeStruct((m, sw), BF16),
        scratch_shapes=[pltpu.VMEM((sw, tn), BF16)] * 2,
        compiler_params=_params(("arbitrary", "arbitrary"), _vmem_estimate(blocks, [((tm, tn), F32)] * 4 + [((sw, tn), BF16)] * 2)),
        name="glu",
    )(y, w_glu, w_glu, b_glu, b_glu, zs)


def _merge_kernel(xa_ref, xs_ref, wpa_ref, wps_ref, ga_ref, gs_ref, o_ref, wpab_ref, wpsb_ref):
    _cast_weight(wpa_ref, wpab_ref)
    _cast_weight(wps_ref, wpsb_ref)
    br_a = jnp.dot(xa_ref[...], wpab_ref[...], preferred_element_type=F32)
    br_s = jnp.dot(xs_ref[...], wpsb_ref[...], preferred_element_type=F32)
    o_ref[...] = (ga_ref[...].astype(F32) * br_a + gs_ref[...].astype(F32) * br_s).astype(o_ref.dtype)


def _merge(xa, xs, w_pa, w_ps, gate, tm, tn):
    m, aw = xa.shape
    sw = xs.shape[1]
    d = w_pa.shape[1]
    nb = d // tn
    blocks = [((tm, aw), BF16), ((tm, sw), BF16), ((aw, tn), F32), ((sw, tn), F32)] + [((tm, tn), BF16)] * 3
    return pl.pallas_call(
        _merge_kernel,
        grid=(nb, m // tm),
        in_specs=[
            pl.BlockSpec((tm, aw), lambda j, i: (i, 0)),
            pl.BlockSpec((tm, sw), lambda j, i: (i, 0)),
            pl.BlockSpec((aw, tn), lambda j, i: (0, j)),
            pl.BlockSpec((sw, tn), lambda j, i: (0, j)),
            pl.BlockSpec((tm, tn), lambda j, i: (i, j)),
            pl.BlockSpec((tm, tn), lambda j, i: (i, nb + j)),
        ],
        out_specs=pl.BlockSpec((tm, tn), lambda j, i: (i, j)),
        out_shape=jax.ShapeDtypeStruct((m, d), BF16),
        scratch_shapes=[pltpu.VMEM((aw, tn), BF16), pltpu.VMEM((sw, tn), BF16)],
        compiler_params=_params(("arbitrary", "arbitrary"),
                                _vmem_estimate(blocks, [((tm, tn), F32)] * 4 + [((aw, tn), BF16), ((sw, tn), BF16)])),
        name="merge",
    )(xa, xs, w_pa, w_ps, gate, gate)


def _out_kernel(mg_ref, w_ref, x_ref, g_ref, o_ref, acc_ref, ssq_ref, *, n_col_blocks, tn):
    j = pl.program_id(1)
    blk = x_ref[...] + jnp.dot(mg_ref[...], w_ref[...], preferred_element_type=F32)
    acc_ref[j] = blk
    part = jnp.sum(blk * blk, axis=-1, keepdims=True)

    @pl.when(j == 0)
    def _():
        ssq_ref[...] = part

    @pl.when(j > 0)
    def _():
        ssq_ref[...] += part

    @pl.when(j == n_col_blocks - 1)
    def _():
        scale = lax.rsqrt(ssq_ref[...] / (n_col_blocks * tn) + NORM_EPS)
        for c in range(n_col_blocks):
            o_ref[:, c * tn:(c + 1) * tn] = acc_ref[c] * scale * g_ref[:, c * tn:(c + 1) * tn]


def _out(merged, row_block0, w_out, x, final_g, tm, tn):
    mx, d = x.shape
    nb = d // tn
    blocks = [((tm, d), BF16), ((d, tn), BF16), ((tm, tn), F32), ((tm, d), F32)]
    return pl.pallas_call(
        functools.partial(_out_kernel, n_col_blocks=nb, tn=tn),
        grid=(mx // tm, nb),
        in_specs=[
            pl.BlockSpec((tm, d), lambda i, j: (row_block0 + i, 0)),
            pl.BlockSpec((d, tn), lambda i, j: (0, j)),
            pl.BlockSpec((tm, tn), lambda i, j: (i, j)),
            pl.BlockSpec((1, d), lambda i, j: (0, 0)),
        ],
        out_specs=pl.BlockSpec((tm, d), lambda i, j: (i, 0)),
        out_shape=jax.ShapeDtypeStruct((mx, d), F32),
        scratch_shapes=[pltpu.VMEM((nb, tm, tn), F32), pltpu.VMEM((tm, 1), F32)],
        compiler_params=_params(("arbitrary", "arbitrary"), _vmem_estimate(blocks, [((tm, d), F32), ((tm, tn), F32), ((tm, LANES), F32)])),
        name="out_norm",
    )(merged, w_out, x, final_g.reshape(1, d))


def _rope_tables(positions):
    half = HEAD_DIM // 2
    inv_freq = ROPE_THETA ** (-jnp.arange(half, dtype=F32) / half)
    ang = positions.astype(F32)[:, None] * inv_freq[None, :]
    cos, sin = jnp.cos(ang), jnp.sin(ang)
    reps = LANES // HEAD_DIM
    return jnp.tile(jnp.concatenate([cos, cos], axis=1), (1, reps)), jnp.tile(jnp.concatenate([-sin, sin], axis=1), (1, reps))


def kernel(x_prompt, x_sample, cache_k, cache_v, state_ssm_re, state_ssm_im, norm_g, w_in, sink, lambda_re, lambda_im,
           log_dt, b_re, b_im, c_re, c_im, d_skip, w_glu, b_glu, w_pa, w_ps, w_out, final_g):
    depth = norm_g.shape[0]
    assert depth == 1, "one trunk layer"
    batch, seq, d = x_prompt.shape
    dec_batch, dec_seq, _ = x_sample.shape
    aw = w_pa.shape[1]
    sw = w_ps.shape[1]
    nh = aw // HEAD_DIM
    g = max(1, nh // GQA_GROUPING)
    rep = nh // g
    kvw = g * HEAD_DIM
    ng, p = lambda_re.shape[1:]
    assert dec_seq == CHUNK and cache_k.shape[2] == WINDOW and seq % (2 * CHUNK) == 0
    assert kvw % LANES == 0 and rep % 2 == 0 and sw == ng * SSM_GROUP and (1 << SSM_POW_BITS) == SSM_CHUNK
    assert 2 * p == LANES and sw % LANES == 0
    mp, ms = batch * seq, dec_batch * dec_seq
    m = mp + ms
    widths = (aw, 2 * kvw, aw, sw, sw, 2 * d)
    assert sum(widths) == w_in.shape[2]
    c_q, c_kv, c_za, c_u, c_zs, c_gate = (sum(widths[:n]) for n in range(len(widths)))
    tm = _pick(math.gcd(mp, ms), (512, 256, 128))
    tmp = max(t for t in range(16, 1153, 16) if m % t == 0)
    tn = _pick(math.gcd(c_kv, c_za, c_u, c_zs, c_gate, 2 * d), (512, 256, 128))

    xp = x_prompt.reshape(mp, d)
    xs = x_sample.reshape(ms, d)
    w_in2 = w_in.reshape(d, w_in.shape[2])
    positions = jnp.concatenate([jnp.tile(jnp.arange(seq, dtype=jnp.int32), batch),
                                 jnp.tile(PAST_LEN + jnp.arange(dec_seq, dtype=jnp.int32), dec_batch)])
    cos, sin = _rope_tables(positions)

    h = _rmsnorm(xp, xs, norm_g[0], tm)
    q_hm = _proj_q(h, w_in2, cos, sin, aw, tmp, tn)
    k_f, v_f, k_hm, v_hm = _proj_kv(h, w_in2, cos, sin, c_kv, kvw, tmp)
    za = _proj_act(h, w_in2, c_za, aw, "silu", tmp, tn, "proj_za")
    nk = m // SSM_CHUNK
    m_pad = -(-nk // LANES) * LANES * SSM_CHUNK
    u = _proj_u(h, w_in2, c_u, sw, m_pad, _pick(math.gcd(m, m_pad), (512, 256, 128)), tn)
    zs = _proj_act(h, w_in2, c_zs, sw, "silu", tmp, tn, "proj_zs")
    gate = _proj_act(h, w_in2, c_gate, 2 * d, "sigmoid", tmp, tn, "proj_gate")

    to_heads = lambda c: jnp.transpose(c[0], (2, 0, 1, 3)).astype(BF16)
    new_rows = lambda a: a[:, mp:].reshape(g, dec_batch, dec_seq, HEAD_DIM)
    ks = jnp.concatenate([to_heads(cache_k), new_rows(k_hm)], axis=2).reshape(g, dec_batch * (WINDOW + CHUNK), HEAD_DIM)
    vs = jnp.concatenate([to_heads(cache_v), new_rows(v_hm)], axis=2).reshape(g, dec_batch * (WINDOW + CHUNK), HEAD_DIM)
    sink_rows = jnp.repeat(sink[0].reshape(g, rep), CHUNK, axis=1).reshape(g, 1, rep * CHUNK)
    xa = _attention(q_hm, k_hm, v_hm, ks, vs, sink_rows, za, batch, seq, dec_batch, aw)

    met, ft, a1, a2, a2s = _ssm_params(lambda_re[0], lambda_im[0], log_dt[0], b_re[0], b_im[0], c_re[0], c_im[0], d_skip[0])
    h0 = jnp.concatenate([state_ssm_re[0], state_ssm_im[0]], axis=-1).reshape(dec_batch, ng * 2 * p)
    y, fin = _ssm(u, met, ft, a1, a2, a2s, h0, batch, seq // SSM_CHUNK, dec_batch, dec_seq // SSM_CHUNK)

    tmm = max(t for t in range(16, 577, 16) if m % t == 0)
    x_ssm = _glu(y, w_glu.reshape(sw, 2 * sw), b_glu.reshape(1, 2 * sw), zs, tmm, tn)
    merged = _merge(xa, x_ssm, w_pa.reshape(aw, d), w_ps.reshape(sw, d), gate, tmm, tn)
    w_out_b = w_out[0].astype(BF16)
    tno = _pick(d, (512, 256))
    y_prompt = _out(merged, 0, w_out_b, xp, final_g, tm, tno).reshape(batch, seq, d)
    y_sample = _out(merged, mp // tm, w_out_b, xs, final_g, tm, tno).reshape(dec_batch, dec_seq, d)

    keep = min(WINDOW, seq)
    last_rows = lambda a: a[:mp].reshape(batch, seq, g, HEAD_DIM)[:, seq - keep:][None]
    dec_rows = lambda a: a[mp:].reshape(dec_batch, dec_seq, g, HEAD_DIM)[None]
    fin = fin.reshape(batch + dec_batch, ng, 2, p)
    return (y_prompt, y_sample, last_rows(k_f), last_rows(v_f), fin[:batch, :, 0][None], fin[:batch, :, 1][None],
            dec_rows(k_f), dec_rows(v_f), fin[batch:, :, 0][None], fin[batch:, :, 1][None])
```

```python
import functools
import math

import jax
import jax.numpy as jnp
from jax import lax
from jax.experimental import pallas as pl
from jax.experimental.pallas import tpu as pltpu

CHUNK = 64
WINDOW = 128
HEAD_DIM = 64
GQA_GROUPING = 8
SSM_GROUP = 16
PAST_LEN = 1024
ROPE_THETA = 10000.0
NORM_EPS = 1e-5
LAMBDA_RE_MAX = -1e-4
LOG2_E = math.log2(math.e)

SSM_CHUNK = 16
SSM_POW_BITS = 4
LANES = 128
V7X_VMEM_BYTES = 64 * 1024 * 1024
BF16 = jnp.bfloat16
F32 = jnp.float32


def _pick(n, prefs):
    for p in prefs:
        if n % p == 0:
            return p
    raise ValueError(f"no tile in {prefs} divides {n}")


def _params(sem, vmem_bytes):
    limit = min(int(vmem_bytes), V7X_VMEM_BYTES - 4 * 1024 * 1024)
    return pltpu.CompilerParams(dimension_semantics=sem, vmem_limit_bytes=limit)


def _sigmoid(x):
    return 0.5 * jnp.tanh(0.5 * x) + 0.5


def _nbytes(shape, dtype):
    return math.prod(shape) * jnp.dtype(dtype).itemsize


def _vmem_estimate(blocks, temps=()):
    return 2 * (2 * sum(_nbytes(s, d) for s, d in blocks) + sum(_nbytes(s, d) for s, d in temps))


def _rmsnorm_kernel(xp_ref, xs_ref, g_ref, o_ref, *, n_prompt_blocks):
    i = pl.program_id(0)

    def norm(x_ref):
        x = x_ref[...]
        y = x * lax.rsqrt(jnp.mean(x * x, axis=-1, keepdims=True) + NORM_EPS)
        o_ref[...] = (y * g_ref[...]).astype(o_ref.dtype)

    @pl.when(i < n_prompt_blocks)
    def _():
        norm(xp_ref)

    @pl.when(i >= n_prompt_blocks)
    def _():
        norm(xs_ref)


def _rmsnorm(xp, xs, g, tm):
    mp, d = xp.shape
    ms = xs.shape[0]
    npb, nsb = mp // tm, ms // tm
    return pl.pallas_call(
        functools.partial(_rmsnorm_kernel, n_prompt_blocks=npb),
        grid=(npb + nsb,),
        in_specs=[
            pl.BlockSpec((tm, d), lambda i: (jnp.minimum(i, npb - 1), 0)),
            pl.BlockSpec((tm, d), lambda i: (jnp.maximum(i - npb, 0), 0)),
            pl.BlockSpec((1, d), lambda i: (0, 0)),
        ],
        out_specs=pl.BlockSpec((tm, d), lambda i: (i, 0)),
        out_shape=jax.ShapeDtypeStruct((mp + ms, d), BF16),
        compiler_params=_params(("arbitrary",), _vmem_estimate([((tm, d), F32)] * 2 + [((tm, d), BF16)], [((tm, d), F32)])),
        name="rmsnorm_in",
    )(xp, xs, g.reshape(1, d))


def _rope(acc, cos_ref, sin_ref):
    tm, tn = acc.shape
    reps = tn // LANES
    cos = jnp.tile(cos_ref[...], (1, reps))
    sin = jnp.tile(sin_ref[...], (1, reps))
    lane = lax.broadcasted_iota(jnp.int32, (tm, tn), 1)
    low = (lane % HEAD_DIM) < (HEAD_DIM // 2)
    partner = jnp.where(low, pltpu.roll(acc, tn - HEAD_DIM // 2, 1), pltpu.roll(acc, HEAD_DIM // 2, 1))
    return acc * cos + partner * sin


def _store_heads(o_ref, val):
    for h in range(val.shape[1] // HEAD_DIM):
        o_ref[h] = val[:, h * HEAD_DIM:(h + 1) * HEAD_DIM].astype(o_ref.dtype)


def _cast_weight(w_ref, wb_ref):
    @pl.when(pl.program_id(1) == 0)
    def _():
        wb_ref[...] = w_ref[...].astype(wb_ref.dtype)


def _row_parts(tm):
    n = 2 if tm % 32 == 0 and tm >= 1024 else 1
    return [slice(r * (tm // n), (r + 1) * (tm // n)) for r in range(n)]


def _proj_q_kernel(h_ref, w_ref, cos_ref, sin_ref, q_ref, wb_ref):
    _cast_weight(w_ref, wb_ref)
    for rows in _row_parts(h_ref.shape[0]):
        acc = jnp.dot(h_ref[rows, :], wb_ref[...], preferred_element_type=F32)
        val = _rope(acc, cos_ref.at[rows, :], sin_ref.at[rows, :]) * (HEAD_DIM ** -0.5 * LOG2_E)
        for h in range(val.shape[1] // HEAD_DIM):
            q_ref[h, rows, :] = val[:, h * HEAD_DIM:(h + 1) * HEAD_DIM].astype(q_ref.dtype)


def _proj_kv_kernel(h_ref, w_ref, cos_ref, sin_ref, kf_ref, vf_ref, kh_ref, vh_ref, wb_ref, *, kvw):
    _cast_weight(w_ref, wb_ref)
    acc = jnp.dot(h_ref[...], wb_ref[...], preferred_element_type=F32)
    k = _rope(acc[:, :kvw], cos_ref, sin_ref)
    v = acc[:, kvw:]
    kf_ref[...] = k
    vf_ref[...] = v
    _store_heads(kh_ref, k)
    _store_heads(vh_ref, v)


def _proj_act_kernel(h_ref, w_ref, o_ref, wb_ref, *, act):
    _cast_weight(w_ref, wb_ref)
    for rows in _row_parts(h_ref.shape[0]):
        acc = jnp.dot(h_ref[rows, :], wb_ref[...], preferred_element_type=F32)
        if act == "silu":
            acc = acc * _sigmoid(acc)
        elif act == "sigmoid":
            acc = _sigmoid(acc)
        o_ref[rows, :] = acc.astype(o_ref.dtype)


def _proj_u_kernel(h_ref, w_ref, o_ref, wb_ref, *, n_real):
    _cast_weight(w_ref, wb_ref)
    i = pl.program_id(1)

    @pl.when(i < n_real)
    def _():
        o_ref[...] = jnp.dot(h_ref[...], wb_ref[...], preferred_element_type=F32)

    @pl.when(i >= n_real)
    def _():
        o_ref[...] = jnp.zeros(o_ref.shape, o_ref.dtype)


def _proj_specs(d, tm, tn, col0):
    assert col0 % tn == 0
    cb0 = col0 // tn
    return [pl.BlockSpec((tm, d), lambda j, i: (i, 0)), pl.BlockSpec((d, tn), lambda j, i: (0, cb0 + j))]


def _proj_vmem(d, tm, tn, outs):
    return _vmem_estimate([((tm, d), BF16), ((d, tn), F32)] + outs, [((tm, tn), F32)] * 3 + [((d, tn), BF16)])


def _proj_q(h, w, cos, sin, aw, tm, tn):
    m, d = h.shape
    nh = aw // HEAD_DIM
    tab = pl.BlockSpec((tm, LANES), lambda j, i: (i, 0))
    return pl.pallas_call(
        _proj_q_kernel,
        grid=(aw // tn, m // tm),
        in_specs=_proj_specs(d, tm, tn, 0) + [tab, tab],
        out_specs=pl.BlockSpec((tn // HEAD_DIM, tm, HEAD_DIM), lambda j, i: (j, i, 0)),
        out_shape=jax.ShapeDtypeStruct((nh, m, HEAD_DIM), BF16),
        scratch_shapes=[pltpu.VMEM((d, tn), BF16)],
        compiler_params=_params(("arbitrary", "arbitrary"), _proj_vmem(d, tm, tn, [((tm, 2 * tn), BF16), ((tm, 2 * LANES), F32)])),
        name="proj_q",
    )(h, w, cos, sin)


def _proj_kv(h, w, cos, sin, col0, kvw, tm):
    m, d = h.shape
    g = kvw // HEAD_DIM
    tn = 2 * kvw
    tab = pl.BlockSpec((tm, LANES), lambda j, i: (i, 0))
    flat = pl.BlockSpec((tm, kvw), lambda j, i: (i, 0))
    heads = pl.BlockSpec((g, tm, HEAD_DIM), lambda j, i: (0, i, 0))
    return pl.pallas_call(
        functools.partial(_proj_kv_kernel, kvw=kvw),
        grid=(1, m // tm),
        in_specs=_proj_specs(d, tm, tn, col0) + [tab, tab],
        out_specs=[flat, flat, heads, heads],
        out_shape=[jax.ShapeDtypeStruct((m, kvw), F32)] * 2 + [jax.ShapeDtypeStruct((g, m, HEAD_DIM), BF16)] * 2,
        scratch_shapes=[pltpu.VMEM((d, tn), BF16)],
        compiler_params=_params(("arbitrary", "arbitrary"), _proj_vmem(d, tm, tn, [((tm, 2 * tn), F32), ((tm, 2 * tn), BF16), ((tm, 2 * LANES), F32)])),
        name="proj_kv",
    )(h, w, cos, sin)


def _proj_act(h, w, col0, ncols, act, tm, tn, name):
    m, d = h.shape
    return pl.pallas_call(
        functools.partial(_proj_act_kernel, act=act),
        grid=(ncols // tn, m // tm),
        in_specs=_proj_specs(d, tm, tn, col0),
        out_specs=pl.BlockSpec((tm, tn), lambda j, i: (i, j)),
        out_shape=jax.ShapeDtypeStruct((m, ncols), BF16),
        scratch_shapes=[pltpu.VMEM((d, tn), BF16)],
        compiler_params=_params(("arbitrary", "arbitrary"), _proj_vmem(d, tm, tn, [((tm, tn), BF16)])),
        name=name,
    )(h, w)


def _proj_u(h, w, col0, ncols, m_pad, tm, tn):
    m, d = h.shape
    n_real = m // tm
    cb0 = col0 // tn
    return pl.pallas_call(
        functools.partial(_proj_u_kernel, n_real=n_real),
        grid=(ncols // tn, m_pad // tm),
        in_specs=[pl.BlockSpec((tm, d), lambda j, i: (jnp.minimum(i, n_real - 1), 0)), pl.BlockSpec((d, tn), lambda j, i: (0, cb0 + j))],
        out_specs=pl.BlockSpec((tm, tn), lambda j, i: (i, j)),
        out_shape=jax.ShapeDtypeStruct((m_pad, ncols), F32),
        scratch_shapes=[pltpu.VMEM((d, tn), BF16)],
        compiler_params=_params(("arbitrary", "arbitrary"), _proj_vmem(d, tm, tn, [((tm, tn), F32)])),
        name="proj_u",
    )(h, w)


def _attn_weights(q, kw, sink, n_valid):
    st = lax.dot_general(kw, q, (((1,), (1,)), ((), ())), preferred_element_type=F32)
    if n_valid is not None:
        row = lax.broadcasted_iota(jnp.int32, st.shape, 0)
        st = jnp.where(row < n_valid, st, -jnp.inf)
    m = jnp.maximum(jnp.max(st, axis=0, keepdims=True), sink)
    e = jnp.exp2(st - m)
    return e.astype(BF16), jnp.sum(e, axis=0, keepdims=True) + jnp.exp2(sink - m)


def _attn_values(vw, e, denom):
    ot = jnp.dot(vw.astype(F32).T.astype(BF16), e, preferred_element_type=F32) / denom
    return ot.T


def _store_unit(o_ref, za_ref, row0, o, rep):
    for r in range(0, rep, 2):
        pair = jnp.concatenate([o[r * CHUNK:(r + 1) * CHUNK], o[(r + 1) * CHUNK:(r + 2) * CHUNK]], axis=1)
        gate = za_ref[row0:row0 + CHUNK, r * HEAD_DIM:(r + 2) * HEAD_DIM].astype(F32)
        o_ref[row0:row0 + CHUNK, r * HEAD_DIM:(r + 2) * HEAD_DIM] = (pair * gate).astype(o_ref.dtype)


def _attn_kernel(q_ref, k_ref, v_ref, ks_ref, vs_ref, sink_ref, za_ref, o_ref, *, n_units, rep, steps_per_stream, n_prompt_steps):
    step = pl.program_id(1)
    win = WINDOW + CHUNK
    sink = sink_ref[0] * LOG2_E

    def run(windows):
        staged = []
        for c, (kw, vw, n_valid) in enumerate(windows):
            q = q_ref[:, c * CHUNK:(c + 1) * CHUNK, :].reshape(rep * CHUNK, HEAD_DIM)
            staged.append((vw,) + _attn_weights(q, kw, sink, n_valid))
        for c in range(n_units):
            _store_unit(o_ref, za_ref, c * CHUNK, _attn_values(*staged[c]), rep)

    @pl.when(step < n_prompt_steps)
    def _():
        cb = step % steps_per_stream
        windows = []
        for c in range(n_units):
            start = pl.multiple_of(jnp.maximum(cb * n_units + c - WINDOW // CHUNK, 0) * CHUNK, CHUNK)
            n_valid = jnp.where(cb == 0, (c + 1) * CHUNK, win) if c < WINDOW // CHUNK else None
            windows.append((k_ref[0, pl.ds(start, win), :], v_ref[0, pl.ds(start, win), :], n_valid))
        run(windows)

    @pl.when(step == n_prompt_steps)
    def _():
        run([(ks_ref[0, b * win:(b + 1) * win, :], vs_ref[0, b * win:(b + 1) * win, :], None) for b in range(n_units)])


def _attention(q_hm, k_hm, v_hm, ks, vs, sink_rows, za, batch, seq, dec_batch, aw):
    nh, m, _ = q_hm.shape
    g = k_hm.shape[0]
    rep = nh // g
    win = WINDOW + CHUNK
    n_units = dec_batch
    rows = n_units * CHUNK
    assert seq % rows == 0 and n_units >= WINDOW // CHUNK
    steps_per_stream = seq // rows
    n_prompt_steps = batch * steps_per_stream
    blocks = ([((rep, rows, LANES), BF16)] + [((seq, LANES), BF16)] * 2 + [((dec_batch * win, LANES), BF16)] * 2
              + [((rows, rep * HEAD_DIM), BF16)] * 2)
    temps = [((rep * CHUNK, 2 * LANES), F32)] * (4 * n_units)
    stream = lambda gi, s: (gi, jnp.minimum(s // steps_per_stream, batch - 1), 0)
    return pl.pallas_call(
        functools.partial(_attn_kernel, n_units=n_units, rep=rep, steps_per_stream=steps_per_stream, n_prompt_steps=n_prompt_steps),
        grid=(g, n_prompt_steps + 1),
        in_specs=[
            pl.BlockSpec((rep, rows, HEAD_DIM), lambda gi, s: (gi, s, 0)),
            pl.BlockSpec((1, seq, HEAD_DIM), stream),
            pl.BlockSpec((1, seq, HEAD_DIM), stream),
            pl.BlockSpec((1, dec_batch * win, HEAD_DIM), lambda gi, s: (gi, 0, 0)),
            pl.BlockSpec((1, dec_batch * win, HEAD_DIM), lambda gi, s: (gi, 0, 0)),
            pl.BlockSpec((1, 1, rep * CHUNK), lambda gi, s: (gi, 0, 0)),
            pl.BlockSpec((rows, rep * HEAD_DIM), lambda gi, s: (s, gi)),
        ],
        out_specs=pl.BlockSpec((rows, rep * HEAD_DIM), lambda gi, s: (s, gi)),
        out_shape=jax.ShapeDtypeStruct((m, aw), BF16),
        compiler_params=_params(("arbitrary",) * 2, _vmem_estimate(blocks, temps)),
        name="attention",
    )(q_hm, k_hm, v_hm, ks, vs, sink_rows, za)


def _ssm_disc_kernel(lre_ref, lim_ref, ldt_ref, are_ref, aim_ref, dre_ref, dim_ref, fre_ref, fim_ref):
    lr = jnp.minimum(lre_ref[...], LAMBDA_RE_MAX)
    li = lim_ref[...]
    dt = jnp.exp(ldt_ref[...])
    mag = jnp.exp(lr * dt)
    a_re = mag * jnp.cos(li * dt)
    a_im = mag * jnp.sin(li * dt)
    den = lr * lr + li * li
    nr = a_re - 1.0
    fre_ref[...] = (nr * lr + a_im * li) / den
    fim_ref[...] = (a_im * lr - nr * li) / den
    are_ref[...] = a_re
    aim_ref[...] = a_im
    for _ in range(SSM_POW_BITS):
        a_re, a_im = a_re * a_re - a_im * a_im, 2.0 * a_re * a_im
    dre_ref[...] = a_re
    dim_ref[...] = a_im


def _cmul(ar, ai, br, bi):
    return ar * br - ai * bi, ar * bi + ai * br


def _ssm_build_kernel(are_ref, aim_ref, dre_ref, dim_ref, fre_ref, fim_ref, btr_ref, bti_ref, cr_ref, ci_ref, dv_ref,
                      met_ref, ft_ref, a1_ref, a2_ref, a2s_ref, *, groups):
    lc = SSM_CHUNK * SSM_GROUP
    sub = lax.broadcasted_iota(jnp.int32, (SSM_GROUP, lc), 0)
    lane = lax.broadcasted_iota(jnp.int32, (SSM_GROUP, lc), 1)
    for gi in range(groups):
        row = slice(gi, gi + 1)
        a_re, a_im = are_ref[row, :], aim_ref[row, :]
        pw = [(jnp.ones_like(a_re), jnp.zeros_like(a_re))]
        for _ in range(SSM_CHUNK):
            pw.append(_cmul(pw[-1][0], pw[-1][1], a_re, a_im))
        c_re, c_im = cr_ref[gi], ci_ref[gi]
        wt = [_cmul(pr, pi, c_re, c_im) for pr, pi in pw]
        wt_re = jnp.concatenate([w[0] for w in wt[:SSM_CHUNK]], axis=0)
        wt_im = jnp.concatenate([w[1] for w in wt[:SSM_CHUNK]], axis=0)
        et_re = jnp.concatenate([w[0] for w in wt[1:]], axis=0)
        et_im = jnp.concatenate([w[1] for w in wt[1:]], axis=0)
        bb_re, bb_im = _cmul(fre_ref[row, :], fim_ref[row, :], btr_ref[gi], bti_ref[gi])
        r0 = lax.dot_general(jnp.concatenate([bb_re, bb_im], axis=1), jnp.concatenate([wt_re, -wt_im], axis=1),
                             (((1,), (1,)), ((), ())), preferred_element_type=F32, precision=lax.Precision.HIGHEST)
        r0 = r0 + jnp.where(sub == lane, dv_ref[gi], 0.0)
        rows = [r0] + [jnp.where(lane >= s * SSM_GROUP, pltpu.roll(r0, s * SSM_GROUP, 1), 0.0) for s in range(1, SSM_CHUNK)]
        mt = jnp.concatenate(rows, axis=0).T
        met_ref[gi] = jnp.concatenate([mt, et_re, -et_im], axis=1).astype(met_ref.dtype)
        fb = [_cmul(pw[SSM_CHUNK - 1 - s][0], pw[SSM_CHUNK - 1 - s][1], bb_re, bb_im) for s in range(SSM_CHUNK)]
        f_all = jnp.concatenate([jnp.concatenate([x[0] for x in fb], axis=0), jnp.concatenate([x[1] for x in fb], axis=0)], axis=1)
        ft_ref[gi] = f_all.T.astype(ft_ref.dtype)
        d_re, d_im = dre_ref[row, :], dim_ref[row, :]
        a1_ref[gi] = jnp.concatenate([d_re, d_re], axis=1)
        a2_ref[gi] = jnp.concatenate([-d_im, d_im], axis=1)
        a2s_ref[gi] = jnp.concatenate([d_im, -d_im], axis=1)


def _ssm_params(lambda_re, lambda_im, log_dt, b_re, b_im, c_re, c_im, d_skip):
    ng, p = lambda_re.shape
    lc = SSM_CHUNK * SSM_GROUP
    full = pl.BlockSpec((ng, p), lambda: (0, 0))
    disc = pl.pallas_call(
        _ssm_disc_kernel,
        in_specs=[full, full, pl.BlockSpec((ng, 1), lambda: (0, 0))],
        out_specs=[full] * 6,
        out_shape=[jax.ShapeDtypeStruct((ng, p), F32)] * 6,
        name="ssm_disc",
    )(lambda_re, lambda_im, log_dt.reshape(ng, 1))
    gb = _pick(ng, (8,))
    bt_re = jnp.swapaxes(b_re, 1, 2)
    bt_im = jnp.swapaxes(b_im, 1, 2)
    dvec = jnp.pad(d_skip, ((0, 0), (0, lc - SSM_GROUP))).reshape(ng, 1, lc)
    rows = pl.BlockSpec((gb, p), lambda i: (i, 0))
    mats = pl.BlockSpec((gb, SSM_GROUP, p), lambda i: (i, 0, 0))
    dec = pl.BlockSpec((gb, 1, 2 * p), lambda i: (i, 0, 0))
    blocks = ([((gb, LANES), F32)] * 6 + [((gb, SSM_GROUP, LANES), F32)] * 4 + [((gb, 8, lc), F32)]
              + [((gb, lc, lc + 2 * p), BF16), ((gb, 2 * p, lc), BF16)] + [((gb, 8, LANES), F32)] * 3)
    met, ft, a1, a2, a2s = pl.pallas_call(
        functools.partial(_ssm_build_kernel, groups=gb),
        grid=(ng // gb,),
        in_specs=[rows] * 6 + [mats] * 4 + [pl.BlockSpec((gb, 1, lc), lambda i: (i, 0, 0))],
        out_specs=[pl.BlockSpec((gb, lc, lc + 2 * p), lambda i: (i, 0, 0)), pl.BlockSpec((gb, 2 * p, lc), lambda i: (i, 0, 0)), dec, dec, dec],
        out_shape=[jax.ShapeDtypeStruct((ng, lc, lc + 2 * p), BF16), jax.ShapeDtypeStruct((ng, 2 * p, lc), BF16)]
        + [jax.ShapeDtypeStruct((ng, 1, 2 * p), F32)] * 3,
        compiler_params=_params(("arbitrary",), _vmem_estimate(blocks, [((lc, lc + 2 * p), F32)] * 8)),
        name="ssm_build",
    )(*disc, bt_re, bt_im, c_re, c_im, dvec)
    flat = lambda a: a.reshape(1, ng * 2 * p)
    return met, ft, flat(a1), flat(a2), flat(a2s)


def _chunk_major_inputs(u_ref, n_blocks):
    rows = LANES * SSM_CHUNK
    return [[u_ref[pl.ds(kb * rows + s, LANES, stride=SSM_CHUNK), :].T.astype(BF16) for kb in range(n_blocks)]
            for s in range(SSM_CHUNK)]


def _group_operand(xt, gi, n_blocks):
    return [jnp.concatenate([xt[s][kb][gi * SSM_GROUP:(gi + 1) * SSM_GROUP, :] for kb in range(n_blocks)], axis=1)
            for s in range(SSM_CHUNK)]


def _ssm_state_in_kernel(u_ref, ft_ref, s_ref, *, n_blocks, p):
    groups = LANES // SSM_GROUP
    xt = _chunk_major_inputs(u_ref, n_blocks)
    for gi in range(groups):
        rhs = jnp.concatenate(_group_operand(xt, gi, n_blocks), axis=0)
        st = jnp.dot(ft_ref[gi], rhs, preferred_element_type=F32)
        for kb in range(n_blocks):
            s_ref[kb * LANES:(kb + 1) * LANES, gi * 2 * p:(gi + 1) * 2 * p] = st[:, kb * LANES:(kb + 1) * LANES].T


def _ssm_out_kernel(u_ref, h_ref, met_ref, y_ref, *, n_blocks, p):
    groups = LANES // SSM_GROUP
    xt = _chunk_major_inputs(u_ref, n_blocks)
    zt = [[None] * groups for _ in range(SSM_CHUNK)]
    for gi in range(groups):
        ht = jnp.concatenate([h_ref[kb * LANES:(kb + 1) * LANES, gi * 2 * p:(gi + 1) * 2 * p].T for kb in range(n_blocks)], axis=1)
        rhs = jnp.concatenate(_group_operand(xt, gi, n_blocks) + [ht.astype(BF16)], axis=0)
        yt = jnp.dot(met_ref[gi], rhs, preferred_element_type=F32)
        for t in range(SSM_CHUNK):
            zt[t][gi] = yt[t * SSM_GROUP:(t + 1) * SSM_GROUP, :]
    rows = LANES * SSM_CHUNK
    for t in range(SSM_CHUNK):
        z = jnp.concatenate(zt[t], axis=0)
        for kb in range(n_blocks):
            y_ref[pl.ds(kb * rows + t, LANES, stride=SSM_CHUNK), :] = z[:, kb * LANES:(kb + 1) * LANES].T


def _ssm_scan_kernel(s_ref, h0_ref, a1_ref, a2_ref, a2s_ref, h_ref, fin_ref, ss_ref, *, batch, kp, dec_batch, ks, p):
    def swap_halves(x):
        lane = lax.broadcasted_iota(jnp.int32, x.shape, 1)
        return jnp.where(lane % (2 * p) < p, pltpu.roll(x, x.shape[1] - p, 1), pltpu.roll(x, p, 1))

    ss_ref[...] = swap_halves(s_ref[...])
    a1, a2, a2s = a1_ref[...], a2_ref[...], a2s_ref[...]
    zero = jnp.zeros_like(a1)

    def step(k, h, hs):
        h_ref[pl.ds(k, 1), :] = h
        s = s_ref[pl.ds(k, 1), :]
        ss = ss_ref[pl.ds(k, 1), :]
        return a1 * h + a2 * hs + s, a1 * hs + a2s * h + ss

    def body(k, carry):
        out = []
        for b in range(batch):
            out.extend(step(b * kp + k, carry[2 * b], carry[2 * b + 1]))
        return tuple(out)

    fin = lax.fori_loop(0, kp, body, (zero,) * (2 * batch))
    for b in range(batch):
        fin_ref[b:b + 1, :] = fin[2 * b]
    h0s_all = swap_halves(h0_ref[...])
    for b in range(dec_batch):
        h, hs = h0_ref[b:b + 1, :], h0s_all[b:b + 1, :]
        for k in range(ks):
            h, hs = step(batch * kp + b * ks + k, h, hs)
        fin_ref[batch + b:batch + b + 1, :] = h
    n_real = batch * kp + dec_batch * ks
    if n_real < h_ref.shape[0]:
        h_ref[n_real:, :] = jnp.zeros((h_ref.shape[0] - n_real, h_ref.shape[1]), F32)


def _ssm(u, met, ft, a1, a2, a2s, h0, batch, kp, dec_batch, ks):
    m_pad, sw = u.shape
    ng, p2, lc = ft.shape
    p = p2 // 2
    nk_pad = m_pad // SSM_CHUNK
    n_blocks = nk_pad // LANES
    groups = LANES // SSM_GROUP
    u_spec = pl.BlockSpec((m_pad, LANES), lambda i: (0, i))
    st_spec = pl.BlockSpec((nk_pad, groups * p2), lambda i: (0, i))
    temps = [((lc + p2, nk_pad), BF16), ((lc, nk_pad), F32), ((m_pad, LANES), BF16)]
    s = pl.pallas_call(
        functools.partial(_ssm_state_in_kernel, n_blocks=n_blocks, p=p),
        grid=(ng // groups,),
        in_specs=[u_spec, pl.BlockSpec((groups, p2, lc), lambda i: (i, 0, 0))],
        out_specs=st_spec,
        out_shape=jax.ShapeDtypeStruct((nk_pad, ng * p2), F32),
        compiler_params=_params(("arbitrary",), _vmem_estimate([((m_pad, LANES), F32), ((groups, p2, lc), BF16), ((nk_pad, groups * p2), F32)], temps)),
        name="ssm_state_in",
    )(u, ft)
    lb = _pick(ng * p2, (1024, 512, 256, 128))
    nseq = batch + dec_batch
    rowb = pl.BlockSpec((1, lb), lambda i: (0, i))
    h, fin = pl.pallas_call(
        functools.partial(_ssm_scan_kernel, batch=batch, kp=kp, dec_batch=dec_batch, ks=ks, p=p),
        grid=(ng * p2 // lb,),
        in_specs=[pl.BlockSpec((nk_pad, lb), lambda i: (0, i)), pl.BlockSpec((dec_batch, lb), lambda i: (0, i)), rowb, rowb, rowb],
        out_specs=[pl.BlockSpec((nk_pad, lb), lambda i: (0, i)), pl.BlockSpec((nseq, lb), lambda i: (0, i))],
        out_shape=[jax.ShapeDtypeStruct((nk_pad, ng * p2), F32), jax.ShapeDtypeStruct((nseq, ng * p2), F32)],
        scratch_shapes=[pltpu.VMEM((nk_pad, lb), F32)],
        compiler_params=_params(("arbitrary",), _vmem_estimate([((nk_pad, lb), F32)] * 2, [((nk_pad, lb), F32)])),
        name="ssm_scan",
    )(s, h0, a1, a2, a2s)
    y = pl.pallas_call(
        functools.partial(_ssm_out_kernel, n_blocks=n_blocks, p=p),
        grid=(ng // groups,),
        in_specs=[u_spec, st_spec, pl.BlockSpec((groups, lc, lc + p2), lambda i: (i, 0, 0))],
        out_specs=u_spec,
        out_shape=jax.ShapeDtypeStruct((m_pad, sw), F32),
        compiler_params=_params(("arbitrary",), _vmem_estimate(
            [((m_pad, LANES), F32)] * 2 + [((nk_pad, groups * p2), F32), ((groups, lc, lc + p2), BF16)], temps)),
        name="ssm_out",
    )(u, h, met)
    return y, fin


def _glu_kernel(y_ref, wa_ref, wg_ref, ba_ref, bg_ref, zs_ref, o_ref, wab_ref, wgb_ref):
    _cast_weight(wa_ref, wab_ref)
    _cast_weight(wg_ref, wgb_ref)
    y = y_ref[...].astype(BF16)
    a = jnp.dot(y, wab_ref[...], preferred_element_type=F32) + ba_ref[...]
    g = jnp.dot(y, wgb_ref[...], preferred_element_type=F32) + bg_ref[...]
    o_ref[...] = (a * _sigmoid(g) * zs_ref[...].astype(F32)).astype(o_ref.dtype)


def _glu(y, w_glu, b_glu, zs, tm, tn):
    m, sw = zs.shape
    nb = sw // tn
    blocks = [((tm, sw), F32), ((sw, tn), F32), ((sw, tn), F32), ((tm, tn), BF16), ((tm, tn), BF16)]
    return pl.pallas_call(
        _glu_kernel,
        grid=(nb, m // tm),
        in_specs=[
            pl.BlockSpec((tm, sw), lambda j, i: (i, 0)),
            pl.BlockSpec((sw, tn), lambda j, i: (0, j)),
            pl.BlockSpec((sw, tn), lambda j, i: (0, nb + j)),
            pl.BlockSpec((1, tn), lambda j, i: (0, j)),
            pl.BlockSpec((1, tn), lambda j, i: (0, nb + j)),
            pl.BlockSpec((tm, tn), lambda j, i: (i, j)),
        ],
        out_specs=pl.BlockSpec((tm, tn), lambda j, i: (i, j)),
        out_shape=jax.ShapeDtypeStruct((m, sw), BF16),
        scratch_shapes=[pltpu.VMEM((sw, tn), BF16)] * 2,
        compiler_params=_params(("arbitrary", "arbitrary"), _vmem_estimate(blocks, [((tm, tn), F32)] * 4 + [((sw, tn), BF16)] * 2)),
        name="glu",
    )(y, w_glu, w_glu, b_glu, b_glu, zs)


def _merge_kernel(xa_ref, xs_ref, wpa_ref, wps_ref, ga_ref, gs_ref, o_ref, wpab_ref, wpsb_ref):
    _cast_weight(wpa_ref, wpab_ref)
    _cast_weight(wps_ref, wpsb_ref)
    br_a = jnp.dot(xa_ref[...], wpab_ref[...], preferred_element_type=F32)
    br_s = jnp.dot(xs_ref[...], wpsb_ref[...], preferred_element_type=F32)
    o_ref[...] = (ga_ref[...].astype(F32) * br_a + gs_ref[...].astype(F32) * br_s).astype(o_ref.dtype)


def _merge(xa, xs, w_pa, w_ps, gate, tm, tn):
    m, aw = xa.shape
    sw = xs.shape[1]
    d = w_pa.shape[1]
    nb = d // tn
    blocks = [((tm, aw), BF16), ((tm, sw), BF16), ((aw, tn), F32), ((sw, tn), F32)] + [((tm, tn), BF16)] * 3
    return pl.pallas_call(
        _merge_kernel,
        grid=(nb, m // tm),
        in_specs=[
            pl.BlockSpec((tm, aw), lambda j, i: (i, 0)),
            pl.BlockSpec((tm, sw), lambda j, i: (i, 0)),
            pl.BlockSpec((aw, tn), lambda j, i: (0, j)),
            pl.BlockSpec((sw, tn), lambda j, i: (0, j)),
            pl.BlockSpec((tm, tn), lambda j, i: (i, j)),
            pl.BlockSpec((tm, tn), lambda j, i: (i, nb + j)),
        ],
        out_specs=pl.BlockSpec((tm, tn), lambda j, i: (i, j)),
        out_shape=jax.ShapeDtypeStruct((m, d), BF16),
        scratch_shapes=[pltpu.VMEM((aw, tn), BF16), pltpu.VMEM((sw, tn), BF16)],
        compiler_params=_params(("arbitrary", "arbitrary"),
                                _vmem_estimate(blocks, [((tm, tn), F32)] * 4 + [((aw, tn), BF16), ((sw, tn), BF16)])),
        name="merge",
    )(xa, xs, w_pa, w_ps, gate, gate)


def _out_kernel(mg_ref, w_ref, x_ref, g_ref, o_ref, acc_ref, ssq_ref, *, n_col_blocks, tn):
    j = pl.program_id(1)
    blk = x_ref[...] + jnp.dot(mg_ref[...], w_ref[...], preferred_element_type=F32)
    acc_ref[j] = blk
    part = jnp.sum(blk * blk, axis=-1, keepdims=True)

    @pl.when(j == 0)
    def _():
        ssq_ref[...] = part

    @pl.when(j > 0)
    def _():
        ssq_ref[...] += part

    @pl.when(j == n_col_blocks - 1)
    def _():
        scale = lax.rsqrt(ssq_ref[...] / (n_col_blocks * tn) + NORM_EPS)
        for c in range(n_col_blocks):
            o_ref[:, c * tn:(c + 1) * tn] = acc_ref[c] * scale * g_ref[:, c * tn:(c + 1) * tn]


def _out(merged, row_block0, w_out, x, final_g, tm, tn):
    mx, d = x.shape
    nb = d // tn
    blocks = [((tm, d), BF16), ((d, tn), BF16), ((tm, tn), F32), ((tm, d), F32)]
    return pl.pallas_call(
        functools.partial(_out_kernel, n_col_blocks=nb, tn=tn),
        grid=(mx // tm, nb),
        in_specs=[
            pl.BlockSpec((tm, d), lambda i, j: (row_block0 + i, 0)),
            pl.BlockSpec((d, tn), lambda i, j: (0, j)),
            pl.BlockSpec((tm, tn), lambda i, j: (i, j)),
            pl.BlockSpec((1, d), lambda i, j: (0, 0)),
        ],
        out_specs=pl.BlockSpec((tm, d), lambda i, j: (i, 0)),
        out_shape=jax.ShapeDtypeStruct((mx, d), F32),
        scratch_shapes=[pltpu.VMEM((nb, tm, tn), F32), pltpu.VMEM((tm, 1), F32)],
        compiler_params=_params(("arbitrary", "arbitrary"), _vmem_estimate(blocks, [((tm, d), F32), ((tm, tn), F32), ((tm, LANES), F32)])),
        name="out_norm",
    )(merged, w_out, x, final_g.reshape(1, d))


def _rope_tables(positions):
    half = HEAD_DIM // 2
    inv_freq = ROPE_THETA ** (-jnp.arange(half, dtype=F32) / half)
    ang = positions.astype(F32)[:, None] * inv_freq[None, :]
    cos, sin = jnp.cos(ang), jnp.sin(ang)
    reps = LANES // HEAD_DIM
    return jnp.tile(jnp.concatenate([cos, cos], axis=1), (1, reps)), jnp.tile(jnp.concatenate([-sin, sin], axis=1), (1, reps))


def kernel(x_prompt, x_sample, cache_k, cache_v, state_ssm_re, state_ssm_im, norm_g, w_in, sink, lambda_re, lambda_im,
           log_dt, b_re, b_im, c_re, c_im, d_skip, w_glu, b_glu, w_pa, w_ps, w_out, final_g):
    depth = norm_g.shape[0]
    assert depth == 1, "one trunk layer"
    batch, seq, d = x_prompt.shape
    dec_batch, dec_seq, _ = x_sample.shape
    aw = w_pa.shape[1]
    sw = w_ps.shape[1]
    nh = aw // HEAD_DIM
    g = max(1, nh // GQA_GROUPING)
    rep = nh // g
    kvw = g * HEAD_DIM
    ng, p = lambda_re.shape[1:]
    assert dec_seq == CHUNK and cache_k.shape[2] == WINDOW and seq % (2 * CHUNK) == 0
    assert kvw % LANES == 0 and rep % 2 == 0 and sw == ng * SSM_GROUP and (1 << SSM_POW_BITS) == SSM_CHUNK
    assert 2 * p == LANES and sw % LANES == 0
    mp, ms = batch * seq, dec_batch * dec_seq
    m = mp + ms
    widths = (aw, 2 * kvw, aw, sw, sw, 2 * d)
    assert sum(widths) == w_in.shape[2]
    c_q, c_kv, c_za, c_u, c_zs, c_gate = (sum(widths[:n]) for n in range(len(widths)))
    tm = _pick(math.gcd(mp, ms), (512, 256, 128))
    tmp = max(t for t in range(16, 1153, 16) if m % t == 0)
    tn = _pick(math.gcd(c_kv, c_za, c_u, c_zs, c_gate, 2 * d), (512, 256, 128))

    xp = x_prompt.reshape(mp, d)
    xs = x_sample.reshape(ms, d)
    w_in2 = w_in.reshape(d, w_in.shape[2])
    positions = jnp.concatenate([jnp.tile(jnp.arange(seq, dtype=jnp.int32), batch),
                                 jnp.tile(PAST_LEN + jnp.arange(dec_seq, dtype=jnp.int32), dec_batch)])
    cos, sin = _rope_tables(positions)

    h = _rmsnorm(xp, xs, norm_g[0], tm)
    q_hm = _proj_q(h, w_in2, cos, sin, aw, tmp, tn)
    k_f, v_f, k_hm, v_hm = _proj_kv(h, w_in2, cos, sin, c_kv, kvw, tmp)
    za = _proj_act(h, w_in2, c_za, aw, "silu", tmp, tn, "proj_za")
    nk = m // SSM_CHUNK
    m_pad = -(-nk // LANES) * LANES * SSM_CHUNK
    u = _proj_u(h, w_in2, c_u, sw, m_pad, _pick(math.gcd(m, m_pad), (512, 256, 128)), tn)
    zs = _proj_act(h, w_in2, c_zs, sw, "silu", tmp, tn, "proj_zs")
    gate = _proj_act(h, w_in2, c_gate, 2 * d, "sigmoid", tmp, tn, "proj_gate")

    to_heads = lambda c: jnp.transpose(c[0], (2, 0, 1, 3)).astype(BF16)
    new_rows = lambda a: a[:, mp:].reshape(g, dec_batch, dec_seq, HEAD_DIM)
    ks = jnp.concatenate([to_heads(cache_k), new_rows(k_hm)], axis=2).reshape(g, dec_batch * (WINDOW + CHUNK), HEAD_DIM)
    vs = jnp.concatenate([to_heads(cache_v), new_rows(v_hm)], axis=2).reshape(g, dec_batch * (WINDOW + CHUNK), HEAD_DIM)
    sink_rows = jnp.repeat(sink[0].reshape(g, rep), CHUNK, axis=1).reshape(g, 1, rep * CHUNK)
    xa = _attention(q_hm, k_hm, v_hm, ks, vs, sink_rows, za, batch, seq, dec_batch, aw)

    met, ft, a1, a2, a2s = _ssm_params(lambda_re[0], lambda_im[0], log_dt[0], b_re[0], b_im[0], c_re[0], c_im[0], d_skip[0])
    h0 = jnp.concatenate([state_ssm_re[0], state_ssm_im[0]], axis=-1).reshape(dec_batch, ng * 2 * p)
    y, fin = _ssm(u, met, ft, a1, a2, a2s, h0, batch, seq // SSM_CHUNK, dec_batch, dec_seq // SSM_CHUNK)

    tmm = max(t for t in range(16, 577, 16) if m % t == 0)
    x_ssm = _glu(y, w_glu.reshape(sw, 2 * sw), b_glu.reshape(1, 2 * sw), zs, tmm, tn)
    merged = _merge(xa, x_ssm, w_pa.reshape(aw, d), w_ps.reshape(sw, d), gate, tmm, tn)
    w_out_b = w_out[0].astype(BF16)
    tno = _pick(d, (512, 256))
    y_prompt = _out(merged, 0, w_out_b, xp, final_g, tm, tno).reshape(batch, seq, d)
    y_sample = _out(merged, mp // tm, w_out_b, xs, final_g, tm, tno).reshape(dec_batch, dec_seq, d)

    keep = min(WINDOW, seq)
    last_rows = lambda a: a[:mp].reshape(batch, seq, g, HEAD_DIM)[:, seq - keep:][None]
    dec_rows = lambda a: a[mp:].reshape(dec_batch, dec_seq, g, HEAD_DIM)[None]
    fin = fin.reshape(batch + dec_batch, ng, 2, p)
    return (y_prompt, y_sample, last_rows(k_f), last_rows(v_f), fin[:batch, :, 0][None], fin[:batch, :, 1][None],
            dec_rows(k_f), dec_rows(v_f), fin[batch:, :, 0][None], fin[batch:, :, 1][None])
```

```python
import functools
import math

import jax
import jax.numpy as jnp
from jax import lax
from jax.experimental import pallas as pl
from jax.experimental.pallas import tpu as pltpu

CHUNK = 64
WINDOW = 128
HEAD_DIM = 64
GQA_GROUPING = 8
SSM_GROUP = 16
PAST_LEN = 1024
ROPE_THETA = 10000.0
NORM_EPS = 1e-5
LAMBDA_RE_MAX = -1e-4
LOG2_E = math.log2(math.e)

SSM_CHUNK = 16
SSM_POW_BITS = 4
LANES = 128
V7X_VMEM_BYTES = 64 * 1024 * 1024
BF16 = jnp.bfloat16
F32 = jnp.float32


def _pick(n, prefs):
    for p in prefs:
        if n % p == 0:
            return p
    raise ValueError(f"no tile in {prefs} divides {n}")


def _params(sem, vmem_bytes):
    limit = min(int(vmem_bytes), V7X_VMEM_BYTES - 4 * 1024 * 1024)
    return pltpu.CompilerParams(dimension_semantics=sem, vmem_limit_bytes=limit)


def _sigmoid(x):
    return 0.5 * jnp.tanh(0.5 * x) + 0.5


def _nbytes(shape, dtype):
    return math.prod(shape) * jnp.dtype(dtype).itemsize


def _vmem_estimate(blocks, temps=()):
    return 2 * (2 * sum(_nbytes(s, d) for s, d in blocks) + sum(_nbytes(s, d) for s, d in temps))


def _rmsnorm_kernel(xp_ref, xs_ref, g_ref, o_ref, *, n_prompt_blocks):
    i = pl.program_id(0)

    def norm(x_ref):
        x = x_ref[...]
        y = x * lax.rsqrt(jnp.mean(x * x, axis=-1, keepdims=True) + NORM_EPS)
        o_ref[...] = (y * g_ref[...]).astype(o_ref.dtype)

    @pl.when(i < n_prompt_blocks)
    def _():
        norm(xp_ref)

    @pl.when(i >= n_prompt_blocks)
    def _():
        norm(xs_ref)


def _rmsnorm(xp, xs, g, tm):
    mp, d = xp.shape
    ms = xs.shape[0]
    npb, nsb = mp // tm, ms // tm
    return pl.pallas_call(
        functools.partial(_rmsnorm_kernel, n_prompt_blocks=npb),
        grid=(npb + nsb,),
        in_specs=[
            pl.BlockSpec((tm, d), lambda i: (jnp.minimum(i, npb - 1), 0)),
            pl.BlockSpec((tm, d), lambda i: (jnp.maximum(i - npb, 0), 0)),
            pl.BlockSpec((1, d), lambda i: (0, 0)),
        ],
        out_specs=pl.BlockSpec((tm, d), lambda i: (i, 0)),
        out_shape=jax.ShapeDtypeStruct((mp + ms, d), BF16),
        compiler_params=_params(("arbitrary",), _vmem_estimate([((tm, d), F32)] * 2 + [((tm, d), BF16)], [((tm, d), F32)])),
        name="rmsnorm_in",
    )(xp, xs, g.reshape(1, d))


def _rope(acc, cos_ref, sin_ref):
    tm, tn = acc.shape
    reps = tn // LANES
    cos = jnp.tile(cos_ref[...], (1, reps))
    sin = jnp.tile(sin_ref[...], (1, reps))
    lane = lax.broadcasted_iota(jnp.int32, (tm, tn), 1)
    low = (lane % HEAD_DIM) < (HEAD_DIM // 2)
    partner = jnp.where(low, pltpu.roll(acc, tn - HEAD_DIM // 2, 1), pltpu.roll(acc, HEAD_DIM // 2, 1))
    return acc * cos + partner * sin


def _store_heads(o_ref, val):
    for h in range(val.shape[1] // HEAD_DIM):
        o_ref[h] = val[:, h * HEAD_DIM:(h + 1) * HEAD_DIM].astype(o_ref.dtype)


def _cast_weight(w_ref, wb_ref):
    @pl.when(pl.program_id(1) == 0)
    def _():
        wb_ref[...] = w_ref[...].astype(wb_ref.dtype)


def _row_parts(tm):
    n = 2 if tm % 32 == 0 and tm >= 1024 else 1
    return [slice(r * (tm // n), (r + 1) * (tm // n)) for r in range(n)]


def _proj_q_kernel(h_ref, w_ref, cos_ref, sin_ref, q_ref, wb_ref):
    _cast_weight(w_ref, wb_ref)
    for rows in _row_parts(h_ref.shape[0]):
        acc = jnp.dot(h_ref[rows, :], wb_ref[...], preferred_element_type=F32)
        val = _rope(acc, cos_ref.at[rows, :], sin_ref.at[rows, :]) * (HEAD_DIM ** -0.5 * LOG2_E)
        for h in range(val.shape[1] // HEAD_DIM):
            q_ref[h, rows, :] = val[:, h * HEAD_DIM:(h + 1) * HEAD_DIM].astype(q_ref.dtype)


def _proj_kv_kernel(h_ref, w_ref, cos_ref, sin_ref, kf_ref, vf_ref, kh_ref, vh_ref, wb_ref, *, kvw):
    _cast_weight(w_ref, wb_ref)
    acc = jnp.dot(h_ref[...], wb_ref[...], preferred_element_type=F32)
    k = _rope(acc[:, :kvw], cos_ref, sin_ref)
    v = acc[:, kvw:]
    kf_ref[...] = k
    vf_ref[...] = v
    _store_heads(kh_ref, k)
    _store_heads(vh_ref, v)


def _proj_act_kernel(h_ref, w_ref, o_ref, wb_ref, *, act):
    _cast_weight(w_ref, wb_ref)
    for rows in _row_parts(h_ref.shape[0]):
        acc = jnp.dot(h_ref[rows, :], wb_ref[...], preferred_element_type=F32)
        if act == "silu":
            acc = acc * _sigmoid(acc)
        elif act == "sigmoid":
            acc = _sigmoid(acc)
        o_ref[rows, :] = acc.astype(o_ref.dtype)


def _proj_specs(d, tm, tn, col0):
    assert col0 % tn == 0
    cb0 = col0 // tn
    return [pl.BlockSpec((tm, d), lambda j, i: (i, 0)), pl.BlockSpec((d, tn), lambda j, i: (0, cb0 + j))]


def _proj_vmem(d, tm, tn, outs):
    return _vmem_estimate([((tm, d), BF16), ((d, tn), F32)] + outs, [((tm, tn), F32)] * 3 + [((d, tn), BF16)])


def _proj_q(h, w, cos, sin, aw, tm, tn):
    m, d = h.shape
    nh = aw // HEAD_DIM
    tab = pl.BlockSpec((tm, LANES), lambda j, i: (i, 0))
    return pl.pallas_call(
        _proj_q_kernel,
        grid=(aw // tn, m // tm),
        in_specs=_proj_specs(d, tm, tn, 0) + [tab, tab],
        out_specs=pl.BlockSpec((tn // HEAD_DIM, tm, HEAD_DIM), lambda j, i: (j, i, 0)),
        out_shape=jax.ShapeDtypeStruct((nh, m, HEAD_DIM), BF16),
        scratch_shapes=[pltpu.VMEM((d, tn), BF16)],
        compiler_params=_params(("arbitrary", "arbitrary"), _proj_vmem(d, tm, tn, [((tm, 2 * tn), BF16), ((tm, 2 * LANES), F32)])),
        name="proj_q",
    )(h, w, cos, sin)


def _proj_kv(h, w, cos, sin, col0, kvw, tm):
    m, d = h.shape
    g = kvw // HEAD_DIM
    tn = 2 * kvw
    tab = pl.BlockSpec((tm, LANES), lambda j, i: (i, 0))
    flat = pl.BlockSpec((tm, kvw), lambda j, i: (i, 0))
    heads = pl.BlockSpec((g, tm, HEAD_DIM), lambda j, i: (0, i, 0))
    return pl.pallas_call(
        functools.partial(_proj_kv_kernel, kvw=kvw),
        grid=(1, m // tm),
        in_specs=_proj_specs(d, tm, tn, col0) + [tab, tab],
        out_specs=[flat, flat, heads, heads],
        out_shape=[jax.ShapeDtypeStruct((m, kvw), F32)] * 2 + [jax.ShapeDtypeStruct((g, m, HEAD_DIM), BF16)] * 2,
        scratch_shapes=[pltpu.VMEM((d, tn), BF16)],
        compiler_params=_params(("arbitrary", "arbitrary"), _proj_vmem(d, tm, tn, [((tm, 2 * tn), F32), ((tm, 2 * tn), BF16), ((tm, 2 * LANES), F32)])),
        name="proj_kv",
    )(h, w, cos, sin)


def _proj_act(h, w, col0, ncols, act, tm, tn, name, out_dtype=BF16):
    m, d = h.shape
    return pl.pallas_call(
        functools.partial(_proj_act_kernel, act=act),
        grid=(ncols // tn, m // tm),
        in_specs=_proj_specs(d, tm, tn, col0),
        out_specs=pl.BlockSpec((tm, tn), lambda j, i: (i, j)),
        out_shape=jax.ShapeDtypeStruct((m, ncols), out_dtype),
        scratch_shapes=[pltpu.VMEM((d, tn), BF16)],
        compiler_params=_params(("arbitrary", "arbitrary"), _proj_vmem(d, tm, tn, [((tm, tn), out_dtype)])),
        name=name,
    )(h, w)


def _attn_weights(q, kw, sink, n_valid):
    st = lax.dot_general(kw, q, (((1,), (1,)), ((), ())), preferred_element_type=F32)
    if n_valid is not None:
        row = lax.broadcasted_iota(jnp.int32, st.shape, 0)
        st = jnp.where(row < n_valid, st, -jnp.inf)
    m = jnp.maximum(jnp.max(st, axis=0, keepdims=True), sink)
    e = jnp.exp2(st - m)
    return e.astype(BF16), jnp.sum(e, axis=0, keepdims=True) + jnp.exp2(sink - m)


def _attn_values(vw, e, denom):
    ot = jnp.dot(vw.astype(F32).T.astype(BF16), e, preferred_element_type=F32) / denom
    return ot.T


def _store_unit(o_ref, za_ref, row0, o, rep):
    for r in range(0, rep, 2):
        pair = jnp.concatenate([o[r * CHUNK:(r + 1) * CHUNK], o[(r + 1) * CHUNK:(r + 2) * CHUNK]], axis=1)
        gate = za_ref[row0:row0 + CHUNK, r * HEAD_DIM:(r + 2) * HEAD_DIM].astype(F32)
        o_ref[row0:row0 + CHUNK, r * HEAD_DIM:(r + 2) * HEAD_DIM] = (pair * gate).astype(o_ref.dtype)


def _attn_kernel(q_ref, k_ref, v_ref, ks_ref, vs_ref, sink_ref, za_ref, o_ref, *, n_units, rep, steps_per_stream, n_prompt_steps):
    step = pl.program_id(1)
    win = WINDOW + CHUNK
    sink = sink_ref[0] * LOG2_E

    def run(windows):
        staged = []
        for c, (kw, vw, n_valid) in enumerate(windows):
            q = q_ref[:, c * CHUNK:(c + 1) * CHUNK, :].reshape(rep * CHUNK, HEAD_DIM)
            staged.append((vw,) + _attn_weights(q, kw, sink, n_valid))
        for c in range(n_units):
            _store_unit(o_ref, za_ref, c * CHUNK, _attn_values(*staged[c]), rep)

    @pl.when(step < n_prompt_steps)
    def _():
        cb = step % steps_per_stream
        windows = []
        for c in range(n_units):
            start = pl.multiple_of(jnp.maximum(cb * n_units + c - WINDOW // CHUNK, 0) * CHUNK, CHUNK)
            n_valid = jnp.where(cb == 0, (c + 1) * CHUNK, win) if c < WINDOW // CHUNK else None
            windows.append((k_ref[0, pl.ds(start, win), :], v_ref[0, pl.ds(start, win), :], n_valid))
        run(windows)

    @pl.when(step == n_prompt_steps)
    def _():
        run([(ks_ref[0, b * win:(b + 1) * win, :], vs_ref[0, b * win:(b + 1) * win, :], None) for b in range(n_units)])


def _attention(q_hm, k_hm, v_hm, ks, vs, sink_rows, za, batch, seq, dec_batch, aw):
    nh, m, _ = q_hm.shape
    g = k_hm.shape[0]
    rep = nh // g
    win = WINDOW + CHUNK
    n_units = dec_batch
    rows = n_units * CHUNK
    assert seq % rows == 0 and n_units >= WINDOW // CHUNK
    steps_per_stream = seq // rows
    n_prompt_steps = batch * steps_per_stream
    blocks = ([((rep, rows, LANES), BF16)] + [((seq, LANES), BF16)] * 2 + [((dec_batch * win, LANES), BF16)] * 2
              + [((rows, rep * HEAD_DIM), BF16)] * 2)
    temps = [((rep * CHUNK, 2 * LANES), F32)] * (4 * n_units)
    stream = lambda gi, s: (gi, jnp.minimum(s // steps_per_stream, batch - 1), 0)
    return pl.pallas_call(
        functools.partial(_attn_kernel, n_units=n_units, rep=rep, steps_per_stream=steps_per_stream, n_prompt_steps=n_prompt_steps),
        grid=(g, n_prompt_steps + 1),
        in_specs=[
            pl.BlockSpec((rep, rows, HEAD_DIM), lambda gi, s: (gi, s, 0)),
            pl.BlockSpec((1, seq, HEAD_DIM), stream),
            pl.BlockSpec((1, seq, HEAD_DIM), stream),
            pl.BlockSpec((1, dec_batch * win, HEAD_DIM), lambda gi, s: (gi, 0, 0)),
            pl.BlockSpec((1, dec_batch * win, HEAD_DIM), lambda gi, s: (gi, 0, 0)),
            pl.BlockSpec((1, 1, rep * CHUNK), lambda gi, s: (gi, 0, 0)),
            pl.BlockSpec((rows, rep * HEAD_DIM), lambda gi, s: (s, gi)),
        ],
        out_specs=pl.BlockSpec((rows, rep * HEAD_DIM), lambda gi, s: (s, gi)),
        out_shape=jax.ShapeDtypeStruct((m, aw), BF16),
        compiler_params=_params(("arbitrary",) * 2, _vmem_estimate(blocks, temps)),
        name="attention",
    )(q_hm, k_hm, v_hm, ks, vs, sink_rows, za)


def _ssm_disc_kernel(lre_ref, lim_ref, ldt_ref, are_ref, aim_ref, dre_ref, dim_ref, fre_ref, fim_ref):
    lr = jnp.minimum(lre_ref[...], LAMBDA_RE_MAX)
    li = lim_ref[...]
    dt = jnp.exp(ldt_ref[...])
    mag = jnp.exp(lr * dt)
    a_re = mag * jnp.cos(li * dt)
    a_im = mag * jnp.sin(li * dt)
    den = lr * lr + li * li
    nr = a_re - 1.0
    fre_ref[...] = (nr * lr + a_im * li) / den
    fim_ref[...] = (a_im * lr - nr * li) / den
    are_ref[...] = a_re
    aim_ref[...] = a_im
    for _ in range(SSM_POW_BITS):
        a_re, a_im = a_re * a_re - a_im * a_im, 2.0 * a_re * a_im
    dre_ref[...] = a_re
    dim_ref[...] = a_im


def _cmul(ar, ai, br, bi):
    return ar * br - ai * bi, ar * bi + ai * br


def _ssm_build_kernel(are_ref, aim_ref, dre_ref, dim_ref, fre_ref, fim_ref, btr_ref, bti_ref, cr_ref, ci_ref, dv_ref,
                      met_ref, ft_ref, a1_ref, a2_ref, a2s_ref, *, groups):
    lc = SSM_CHUNK * SSM_GROUP
    sub = lax.broadcasted_iota(jnp.int32, (SSM_GROUP, lc), 0)
    lane = lax.broadcasted_iota(jnp.int32, (SSM_GROUP, lc), 1)
    for gi in range(groups):
        row = slice(gi, gi + 1)
        a_re, a_im = are_ref[row, :], aim_ref[row, :]
        pw = [(jnp.ones_like(a_re), jnp.zeros_like(a_re))]
        for _ in range(SSM_CHUNK):
            pw.append(_cmul(pw[-1][0], pw[-1][1], a_re, a_im))
        c_re, c_im = cr_ref[gi], ci_ref[gi]
        wt = [_cmul(pr, pi, c_re, c_im) for pr, pi in pw]
        wt_re = jnp.concatenate([w[0] for w in wt[:SSM_CHUNK]], axis=0)
        wt_im = jnp.concatenate([w[1] for w in wt[:SSM_CHUNK]], axis=0)
        et_re = jnp.concatenate([w[0] for w in wt[1:]], axis=0)
        et_im = jnp.concatenate([w[1] for w in wt[1:]], axis=0)
        bb_re, bb_im = _cmul(fre_ref[row, :], fim_ref[row, :], btr_ref[gi], bti_ref[gi])
        r0 = lax.dot_general(jnp.concatenate([bb_re, bb_im], axis=1), jnp.concatenate([wt_re, -wt_im], axis=1),
                             (((1,), (1,)), ((), ())), preferred_element_type=F32, precision=lax.Precision.HIGHEST)
        r0 = r0 + jnp.where(sub == lane, dv_ref[gi], 0.0)
        rows = [r0] + [jnp.where(lane >= s * SSM_GROUP, pltpu.roll(r0, s * SSM_GROUP, 1), 0.0) for s in range(1, SSM_CHUNK)]
        mt = jnp.concatenate(rows, axis=0).T
        met_ref[gi] = jnp.concatenate([mt, et_re, -et_im], axis=1).astype(met_ref.dtype)
        fb = [_cmul(pw[SSM_CHUNK - 1 - s][0], pw[SSM_CHUNK - 1 - s][1], bb_re, bb_im) for s in range(SSM_CHUNK)]
        f_all = jnp.concatenate([jnp.concatenate([x[0] for x in fb], axis=0), jnp.concatenate([x[1] for x in fb], axis=0)], axis=1)
        ft_ref[gi] = f_all.T.astype(ft_ref.dtype)
        d_re, d_im = dre_ref[row, :], dim_ref[row, :]
        a1_ref[gi] = jnp.concatenate([d_re, d_re], axis=1)
        a2_ref[gi] = jnp.concatenate([-d_im, d_im], axis=1)
        a2s_ref[gi] = jnp.concatenate([d_im, -d_im], axis=1)


def _ssm_params(lambda_re, lambda_im, log_dt, b_re, b_im, c_re, c_im, d_skip):
    ng, p = lambda_re.shape
    lc = SSM_CHUNK * SSM_GROUP
    full = pl.BlockSpec((ng, p), lambda: (0, 0))
    disc = pl.pallas_call(
        _ssm_disc_kernel,
        in_specs=[full, full, pl.BlockSpec((ng, 1), lambda: (0, 0))],
        out_specs=[full] * 6,
        out_shape=[jax.ShapeDtypeStruct((ng, p), F32)] * 6,
        name="ssm_disc",
    )(lambda_re, lambda_im, log_dt.reshape(ng, 1))
    gb = _pick(ng, (8,))
    bt_re = jnp.swapaxes(b_re, 1, 2)
    bt_im = jnp.swapaxes(b_im, 1, 2)
    dvec = jnp.pad(d_skip, ((0, 0), (0, lc - SSM_GROUP))).reshape(ng, 1, lc)
    rows = pl.BlockSpec((gb, p), lambda i: (i, 0))
    mats = pl.BlockSpec((gb, SSM_GROUP, p), lambda i: (i, 0, 0))
    dec = pl.BlockSpec((gb, 1, 2 * p), lambda i: (i, 0, 0))
    blocks = ([((gb, LANES), F32)] * 6 + [((gb, SSM_GROUP, LANES), F32)] * 4 + [((gb, 8, lc), F32)]
              + [((gb, lc, lc + 2 * p), BF16), ((gb, 2 * p, lc), BF16)] + [((gb, 8, LANES), F32)] * 3)
    met, ft, a1, a2, a2s = pl.pallas_call(
        functools.partial(_ssm_build_kernel, groups=gb),
        grid=(ng // gb,),
        in_specs=[rows] * 6 + [mats] * 4 + [pl.BlockSpec((gb, 1, lc), lambda i: (i, 0, 0))],
        out_specs=[pl.BlockSpec((gb, lc, lc + 2 * p), lambda i: (i, 0, 0)), pl.BlockSpec((gb, 2 * p, lc), lambda i: (i, 0, 0)), dec, dec, dec],
        out_shape=[jax.ShapeDtypeStruct((ng, lc, lc + 2 * p), BF16), jax.ShapeDtypeStruct((ng, 2 * p, lc), BF16)]
        + [jax.ShapeDtypeStruct((ng, 1, 2 * p), F32)] * 3,
        compiler_params=_params(("arbitrary",), _vmem_estimate(blocks, [((lc, lc + 2 * p), F32)] * 8)),
        name="ssm_build",
    )(*disc, bt_re, bt_im, c_re, c_im, dvec)
    flat = lambda a: a.reshape(1, ng * 2 * p)
    return met, ft, flat(a1), flat(a2), flat(a2s)


def _chunk_blocks(nk):
    return [(k0, min(LANES, nk - k0)) for k0 in range(0, nk, LANES)]


def _pad_rows(x):
    n = x.shape[0]
    return x if n == LANES else jnp.concatenate([x, jnp.zeros((LANES - n, x.shape[1]), x.dtype)], axis=0)


def _ssm_inputs_phase(u_ref, ft_ref, rhs_ref, s_ref, *, nk, p):
    groups = LANES // SSM_GROUP
    blocks = _chunk_blocks(nk)
    xt = [[_pad_rows(u_ref[pl.ds(k0 * SSM_CHUNK + s, n, stride=SSM_CHUNK), :]).T.astype(BF16) for k0, n in blocks]
          for s in range(SSM_CHUNK)]
    for gi in range(groups):
        rhs = jnp.concatenate([jnp.concatenate([xt[s][kb][gi * SSM_GROUP:(gi + 1) * SSM_GROUP, :] for kb in range(len(blocks))], axis=1)
                               for s in range(SSM_CHUNK)], axis=0)
        rhs_ref[gi] = rhs
        st = jnp.dot(ft_ref[gi], rhs, preferred_element_type=F32)
        for kb, (k0, n) in enumerate(blocks):
            s_ref[k0:k0 + n, gi * 2 * p:(gi + 1) * 2 * p] = st[:, kb * LANES:(kb + 1) * LANES].T[:n, :]


def _ssm_outputs_phase(rhs_ref, h_ref, met_ref, y_ref, *, nk, p):
    groups = LANES // SSM_GROUP
    blocks = _chunk_blocks(nk)
    zt = [[None] * groups for _ in range(SSM_CHUNK)]
    for gi in range(groups):
        ht = jnp.concatenate([_pad_rows(h_ref[k0:k0 + n, gi * 2 * p:(gi + 1) * 2 * p]).T for k0, n in blocks], axis=1)
        rhs = jnp.concatenate([rhs_ref[gi], ht.astype(BF16)], axis=0)
        yt = jnp.dot(met_ref[gi], rhs, preferred_element_type=F32)
        for t in range(SSM_CHUNK):
            zt[t][gi] = yt[t * SSM_GROUP:(t + 1) * SSM_GROUP, :]
    for t in range(SSM_CHUNK):
        z = jnp.concatenate(zt[t], axis=0)
        for kb, (k0, n) in enumerate(blocks):
            y_ref[pl.ds(k0 * SSM_CHUNK + t, n, stride=SSM_CHUNK), :] = z[:, kb * LANES:(kb + 1) * LANES].T[:n, :]


def _ssm_scan_phase(s_ref, h0_ref, a1_ref, a2_ref, a2s_ref, h_ref, fin_ref, ss_ref, *, batch, kp, dec_batch, ks, p):
    def swap_halves(x):
        lane = lax.broadcasted_iota(jnp.int32, x.shape, 1)
        return jnp.where(lane % (2 * p) < p, pltpu.roll(x, x.shape[1] - p, 1), pltpu.roll(x, p, 1))

    ss_ref[...] = swap_halves(s_ref[...])
    a1, a2, a2s = a1_ref[...], a2_ref[...], a2s_ref[...]
    zero = jnp.zeros_like(a1)

    def step(k, h, hs):
        h_ref[pl.ds(k, 1), :] = h
        s = s_ref[pl.ds(k, 1), :]
        ss = ss_ref[pl.ds(k, 1), :]
        return a1 * h + a2 * hs + s, a1 * hs + a2s * h + ss

    def body(k, carry):
        out = []
        for b in range(batch):
            out.extend(step(b * kp + k, carry[2 * b], carry[2 * b + 1]))
        return tuple(out)

    fin = lax.fori_loop(0, kp, body, (zero,) * (2 * batch))
    for b in range(batch):
        fin_ref[b:b + 1, :] = fin[2 * b]
    h0s_all = swap_halves(h0_ref[...])
    for b in range(dec_batch):
        h, hs = h0_ref[b:b + 1, :], h0s_all[b:b + 1, :]
        for k in range(ks):
            h, hs = step(batch * kp + b * ks + k, h, hs)
        fin_ref[batch + b:batch + b + 1, :] = h


def _ssm_kernel(u_ref, ft_ref, met_ref, h0_ref, a1_ref, a2_ref, a2s_ref, y_ref, fin_ref, rhs_ref, s_ref, ss_ref, h_ref,
                *, batch, kp, dec_batch, ks, p):
    nk = batch * kp + dec_batch * ks
    _ssm_inputs_phase(u_ref, ft_ref, rhs_ref, s_ref, nk=nk, p=p)
    _ssm_scan_phase(s_ref, h0_ref, a1_ref, a2_ref, a2s_ref, h_ref, fin_ref, ss_ref, batch=batch, kp=kp, dec_batch=dec_batch, ks=ks, p=p)
    _ssm_outputs_phase(rhs_ref, h_ref, met_ref, y_ref, nk=nk, p=p)


def _ssm(u, met, ft, a1, a2, a2s, h0, batch, kp, dec_batch, ks):
    m, sw = u.shape
    ng, p2, lc = ft.shape
    p = p2 // 2
    nk = m // SSM_CHUNK
    assert nk == batch * kp + dec_batch * ks and (nk % LANES) % 8 == 0
    nk_lanes = -(-nk // LANES) * LANES
    groups = LANES // SSM_GROUP
    lb = groups * p2
    nseq = batch + dec_batch
    u_spec = pl.BlockSpec((m, LANES), lambda i: (0, i))
    rowb = pl.BlockSpec((1, lb), lambda i: (0, i))
    blocks = [((m, LANES), F32)] * 2 + [((groups, p2, lc), BF16), ((groups, lc, lc + p2), BF16), ((nseq, lb), F32)]
    scratch = [((groups, lc, nk_lanes), BF16)] + [((nk, lb), F32)] * 3
    return pl.pallas_call(
        functools.partial(_ssm_kernel, batch=batch, kp=kp, dec_batch=dec_batch, ks=ks, p=p),
        grid=(ng // groups,),
        in_specs=[u_spec, pl.BlockSpec((groups, p2, lc), lambda i: (i, 0, 0)), pl.BlockSpec((groups, lc, lc + p2), lambda i: (i, 0, 0)),
                  pl.BlockSpec((dec_batch, lb), lambda i: (0, i)), rowb, rowb, rowb],
        out_specs=[u_spec, pl.BlockSpec((nseq, lb), lambda i: (0, i))],
        out_shape=[jax.ShapeDtypeStruct((m, sw), F32), jax.ShapeDtypeStruct((nseq, ng * p2), F32)],
        scratch_shapes=[pltpu.VMEM(s, d) for s, d in scratch],
        compiler_params=_params(("arbitrary",), _vmem_estimate(blocks, scratch + [((lc + p2, nk_lanes), BF16), ((lc, nk_lanes), F32)])),
        name="ssm",
    )(u, ft, met, h0, a1, a2, a2s)


def _glu_kernel(y_ref, wa_ref, wg_ref, ba_ref, bg_ref, zs_ref, o_ref, wab_ref, wgb_ref):
    _cast_weight(wa_ref, wab_ref)
    _cast_weight(wg_ref, wgb_ref)
    for rows in _row_parts(y_ref.shape[0]):
        y = y_ref[rows, :].astype(BF16)
        a = jnp.dot(y, wab_ref[...], preferred_element_type=F32) + ba_ref[...]
        g = jnp.dot(y, wgb_ref[...], preferred_element_type=F32) + bg_ref[...]
        o_ref[rows, :] = (a * _sigmoid(g) * zs_ref[rows, :].astype(F32)).astype(o_ref.dtype)


def _glu(y, w_glu, b_glu, zs, tm, tn):
    m, sw = zs.shape
    nb = sw // tn
    blocks = [((tm, sw), F32), ((sw, tn), F32), ((sw, tn), F32), ((tm, tn), BF16), ((tm, tn), BF16)]
    return pl.pallas_call(
        _glu_kernel,
        grid=(nb, m // tm),
        in_specs=[
            pl.BlockSpec((tm, sw), lambda j, i: (i, 0)),
            pl.BlockSpec((sw, tn), lambda j, i: (0, j)),
            pl.BlockSpec((sw, tn), lambda j, i: (0, nb + j)),
            pl.BlockSpec((1, tn), lambda j, i: (0, j)),
            pl.BlockSpec((1, tn), lambda j, i: (0, nb + j)),
            pl.BlockSpec((tm, tn), lambda j, i: (i, j)),
        ],
        out_specs=pl.BlockSpec((tm, tn), lambda j, i: (i, j)),
        out_shape=jax.ShapeDtypeStruct((m, sw), BF16),
        scratch_shapes=[pltpu.VMEM((sw, tn), BF16)] * 2,
        compiler_params=_params(("arbitrary", "arbitrary"), _vmem_estimate(blocks, [((tm, tn), F32)] * 4 + [((sw, tn), BF16)] * 2)),
        name="glu",
    )(y, w_glu, w_glu, b_glu, b_glu, zs)


def _merge_kernel(xa_ref, xs_ref, wpa_ref, wps_ref, ga_ref, gs_ref, o_ref, wpab_ref, wpsb_ref):
    _cast_weight(wpa_ref, wpab_ref)
    _cast_weight(wps_ref, wpsb_ref)
    for rows in _row_parts(xa_ref.shape[0]):
        br_a = jnp.dot(xa_ref[rows, :], wpab_ref[...], preferred_element_type=F32)
        br_s = jnp.dot(xs_ref[rows, :], wpsb_ref[...], preferred_element_type=F32)
        o_ref[rows, :] = (ga_ref[rows, :].astype(F32) * br_a + gs_ref[rows, :].astype(F32) * br_s).astype(o_ref.dtype)


def _merge(xa, xs, w_pa, w_ps, gate, tm, tn):
    m, aw = xa.shape
    sw = xs.shape[1]
    d = w_pa.shape[1]
    nb = d // tn
    blocks = [((tm, aw), BF16), ((tm, sw), BF16), ((aw, tn), F32), ((sw, tn), F32)] + [((tm, tn), BF16)] * 3
    return pl.pallas_call(
        _merge_kernel,
        grid=(nb, m // tm),
        in_specs=[
            pl.BlockSpec((tm, aw), lambda j, i: (i, 0)),
            pl.BlockSpec((tm, sw), lambda j, i: (i, 0)),
            pl.BlockSpec((aw, tn), lambda j, i: (0, j)),
            pl.BlockSpec((sw, tn), lambda j, i: (0, j)),
            pl.BlockSpec((tm, tn), lambda j, i: (i, j)),
            pl.BlockSpec((tm, tn), lambda j, i: (i, nb + j)),
        ],
        out_specs=pl.BlockSpec((tm, tn), lambda j, i: (i, j)),
        out_shape=jax.ShapeDtypeStruct((m, d), BF16),
        scratch_shapes=[pltpu.VMEM((aw, tn), BF16), pltpu.VMEM((sw, tn), BF16)],
        compiler_params=_params(("arbitrary", "arbitrary"),
                                _vmem_estimate(blocks, [((tm, tn), F32)] * 4 + [((aw, tn), BF16), ((sw, tn), BF16)])),
        name="merge",
    )(xa, xs, w_pa, w_ps, gate, gate)


def _out_kernel(mg_ref, w_ref, x_ref, g_ref, o_ref, ssq_ref, *, n_col_blocks, tn):
    j = pl.program_id(1)
    blk = x_ref[...] + jnp.dot(mg_ref[...], w_ref[...], preferred_element_type=F32)
    o_ref[:, pl.ds(pl.multiple_of(j * tn, tn), tn)] = blk
    part = jnp.sum(blk * blk, axis=-1, keepdims=True)

    @pl.when(j == 0)
    def _():
        ssq_ref[...] = part

    @pl.when(j > 0)
    def _():
        ssq_ref[...] += part

    @pl.when(j == n_col_blocks - 1)
    def _():
        scale = lax.rsqrt(ssq_ref[...] / (n_col_blocks * tn) + NORM_EPS)
        o_ref[...] = o_ref[...] * scale * g_ref[...]


def _out(merged, row_block0, w_out, x, final_g, tm, tn):
    mx, d = x.shape
    nb = d // tn
    blocks = [((tm, d), BF16), ((d, tn), BF16), ((tm, tn), F32), ((tm, d), F32)]
    return pl.pallas_call(
        functools.partial(_out_kernel, n_col_blocks=nb, tn=tn),
        grid=(mx // tm, nb),
        in_specs=[
            pl.BlockSpec((tm, d), lambda i, j: (row_block0 + i, 0)),
            pl.BlockSpec((d, tn), lambda i, j: (0, j)),
            pl.BlockSpec((tm, tn), lambda i, j: (i, j)),
            pl.BlockSpec((1, d), lambda i, j: (0, 0)),
        ],
        out_specs=pl.BlockSpec((tm, d), lambda i, j: (i, 0)),
        out_shape=jax.ShapeDtypeStruct((mx, d), F32),
        scratch_shapes=[pltpu.VMEM((tm, 1), F32)],
        compiler_params=_params(("arbitrary", "arbitrary"), _vmem_estimate(blocks, [((tm, tn), F32)] * 3 + [((tm, LANES), F32)])),
        name="out_norm",
    )(merged, w_out, x, final_g.reshape(1, d))


def _rope_tables(positions):
    half = HEAD_DIM // 2
    inv_freq = ROPE_THETA ** (-jnp.arange(half, dtype=F32) / half)
    ang = positions.astype(F32)[:, None] * inv_freq[None, :]
    cos, sin = jnp.cos(ang), jnp.sin(ang)
    reps = LANES // HEAD_DIM
    return jnp.tile(jnp.concatenate([cos, cos], axis=1), (1, reps)), jnp.tile(jnp.concatenate([-sin, sin], axis=1), (1, reps))


def kernel(x_prompt, x_sample, cache_k, cache_v, state_ssm_re, state_ssm_im, norm_g, w_in, sink, lambda_re, lambda_im,
           log_dt, b_re, b_im, c_re, c_im, d_skip, w_glu, b_glu, w_pa, w_ps, w_out, final_g):
    depth = norm_g.shape[0]
    assert depth == 1, "one trunk layer"
    batch, seq, d = x_prompt.shape
    dec_batch, dec_seq, _ = x_sample.shape
    aw = w_pa.shape[1]
    sw = w_ps.shape[1]
    nh = aw // HEAD_DIM
    g = max(1, nh // GQA_GROUPING)
    rep = nh // g
    kvw = g * HEAD_DIM
    ng, p = lambda_re.shape[1:]
    assert dec_seq == CHUNK and cache_k.shape[2] == WINDOW and seq % (2 * CHUNK) == 0
    assert kvw % LANES == 0 and rep % 2 == 0 and sw == ng * SSM_GROUP and (1 << SSM_POW_BITS) == SSM_CHUNK
    assert 2 * p == LANES and sw % LANES == 0
    mp, ms = batch * seq, dec_batch * dec_seq
    m = mp + ms
    widths = (aw, 2 * kvw, aw, sw, sw, 2 * d)
    assert sum(widths) == w_in.shape[2]
    c_q, c_kv, c_za, c_u, c_zs, c_gate = (sum(widths[:n]) for n in range(len(widths)))
    tm = _pick(math.gcd(mp, ms), (512, 256, 128))
    tmp = max(t for t in range(16, 1153, 16) if m % t == 0)
    tn = _pick(math.gcd(c_kv, c_za, c_u, c_zs, c_gate, 2 * d), (512, 256, 128))

    xp = x_prompt.reshape(mp, d)
    xs = x_sample.reshape(ms, d)
    w_in2 = w_in.reshape(d, w_in.shape[2])
    positions = jnp.concatenate([jnp.tile(jnp.arange(seq, dtype=jnp.int32), batch),
                                 jnp.tile(PAST_LEN + jnp.arange(dec_seq, dtype=jnp.int32), dec_batch)])
    cos, sin = _rope_tables(positions)

    h = _rmsnorm(xp, xs, norm_g[0], tm)
    q_hm = _proj_q(h, w_in2, cos, sin, aw, tmp, tn)
    k_f, v_f, k_hm, v_hm = _proj_kv(h, w_in2, cos, sin, c_kv, kvw, tmp)
    za = _proj_act(h, w_in2, c_za, aw, "silu", tmp, tn, "proj_za")
    u = _proj_act(h, w_in2, c_u, sw, "none", tmp, tn, "proj_u", out_dtype=F32)
    zs = _proj_act(h, w_in2, c_zs, sw, "silu", tmp, tn, "proj_zs")
    gate = _proj_act(h, w_in2, c_gate, 2 * d, "sigmoid", tmp, tn, "proj_gate")

    to_heads = lambda c: jnp.transpose(c[0], (2, 0, 1, 3)).astype(BF16)
    new_rows = lambda a: a[:, mp:].reshape(g, dec_batch, dec_seq, HEAD_DIM)
    ks = jnp.concatenate([to_heads(cache_k), new_rows(k_hm)], axis=2).reshape(g, dec_batch * (WINDOW + CHUNK), HEAD_DIM)
    vs = jnp.concatenate([to_heads(cache_v), new_rows(v_hm)], axis=2).reshape(g, dec_batch * (WINDOW + CHUNK), HEAD_DIM)
    sink_rows = jnp.repeat(sink[0].reshape(g, rep), CHUNK, axis=1).reshape(g, 1, rep * CHUNK)
    xa = _attention(q_hm, k_hm, v_hm, ks, vs, sink_rows, za, batch, seq, dec_batch, aw)

    met, ft, a1, a2, a2s = _ssm_params(lambda_re[0], lambda_im[0], log_dt[0], b_re[0], b_im[0], c_re[0], c_im[0], d_skip[0])
    h0 = jnp.concatenate([state_ssm_re[0], state_ssm_im[0]], axis=-1).reshape(dec_batch, ng * 2 * p)
    y, fin = _ssm(u, met, ft, a1, a2, a2s, h0, batch, seq // SSM_CHUNK, dec_batch, dec_seq // SSM_CHUNK)

    tmm = tmp
    x_ssm = _glu(y, w_glu.reshape(sw, 2 * sw), b_glu.reshape(1, 2 * sw), zs, tmm, tn)
    merged = _merge(xa, x_ssm, w_pa.reshape(aw, d), w_ps.reshape(sw, d), gate, tmm, tn)
    w_out_b = w_out[0].astype(BF16)
    tno = _pick(d, (1024, 512, 256))
    y_prompt = _out(merged, 0, w_out_b, xp, final_g, tm, tno).reshape(batch, seq, d)
    y_sample = _out(merged, mp // tm, w_out_b, xs, final_g, tm, tno).reshape(dec_batch, dec_seq, d)

    keep = min(WINDOW, seq)
    last_rows = lambda a: jnp.stack([a[(b + 1) * seq - keep:(b + 1) * seq] for b in range(batch)]).reshape(1, batch, keep, g, HEAD_DIM)
    dec_rows = lambda a: a[mp:].reshape(dec_batch, dec_seq, g, HEAD_DIM)[None]
    fin = fin.reshape(batch + dec_batch, ng, 2, p)
    return (y_prompt, y_sample, last_rows(k_f), last_rows(v_f), fin[:batch, :, 0][None], fin[:batch, :, 1][None],
            dec_rows(k_f), dec_rows(v_f), fin[batch:, :, 0][None], fin[batch:, :, 1][None])
```

```python
import functools
import math

import jax
import jax.numpy as jnp
from jax import lax
from jax.experimental import pallas as pl
from jax.experimental.pallas import tpu as pltpu

CHUNK = 64
WINDOW = 128
HEAD_DIM = 64
GQA_GROUPING = 8
SSM_GROUP = 16
PAST_LEN = 1024
ROPE_THETA = 10000.0
NORM_EPS = 1e-5
LAMBDA_RE_MAX = -1e-4
LOG2_E = math.log2(math.e)

SSM_CHUNK = 16
SSM_POW_BITS = 4
LANES = 128
V7X_VMEM_BYTES = 64 * 1024 * 1024
BF16 = jnp.bfloat16
F32 = jnp.float32


def _pick(n, prefs):
    for p in prefs:
        if n % p == 0:
            return p
    raise ValueError(f"no tile in {prefs} divides {n}")


def _params(sem, vmem_bytes):
    limit = min(int(vmem_bytes), V7X_VMEM_BYTES - 4 * 1024 * 1024)
    return pltpu.CompilerParams(dimension_semantics=sem, vmem_limit_bytes=limit)


def _sigmoid(x):
    return 0.5 * jnp.tanh(0.5 * x) + 0.5


def _nbytes(shape, dtype):
    return math.prod(shape) * jnp.dtype(dtype).itemsize


def _vmem_estimate(blocks, temps=()):
    return 2 * (2 * sum(_nbytes(s, d) for s, d in blocks) + sum(_nbytes(s, d) for s, d in temps))


def _rmsnorm_kernel(xp_ref, xs_ref, g_ref, o_ref, *, n_prompt_blocks):
    i = pl.program_id(0)

    def norm(x_ref):
        x = x_ref[...]
        y = x * lax.rsqrt(jnp.mean(x * x, axis=-1, keepdims=True) + NORM_EPS)
        o_ref[...] = (y * g_ref[...]).astype(o_ref.dtype)

    @pl.when(i < n_prompt_blocks)
    def _():
        norm(xp_ref)

    @pl.when(i >= n_prompt_blocks)
    def _():
        norm(xs_ref)


def _rmsnorm(xp, xs, g, tm):
    mp, d = xp.shape
    ms = xs.shape[0]
    npb, nsb = mp // tm, ms // tm
    return pl.pallas_call(
        functools.partial(_rmsnorm_kernel, n_prompt_blocks=npb),
        grid=(npb + nsb,),
        in_specs=[
            pl.BlockSpec((tm, d), lambda i: (jnp.minimum(i, npb - 1), 0)),
            pl.BlockSpec((tm, d), lambda i: (jnp.maximum(i - npb, 0), 0)),
            pl.BlockSpec((1, d), lambda i: (0, 0)),
        ],
        out_specs=pl.BlockSpec((tm, d), lambda i: (i, 0)),
        out_shape=jax.ShapeDtypeStruct((mp + ms, d), BF16),
        compiler_params=_params(("arbitrary",), _vmem_estimate([((tm, d), F32)] * 2 + [((tm, d), BF16)], [((tm, d), F32)])),
        name="rmsnorm_in",
    )(xp, xs, g.reshape(1, d))


def _rope(acc, cos_ref, sin_ref):
    tm, tn = acc.shape
    reps = tn // LANES
    cos = jnp.tile(cos_ref[...], (1, reps))
    sin = jnp.tile(sin_ref[...], (1, reps))
    lane = lax.broadcasted_iota(jnp.int32, (tm, tn), 1)
    low = (lane % HEAD_DIM) < (HEAD_DIM // 2)
    partner = jnp.where(low, pltpu.roll(acc, tn - HEAD_DIM // 2, 1), pltpu.roll(acc, HEAD_DIM // 2, 1))
    return acc * cos + partner * sin


def _store_heads(o_ref, val):
    for h in range(val.shape[1] // HEAD_DIM):
        o_ref[h] = val[:, h * HEAD_DIM:(h + 1) * HEAD_DIM].astype(o_ref.dtype)


def _cast_weight(w_ref, wb_ref):
    @pl.when(pl.program_id(1) == 0)
    def _():
        wb_ref[...] = w_ref[...].astype(wb_ref.dtype)


def _stream_weights(w_hbm, wbuf_ref, stage_ref, sem_ref, *, col0, n_blocks, n_chunks):
    d, tn = wbuf_ref.shape[1:]
    ck = d // n_chunks
    total = n_blocks * n_chunks
    j = pl.program_id(0)
    t = j * n_chunks + pl.program_id(1)

    def aligned(x, a):
        return x if isinstance(x, int) else pl.multiple_of(x, a)

    def chunk_copy(blk, chunk, slot):
        src = w_hbm.at[pl.ds(aligned(chunk * ck, ck), ck), pl.ds(aligned(col0 + blk * tn, LANES), tn)]
        return pltpu.make_async_copy(src, stage_ref.at[slot], sem_ref.at[slot])

    def generation(s):
        s = jnp.asarray(s, jnp.int32)
        k = s + n_chunks - 1
        past = (k > total - 1).astype(jnp.int32)
        k = jnp.minimum(k, total - 1)
        blk, chunk = k // n_chunks, k % n_chunks
        return chunk_copy(blk, chunk, s % 2), (blk + past) % 2, chunk

    def land(copy, half, chunk, slot):
        copy.wait()
        wbuf_ref[half, pl.ds(aligned(chunk * ck, ck), ck), :] = stage_ref[slot].astype(wbuf_ref.dtype)

    @pl.when(t == 0)
    def _():
        for c in range(n_chunks - 1):
            first = chunk_copy(0, c, 1)
            first.start()
            land(first, 0, c, 1)
        generation(0)[0].start()

    @pl.when(t + 1 < total)
    def _():
        generation(t + 1)[0].start()

    copy, half, chunk = generation(t)
    land(copy, half, chunk, t % 2)
    return wbuf_ref.at[j % 2]


def _row_parts(tm):
    n = 2 if tm % 32 == 0 and tm >= 1024 else 1
    return [slice(r * (tm // n), (r + 1) * (tm // n)) for r in range(n)]


def _proj_q_kernel(h_ref, w_ref, cos_ref, sin_ref, q_ref, wb_ref):
    _cast_weight(w_ref, wb_ref)
    for rows in _row_parts(h_ref.shape[0]):
        acc = jnp.dot(h_ref[rows, :], wb_ref[...], preferred_element_type=F32)
        val = _rope(acc, cos_ref.at[rows, :], sin_ref.at[rows, :]) * (HEAD_DIM ** -0.5 * LOG2_E)
        for h in range(val.shape[1] // HEAD_DIM):
            q_ref[h, rows, :] = val[:, h * HEAD_DIM:(h + 1) * HEAD_DIM].astype(q_ref.dtype)


def _proj_kv_kernel(h_ref, w_ref, cos_ref, sin_ref, kf_ref, vf_ref, kh_ref, vh_ref, wb_ref, *, kvw):
    _cast_weight(w_ref, wb_ref)
    acc = jnp.dot(h_ref[...], wb_ref[...], preferred_element_type=F32)
    k = _rope(acc[:, :kvw], cos_ref, sin_ref)
    v = acc[:, kvw:]
    kf_ref[...] = k
    vf_ref[...] = v
    _store_heads(kh_ref, k)
    _store_heads(vh_ref, v)


def _proj_act_kernel(h_ref, w_hbm, o_ref, wbuf_ref, stage_ref, sem_ref, *, act, col0, n_blocks, n_chunks):
    wb_ref = _stream_weights(w_hbm, wbuf_ref, stage_ref, sem_ref, col0=col0, n_blocks=n_blocks, n_chunks=n_chunks)
    for rows in _row_parts(h_ref.shape[0]):
        acc = jnp.dot(h_ref[rows, :], wb_ref[...], preferred_element_type=F32)
        if act == "silu":
            acc = acc * _sigmoid(acc)
        elif act == "sigmoid":
            acc = _sigmoid(acc)
        o_ref[rows, :] = acc.astype(o_ref.dtype)


def _proj_specs(d, tm, tn, col0):
    assert col0 % tn == 0
    cb0 = col0 // tn
    return [pl.BlockSpec((tm, d), lambda j, i: (i, 0)), pl.BlockSpec((d, tn), lambda j, i: (0, cb0 + j))]


def _proj_vmem(d, tm, tn, outs):
    return _vmem_estimate([((tm, d), BF16), ((d, tn), F32)] + outs, [((tm, tn), F32)] * 3 + [((d, tn), BF16)])


def _proj_q(h, w, cos, sin, aw, tm, tn):
    m, d = h.shape
    nh = aw // HEAD_DIM
    tab = pl.BlockSpec((tm, LANES), lambda j, i: (i, 0))
    return pl.pallas_call(
        _proj_q_kernel,
        grid=(aw // tn, m // tm),
        in_specs=_proj_specs(d, tm, tn, 0) + [tab, tab],
        out_specs=pl.BlockSpec((tn // HEAD_DIM, tm, HEAD_DIM), lambda j, i: (j, i, 0)),
        out_shape=jax.ShapeDtypeStruct((nh, m, HEAD_DIM), BF16),
        scratch_shapes=[pltpu.VMEM((d, tn), BF16)],
        compiler_params=_params(("arbitrary", "arbitrary"), _proj_vmem(d, tm, tn, [((tm, 2 * tn), BF16), ((tm, 2 * LANES), F32)])),
        name="proj_q",
    )(h, w, cos, sin)


def _proj_kv(h, w, cos, sin, col0, kvw, tm):
    m, d = h.shape
    g = kvw // HEAD_DIM
    tn = 2 * kvw
    tab = pl.BlockSpec((tm, LANES), lambda j, i: (i, 0))
    flat = pl.BlockSpec((tm, kvw), lambda j, i: (i, 0))
    heads = pl.BlockSpec((g, tm, HEAD_DIM), lambda j, i: (0, i, 0))
    return pl.pallas_call(
        functools.partial(_proj_kv_kernel, kvw=kvw),
        grid=(1, m // tm),
        in_specs=_proj_specs(d, tm, tn, col0) + [tab, tab],
        out_specs=[flat, flat, heads, heads],
        out_shape=[jax.ShapeDtypeStruct((m, kvw), F32)] * 2 + [jax.ShapeDtypeStruct((g, m, HEAD_DIM), BF16)] * 2,
        scratch_shapes=[pltpu.VMEM((d, tn), BF16)],
        compiler_params=_params(("arbitrary", "arbitrary"), _proj_vmem(d, tm, tn, [((tm, 2 * tn), F32), ((tm, 2 * tn), BF16), ((tm, 2 * LANES), F32)])),
        name="proj_kv",
    )(h, w, cos, sin)


def _proj_act(h, w, col0, ncols, act, tm, tn, name, out_dtype=BF16):
    m, d = h.shape
    n_blocks, n_chunks = ncols // tn, m // tm
    assert d % (8 * n_chunks) == 0 and col0 % LANES == 0
    scratch = [((2, d, tn), BF16), ((2, d // n_chunks, tn), F32)]
    return pl.pallas_call(
        functools.partial(_proj_act_kernel, act=act, col0=col0, n_blocks=n_blocks, n_chunks=n_chunks),
        grid=(n_blocks, n_chunks),
        in_specs=[pl.BlockSpec((tm, d), lambda j, i: (i, 0)), pl.BlockSpec(memory_space=pl.ANY)],
        out_specs=pl.BlockSpec((tm, tn), lambda j, i: (i, j)),
        out_shape=jax.ShapeDtypeStruct((m, ncols), out_dtype),
        scratch_shapes=[pltpu.VMEM(s, t) for s, t in scratch] + [pltpu.SemaphoreType.DMA((2,))],
        compiler_params=_params(("arbitrary", "arbitrary"),
                                _vmem_estimate([((tm, d), BF16), ((tm, tn), out_dtype)], scratch + [((tm // len(_row_parts(tm)), tn), F32)] * 3)),
        name=name,
    )(h, w)


def _attn_weights(q, kw, sink, n_valid):
    st = lax.dot_general(kw, q, (((1,), (1,)), ((), ())), preferred_element_type=F32)
    if n_valid is not None:
        row = lax.broadcasted_iota(jnp.int32, st.shape, 0)
        st = jnp.where(row < n_valid, st, -jnp.inf)
    m = jnp.maximum(jnp.max(st, axis=0, keepdims=True), sink)
    e = jnp.exp2(st - m)
    return e.astype(BF16), jnp.sum(e, axis=0, keepdims=True) + jnp.exp2(sink - m)


def _attn_values(vw, e, denom):
    ot = jnp.dot(vw.astype(F32).T.astype(BF16), e, preferred_element_type=F32) / denom
    return ot.T


def _store_unit(o_ref, za_ref, row0, o, rep):
    for r in range(0, rep, 2):
        pair = jnp.concatenate([o[r * CHUNK:(r + 1) * CHUNK], o[(r + 1) * CHUNK:(r + 2) * CHUNK]], axis=1)
        gate = za_ref[row0:row0 + CHUNK, r * HEAD_DIM:(r + 2) * HEAD_DIM].astype(F32)
        o_ref[row0:row0 + CHUNK, r * HEAD_DIM:(r + 2) * HEAD_DIM] = (pair * gate).astype(o_ref.dtype)


def _attn_kernel(q_ref, k_ref, v_ref, ks_ref, vs_ref, sink_ref, za_ref, o_ref, *, n_units, rep, steps_per_stream, n_prompt_steps):
    step = pl.program_id(1)
    win = WINDOW + CHUNK
    sink = sink_ref[0] * LOG2_E

    def run(windows):
        staged = []
        for c, (kw, vw, n_valid) in enumerate(windows):
            q = q_ref[:, c * CHUNK:(c + 1) * CHUNK, :].reshape(rep * CHUNK, HEAD_DIM)
            staged.append((vw,) + _attn_weights(q, kw, sink, n_valid))
        for c in range(n_units):
            _store_unit(o_ref, za_ref, c * CHUNK, _attn_values(*staged[c]), rep)

    @pl.when(step < n_prompt_steps)
    def _():
        cb = step % steps_per_stream
        windows = []
        for c in range(n_units):
            start = pl.multiple_of(jnp.maximum(cb * n_units + c - WINDOW // CHUNK, 0) * CHUNK, CHUNK)
            n_valid = jnp.where(cb == 0, (c + 1) * CHUNK, win) if c < WINDOW // CHUNK else None
            windows.append((k_ref[0, pl.ds(start, win), :], v_ref[0, pl.ds(start, win), :], n_valid))
        run(windows)

    @pl.when(step == n_prompt_steps)
    def _():
        run([(ks_ref[0, b * win:(b + 1) * win, :], vs_ref[0, b * win:(b + 1) * win, :], None) for b in range(n_units)])


def _attention(q_hm, k_hm, v_hm, ks, vs, sink_rows, za, batch, seq, dec_batch, aw):
    nh, m, _ = q_hm.shape
    g = k_hm.shape[0]
    rep = nh // g
    win = WINDOW + CHUNK
    n_units = dec_batch
    rows = n_units * CHUNK
    assert seq % rows == 0 and n_units >= WINDOW // CHUNK
    steps_per_stream = seq // rows
    n_prompt_steps = batch * steps_per_stream
    blocks = ([((rep, rows, LANES), BF16)] + [((seq, LANES), BF16)] * 2 + [((dec_batch * win, LANES), BF16)] * 2
              + [((rows, rep * HEAD_DIM), BF16)] * 2)
    temps = [((rep * CHUNK, 2 * LANES), F32)] * (4 * n_units)
    stream = lambda gi, s: (gi, jnp.minimum(s // steps_per_stream, batch - 1), 0)
    return pl.pallas_call(
        functools.partial(_attn_kernel, n_units=n_units, rep=rep, steps_per_stream=steps_per_stream, n_prompt_steps=n_prompt_steps),
        grid=(g, n_prompt_steps + 1),
        in_specs=[
            pl.BlockSpec((rep, rows, HEAD_DIM), lambda gi, s: (gi, s, 0)),
            pl.BlockSpec((1, seq, HEAD_DIM), stream),
            pl.BlockSpec((1, seq, HEAD_DIM), stream),
            pl.BlockSpec((1, dec_batch * win, HEAD_DIM), lambda gi, s: (gi, 0, 0)),
            pl.BlockSpec((1, dec_batch * win, HEAD_DIM), lambda gi, s: (gi, 0, 0)),
            pl.BlockSpec((1, 1, rep * CHUNK), lambda gi, s: (gi, 0, 0)),
            pl.BlockSpec((rows, rep * HEAD_DIM), lambda gi, s: (s, gi)),
        ],
        out_specs=pl.BlockSpec((rows, rep * HEAD_DIM), lambda gi, s: (s, gi)),
        out_shape=jax.ShapeDtypeStruct((m, aw), BF16),
        compiler_params=_params(("arbitrary",) * 2, _vmem_estimate(blocks, temps)),
        name="attention",
    )(q_hm, k_hm, v_hm, ks, vs, sink_rows, za)


def _ssm_disc_kernel(lre_ref, lim_ref, ldt_ref, are_ref, aim_ref, dre_ref, dim_ref, fre_ref, fim_ref):
    lr = jnp.minimum(lre_ref[...], LAMBDA_RE_MAX)
    li = lim_ref[...]
    dt = jnp.exp(ldt_ref[...])
    mag = jnp.exp(lr * dt)
    a_re = mag * jnp.cos(li * dt)
    a_im = mag * jnp.sin(li * dt)
    den = lr * lr + li * li
    nr = a_re - 1.0
    fre_ref[...] = (nr * lr + a_im * li) / den
    fim_ref[...] = (a_im * lr - nr * li) / den
    are_ref[...] = a_re
    aim_ref[...] = a_im
    for _ in range(SSM_POW_BITS):
        a_re, a_im = a_re * a_re - a_im * a_im, 2.0 * a_re * a_im
    dre_ref[...] = a_re
    dim_ref[...] = a_im


def _cmul(ar, ai, br, bi):
    return ar * br - ai * bi, ar * bi + ai * br


def _ssm_build_kernel(are_ref, aim_ref, dre_ref, dim_ref, fre_ref, fim_ref, btr_ref, bti_ref, cr_ref, ci_ref, dv_ref,
                      met_ref, ft_ref, a1_ref, a2_ref, a2s_ref, *, groups):
    lc = SSM_CHUNK * SSM_GROUP
    sub = lax.broadcasted_iota(jnp.int32, (SSM_GROUP, lc), 0)
    lane = lax.broadcasted_iota(jnp.int32, (SSM_GROUP, lc), 1)
    for gi in range(groups):
        row = slice(gi, gi + 1)
        a_re, a_im = are_ref[row, :], aim_ref[row, :]
        pw = [(jnp.ones_like(a_re), jnp.zeros_like(a_re))]
        for _ in range(SSM_CHUNK):
            pw.append(_cmul(pw[-1][0], pw[-1][1], a_re, a_im))
        c_re, c_im = cr_ref[gi], ci_ref[gi]
        wt = [_cmul(pr, pi, c_re, c_im) for pr, pi in pw]
        wt_re = jnp.concatenate([w[0] for w in wt[:SSM_CHUNK]], axis=0)
        wt_im = jnp.concatenate([w[1] for w in wt[:SSM_CHUNK]], axis=0)
        et_re = jnp.concatenate([w[0] for w in wt[1:]], axis=0)
        et_im = jnp.concatenate([w[1] for w in wt[1:]], axis=0)
        bb_re, bb_im = _cmul(fre_ref[row, :], fim_ref[row, :], btr_ref[gi], bti_ref[gi])
        r0 = lax.dot_general(jnp.concatenate([bb_re, bb_im], axis=1), jnp.concatenate([wt_re, -wt_im], axis=1),
                             (((1,), (1,)), ((), ())), preferred_element_type=F32, precision=lax.Precision.HIGHEST)
        r0 = r0 + jnp.where(sub == lane, dv_ref[gi], 0.0)
        rows = [r0] + [jnp.where(lane >= s * SSM_GROUP, pltpu.roll(r0, s * SSM_GROUP, 1), 0.0) for s in range(1, SSM_CHUNK)]
        mt = jnp.concatenate(rows, axis=0).T
        met_ref[gi] = jnp.concatenate([mt, et_re, -et_im], axis=1).astype(met_ref.dtype)
        fb = [_cmul(pw[SSM_CHUNK - 1 - s][0], pw[SSM_CHUNK - 1 - s][1], bb_re, bb_im) for s in range(SSM_CHUNK)]
        f_all = jnp.concatenate([jnp.concatenate([x[0] for x in fb], axis=0), jnp.concatenate([x[1] for x in fb], axis=0)], axis=1)
        ft_ref[gi] = f_all.T.astype(ft_ref.dtype)
        d_re, d_im = dre_ref[row, :], dim_ref[row, :]
        a1_ref[gi] = jnp.concatenate([d_re, d_re], axis=1)
        a2_ref[gi] = jnp.concatenate([-d_im, d_im], axis=1)
        a2s_ref[gi] = jnp.concatenate([d_im, -d_im], axis=1)


def _ssm_params(lambda_re, lambda_im, log_dt, b_re, b_im, c_re, c_im, d_skip):
    ng, p = lambda_re.shape
    lc = SSM_CHUNK * SSM_GROUP
    full = pl.BlockSpec((ng, p), lambda: (0, 0))
    disc = pl.pallas_call(
        _ssm_disc_kernel,
        in_specs=[full, full, pl.BlockSpec((ng, 1), lambda: (0, 0))],
        out_specs=[full] * 6,
        out_shape=[jax.ShapeDtypeStruct((ng, p), F32)] * 6,
        name="ssm_disc",
    )(lambda_re, lambda_im, log_dt.reshape(ng, 1))
    gb = _pick(ng, (8,))
    bt_re = jnp.swapaxes(b_re, 1, 2)
    bt_im = jnp.swapaxes(b_im, 1, 2)
    dvec = jnp.pad(d_skip, ((0, 0), (0, lc - SSM_GROUP))).reshape(ng, 1, lc)
    rows = pl.BlockSpec((gb, p), lambda i: (i, 0))
    mats = pl.BlockSpec((gb, SSM_GROUP, p), lambda i: (i, 0, 0))
    dec = pl.BlockSpec((gb, 1, 2 * p), lambda i: (i, 0, 0))
    blocks = ([((gb, LANES), F32)] * 6 + [((gb, SSM_GROUP, LANES), F32)] * 4 + [((gb, 8, lc), F32)]
              + [((gb, lc, lc + 2 * p), BF16), ((gb, 2 * p, lc), BF16)] + [((gb, 8, LANES), F32)] * 3)
    met, ft, a1, a2, a2s = pl.pallas_call(
        functools.partial(_ssm_build_kernel, groups=gb),
        grid=(ng // gb,),
        in_specs=[rows] * 6 + [mats] * 4 + [pl.BlockSpec((gb, 1, lc), lambda i: (i, 0, 0))],
        out_specs=[pl.BlockSpec((gb, lc, lc + 2 * p), lambda i: (i, 0, 0)), pl.BlockSpec((gb, 2 * p, lc), lambda i: (i, 0, 0)), dec, dec, dec],
        out_shape=[jax.ShapeDtypeStruct((ng, lc, lc + 2 * p), BF16), jax.ShapeDtypeStruct((ng, 2 * p, lc), BF16)]
        + [jax.ShapeDtypeStruct((ng, 1, 2 * p), F32)] * 3,
        compiler_params=_params(("arbitrary",), _vmem_estimate(blocks, [((lc, lc + 2 * p), F32)] * 8)),
        name="ssm_build",
    )(*disc, bt_re, bt_im, c_re, c_im, dvec)
    flat = lambda a: a.reshape(1, ng * 2 * p)
    return met, ft, flat(a1), flat(a2), flat(a2s)


def _chunk_blocks(nk):
    return [(k0, min(LANES, nk - k0)) for k0 in range(0, nk, LANES)]


def _pad_rows(x):
    n = x.shape[0]
    return x if n == LANES else jnp.concatenate([x, jnp.zeros((LANES - n, x.shape[1]), x.dtype)], axis=0)


def _ssm_inputs_phase(u_ref, ft_ref, rhs_ref, s_ref, *, nk, p):
    groups = LANES // SSM_GROUP
    blocks = _chunk_blocks(nk)
    xt = [[_pad_rows(u_ref[pl.ds(k0 * SSM_CHUNK + s, n, stride=SSM_CHUNK), :]).T.astype(BF16) for k0, n in blocks]
          for s in range(SSM_CHUNK)]
    for gi in range(groups):
        rhs = jnp.concatenate([jnp.concatenate([xt[s][kb][gi * SSM_GROUP:(gi + 1) * SSM_GROUP, :] for kb in range(len(blocks))], axis=1)
                               for s in range(SSM_CHUNK)], axis=0)
        rhs_ref[gi] = rhs
        st = jnp.dot(ft_ref[gi], rhs, preferred_element_type=F32)
        for kb, (k0, n) in enumerate(blocks):
            s_ref[k0:k0 + n, gi * 2 * p:(gi + 1) * 2 * p] = st[:, kb * LANES:(kb + 1) * LANES].T[:n, :]


def _ssm_outputs_phase(rhs_ref, h_ref, met_ref, y_ref, *, nk, p):
    groups = LANES // SSM_GROUP
    blocks = _chunk_blocks(nk)
    zt = [[None] * groups for _ in range(SSM_CHUNK)]
    for gi in range(groups):
        ht = jnp.concatenate([_pad_rows(h_ref[k0:k0 + n, gi * 2 * p:(gi + 1) * 2 * p]).T for k0, n in blocks], axis=1)
        rhs = jnp.concatenate([rhs_ref[gi], ht.astype(BF16)], axis=0)
        yt = jnp.dot(met_ref[gi], rhs, preferred_element_type=F32)
        for t in range(SSM_CHUNK):
            zt[t][gi] = yt[t * SSM_GROUP:(t + 1) * SSM_GROUP, :]
    for t in range(SSM_CHUNK):
        z = jnp.concatenate(zt[t], axis=0)
        for kb, (k0, n) in enumerate(blocks):
            y_ref[pl.ds(k0 * SSM_CHUNK + t, n, stride=SSM_CHUNK), :] = z[:, kb * LANES:(kb + 1) * LANES].T[:n, :]


def _ssm_scan_phase(s_ref, h0_ref, a1_ref, a2_ref, a2s_ref, h_ref, fin_ref, ss_ref, *, batch, kp, dec_batch, ks, p):
    def swap_halves(x):
        lane = lax.broadcasted_iota(jnp.int32, x.shape, 1)
        return jnp.where(lane % (2 * p) < p, pltpu.roll(x, x.shape[1] - p, 1), pltpu.roll(x, p, 1))

    ss_ref[...] = swap_halves(s_ref[...])
    a1, a2, a2s = a1_ref[...], a2_ref[...], a2s_ref[...]
    zero = jnp.zeros_like(a1)

    def step(k, h, hs):
        h_ref[pl.ds(k, 1), :] = h
        s = s_ref[pl.ds(k, 1), :]
        ss = ss_ref[pl.ds(k, 1), :]
        return a1 * h + a2 * hs + s, a1 * hs + a2s * h + ss

    def body(k, carry):
        out = []
        for b in range(batch):
            out.extend(step(b * kp + k, carry[2 * b], carry[2 * b + 1]))
        return tuple(out)

    fin = lax.fori_loop(0, kp, body, (zero,) * (2 * batch))
    for b in range(batch):
        fin_ref[b:b + 1, :] = fin[2 * b]
    h0s_all = swap_halves(h0_ref[...])
    for b in range(dec_batch):
        h, hs = h0_ref[b:b + 1, :], h0s_all[b:b + 1, :]
        for k in range(ks):
            h, hs = step(batch * kp + b * ks + k, h, hs)
        fin_ref[batch + b:batch + b + 1, :] = h


def _ssm_kernel(u_ref, ft_ref, met_ref, h0_ref, a1_ref, a2_ref, a2s_ref, y_ref, fin_ref, rhs_ref, s_ref, ss_ref, h_ref,
                *, batch, kp, dec_batch, ks, p):
    nk = batch * kp + dec_batch * ks
    _ssm_inputs_phase(u_ref, ft_ref, rhs_ref, s_ref, nk=nk, p=p)
    _ssm_scan_phase(s_ref, h0_ref, a1_ref, a2_ref, a2s_ref, h_ref, fin_ref, ss_ref, batch=batch, kp=kp, dec_batch=dec_batch, ks=ks, p=p)
    _ssm_outputs_phase(rhs_ref, h_ref, met_ref, y_ref, nk=nk, p=p)


def _ssm(u, met, ft, a1, a2, a2s, h0, batch, kp, dec_batch, ks):
    m, sw = u.shape
    ng, p2, lc = ft.shape
    p = p2 // 2
    nk = m // SSM_CHUNK
    assert nk == batch * kp + dec_batch * ks and (nk % LANES) % 8 == 0
    nk_lanes = -(-nk // LANES) * LANES
    groups = LANES // SSM_GROUP
    lb = groups * p2
    nseq = batch + dec_batch
    u_spec = pl.BlockSpec((m, LANES), lambda i: (0, i))
    rowb = pl.BlockSpec((1, lb), lambda i: (0, i))
    blocks = [((m, LANES), F32)] * 2 + [((groups, p2, lc), BF16), ((groups, lc, lc + p2), BF16), ((nseq, lb), F32)]
    scratch = [((groups, lc, nk_lanes), BF16)] + [((nk, lb), F32)] * 3
    return pl.pallas_call(
        functools.partial(_ssm_kernel, batch=batch, kp=kp, dec_batch=dec_batch, ks=ks, p=p),
        grid=(ng // groups,),
        in_specs=[u_spec, pl.BlockSpec((groups, p2, lc), lambda i: (i, 0, 0)), pl.BlockSpec((groups, lc, lc + p2), lambda i: (i, 0, 0)),
                  pl.BlockSpec((dec_batch, lb), lambda i: (0, i)), rowb, rowb, rowb],
        out_specs=[u_spec, pl.BlockSpec((nseq, lb), lambda i: (0, i))],
        out_shape=[jax.ShapeDtypeStruct((m, sw), F32), jax.ShapeDtypeStruct((nseq, ng * p2), F32)],
        scratch_shapes=[pltpu.VMEM(s, d) for s, d in scratch],
        compiler_params=_params(("arbitrary",), _vmem_estimate(blocks, scratch + [((lc + p2, nk_lanes), BF16), ((lc, nk_lanes), F32)])),
        name="ssm",
    )(u, ft, met, h0, a1, a2, a2s)


def _glu_kernel(y_ref, wa_ref, wg_ref, ba_ref, bg_ref, zs_ref, o_ref, wab_ref, wgb_ref):
    _cast_weight(wa_ref, wab_ref)
    _cast_weight(wg_ref, wgb_ref)
    for rows in _row_parts(y_ref.shape[0]):
        y = y_ref[rows, :].astype(BF16)
        a = jnp.dot(y, wab_ref[...], preferred_element_type=F32) + ba_ref[...]
        g = jnp.dot(y, wgb_ref[...], preferred_element_type=F32) + bg_ref[...]
        o_ref[rows, :] = (a * _sigmoid(g) * zs_ref[rows, :].astype(F32)).astype(o_ref.dtype)


def _glu(y, w_glu, b_glu, zs, tm, tn):
    m, sw = zs.shape
    nb = sw // tn
    blocks = [((tm, sw), F32), ((sw, tn), F32), ((sw, tn), F32), ((tm, tn), BF16), ((tm, tn), BF16)]
    return pl.pallas_call(
        _glu_kernel,
        grid=(nb, m // tm),
        in_specs=[
            pl.BlockSpec((tm, sw), lambda j, i: (i, 0)),
            pl.BlockSpec((sw, tn), lambda j, i: (0, j)),
            pl.BlockSpec((sw, tn), lambda j, i: (0, nb + j)),
            pl.BlockSpec((1, tn), lambda j, i: (0, j)),
            pl.BlockSpec((1, tn), lambda j, i: (0, nb + j)),
            pl.BlockSpec((tm, tn), lambda j, i: (i, j)),
        ],
        out_specs=pl.BlockSpec((tm, tn), lambda j, i: (i, j)),
        out_shape=jax.ShapeDtypeStruct((m, sw), BF16),
        scratch_shapes=[pltpu.VMEM((sw, tn), BF16)] * 2,
        compiler_params=_params(("arbitrary", "arbitrary"), _vmem_estimate(blocks, [((tm, tn), F32)] * 4 + [((sw, tn), BF16)] * 2)),
        name="glu",
    )(y, w_glu, w_glu, b_glu, b_glu, zs)


def _merge_kernel(xa_ref, xs_ref, wpa_ref, wps_ref, ga_ref, gs_ref, o_ref, wpab_ref, wpsb_ref):
    _cast_weight(wpa_ref, wpab_ref)
    _cast_weight(wps_ref, wpsb_ref)
    for rows in _row_parts(xa_ref.shape[0]):
        br_a = jnp.dot(xa_ref[rows, :], wpab_ref[...], preferred_element_type=F32)
        br_s = jnp.dot(xs_ref[rows, :], wpsb_ref[...], preferred_element_type=F32)
        o_ref[rows, :] = (ga_ref[rows, :].astype(F32) * br_a + gs_ref[rows, :].astype(F32) * br_s).astype(o_ref.dtype)


def _merge(xa, xs, w_pa, w_ps, gate, tm, tn):
    m, aw = xa.shape
    sw = xs.shape[1]
    d = w_pa.shape[1]
    nb = d // tn
    blocks = [((tm, aw), BF16), ((tm, sw), BF16), ((aw, tn), F32), ((sw, tn), F32)] + [((tm, tn), BF16)] * 3
    return pl.pallas_call(
        _merge_kernel,
        grid=(nb, m // tm),
        in_specs=[
            pl.BlockSpec((tm, aw), lambda j, i: (i, 0)),
            pl.BlockSpec((tm, sw), lambda j, i: (i, 0)),
            pl.BlockSpec((aw, tn), lambda j, i: (0, j)),
            pl.BlockSpec((sw, tn), lambda j, i: (0, j)),
            pl.BlockSpec((tm, tn), lambda j, i: (i, j)),
            pl.BlockSpec((tm, tn), lambda j, i: (i, nb + j)),
        ],
        out_specs=pl.BlockSpec((tm, tn), lambda j, i: (i, j)),
        out_shape=jax.ShapeDtypeStruct((m, d), BF16),
        scratch_shapes=[pltpu.VMEM((aw, tn), BF16), pltpu.VMEM((sw, tn), BF16)],
        compiler_params=_params(("arbitrary", "arbitrary"),
                                _vmem_estimate(blocks, [((tm, tn), F32)] * 4 + [((aw, tn), BF16), ((sw, tn), BF16)])),
        name="merge",
    )(xa, xs, w_pa, w_ps, gate, gate)


def _out_kernel(mg_ref, w_ref, x_ref, g_ref, o_ref, ssq_ref, *, n_col_blocks, tn):
    j = pl.program_id(1)
    blk = x_ref[...] + jnp.dot(mg_ref[...], w_ref[...], preferred_element_type=F32)
    o_ref[:, pl.ds(pl.multiple_of(j * tn, tn), tn)] = blk
    part = jnp.sum(blk * blk, axis=-1, keepdims=True)

    @pl.when(j == 0)
    def _():
        ssq_ref[...] = part

    @pl.when(j > 0)
    def _():
        ssq_ref[...] += part

    @pl.when(j == n_col_blocks - 1)
    def _():
        scale = lax.rsqrt(ssq_ref[...] / (n_col_blocks * tn) + NORM_EPS)
        o_ref[...] = o_ref[...] * scale * g_ref[...]


def _out(merged, row_block0, w_out, x, final_g, tm, tn):
    mx, d = x.shape
    nb = d // tn
    blocks = [((tm, d), BF16), ((d, tn), BF16), ((tm, tn), F32), ((tm, d), F32)]
    return pl.pallas_call(
        functools.partial(_out_kernel, n_col_blocks=nb, tn=tn),
        grid=(mx // tm, nb),
        in_specs=[
            pl.BlockSpec((tm, d), lambda i, j: (row_block0 + i, 0)),
            pl.BlockSpec((d, tn), lambda i, j: (0, j)),
            pl.BlockSpec((tm, tn), lambda i, j: (i, j)),
            pl.BlockSpec((1, d), lambda i, j: (0, 0)),
        ],
        out_specs=pl.BlockSpec((tm, d), lambda i, j: (i, 0)),
        out_shape=jax.ShapeDtypeStruct((mx, d), F32),
        scratch_shapes=[pltpu.VMEM((tm, 1), F32)],
        compiler_params=_params(("arbitrary", "arbitrary"), _vmem_estimate(blocks, [((tm, tn), F32)] * 3 + [((tm, LANES), F32)])),
        name="out_norm",
    )(merged, w_out, x, final_g.reshape(1, d))


def _rope_tables(positions):
    half = HEAD_DIM // 2
    inv_freq = ROPE_THETA ** (-jnp.arange(half, dtype=F32) / half)
    ang = positions.astype(F32)[:, None] * inv_freq[None, :]
    cos, sin = jnp.cos(ang), jnp.sin(ang)
    reps = LANES // HEAD_DIM
    return jnp.tile(jnp.concatenate([cos, cos], axis=1), (1, reps)), jnp.tile(jnp.concatenate([-sin, sin], axis=1), (1, reps))


def kernel(x_prompt, x_sample, cache_k, cache_v, state_ssm_re, state_ssm_im, norm_g, w_in, sink, lambda_re, lambda_im,
           log_dt, b_re, b_im, c_re, c_im, d_skip, w_glu, b_glu, w_pa, w_ps, w_out, final_g):
    depth = norm_g.shape[0]
    assert depth == 1, "one trunk layer"
    batch, seq, d = x_prompt.shape
    dec_batch, dec_seq, _ = x_sample.shape
    aw = w_pa.shape[1]
    sw = w_ps.shape[1]
    nh = aw // HEAD_DIM
    g = max(1, nh // GQA_GROUPING)
    rep = nh // g
    kvw = g * HEAD_DIM
    ng, p = lambda_re.shape[1:]
    assert dec_seq == CHUNK and cache_k.shape[2] == WINDOW and seq % (2 * CHUNK) == 0
    assert kvw % LANES == 0 and rep % 2 == 0 and sw == ng * SSM_GROUP and (1 << SSM_POW_BITS) == SSM_CHUNK
    assert 2 * p == LANES and sw % LANES == 0
    mp, ms = batch * seq, dec_batch * dec_seq
    m = mp + ms
    widths = (aw, 2 * kvw, aw, sw, sw, 2 * d)
    assert sum(widths) == w_in.shape[2]
    c_q, c_kv, c_za, c_u, c_zs, c_gate = (sum(widths[:n]) for n in range(len(widths)))
    tm = _pick(math.gcd(mp, ms), (512, 256, 128))
    tmp = max(t for t in range(16, 1153, 16) if m % t == 0)
    tn = _pick(math.gcd(c_kv, c_za, c_u, c_zs, c_gate, 2 * d), (512, 256, 128))

    xp = x_prompt.reshape(mp, d)
    xs = x_sample.reshape(ms, d)
    w_in2 = w_in.reshape(d, w_in.shape[2])
    positions = jnp.concatenate([jnp.tile(jnp.arange(seq, dtype=jnp.int32), batch),
                                 jnp.tile(PAST_LEN + jnp.arange(dec_seq, dtype=jnp.int32), dec_batch)])
    cos, sin = _rope_tables(positions)

    h = _rmsnorm(xp, xs, norm_g[0], tm)
    q_hm = _proj_q(h, w_in2, cos, sin, aw, tmp, tn)
    k_f, v_f, k_hm, v_hm = _proj_kv(h, w_in2, cos, sin, c_kv, kvw, tmp)
    tna = _pick(math.gcd(aw, sw), (1024, 512, 256))
    za = _proj_act(h, w_in2, c_za, aw, "silu", tmp, tna, "proj_za")
    u = _proj_act(h, w_in2, c_u, sw, "none", tmp, tna, "proj_u", out_dtype=F32)
    zs = _proj_act(h, w_in2, c_zs, sw, "silu", tmp, tna, "proj_zs")
    gate = _proj_act(h, w_in2, c_gate, 2 * d, "sigmoid", tmp, tna, "proj_gate")

    to_heads = lambda c: jnp.transpose(c[0], (2, 0, 1, 3)).astype(BF16)
    new_rows = lambda a: a[:, mp:].reshape(g, dec_batch, dec_seq, HEAD_DIM)
    ks = jnp.concatenate([to_heads(cache_k), new_rows(k_hm)], axis=2).reshape(g, dec_batch * (WINDOW + CHUNK), HEAD_DIM)
    vs = jnp.concatenate([to_heads(cache_v), new_rows(v_hm)], axis=2).reshape(g, dec_batch * (WINDOW + CHUNK), HEAD_DIM)
    sink_rows = jnp.repeat(sink[0].reshape(g, rep), CHUNK, axis=1).reshape(g, 1, rep * CHUNK)
    xa = _attention(q_hm, k_hm, v_hm, ks, vs, sink_rows, za, batch, seq, dec_batch, aw)

    met, ft, a1, a2, a2s = _ssm_params(lambda_re[0], lambda_im[0], log_dt[0], b_re[0], b_im[0], c_re[0], c_im[0], d_skip[0])
    h0 = jnp.concatenate([state_ssm_re[0], state_ssm_im[0]], axis=-1).reshape(dec_batch, ng * 2 * p)
    y, fin = _ssm(u, met, ft, a1, a2, a2s, h0, batch, seq // SSM_CHUNK, dec_batch, dec_seq // SSM_CHUNK)

    tmm = tmp
    x_ssm = _glu(y, w_glu.reshape(sw, 2 * sw), b_glu.reshape(1, 2 * sw), zs, tmm, tn)
    merged = _merge(xa, x_ssm, w_pa.reshape(aw, d), w_ps.reshape(sw, d), gate, tmm, tn)
    w_out_b = w_out[0].astype(BF16)
    tno = _pick(d, (1024, 512, 256))
    y_prompt = _out(merged, 0, w_out_b, xp, final_g, tm, tno).reshape(batch, seq, d)
    y_sample = _out(merged, mp // tm, w_out_b, xs, final_g, tm, tno).reshape(dec_batch, dec_seq, d)

    keep = min(WINDOW, seq)
    last_rows = lambda a: jnp.stack([a[(b + 1) * seq - keep:(b + 1) * seq] for b in range(batch)]).reshape(1, batch, keep, g, HEAD_DIM)
    dec_rows = lambda a: a[mp:].reshape(dec_batch, dec_seq, g, HEAD_DIM)[None]
    fin = fin.reshape(batch + dec_batch, ng, 2, p)
    return (y_prompt, y_sample, last_rows(k_f), last_rows(v_f), fin[:batch, :, 0][None], fin[:batch, :, 1][None],
            dec_rows(k_f), dec_rows(v_f), fin[batch:, :, 0][None], fin[batch:, :, 1][None])
```

```python
import functools
import math

import jax
import jax.numpy as jnp
from jax import lax
from jax.experimental import pallas as pl
from jax.experimental.pallas import tpu as pltpu

CHUNK = 64
WINDOW = 128
HEAD_DIM = 64
GQA_GROUPING = 8
SSM_GROUP = 16
PAST_LEN = 1024
ROPE_THETA = 10000.0
NORM_EPS = 1e-5
LAMBDA_RE_MAX = -1e-4
LOG2_E = math.log2(math.e)

SSM_CHUNK = 16
SSM_POW_BITS = 4
LANES = 128
V7X_VMEM_BYTES = 64 * 1024 * 1024
BF16 = jnp.bfloat16
F32 = jnp.float32


def _pick(n, prefs):
    for p in prefs:
        if n % p == 0:
            return p
    raise ValueError(f"no tile in {prefs} divides {n}")


def _params(sem, vmem_bytes):
    limit = min(int(vmem_bytes), V7X_VMEM_BYTES - 4 * 1024 * 1024)
    return pltpu.CompilerParams(dimension_semantics=sem, vmem_limit_bytes=limit)


def _sigmoid(x):
    return 0.5 * jnp.tanh(0.5 * x) + 0.5


def _nbytes(shape, dtype):
    return math.prod(shape) * jnp.dtype(dtype).itemsize


def _vmem_estimate(blocks, temps=()):
    return 2 * (2 * sum(_nbytes(s, d) for s, d in blocks) + sum(_nbytes(s, d) for s, d in temps))


def _rmsnorm_kernel(xp_ref, xs_ref, g_ref, o_ref, *, n_prompt_blocks):
    i = pl.program_id(0)

    def norm(x_ref):
        x = x_ref[...]
        y = x * lax.rsqrt(jnp.mean(x * x, axis=-1, keepdims=True) + NORM_EPS)
        o_ref[...] = (y * g_ref[...]).astype(o_ref.dtype)

    @pl.when(i < n_prompt_blocks)
    def _():
        norm(xp_ref)

    @pl.when(i >= n_prompt_blocks)
    def _():
        norm(xs_ref)


def _rmsnorm(xp, xs, g, tm):
    mp, d = xp.shape
    ms = xs.shape[0]
    npb, nsb = mp // tm, ms // tm
    return pl.pallas_call(
        functools.partial(_rmsnorm_kernel, n_prompt_blocks=npb),
        grid=(npb + nsb,),
        in_specs=[
            pl.BlockSpec((tm, d), lambda i: (jnp.minimum(i, npb - 1), 0)),
            pl.BlockSpec((tm, d), lambda i: (jnp.maximum(i - npb, 0), 0)),
            pl.BlockSpec((1, d), lambda i: (0, 0)),
        ],
        out_specs=pl.BlockSpec((tm, d), lambda i: (i, 0)),
        out_shape=jax.ShapeDtypeStruct((mp + ms, d), BF16),
        compiler_params=_params(("arbitrary",), _vmem_estimate([((tm, d), F32)] * 2 + [((tm, d), BF16)], [((tm, d), F32)])),
        name="rmsnorm_in",
    )(xp, xs, g.reshape(1, d))


def _rope(acc, cos_ref, sin_ref):
    tm, tn = acc.shape
    reps = tn // LANES
    cos = jnp.tile(cos_ref[...], (1, reps))
    sin = jnp.tile(sin_ref[...], (1, reps))
    lane = lax.broadcasted_iota(jnp.int32, (tm, tn), 1)
    low = (lane % HEAD_DIM) < (HEAD_DIM // 2)
    partner = jnp.where(low, pltpu.roll(acc, tn - HEAD_DIM // 2, 1), pltpu.roll(acc, HEAD_DIM // 2, 1))
    return acc * cos + partner * sin


def _store_heads(o_ref, val):
    for h in range(val.shape[1] // HEAD_DIM):
        o_ref[h] = val[:, h * HEAD_DIM:(h + 1) * HEAD_DIM].astype(o_ref.dtype)


def _cast_weight(w_ref, wb_ref):
    @pl.when(pl.program_id(1) == 0)
    def _():
        wb_ref[...] = w_ref[...].astype(wb_ref.dtype)


def _stream_weights(w_hbm, wbuf_ref, stage_ref, sem_ref, *, col_of_block, n_blocks, n_chunks):
    d, tn = wbuf_ref.shape[1:]
    ck = d // n_chunks
    total = n_blocks * n_chunks
    j = pl.program_id(0)
    t = j * n_chunks + pl.program_id(1)

    def aligned(x, a):
        return x if isinstance(x, int) else pl.multiple_of(x, a)

    def chunk_copy(blk, chunk, slot):
        src = w_hbm.at[pl.ds(aligned(chunk * ck, ck), ck), pl.ds(aligned(col_of_block(blk), LANES), tn)]
        return pltpu.make_async_copy(src, stage_ref.at[slot], sem_ref.at[slot])

    def generation(s):
        s = jnp.asarray(s, jnp.int32)
        k = s + n_chunks - 1
        past = (k > total - 1).astype(jnp.int32)
        k = jnp.minimum(k, total - 1)
        blk, chunk = k // n_chunks, k % n_chunks
        return chunk_copy(blk, chunk, s % 2), (blk + past) % 2, chunk

    def land(copy, half, chunk, slot):
        copy.wait()
        wbuf_ref[half, pl.ds(aligned(chunk * ck, ck), ck), :] = stage_ref[slot].astype(wbuf_ref.dtype)

    @pl.when(t == 0)
    def _():
        slot_of = lambda c: (c + n_chunks - 1) % 2
        first = [chunk_copy(0, c, slot_of(c)) for c in range(n_chunks)]
        first[0].start()
        for c in range(n_chunks - 1):
            first[c + 1].start()
            land(first[c], 0, c, slot_of(c))

    @pl.when(t + 1 < total)
    def _():
        generation(t + 1)[0].start()

    copy, half, chunk = generation(t)
    land(copy, half, chunk, t % 2)
    return wbuf_ref.at[j % 2]


def _row_parts(tm):
    n = 2 if tm % 32 == 0 and tm >= 1024 else 1
    return [slice(r * (tm // n), (r + 1) * (tm // n)) for r in range(n)]


def _proj_q_kernel(h_ref, w_ref, cos_ref, sin_ref, q_ref, wb_ref):
    _cast_weight(w_ref, wb_ref)
    for rows in _row_parts(h_ref.shape[0]):
        acc = jnp.dot(h_ref[rows, :], wb_ref[...], preferred_element_type=F32)
        val = _rope(acc, cos_ref.at[rows, :], sin_ref.at[rows, :]) * (HEAD_DIM ** -0.5 * LOG2_E)
        for h in range(val.shape[1] // HEAD_DIM):
            q_ref[h, rows, :] = val[:, h * HEAD_DIM:(h + 1) * HEAD_DIM].astype(q_ref.dtype)


def _proj_kv_kernel(h_ref, w_ref, cos_ref, sin_ref, kf_ref, vf_ref, kh_ref, vh_ref, wb_ref, *, kvw):
    _cast_weight(w_ref, wb_ref)
    acc = jnp.dot(h_ref[...], wb_ref[...], preferred_element_type=F32)
    k = _rope(acc[:, :kvw], cos_ref, sin_ref)
    v = acc[:, kvw:]
    kf_ref[...] = k
    vf_ref[...] = v
    _store_heads(kh_ref, k)
    _store_heads(vh_ref, v)


def _proj_act_kernel(h_ref, w_hbm, o_ref, wbuf_ref, stage_ref, sem_ref, *, segments, n_chunks):
    tn = o_ref.shape[1]
    n_blocks = sum(s[1] for s in segments)
    in_seg = lambda blk, s: (blk >= s[0]) & (blk < s[0] + s[1])

    def col_of_block(blk):
        if isinstance(blk, int):
            return next(s[2] + (blk - s[0]) * tn for s in segments if s[0] <= blk < s[0] + s[1])
        return sum(jnp.where(in_seg(blk, s), s[2] + (blk - s[0]) * tn, 0) for s in segments)

    wb_ref = _stream_weights(w_hbm, wbuf_ref, stage_ref, sem_ref, col_of_block=col_of_block, n_blocks=n_blocks, n_chunks=n_chunks)
    acts = {s[3] for s in segments}
    assert acts in ({"none"}, {"silu"}, {"sigmoid"}, {"silu", "sigmoid"})
    j = pl.program_id(0)
    for rows in _row_parts(h_ref.shape[0]):
        acc = jnp.dot(h_ref[rows, :], wb_ref[...], preferred_element_type=F32)
        if acts != {"none"}:
            sig = _sigmoid(acc)
            if acts == {"silu"}:
                acc = acc * sig
            elif acts == {"sigmoid"}:
                acc = sig
            else:
                is_silu = functools.reduce(lambda a, b: a | b, [in_seg(j, s) for s in segments if s[3] == "silu"])
                acc = jnp.where(is_silu, acc * sig, sig)
        o_ref[rows, :] = acc.astype(o_ref.dtype)


def _proj_specs(d, tm, tn, col0):
    assert col0 % tn == 0
    cb0 = col0 // tn
    return [pl.BlockSpec((tm, d), lambda j, i: (i, 0)), pl.BlockSpec((d, tn), lambda j, i: (0, cb0 + j))]


def _proj_vmem(d, tm, tn, outs):
    return _vmem_estimate([((tm, d), BF16), ((d, tn), F32)] + outs, [((tm, tn), F32)] * 3 + [((d, tn), BF16)])


def _proj_q(h, w, cos, sin, aw, tm, tn):
    m, d = h.shape
    nh = aw // HEAD_DIM
    tab = pl.BlockSpec((tm, LANES), lambda j, i: (i, 0))
    return pl.pallas_call(
        _proj_q_kernel,
        grid=(aw // tn, m // tm),
        in_specs=_proj_specs(d, tm, tn, 0) + [tab, tab],
        out_specs=pl.BlockSpec((tn // HEAD_DIM, tm, HEAD_DIM), lambda j, i: (j, i, 0)),
        out_shape=jax.ShapeDtypeStruct((nh, m, HEAD_DIM), BF16),
        scratch_shapes=[pltpu.VMEM((d, tn), BF16)],
        compiler_params=_params(("arbitrary", "arbitrary"), _proj_vmem(d, tm, tn, [((tm, 2 * tn), BF16), ((tm, 2 * LANES), F32)])),
        name="proj_q",
    )(h, w, cos, sin)


def _proj_kv(h, w, cos, sin, col0, kvw, tm):
    m, d = h.shape
    g = kvw // HEAD_DIM
    tn = 2 * kvw
    tab = pl.BlockSpec((tm, LANES), lambda j, i: (i, 0))
    flat = pl.BlockSpec((tm, kvw), lambda j, i: (i, 0))
    heads = pl.BlockSpec((g, tm, HEAD_DIM), lambda j, i: (0, i, 0))
    return pl.pallas_call(
        functools.partial(_proj_kv_kernel, kvw=kvw),
        grid=(1, m // tm),
        in_specs=_proj_specs(d, tm, tn, col0) + [tab, tab],
        out_specs=[flat, flat, heads, heads],
        out_shape=[jax.ShapeDtypeStruct((m, kvw), F32)] * 2 + [jax.ShapeDtypeStruct((g, m, HEAD_DIM), BF16)] * 2,
        scratch_shapes=[pltpu.VMEM((d, tn), BF16)],
        compiler_params=_params(("arbitrary", "arbitrary"), _proj_vmem(d, tm, tn, [((tm, 2 * tn), F32), ((tm, 2 * tn), BF16), ((tm, 2 * LANES), F32)])),
        name="proj_kv",
    )(h, w, cos, sin)


def _proj_act(h, w, runs, tm, tn, name, out_dtype=BF16):
    m, d = h.shape
    n_chunks = m // tm
    segments, b0 = [], 0
    for col0, width, act in runs:
        assert width % tn == 0 and col0 % LANES == 0
        segments.append((b0, width // tn, col0, act))
        b0 += width // tn
    n_blocks, ncols = b0, b0 * tn
    assert d % (8 * n_chunks) == 0
    scratch = [((2, d, tn), BF16), ((2, d // n_chunks, tn), F32)]
    return pl.pallas_call(
        functools.partial(_proj_act_kernel, segments=tuple(segments), n_chunks=n_chunks),
        grid=(n_blocks, n_chunks),
        in_specs=[pl.BlockSpec((tm, d), lambda j, i: (i, 0)), pl.BlockSpec(memory_space=pl.ANY)],
        out_specs=pl.BlockSpec((tm, tn), lambda j, i: (i, j)),
        out_shape=jax.ShapeDtypeStruct((m, ncols), out_dtype),
        scratch_shapes=[pltpu.VMEM(s, t) for s, t in scratch] + [pltpu.SemaphoreType.DMA((2,))],
        compiler_params=_params(("arbitrary", "arbitrary"),
                                _vmem_estimate([((tm, d), BF16), ((tm, tn), out_dtype)], scratch + [((tm // len(_row_parts(tm)), tn), F32)] * 3)),
        name=name,
    )(h, w)


def _attn_weights(q, kw, sink, n_valid):
    st = lax.dot_general(kw, q, (((1,), (1,)), ((), ())), preferred_element_type=F32)
    if n_valid is not None:
        row = lax.broadcasted_iota(jnp.int32, st.shape, 0)
        st = jnp.where(row < n_valid, st, -jnp.inf)
    m = jnp.maximum(jnp.max(st, axis=0, keepdims=True), sink)
    e = jnp.exp2(st - m)
    return e.astype(BF16), jnp.sum(e, axis=0, keepdims=True) + jnp.exp2(sink - m)


def _attn_values(vw, e, denom):
    ot = jnp.dot(vw.astype(F32).T.astype(BF16), e, preferred_element_type=F32) / denom
    return ot.T


def _store_unit(o_ref, za_ref, row0, o, rep):
    for r in range(0, rep, 2):
        pair = jnp.concatenate([o[r * CHUNK:(r + 1) * CHUNK], o[(r + 1) * CHUNK:(r + 2) * CHUNK]], axis=1)
        gate = za_ref[row0:row0 + CHUNK, r * HEAD_DIM:(r + 2) * HEAD_DIM].astype(F32)
        o_ref[row0:row0 + CHUNK, r * HEAD_DIM:(r + 2) * HEAD_DIM] = (pair * gate).astype(o_ref.dtype)


def _attn_kernel(q_ref, k_ref, v_ref, ks_ref, vs_ref, sink_ref, za_ref, o_ref, *, n_units, rep, steps_per_stream, n_prompt_steps):
    step = pl.program_id(1)
    win = WINDOW + CHUNK
    sink = sink_ref[0] * LOG2_E

    def run(windows):
        staged = []
        for c, (kw, vw, n_valid) in enumerate(windows):
            q = q_ref[:, c * CHUNK:(c + 1) * CHUNK, :].reshape(rep * CHUNK, HEAD_DIM)
            staged.append((vw,) + _attn_weights(q, kw, sink, n_valid))
        for c in range(n_units):
            _store_unit(o_ref, za_ref, c * CHUNK, _attn_values(*staged[c]), rep)

    @pl.when(step < n_prompt_steps)
    def _():
        cb = step % steps_per_stream
        windows = []
        for c in range(n_units):
            start = pl.multiple_of(jnp.maximum(cb * n_units + c - WINDOW // CHUNK, 0) * CHUNK, CHUNK)
            n_valid = jnp.where(cb == 0, (c + 1) * CHUNK, win) if c < WINDOW // CHUNK else None
            windows.append((k_ref[0, pl.ds(start, win), :], v_ref[0, pl.ds(start, win), :], n_valid))
        run(windows)

    @pl.when(step == n_prompt_steps)
    def _():
        run([(ks_ref[0, b * win:(b + 1) * win, :], vs_ref[0, b * win:(b + 1) * win, :], None) for b in range(n_units)])


def _attention(q_hm, k_hm, v_hm, ks, vs, sink_rows, za, batch, seq, dec_batch, aw):
    nh, m, _ = q_hm.shape
    g = k_hm.shape[0]
    rep = nh // g
    win = WINDOW + CHUNK
    n_units = dec_batch
    rows = n_units * CHUNK
    assert seq % rows == 0 and n_units >= WINDOW // CHUNK
    steps_per_stream = seq // rows
    n_prompt_steps = batch * steps_per_stream
    blocks = ([((rep, rows, LANES), BF16)] + [((seq, LANES), BF16)] * 2 + [((dec_batch * win, LANES), BF16)] * 2
              + [((rows, rep * HEAD_DIM), BF16)] * 2)
    temps = [((rep * CHUNK, 2 * LANES), F32)] * (4 * n_units)
    stream = lambda gi, s: (gi, jnp.minimum(s // steps_per_stream, batch - 1), 0)
    return pl.pallas_call(
        functools.partial(_attn_kernel, n_units=n_units, rep=rep, steps_per_stream=steps_per_stream, n_prompt_steps=n_prompt_steps),
        grid=(g, n_prompt_steps + 1),
        in_specs=[
            pl.BlockSpec((rep, rows, HEAD_DIM), lambda gi, s: (gi, s, 0)),
            pl.BlockSpec((1, seq, HEAD_DIM), stream),
            pl.BlockSpec((1, seq, HEAD_DIM), stream),
            pl.BlockSpec((1, dec_batch * win, HEAD_DIM), lambda gi, s: (gi, 0, 0)),
            pl.BlockSpec((1, dec_batch * win, HEAD_DIM), lambda gi, s: (gi, 0, 0)),
            pl.BlockSpec((1, 1, rep * CHUNK), lambda gi, s: (gi, 0, 0)),
            pl.BlockSpec((rows, rep * HEAD_DIM), lambda gi, s: (s, gi)),
        ],
        out_specs=pl.BlockSpec((rows, rep * HEAD_DIM), lambda gi, s: (s, gi)),
        out_shape=jax.ShapeDtypeStruct((m, aw), BF16),
        compiler_params=_params(("arbitrary",) * 2, _vmem_estimate(blocks, temps)),
        name="attention",
    )(q_hm, k_hm, v_hm, ks, vs, sink_rows, za)


def _ssm_disc_kernel(lre_ref, lim_ref, ldt_ref, are_ref, aim_ref, dre_ref, dim_ref, fre_ref, fim_ref):
    lr = jnp.minimum(lre_ref[...], LAMBDA_RE_MAX)
    li = lim_ref[...]
    dt = jnp.exp(ldt_ref[...])
    mag = jnp.exp(lr * dt)
    a_re = mag * jnp.cos(li * dt)
    a_im = mag * jnp.sin(li * dt)
    den = lr * lr + li * li
    nr = a_re - 1.0
    fre_ref[...] = (nr * lr + a_im * li) / den
    fim_ref[...] = (a_im * lr - nr * li) / den
    are_ref[...] = a_re
    aim_ref[...] = a_im
    for _ in range(SSM_POW_BITS):
        a_re, a_im = a_re * a_re - a_im * a_im, 2.0 * a_re * a_im
    dre_ref[...] = a_re
    dim_ref[...] = a_im


def _cmul(ar, ai, br, bi):
    return ar * br - ai * bi, ar * bi + ai * br


def _ssm_build_kernel(are_ref, aim_ref, dre_ref, dim_ref, fre_ref, fim_ref, btr_ref, bti_ref, cr_ref, ci_ref, dv_ref,
                      met_ref, ft_ref, a1_ref, a2_ref, a2s_ref, *, groups):
    lc = SSM_CHUNK * SSM_GROUP
    sub = lax.broadcasted_iota(jnp.int32, (SSM_GROUP, lc), 0)
    lane = lax.broadcasted_iota(jnp.int32, (SSM_GROUP, lc), 1)
    for gi in range(groups):
        row = slice(gi, gi + 1)
        a_re, a_im = are_ref[row, :], aim_ref[row, :]
        pw = [(jnp.ones_like(a_re), jnp.zeros_like(a_re))]
        for _ in range(SSM_CHUNK):
            pw.append(_cmul(pw[-1][0], pw[-1][1], a_re, a_im))
        c_re, c_im = cr_ref[gi], ci_ref[gi]
        wt = [_cmul(pr, pi, c_re, c_im) for pr, pi in pw]
        wt_re = jnp.concatenate([w[0] for w in wt[:SSM_CHUNK]], axis=0)
        wt_im = jnp.concatenate([w[1] for w in wt[:SSM_CHUNK]], axis=0)
        et_re = jnp.concatenate([w[0] for w in wt[1:]], axis=0)
        et_im = jnp.concatenate([w[1] for w in wt[1:]], axis=0)
        bb_re, bb_im = _cmul(fre_ref[row, :], fim_ref[row, :], btr_ref[gi], bti_ref[gi])
        r0 = lax.dot_general(jnp.concatenate([bb_re, bb_im], axis=1), jnp.concatenate([wt_re, -wt_im], axis=1),
                             (((1,), (1,)), ((), ())), preferred_element_type=F32, precision=lax.Precision.HIGHEST)
        r0 = r0 + jnp.where(sub == lane, dv_ref[gi], 0.0)
        rows = [r0] + [jnp.where(lane >= s * SSM_GROUP, pltpu.roll(r0, s * SSM_GROUP, 1), 0.0) for s in range(1, SSM_CHUNK)]
        mt = jnp.concatenate(rows, axis=0).T
        met_ref[gi] = jnp.concatenate([mt, et_re, -et_im], axis=1).astype(met_ref.dtype)
        fb = [_cmul(pw[SSM_CHUNK - 1 - s][0], pw[SSM_CHUNK - 1 - s][1], bb_re, bb_im) for s in range(SSM_CHUNK)]
        f_all = jnp.concatenate([jnp.concatenate([x[0] for x in fb], axis=0), jnp.concatenate([x[1] for x in fb], axis=0)], axis=1)
        ft_ref[gi] = f_all.T.astype(ft_ref.dtype)
        d_re, d_im = dre_ref[row, :], dim_ref[row, :]
        a1_ref[gi] = jnp.concatenate([d_re, d_re], axis=1)
        a2_ref[gi] = jnp.concatenate([-d_im, d_im], axis=1)
        a2s_ref[gi] = jnp.concatenate([d_im, -d_im], axis=1)


def _ssm_params(lambda_re, lambda_im, log_dt, b_re, b_im, c_re, c_im, d_skip):
    ng, p = lambda_re.shape
    lc = SSM_CHUNK * SSM_GROUP
    full = pl.BlockSpec((ng, p), lambda: (0, 0))
    disc = pl.pallas_call(
        _ssm_disc_kernel,
        in_specs=[full, full, pl.BlockSpec((ng, 1), lambda: (0, 0))],
        out_specs=[full] * 6,
        out_shape=[jax.ShapeDtypeStruct((ng, p), F32)] * 6,
        name="ssm_disc",
    )(lambda_re, lambda_im, log_dt.reshape(ng, 1))
    gb = _pick(ng, (8,))
    bt_re = jnp.swapaxes(b_re, 1, 2)
    bt_im = jnp.swapaxes(b_im, 1, 2)
    dvec = jnp.pad(d_skip, ((0, 0), (0, lc - SSM_GROUP))).reshape(ng, 1, lc)
    rows = pl.BlockSpec((gb, p), lambda i: (i, 0))
    mats = pl.BlockSpec((gb, SSM_GROUP, p), lambda i: (i, 0, 0))
    dec = pl.BlockSpec((gb, 1, 2 * p), lambda i: (i, 0, 0))
    blocks = ([((gb, LANES), F32)] * 6 + [((gb, SSM_GROUP, LANES), F32)] * 4 + [((gb, 8, lc), F32)]
              + [((gb, lc, lc + 2 * p), BF16), ((gb, 2 * p, lc), BF16)] + [((gb, 8, LANES), F32)] * 3)
    met, ft, a1, a2, a2s = pl.pallas_call(
        functools.partial(_ssm_build_kernel, groups=gb),
        grid=(ng // gb,),
        in_specs=[rows] * 6 + [mats] * 4 + [pl.BlockSpec((gb, 1, lc), lambda i: (i, 0, 0))],
        out_specs=[pl.BlockSpec((gb, lc, lc + 2 * p), lambda i: (i, 0, 0)), pl.BlockSpec((gb, 2 * p, lc), lambda i: (i, 0, 0)), dec, dec, dec],
        out_shape=[jax.ShapeDtypeStruct((ng, lc, lc + 2 * p), BF16), jax.ShapeDtypeStruct((ng, 2 * p, lc), BF16)]
        + [jax.ShapeDtypeStruct((ng, 1, 2 * p), F32)] * 3,
        compiler_params=_params(("arbitrary",), _vmem_estimate(blocks, [((lc, lc + 2 * p), F32)] * 8)),
        name="ssm_build",
    )(*disc, bt_re, bt_im, c_re, c_im, dvec)
    flat = lambda a: a.reshape(1, ng * 2 * p)
    return met, ft, flat(a1), flat(a2), flat(a2s)


def _chunk_blocks(nk):
    return [(k0, min(LANES, nk - k0)) for k0 in range(0, nk, LANES)]


def _pad_rows(x):
    n = x.shape[0]
    return x if n == LANES else jnp.concatenate([x, jnp.zeros((LANES - n, x.shape[1]), x.dtype)], axis=0)


def _ssm_inputs_phase(u_ref, ft_ref, rhs_ref, s_ref, *, nk, p):
    groups = LANES // SSM_GROUP
    blocks = _chunk_blocks(nk)
    xt = [[_pad_rows(u_ref[pl.ds(k0 * SSM_CHUNK + s, n, stride=SSM_CHUNK), :].astype(BF16)).T for k0, n in blocks]
          for s in range(SSM_CHUNK)]
    for gi in range(groups):
        rhs = jnp.concatenate([jnp.concatenate([xt[s][kb][gi * SSM_GROUP:(gi + 1) * SSM_GROUP, :] for kb in range(len(blocks))], axis=1)
                               for s in range(SSM_CHUNK)], axis=0)
        rhs_ref[gi] = rhs
        st = jnp.dot(ft_ref[gi], rhs, preferred_element_type=F32)
        for kb, (k0, n) in enumerate(blocks):
            s_ref[k0:k0 + n, gi * 2 * p:(gi + 1) * 2 * p] = st[:, kb * LANES:(kb + 1) * LANES].T[:n, :]


def _ssm_outputs_phase(rhs_ref, h_ref, met_ref, y_ref, *, nk, p):
    groups = LANES // SSM_GROUP
    blocks = _chunk_blocks(nk)
    zt = [[None] * groups for _ in range(SSM_CHUNK)]
    for gi in range(groups):
        ht = jnp.concatenate([_pad_rows(h_ref[k0:k0 + n, gi * 2 * p:(gi + 1) * 2 * p]).T for k0, n in blocks], axis=1)
        rhs = jnp.concatenate([rhs_ref[gi], ht.astype(BF16)], axis=0)
        yt = jnp.dot(met_ref[gi], rhs, preferred_element_type=F32)
        for t in range(SSM_CHUNK):
            zt[t][gi] = yt[t * SSM_GROUP:(t + 1) * SSM_GROUP, :]
    for t in range(SSM_CHUNK):
        z = jnp.concatenate(zt[t], axis=0)
        for kb, (k0, n) in enumerate(blocks):
            y_ref[pl.ds(k0 * SSM_CHUNK + t, n, stride=SSM_CHUNK), :] = z[:, kb * LANES:(kb + 1) * LANES].T[:n, :]


def _ssm_scan_phase(s_ref, h0_ref, a1_ref, a2_ref, a2s_ref, h_ref, fin_ref, ss_ref, *, batch, kp, dec_batch, ks, p):
    def swap_halves(x):
        lane = lax.broadcasted_iota(jnp.int32, x.shape, 1)
        return jnp.where(lane % (2 * p) < p, pltpu.roll(x, x.shape[1] - p, 1), pltpu.roll(x, p, 1))

    ss_ref[...] = swap_halves(s_ref[...])
    a1, a2, a2s = a1_ref[...], a2_ref[...], a2s_ref[...]
    zero = jnp.zeros_like(a1)

    def step(k, h, hs):
        h_ref[pl.ds(k, 1), :] = h
        s = s_ref[pl.ds(k, 1), :]
        ss = ss_ref[pl.ds(k, 1), :]
        return a1 * h + a2 * hs + s, a1 * hs + a2s * h + ss

    def body(k, carry):
        out = []
        for b in range(batch):
            out.extend(step(b * kp + k, carry[2 * b], carry[2 * b + 1]))
        return tuple(out)

    fin = lax.fori_loop(0, kp, body, (zero,) * (2 * batch))
    for b in range(batch):
        fin_ref[b:b + 1, :] = fin[2 * b]
    h0s_all = swap_halves(h0_ref[...])
    for b in range(dec_batch):
        h, hs = h0_ref[b:b + 1, :], h0s_all[b:b + 1, :]
        for k in range(ks):
            h, hs = step(batch * kp + b * ks + k, h, hs)
        fin_ref[batch + b:batch + b + 1, :] = h


def _ssm_kernel(u_ref, ft_ref, met_ref, h0_ref, a1_ref, a2_ref, a2s_ref, y_ref, fin_ref, rhs_ref, s_ref, ss_ref, h_ref,
                *, batch, kp, dec_batch, ks, p):
    nk = batch * kp + dec_batch * ks
    _ssm_inputs_phase(u_ref, ft_ref, rhs_ref, s_ref, nk=nk, p=p)
    _ssm_scan_phase(s_ref, h0_ref, a1_ref, a2_ref, a2s_ref, h_ref, fin_ref, ss_ref, batch=batch, kp=kp, dec_batch=dec_batch, ks=ks, p=p)
    _ssm_outputs_phase(rhs_ref, h_ref, met_ref, y_ref, nk=nk, p=p)


def _ssm(u, met, ft, a1, a2, a2s, h0, batch, kp, dec_batch, ks):
    m, sw = u.shape
    ng, p2, lc = ft.shape
    p = p2 // 2
    nk = m // SSM_CHUNK
    assert nk == batch * kp + dec_batch * ks and (nk % LANES) % 8 == 0
    nk_lanes = -(-nk // LANES) * LANES
    groups = LANES // SSM_GROUP
    lb = groups * p2
    nseq = batch + dec_batch
    u_spec = pl.BlockSpec((m, LANES), lambda i: (0, i))
    rowb = pl.BlockSpec((1, lb), lambda i: (0, i))
    blocks = [((m, LANES), F32)] * 2 + [((groups, p2, lc), BF16), ((groups, lc, lc + p2), BF16), ((nseq, lb), F32)]
    scratch = [((groups, lc, nk_lanes), BF16)] + [((nk, lb), F32)] * 3
    return pl.pallas_call(
        functools.partial(_ssm_kernel, batch=batch, kp=kp, dec_batch=dec_batch, ks=ks, p=p),
        grid=(ng // groups,),
        in_specs=[u_spec, pl.BlockSpec((groups, p2, lc), lambda i: (i, 0, 0)), pl.BlockSpec((groups, lc, lc + p2), lambda i: (i, 0, 0)),
                  pl.BlockSpec((dec_batch, lb), lambda i: (0, i)), rowb, rowb, rowb],
        out_specs=[u_spec, pl.BlockSpec((nseq, lb), lambda i: (0, i))],
        out_shape=[jax.ShapeDtypeStruct((m, sw), F32), jax.ShapeDtypeStruct((nseq, ng * p2), F32)],
        scratch_shapes=[pltpu.VMEM(s, d) for s, d in scratch],
        compiler_params=_params(("arbitrary",), _vmem_estimate(blocks, scratch + [((lc + p2, nk_lanes), BF16), ((lc, nk_lanes), F32)])),
        name="ssm",
    )(u, ft, met, h0, a1, a2, a2s)


def _glu_kernel(y_ref, wa_ref, wg_ref, ba_ref, bg_ref, zs_ref, o_ref, wab_ref, wgb_ref):
    _cast_weight(wa_ref, wab_ref)
    _cast_weight(wg_ref, wgb_ref)
    for rows in _row_parts(y_ref.shape[0]):
        y = y_ref[rows, :].astype(BF16)
        a = jnp.dot(y, wab_ref[...], preferred_element_type=F32) + ba_ref[...]
        g = jnp.dot(y, wgb_ref[...], preferred_element_type=F32) + bg_ref[...]
        o_ref[rows, :] = (a * _sigmoid(g) * zs_ref[rows, :].astype(F32)).astype(o_ref.dtype)


def _glu(y, w_glu, b_glu, acts, zs_col0, tm, tn):
    m, sw = y.shape
    nb = sw // tn
    zb0 = zs_col0 // tn
    blocks = [((tm, sw), F32), ((sw, tn), F32), ((sw, tn), F32), ((tm, tn), BF16), ((tm, tn), BF16)]
    return pl.pallas_call(
        _glu_kernel,
        grid=(nb, m // tm),
        in_specs=[
            pl.BlockSpec((tm, sw), lambda j, i: (i, 0)),
            pl.BlockSpec((sw, tn), lambda j, i: (0, j)),
            pl.BlockSpec((sw, tn), lambda j, i: (0, nb + j)),
            pl.BlockSpec((1, tn), lambda j, i: (0, j)),
            pl.BlockSpec((1, tn), lambda j, i: (0, nb + j)),
            pl.BlockSpec((tm, tn), lambda j, i: (i, zb0 + j)),
        ],
        out_specs=pl.BlockSpec((tm, tn), lambda j, i: (i, j)),
        out_shape=jax.ShapeDtypeStruct((m, sw), BF16),
        scratch_shapes=[pltpu.VMEM((sw, tn), BF16)] * 2,
        compiler_params=_params(("arbitrary", "arbitrary"), _vmem_estimate(blocks, [((tm, tn), F32)] * 4 + [((sw, tn), BF16)] * 2)),
        name="glu",
    )(y, w_glu, w_glu, b_glu, b_glu, acts)


def _merge_kernel(xa_ref, xs_ref, wpa_ref, wps_ref, ga_ref, gs_ref, o_ref, wpab_ref, wpsb_ref):
    _cast_weight(wpa_ref, wpab_ref)
    _cast_weight(wps_ref, wpsb_ref)
    for rows in _row_parts(xa_ref.shape[0]):
        br_a = jnp.dot(xa_ref[rows, :], wpab_ref[...], preferred_element_type=F32)
        br_s = jnp.dot(xs_ref[rows, :], wpsb_ref[...], preferred_element_type=F32)
        o_ref[rows, :] = (ga_ref[rows, :].astype(F32) * br_a + gs_ref[rows, :].astype(F32) * br_s).astype(o_ref.dtype)


def _merge(xa, xs, w_pa, w_ps, acts, gate_col0, tm, tn):
    m, aw = xa.shape
    sw = xs.shape[1]
    d = w_pa.shape[1]
    nb = d // tn
    gb0 = gate_col0 // tn
    blocks = [((tm, aw), BF16), ((tm, sw), BF16), ((aw, tn), F32), ((sw, tn), F32)] + [((tm, tn), BF16)] * 3
    return pl.pallas_call(
        _merge_kernel,
        grid=(nb, m // tm),
        in_specs=[
            pl.BlockSpec((tm, aw), lambda j, i: (i, 0)),
            pl.BlockSpec((tm, sw), lambda j, i: (i, 0)),
            pl.BlockSpec((aw, tn), lambda j, i: (0, j)),
            pl.BlockSpec((sw, tn), lambda j, i: (0, j)),
            pl.BlockSpec((tm, tn), lambda j, i: (i, gb0 + j)),
            pl.BlockSpec((tm, tn), lambda j, i: (i, gb0 + nb + j)),
        ],
        out_specs=pl.BlockSpec((tm, tn), lambda j, i: (i, j)),
        out_shape=jax.ShapeDtypeStruct((m, d), BF16),
        scratch_shapes=[pltpu.VMEM((aw, tn), BF16), pltpu.VMEM((sw, tn), BF16)],
        compiler_params=_params(("arbitrary", "arbitrary"),
                                _vmem_estimate(blocks, [((tm, tn), F32)] * 4 + [((aw, tn), BF16), ((sw, tn), BF16)])),
        name="merge",
    )(xa, xs, w_pa, w_ps, acts, acts)


def _out_kernel(mg_ref, w_ref, x_ref, g_ref, o_ref, ssq_ref, *, n_col_blocks, tn):
    j = pl.program_id(1)
    blk = x_ref[...] + jnp.dot(mg_ref[...], w_ref[...], preferred_element_type=F32)
    o_ref[:, pl.ds(pl.multiple_of(j * tn, tn), tn)] = blk
    part = jnp.sum(blk * blk, axis=-1, keepdims=True)

    @pl.when(j == 0)
    def _():
        ssq_ref[...] = part

    @pl.when(j > 0)
    def _():
        ssq_ref[...] += part

    @pl.when(j == n_col_blocks - 1)
    def _():
        scale = lax.rsqrt(ssq_ref[...] / (n_col_blocks * tn) + NORM_EPS)
        o_ref[...] = o_ref[...] * scale * g_ref[...]


def _out(merged, row_block0, w_out, x, final_g, tm, tn):
    mx, d = x.shape
    nb = d // tn
    blocks = [((tm, d), BF16), ((d, tn), BF16), ((tm, tn), F32), ((tm, d), F32)]
    return pl.pallas_call(
        functools.partial(_out_kernel, n_col_blocks=nb, tn=tn),
        grid=(mx // tm, nb),
        in_specs=[
            pl.BlockSpec((tm, d), lambda i, j: (row_block0 + i, 0)),
            pl.BlockSpec((d, tn), lambda i, j: (0, j)),
            pl.BlockSpec((tm, tn), lambda i, j: (i, j)),
            pl.BlockSpec((1, d), lambda i, j: (0, 0)),
        ],
        out_specs=pl.BlockSpec((tm, d), lambda i, j: (i, 0)),
        out_shape=jax.ShapeDtypeStruct((mx, d), F32),
        scratch_shapes=[pltpu.VMEM((tm, 1), F32)],
        compiler_params=_params(("arbitrary", "arbitrary"), _vmem_estimate(blocks, [((tm, tn), F32)] * 3 + [((tm, LANES), F32)])),
        name="out_norm",
    )(merged, w_out, x, final_g.reshape(1, d))


def _rope_tables(positions):
    half = HEAD_DIM // 2
    inv_freq = ROPE_THETA ** (-jnp.arange(half, dtype=F32) / half)
    ang = positions.astype(F32)[:, None] * inv_freq[None, :]
    cos, sin = jnp.cos(ang), jnp.sin(ang)
    reps = LANES // HEAD_DIM
    return jnp.tile(jnp.concatenate([cos, cos], axis=1), (1, reps)), jnp.tile(jnp.concatenate([-sin, sin], axis=1), (1, reps))


def kernel(x_prompt, x_sample, cache_k, cache_v, state_ssm_re, state_ssm_im, norm_g, w_in, sink, lambda_re, lambda_im,
           log_dt, b_re, b_im, c_re, c_im, d_skip, w_glu, b_glu, w_pa, w_ps, w_out, final_g):
    depth = norm_g.shape[0]
    assert depth == 1, "one trunk layer"
    batch, seq, d = x_prompt.shape
    dec_batch, dec_seq, _ = x_sample.shape
    aw = w_pa.shape[1]
    sw = w_ps.shape[1]
    nh = aw // HEAD_DIM
    g = max(1, nh // GQA_GROUPING)
    rep = nh // g
    kvw = g * HEAD_DIM
    ng, p = lambda_re.shape[1:]
    assert dec_seq == CHUNK and cache_k.shape[2] == WINDOW and seq % (2 * CHUNK) == 0
    assert kvw % LANES == 0 and rep % 2 == 0 and sw == ng * SSM_GROUP and (1 << SSM_POW_BITS) == SSM_CHUNK
    assert 2 * p == LANES and sw % LANES == 0
    mp, ms = batch * seq, dec_batch * dec_seq
    m = mp + ms
    widths = (aw, 2 * kvw, aw, sw, sw, 2 * d)
    assert sum(widths) == w_in.shape[2]
    c_q, c_kv, c_za, c_u, c_zs, c_gate = (sum(widths[:n]) for n in range(len(widths)))
    tm = _pick(math.gcd(mp, ms), (512, 256, 128))
    tmp = max(t for t in range(16, 1153, 16) if m % t == 0)
    tn = _pick(math.gcd(c_kv, c_za, c_u, c_zs, c_gate, 2 * d), (512, 256, 128))

    xp = x_prompt.reshape(mp, d)
    xs = x_sample.reshape(ms, d)
    w_in2 = w_in.reshape(d, w_in.shape[2])
    positions = jnp.concatenate([jnp.tile(jnp.arange(seq, dtype=jnp.int32), batch),
                                 jnp.tile(PAST_LEN + jnp.arange(dec_seq, dtype=jnp.int32), dec_batch)])
    cos, sin = _rope_tables(positions)

    h = _rmsnorm(xp, xs, norm_g[0], tm)
    q_hm = _proj_q(h, w_in2, cos, sin, aw, tmp, tn)
    k_f, v_f, k_hm, v_hm = _proj_kv(h, w_in2, cos, sin, c_kv, kvw, tmp)
    tna = _pick(math.gcd(aw, sw), (1024, 512, 256))
    acts = _proj_act(h, w_in2, [(c_za, aw, "silu"), (c_zs, sw, "silu"), (c_gate, 2 * d, "sigmoid")], tmp, tna, "proj_acts")
    u = _proj_act(h, w_in2, [(c_u, sw, "none")], tmp, tna, "proj_u", out_dtype=F32)

    to_heads = lambda c: jnp.transpose(c[0], (2, 0, 1, 3)).astype(BF16)
    new_rows = lambda a: a[:, mp:].reshape(g, dec_batch, dec_seq, HEAD_DIM)
    ks = jnp.concatenate([to_heads(cache_k), new_rows(k_hm)], axis=2).reshape(g, dec_batch * (WINDOW + CHUNK), HEAD_DIM)
    vs = jnp.concatenate([to_heads(cache_v), new_rows(v_hm)], axis=2).reshape(g, dec_batch * (WINDOW + CHUNK), HEAD_DIM)
    sink_rows = jnp.repeat(sink[0].reshape(g, rep), CHUNK, axis=1).reshape(g, 1, rep * CHUNK)
    xa = _attention(q_hm, k_hm, v_hm, ks, vs, sink_rows, acts, batch, seq, dec_batch, aw)

    met, ft, a1, a2, a2s = _ssm_params(lambda_re[0], lambda_im[0], log_dt[0], b_re[0], b_im[0], c_re[0], c_im[0], d_skip[0])
    h0 = jnp.concatenate([state_ssm_re[0], state_ssm_im[0]], axis=-1).reshape(dec_batch, ng * 2 * p)
    y, fin = _ssm(u, met, ft, a1, a2, a2s, h0, batch, seq // SSM_CHUNK, dec_batch, dec_seq // SSM_CHUNK)

    tmm = tmp
    x_ssm = _glu(y, w_glu.reshape(sw, 2 * sw), b_glu.reshape(1, 2 * sw), acts, aw, tmm, tn)
    merged = _merge(xa, x_ssm, w_pa.reshape(aw, d), w_ps.reshape(sw, d), acts, aw + sw, tmm, tn)
    w_out_b = w_out[0].astype(BF16)
    tno = _pick(d, (1024, 512, 256))
    y_prompt = _out(merged, 0, w_out_b, xp, final_g, tm, tno).reshape(batch, seq, d)
    y_sample = _out(merged, mp // tm, w_out_b, xs, final_g, tm, tno).reshape(dec_batch, dec_seq, d)

    keep = min(WINDOW, seq)
    last_rows = lambda a: jnp.stack([a[(b + 1) * seq - keep:(b + 1) * seq] for b in range(batch)]).reshape(1, batch, keep, g, HEAD_DIM)
    dec_rows = lambda a: a[mp:].reshape(dec_batch, dec_seq, g, HEAD_DIM)[None]
    fin = fin.reshape(batch + dec_batch, ng, 2, p)
    return (y_prompt, y_sample, last_rows(k_f), last_rows(v_f), fin[:batch, :, 0][None], fin[:batch, :, 1][None],
            dec_rows(k_f), dec_rows(v_f), fin[batch:, :, 0][None], fin[batch:, :, 1][None])
```

```python
import functools
import math

import jax
import jax.numpy as jnp
from jax import lax
from jax.experimental import pallas as pl
from jax.experimental.pallas import tpu as pltpu

CHUNK = 64
WINDOW = 128
HEAD_DIM = 64
GQA_GROUPING = 8
SSM_GROUP = 16
PAST_LEN = 1024
ROPE_THETA = 10000.0
NORM_EPS = 1e-5
LAMBDA_RE_MAX = -1e-4
LOG2_E = math.log2(math.e)

SSM_CHUNK = 16
SSM_POW_BITS = 4
LANES = 128
V7X_VMEM_BYTES = 64 * 1024 * 1024
BF16 = jnp.bfloat16
F32 = jnp.float32


def _pick(n, prefs):
    for p in prefs:
        if n % p == 0:
            return p
    raise ValueError(f"no tile in {prefs} divides {n}")


def _params(sem, vmem_bytes):
    limit = min(int(vmem_bytes), V7X_VMEM_BYTES - 4 * 1024 * 1024)
    return pltpu.CompilerParams(dimension_semantics=sem, vmem_limit_bytes=limit)


def _sigmoid(x):
    return 0.5 * jnp.tanh(0.5 * x) + 0.5


def _nbytes(shape, dtype):
    return math.prod(shape) * jnp.dtype(dtype).itemsize


def _vmem_estimate(blocks, temps=()):
    return 2 * (2 * sum(_nbytes(s, d) for s, d in blocks) + sum(_nbytes(s, d) for s, d in temps))


def _rmsnorm_kernel(xp_ref, xs_ref, g_ref, o_ref, *, n_prompt_blocks):
    i = pl.program_id(0)

    def norm(x_ref):
        x = x_ref[...]
        y = x * lax.rsqrt(jnp.mean(x * x, axis=-1, keepdims=True) + NORM_EPS)
        o_ref[...] = (y * g_ref[...]).astype(o_ref.dtype)

    @pl.when(i < n_prompt_blocks)
    def _():
        norm(xp_ref)

    @pl.when(i >= n_prompt_blocks)
    def _():
        norm(xs_ref)


def _rmsnorm(xp, xs, g, tm):
    mp, d = xp.shape
    ms = xs.shape[0]
    npb, nsb = mp // tm, ms // tm
    return pl.pallas_call(
        functools.partial(_rmsnorm_kernel, n_prompt_blocks=npb),
        grid=(npb + nsb,),
        in_specs=[
            pl.BlockSpec((tm, d), lambda i: (jnp.minimum(i, npb - 1), 0)),
            pl.BlockSpec((tm, d), lambda i: (jnp.maximum(i - npb, 0), 0)),
            pl.BlockSpec((1, d), lambda i: (0, 0)),
        ],
        out_specs=pl.BlockSpec((tm, d), lambda i: (i, 0)),
        out_shape=jax.ShapeDtypeStruct((mp + ms, d), BF16),
        compiler_params=_params(("arbitrary",), _vmem_estimate([((tm, d), F32)] * 2 + [((tm, d), BF16)], [((tm, d), F32)])),
        name="rmsnorm_in",
    )(xp, xs, g.reshape(1, d))


def _rope(acc, cos_ref, sin_ref):
    tm, tn = acc.shape
    reps = tn // LANES
    cos = jnp.tile(cos_ref[...], (1, reps))
    sin = jnp.tile(sin_ref[...], (1, reps))
    lane = lax.broadcasted_iota(jnp.int32, (tm, tn), 1)
    low = (lane % HEAD_DIM) < (HEAD_DIM // 2)
    partner = jnp.where(low, pltpu.roll(acc, tn - HEAD_DIM // 2, 1), pltpu.roll(acc, HEAD_DIM // 2, 1))
    return acc * cos + partner * sin


def _store_heads(o_ref, rows, val):
    for h in range(val.shape[1] // HEAD_DIM):
        o_ref[h, rows, :] = val[:, h * HEAD_DIM:(h + 1) * HEAD_DIM].astype(o_ref.dtype)


def _cast_weight(w_ref, wb_ref):
    @pl.when(pl.program_id(1) == 0)
    def _():
        wb_ref[...] = w_ref[...].astype(wb_ref.dtype)


def _stream_weights(w_hbm, wbuf_ref, stage_ref, sem_ref, *, col_of_block, n_blocks, n_chunks):
    d, tn = wbuf_ref.shape[1:]
    ck = d // n_chunks
    total = n_blocks * n_chunks
    j = pl.program_id(0)
    t = j * n_chunks + pl.program_id(1)

    def aligned(x, a):
        return x if isinstance(x, int) else pl.multiple_of(x, a)

    def chunk_copy(blk, chunk, slot):
        src = w_hbm.at[pl.ds(aligned(chunk * ck, ck), ck), pl.ds(aligned(col_of_block(blk), LANES), tn)]
        return pltpu.make_async_copy(src, stage_ref.at[slot], sem_ref.at[slot])

    def generation(s):
        s = jnp.asarray(s, jnp.int32)
        k = s + n_chunks - 1
        past = (k > total - 1).astype(jnp.int32)
        k = jnp.minimum(k, total - 1)
        blk, chunk = k // n_chunks, k % n_chunks
        return chunk_copy(blk, chunk, s % 2), (blk + past) % 2, chunk

    def land(copy, half, chunk, slot):
        copy.wait()
        wbuf_ref[half, pl.ds(aligned(chunk * ck, ck), ck), :] = stage_ref[slot].astype(wbuf_ref.dtype)

    @pl.when(t == 0)
    def _():
        slot_of = lambda c: (c + n_chunks - 1) % 2
        first = [chunk_copy(0, c, slot_of(c)) for c in range(n_chunks)]
        first[0].start()
        for c in range(n_chunks - 1):
            first[c + 1].start()
            land(first[c], 0, c, slot_of(c))

    @pl.when(t + 1 < total)
    def _():
        generation(t + 1)[0].start()

    copy, half, chunk = generation(t)
    land(copy, half, chunk, t % 2)
    return wbuf_ref.at[j % 2]


def _row_parts(tm, n=2):
    n = n if tm % (16 * n) == 0 and tm >= 1024 else 1
    return [slice(r * (tm // n), (r + 1) * (tm // n)) for r in range(n)]


def _proj_q_kernel(h_ref, w_ref, cos_ref, sin_ref, q_ref, wb_ref):
    _cast_weight(w_ref, wb_ref)
    for rows in _row_parts(h_ref.shape[0], 4):
        acc = jnp.dot(h_ref[rows, :], wb_ref[...], preferred_element_type=F32)
        _store_heads(q_ref, rows, _rope(acc, cos_ref.at[rows, :], sin_ref.at[rows, :]) * (HEAD_DIM ** -0.5 * LOG2_E))


def _proj_kv_kernel(h_ref, w_ref, cos_ref, sin_ref, kf_ref, vf_ref, kh_ref, vh_ref, wb_ref, *, kvw):
    _cast_weight(w_ref, wb_ref)
    for rows in _row_parts(h_ref.shape[0], 4):
        acc = jnp.dot(h_ref[rows, :], wb_ref[...], preferred_element_type=F32)
        k = _rope(acc[:, :kvw], cos_ref.at[rows, :], sin_ref.at[rows, :])
        v = acc[:, kvw:]
        kf_ref[rows, :] = k
        vf_ref[rows, :] = v
        _store_heads(kh_ref, rows, k)
        _store_heads(vh_ref, rows, v)


def _proj_act_kernel(h_ref, w_hbm, o_ref, wbuf_ref, stage_ref, sem_ref, *, segments, n_chunks):
    tn = o_ref.shape[1]
    n_blocks = sum(s[1] for s in segments)
    in_seg = lambda blk, s: (blk >= s[0]) & (blk < s[0] + s[1])

    def col_of_block(blk):
        if isinstance(blk, int):
            return next(s[2] + (blk - s[0]) * tn for s in segments if s[0] <= blk < s[0] + s[1])
        return sum(jnp.where(in_seg(blk, s), s[2] + (blk - s[0]) * tn, 0) for s in segments)

    wb_ref = _stream_weights(w_hbm, wbuf_ref, stage_ref, sem_ref, col_of_block=col_of_block, n_blocks=n_blocks, n_chunks=n_chunks)
    acts = {s[3] for s in segments}
    assert acts in ({"none"}, {"silu"}, {"sigmoid"}, {"silu", "sigmoid"})
    j = pl.program_id(0)
    for rows in _row_parts(h_ref.shape[0]):
        acc = jnp.dot(h_ref[rows, :], wb_ref[...], preferred_element_type=F32)
        if acts != {"none"}:
            sig = _sigmoid(acc)
            if acts == {"silu"}:
                acc = acc * sig
            elif acts == {"sigmoid"}:
                acc = sig
            else:
                is_silu = functools.reduce(lambda a, b: a | b, [in_seg(j, s) for s in segments if s[3] == "silu"])
                acc = jnp.where(is_silu, acc * sig, sig)
        o_ref[rows, :] = acc.astype(o_ref.dtype)


def _proj_specs(d, tm, tn, col0):
    assert col0 % tn == 0
    cb0 = col0 // tn
    return [pl.BlockSpec((tm, d), lambda j, i: (i, 0)), pl.BlockSpec((d, tn), lambda j, i: (0, cb0 + j))]


def _proj_vmem(d, tm, tn, outs):
    return _vmem_estimate([((tm, d), BF16), ((d, tn), F32)] + outs, [((tm, tn), F32)] * 3 + [((d, tn), BF16)])


def _proj_q(h, w, cos, sin, aw, tm, tn):
    m, d = h.shape
    nh = aw // HEAD_DIM
    tab = pl.BlockSpec((tm, LANES), lambda j, i: (i, 0))
    return pl.pallas_call(
        _proj_q_kernel,
        grid=(aw // tn, m // tm),
        in_specs=_proj_specs(d, tm, tn, 0) + [tab, tab],
        out_specs=pl.BlockSpec((tn // HEAD_DIM, tm, HEAD_DIM), lambda j, i: (j, i, 0)),
        out_shape=jax.ShapeDtypeStruct((nh, m, HEAD_DIM), BF16),
        scratch_shapes=[pltpu.VMEM((d, tn), BF16)],
        compiler_params=_params(("arbitrary", "arbitrary"), _proj_vmem(d, tm, tn, [((tm, 2 * tn), BF16), ((tm, 2 * LANES), F32)])),
        name="proj_q",
    )(h, w, cos, sin)


def _proj_kv(h, w, cos, sin, col0, kvw, tm):
    m, d = h.shape
    g = kvw // HEAD_DIM
    tn = 2 * kvw
    tab = pl.BlockSpec((tm, LANES), lambda j, i: (i, 0))
    flat = pl.BlockSpec((tm, kvw), lambda j, i: (i, 0))
    heads = pl.BlockSpec((g, tm, HEAD_DIM), lambda j, i: (0, i, 0))
    return pl.pallas_call(
        functools.partial(_proj_kv_kernel, kvw=kvw),
        grid=(1, m // tm),
        in_specs=_proj_specs(d, tm, tn, col0) + [tab, tab],
        out_specs=[flat, flat, heads, heads],
        out_shape=[jax.ShapeDtypeStruct((m, kvw), F32)] * 2 + [jax.ShapeDtypeStruct((g, m, HEAD_DIM), BF16)] * 2,
        scratch_shapes=[pltpu.VMEM((d, tn), BF16)],
        compiler_params=_params(("arbitrary", "arbitrary"), _proj_vmem(d, tm, tn, [((tm, 2 * tn), F32), ((tm, 2 * tn), BF16), ((tm, 2 * LANES), F32)])),
        name="proj_kv",
    )(h, w, cos, sin)


def _proj_act(h, w, runs, tm, tn, name, out_dtype=BF16):
    m, d = h.shape
    n_chunks = m // tm
    segments, b0 = [], 0
    for col0, width, act in runs:
        assert width % tn == 0 and col0 % LANES == 0
        segments.append((b0, width // tn, col0, act))
        b0 += width // tn
    n_blocks, ncols = b0, b0 * tn
    assert d % (8 * n_chunks) == 0
    scratch = [((2, d, tn), BF16), ((2, d // n_chunks, tn), F32)]
    return pl.pallas_call(
        functools.partial(_proj_act_kernel, segments=tuple(segments), n_chunks=n_chunks),
        grid=(n_blocks, n_chunks),
        in_specs=[pl.BlockSpec((tm, d), lambda j, i: (i, 0)), pl.BlockSpec(memory_space=pl.ANY)],
        out_specs=pl.BlockSpec((tm, tn), lambda j, i: (i, j)),
        out_shape=jax.ShapeDtypeStruct((m, ncols), out_dtype),
        scratch_shapes=[pltpu.VMEM(s, t) for s, t in scratch] + [pltpu.SemaphoreType.DMA((2,))],
        compiler_params=_params(("arbitrary", "arbitrary"),
                                _vmem_estimate([((tm, d), BF16), ((tm, tn), out_dtype)], scratch + [((tm // len(_row_parts(tm)), tn), F32)] * 3)),
        name=name,
    )(h, w)


def _attn_weights(q, kw, sink, n_valid):
    st = lax.dot_general(kw, q, (((1,), (1,)), ((), ())), preferred_element_type=F32)
    if n_valid is not None:
        row = lax.broadcasted_iota(jnp.int32, st.shape, 0)
        st = jnp.where(row < n_valid, st, -jnp.inf)
    m = jnp.maximum(jnp.max(st, axis=0, keepdims=True), sink)
    e = jnp.exp2(st - m)
    return e.astype(BF16), jnp.sum(e, axis=0, keepdims=True) + jnp.exp2(sink - m)


def _attn_values(vw, e, denom):
    ot = jnp.dot(vw.astype(F32).T.astype(BF16), e, preferred_element_type=F32) / denom
    return ot.T


def _store_unit(o_ref, za_ref, row0, o, rep):
    for r in range(0, rep, 2):
        pair = jnp.concatenate([o[r * CHUNK:(r + 1) * CHUNK], o[(r + 1) * CHUNK:(r + 2) * CHUNK]], axis=1)
        gate = za_ref[row0:row0 + CHUNK, r * HEAD_DIM:(r + 2) * HEAD_DIM].astype(F32)
        o_ref[row0:row0 + CHUNK, r * HEAD_DIM:(r + 2) * HEAD_DIM] = (pair * gate).astype(o_ref.dtype)


def _attn_kernel(q_ref, k_ref, v_ref, ks_ref, vs_ref, sink_ref, za_ref, wo_ref, o_ref, wob_ref,
                 *, n_units, rep, steps_per_stream, n_prompt_steps, n_cast_steps):
    step = pl.program_id(1)
    win = WINDOW + CHUNK
    sink = sink_ref[0] * LOG2_E

    @pl.when(pl.program_id(0) * (n_prompt_steps + 1) + step < n_cast_steps)
    def _():
        wob_ref[...] = wo_ref[...].astype(wob_ref.dtype)

    def run(windows):
        staged = []
        for c, (kw, vw, n_valid) in enumerate(windows):
            q = q_ref[:, c * CHUNK:(c + 1) * CHUNK, :].reshape(rep * CHUNK, HEAD_DIM)
            staged.append((vw,) + _attn_weights(q, kw, sink, n_valid))
        for c in range(n_units):
            _store_unit(o_ref, za_ref, c * CHUNK, _attn_values(*staged[c]), rep)

    @pl.when(step < n_prompt_steps)
    def _():
        cb = step % steps_per_stream
        windows = []
        for c in range(n_units):
            start = pl.multiple_of(jnp.maximum(cb * n_units + c - WINDOW // CHUNK, 0) * CHUNK, CHUNK)
            n_valid = jnp.where(cb == 0, (c + 1) * CHUNK, win) if c < WINDOW // CHUNK else None
            windows.append((k_ref[0, pl.ds(start, win), :], v_ref[0, pl.ds(start, win), :], n_valid))
        run(windows)

    @pl.when(step == n_prompt_steps)
    def _():
        run([(ks_ref[0, b * win:(b + 1) * win, :], vs_ref[0, b * win:(b + 1) * win, :], None) for b in range(n_units)])


def _attention(q_hm, k_hm, v_hm, ks, vs, sink_rows, za, w_out, batch, seq, dec_batch, aw):
    nh, m, _ = q_hm.shape
    g = k_hm.shape[0]
    rep = nh // g
    win = WINDOW + CHUNK
    n_units = dec_batch
    rows = n_units * CHUNK
    assert seq % rows == 0 and n_units >= WINDOW // CHUNK
    steps_per_stream = seq // rows
    n_prompt_steps = batch * steps_per_stream
    blocks = ([((rep, rows, LANES), BF16)] + [((seq, LANES), BF16)] * 2 + [((dec_batch * win, LANES), BF16)] * 2
              + [((rows, rep * HEAD_DIM), BF16)] * 2)
    temps = [((rep * CHUNK, 2 * LANES), F32)] * (4 * n_units)
    stream = lambda gi, s: (gi, jnp.minimum(s // steps_per_stream, batch - 1), 0)
    dk, dn = w_out.shape
    n_cast_steps = 16
    assert g * (n_prompt_steps + 1) >= n_cast_steps and dk % (16 * n_cast_steps) == 0
    cast_rows = dk // n_cast_steps
    cast_block = pl.BlockSpec((cast_rows, dn), lambda gi, s: (jnp.minimum(gi * (n_prompt_steps + 1) + s, n_cast_steps - 1), 0))
    blocks += [((cast_rows, dn), F32), ((cast_rows, dn), BF16)]
    return pl.pallas_call(
        functools.partial(_attn_kernel, n_units=n_units, rep=rep, steps_per_stream=steps_per_stream, n_prompt_steps=n_prompt_steps,
                          n_cast_steps=n_cast_steps),
        grid=(g, n_prompt_steps + 1),
        in_specs=[
            pl.BlockSpec((rep, rows, HEAD_DIM), lambda gi, s: (gi, s, 0)),
            pl.BlockSpec((1, seq, HEAD_DIM), stream),
            pl.BlockSpec((1, seq, HEAD_DIM), stream),
            pl.BlockSpec((1, dec_batch * win, HEAD_DIM), lambda gi, s: (gi, 0, 0)),
            pl.BlockSpec((1, dec_batch * win, HEAD_DIM), lambda gi, s: (gi, 0, 0)),
            pl.BlockSpec((1, 1, rep * CHUNK), lambda gi, s: (gi, 0, 0)),
            pl.BlockSpec((rows, rep * HEAD_DIM), lambda gi, s: (s, gi)),
            cast_block,
        ],
        out_specs=[pl.BlockSpec((rows, rep * HEAD_DIM), lambda gi, s: (s, gi)), cast_block],
        out_shape=[jax.ShapeDtypeStruct((m, aw), BF16), jax.ShapeDtypeStruct((dk, dn), BF16)],
        compiler_params=_params(("arbitrary",) * 2, _vmem_estimate(blocks, temps)),
        name="attention",
    )(q_hm, k_hm, v_hm, ks, vs, sink_rows, za, w_out)


def _ssm_disc_kernel(lre_ref, lim_ref, ldt_ref, are_ref, aim_ref, dre_ref, dim_ref, fre_ref, fim_ref):
    lr = jnp.minimum(lre_ref[...], LAMBDA_RE_MAX)
    li = lim_ref[...]
    dt = jnp.exp(ldt_ref[...])
    mag = jnp.exp(lr * dt)
    a_re = mag * jnp.cos(li * dt)
    a_im = mag * jnp.sin(li * dt)
    den = lr * lr + li * li
    nr = a_re - 1.0
    fre_ref[...] = (nr * lr + a_im * li) / den
    fim_ref[...] = (a_im * lr - nr * li) / den
    are_ref[...] = a_re
    aim_ref[...] = a_im
    for _ in range(SSM_POW_BITS):
        a_re, a_im = a_re * a_re - a_im * a_im, 2.0 * a_re * a_im
    dre_ref[...] = a_re
    dim_ref[...] = a_im


def _cmul(ar, ai, br, bi):
    return ar * br - ai * bi, ar * bi + ai * br


def _ssm_build_kernel(are_ref, aim_ref, dre_ref, dim_ref, fre_ref, fim_ref, btr_ref, bti_ref, cr_ref, ci_ref, dv_ref,
                      met_ref, ft_ref, a1_ref, a2_ref, a2s_ref, *, groups):
    lc = SSM_CHUNK * SSM_GROUP
    sub = lax.broadcasted_iota(jnp.int32, (SSM_GROUP, lc), 0)
    lane = lax.broadcasted_iota(jnp.int32, (SSM_GROUP, lc), 1)
    for gi in range(groups):
        row = slice(gi, gi + 1)
        a_re, a_im = are_ref[row, :], aim_ref[row, :]
        pw = [(jnp.ones_like(a_re), jnp.zeros_like(a_re))]
        for _ in range(SSM_CHUNK):
            pw.append(_cmul(pw[-1][0], pw[-1][1], a_re, a_im))
        c_re, c_im = cr_ref[gi], ci_ref[gi]
        wt = [_cmul(pr, pi, c_re, c_im) for pr, pi in pw]
        wt_re = jnp.concatenate([w[0] for w in wt[:SSM_CHUNK]], axis=0)
        wt_im = jnp.concatenate([w[1] for w in wt[:SSM_CHUNK]], axis=0)
        et_re = jnp.concatenate([w[0] for w in wt[1:]], axis=0)
        et_im = jnp.concatenate([w[1] for w in wt[1:]], axis=0)
        bb_re, bb_im = _cmul(fre_ref[row, :], fim_ref[row, :], btr_ref[gi], bti_ref[gi])
        r0 = lax.dot_general(jnp.concatenate([bb_re, bb_im], axis=1), jnp.concatenate([wt_re, -wt_im], axis=1),
                             (((1,), (1,)), ((), ())), preferred_element_type=F32, precision=lax.Precision.HIGHEST)
        r0 = r0 + jnp.where(sub == lane, dv_ref[gi], 0.0)
        rows = [r0] + [jnp.where(lane >= s * SSM_GROUP, pltpu.roll(r0, s * SSM_GROUP, 1), 0.0) for s in range(1, SSM_CHUNK)]
        mt = jnp.concatenate(rows, axis=0).T
        met_ref[gi] = jnp.concatenate([mt, et_re, -et_im], axis=1).astype(met_ref.dtype)
        fb = [_cmul(pw[SSM_CHUNK - 1 - s][0], pw[SSM_CHUNK - 1 - s][1], bb_re, bb_im) for s in range(SSM_CHUNK)]
        f_all = jnp.concatenate([jnp.concatenate([x[0] for x in fb], axis=0), jnp.concatenate([x[1] for x in fb], axis=0)], axis=1)
        ft_ref[gi] = f_all.T.astype(ft_ref.dtype)
        d_re, d_im = dre_ref[row, :], dim_ref[row, :]
        a1_ref[gi] = jnp.concatenate([d_re, d_re], axis=1)
        a2_ref[gi] = jnp.concatenate([-d_im, d_im], axis=1)
        a2s_ref[gi] = jnp.concatenate([d_im, -d_im], axis=1)


def _ssm_params(lambda_re, lambda_im, log_dt, b_re, b_im, c_re, c_im, d_skip):
    ng, p = lambda_re.shape
    lc = SSM_CHUNK * SSM_GROUP
    full = pl.BlockSpec((ng, p), lambda: (0, 0))
    disc = pl.pallas_call(
        _ssm_disc_kernel,
        in_specs=[full, full, pl.BlockSpec((ng, 1), lambda: (0, 0))],
        out_specs=[full] * 6,
        out_shape=[jax.ShapeDtypeStruct((ng, p), F32)] * 6,
        name="ssm_disc",
    )(lambda_re, lambda_im, log_dt.reshape(ng, 1))
    gb = _pick(ng, (8,))
    bt_re = jnp.swapaxes(b_re, 1, 2)
    bt_im = jnp.swapaxes(b_im, 1, 2)
    dvec = jnp.pad(d_skip, ((0, 0), (0, lc - SSM_GROUP))).reshape(ng, 1, lc)
    rows = pl.BlockSpec((gb, p), lambda i: (i, 0))
    mats = pl.BlockSpec((gb, SSM_GROUP, p), lambda i: (i, 0, 0))
    dec = pl.BlockSpec((gb, 1, 2 * p), lambda i: (i, 0, 0))
    blocks = ([((gb, LANES), F32)] * 6 + [((gb, SSM_GROUP, LANES), F32)] * 4 + [((gb, 8, lc), F32)]
              + [((gb, lc, lc + 2 * p), BF16), ((gb, 2 * p, lc), BF16)] + [((gb, 8, LANES), F32)] * 3)
    met, ft, a1, a2, a2s = pl.pallas_call(
        functools.partial(_ssm_build_kernel, groups=gb),
        grid=(ng // gb,),
        in_specs=[rows] * 6 + [mats] * 4 + [pl.BlockSpec((gb, 1, lc), lambda i: (i, 0, 0))],
        out_specs=[pl.BlockSpec((gb, lc, lc + 2 * p), lambda i: (i, 0, 0)), pl.BlockSpec((gb, 2 * p, lc), lambda i: (i, 0, 0)), dec, dec, dec],
        out_shape=[jax.ShapeDtypeStruct((ng, lc, lc + 2 * p), BF16), jax.ShapeDtypeStruct((ng, 2 * p, lc), BF16)]
        + [jax.ShapeDtypeStruct((ng, 1, 2 * p), F32)] * 3,
        compiler_params=_params(("arbitrary",), _vmem_estimate(blocks, [((lc, lc + 2 * p), F32)] * 8)),
        name="ssm_build",
    )(*disc, bt_re, bt_im, c_re, c_im, dvec)
    flat = lambda a: a.reshape(1, ng * 2 * p)
    return met, ft, flat(a1), flat(a2), flat(a2s)


def _chunk_blocks(nk):
    return [(k0, min(LANES, nk - k0)) for k0 in range(0, nk, LANES)]


def _pad_rows(x):
    n = x.shape[0]
    return x if n == LANES else jnp.concatenate([x, jnp.zeros((LANES - n, x.shape[1]), x.dtype)], axis=0)


def _ssm_inputs_phase(u_ref, ft_ref, rhs_ref, s_ref, *, nk, p):
    groups = LANES // SSM_GROUP
    blocks = _chunk_blocks(nk)
    xt = [[_pad_rows(u_ref[pl.ds(k0 * SSM_CHUNK + s, n, stride=SSM_CHUNK), :].astype(BF16)).T for k0, n in blocks]
          for s in range(SSM_CHUNK)]
    for gi in range(groups):
        rhs = jnp.concatenate([jnp.concatenate([xt[s][kb][gi * SSM_GROUP:(gi + 1) * SSM_GROUP, :] for kb in range(len(blocks))], axis=1)
                               for s in range(SSM_CHUNK)], axis=0)
        rhs_ref[gi] = rhs
        st = jnp.dot(ft_ref[gi], rhs, preferred_element_type=F32)
        for kb, (k0, n) in enumerate(blocks):
            s_ref[k0:k0 + n, gi * 2 * p:(gi + 1) * 2 * p] = st[:, kb * LANES:(kb + 1) * LANES].T[:n, :]


def _ssm_outputs_phase(rhs_ref, h_ref, met_ref, y_ref, *, nk, p):
    groups = LANES // SSM_GROUP
    blocks = _chunk_blocks(nk)
    zt = [[None] * groups for _ in range(SSM_CHUNK)]
    for gi in range(groups):
        ht = jnp.concatenate([_pad_rows(h_ref[k0:k0 + n, gi * 2 * p:(gi + 1) * 2 * p]).T for k0, n in blocks], axis=1)
        rhs = jnp.concatenate([rhs_ref[gi], ht.astype(BF16)], axis=0)
        yt = jnp.dot(met_ref[gi], rhs, preferred_element_type=F32)
        for t in range(SSM_CHUNK):
            zt[t][gi] = yt[t * SSM_GROUP:(t + 1) * SSM_GROUP, :]
    for t in range(SSM_CHUNK):
        z = jnp.concatenate(zt[t], axis=0)
        for kb, (k0, n) in enumerate(blocks):
            y_ref[pl.ds(k0 * SSM_CHUNK + t, n, stride=SSM_CHUNK), :] = z[:, kb * LANES:(kb + 1) * LANES].T[:n, :]


def _ssm_scan_phase(s_ref, h0_ref, a1_ref, a2_ref, a2s_ref, h_ref, fin_ref, ss_ref, *, batch, kp, dec_batch, ks, p):
    def swap_halves(x):
        lane = lax.broadcasted_iota(jnp.int32, x.shape, 1)
        return jnp.where(lane % (2 * p) < p, pltpu.roll(x, x.shape[1] - p, 1), pltpu.roll(x, p, 1))

    ss_ref[...] = swap_halves(s_ref[...])
    a1, a2, a2s = a1_ref[...], a2_ref[...], a2s_ref[...]
    zero = jnp.zeros_like(a1)

    def step(k, h, hs):
        h_ref[pl.ds(k, 1), :] = h
        s = s_ref[pl.ds(k, 1), :]
        ss = ss_ref[pl.ds(k, 1), :]
        return a1 * h + a2 * hs + s, a1 * hs + a2s * h + ss

    def body(k, carry):
        out = []
        for b in range(batch):
            out.extend(step(b * kp + k, carry[2 * b], carry[2 * b + 1]))
        return tuple(out)

    fin = lax.fori_loop(0, kp, body, (zero,) * (2 * batch))
    for b in range(batch):
        fin_ref[b:b + 1, :] = fin[2 * b]
    h0s_all = swap_halves(h0_ref[...])
    for b in range(dec_batch):
        h, hs = h0_ref[b:b + 1, :], h0s_all[b:b + 1, :]
        for k in range(ks):
            h, hs = step(batch * kp + b * ks + k, h, hs)
        fin_ref[batch + b:batch + b + 1, :] = h


def _ssm_kernel(u_ref, ft_ref, met_ref, h0_ref, a1_ref, a2_ref, a2s_ref, y_ref, fin_ref, rhs_ref, s_ref, ss_ref, h_ref,
                *, batch, kp, dec_batch, ks, p):
    nk = batch * kp + dec_batch * ks
    _ssm_inputs_phase(u_ref, ft_ref, rhs_ref, s_ref, nk=nk, p=p)
    _ssm_scan_phase(s_ref, h0_ref, a1_ref, a2_ref, a2s_ref, h_ref, fin_ref, ss_ref, batch=batch, kp=kp, dec_batch=dec_batch, ks=ks, p=p)
    _ssm_outputs_phase(rhs_ref, h_ref, met_ref, y_ref, nk=nk, p=p)


def _ssm(u, met, ft, a1, a2, a2s, h0, batch, kp, dec_batch, ks):
    m, sw = u.shape
    ng, p2, lc = ft.shape
    p = p2 // 2
    nk = m // SSM_CHUNK
    assert nk == batch * kp + dec_batch * ks and (nk % LANES) % 8 == 0
    nk_lanes = -(-nk // LANES) * LANES
    groups = LANES // SSM_GROUP
    lb = groups * p2
    nseq = batch + dec_batch
    u_spec = pl.BlockSpec((m, LANES), lambda i: (0, i))
    rowb = pl.BlockSpec((1, lb), lambda i: (0, i))
    blocks = [((m, LANES), F32)] * 2 + [((groups, p2, lc), BF16), ((groups, lc, lc + p2), BF16), ((nseq, lb), F32)]
    scratch = [((groups, lc, nk_lanes), BF16)] + [((nk, lb), F32)] * 3
    return pl.pallas_call(
        functools.partial(_ssm_kernel, batch=batch, kp=kp, dec_batch=dec_batch, ks=ks, p=p),
        grid=(ng // groups,),
        in_specs=[u_spec, pl.BlockSpec((groups, p2, lc), lambda i: (i, 0, 0)), pl.BlockSpec((groups, lc, lc + p2), lambda i: (i, 0, 0)),
                  pl.BlockSpec((dec_batch, lb), lambda i: (0, i)), rowb, rowb, rowb],
        out_specs=[u_spec, pl.BlockSpec((nseq, lb), lambda i: (0, i))],
        out_shape=[jax.ShapeDtypeStruct((m, sw), F32), jax.ShapeDtypeStruct((nseq, ng * p2), F32)],
        scratch_shapes=[pltpu.VMEM(s, d) for s, d in scratch],
        compiler_params=_params(("arbitrary",), _vmem_estimate(blocks, scratch + [((lc + p2, nk_lanes), BF16), ((lc, nk_lanes), F32)])),
        name="ssm",
    )(u, ft, met, h0, a1, a2, a2s)


def _glu_kernel(y_ref, wa_ref, wg_ref, ba_ref, bg_ref, zs_ref, o_ref, wab_ref, wgb_ref):
    _cast_weight(wa_ref, wab_ref)
    _cast_weight(wg_ref, wgb_ref)
    for rows in _row_parts(y_ref.shape[0], 4):
        y = y_ref[rows, :].astype(BF16)
        a = jnp.dot(y, wab_ref[...], preferred_element_type=F32) + ba_ref[...]
        g = jnp.dot(y, wgb_ref[...], preferred_element_type=F32) + bg_ref[...]
        o_ref[rows, :] = (a * _sigmoid(g) * zs_ref[rows, :].astype(F32)).astype(o_ref.dtype)


def _glu(y, w_glu, b_glu, acts, zs_col0, tm, tn):
    m, sw = y.shape
    nb = sw // tn
    zb0 = zs_col0 // tn
    blocks = [((tm, sw), F32), ((sw, tn), F32), ((sw, tn), F32), ((tm, tn), BF16), ((tm, tn), BF16)]
    return pl.pallas_call(
        _glu_kernel,
        grid=(nb, m // tm),
        in_specs=[
            pl.BlockSpec((tm, sw), lambda j, i: (i, 0)),
            pl.BlockSpec((sw, tn), lambda j, i: (0, j)),
            pl.BlockSpec((sw, tn), lambda j, i: (0, nb + j)),
            pl.BlockSpec((1, tn), lambda j, i: (0, j)),
            pl.BlockSpec((1, tn), lambda j, i: (0, nb + j)),
            pl.BlockSpec((tm, tn), lambda j, i: (i, zb0 + j)),
        ],
        out_specs=pl.BlockSpec((tm, tn), lambda j, i: (i, j)),
        out_shape=jax.ShapeDtypeStruct((m, sw), BF16),
        scratch_shapes=[pltpu.VMEM((sw, tn), BF16)] * 2,
        compiler_params=_params(("arbitrary", "arbitrary"), _vmem_estimate(blocks, [((tm, tn), F32)] * 4 + [((sw, tn), BF16)] * 2)),
        name="glu",
    )(y, w_glu, w_glu, b_glu, b_glu, acts)


def _merge_kernel(xa_ref, xs_ref, wpa_ref, wps_ref, ga_ref, gs_ref, o_ref, wpab_ref, wpsb_ref):
    _cast_weight(wpa_ref, wpab_ref)
    _cast_weight(wps_ref, wpsb_ref)
    for rows in _row_parts(xa_ref.shape[0]):
        br_a = jnp.dot(xa_ref[rows, :], wpab_ref[...], preferred_element_type=F32)
        br_s = jnp.dot(xs_ref[rows, :], wpsb_ref[...], preferred_element_type=F32)
        o_ref[rows, :] = (ga_ref[rows, :].astype(F32) * br_a + gs_ref[rows, :].astype(F32) * br_s).astype(o_ref.dtype)


def _merge(xa, xs, w_pa, w_ps, acts, gate_col0, tm, tn):
    m, aw = xa.shape
    sw = xs.shape[1]
    d = w_pa.shape[1]
    nb = d // tn
    gb0 = gate_col0 // tn
    blocks = [((tm, aw), BF16), ((tm, sw), BF16), ((aw, tn), F32), ((sw, tn), F32)] + [((tm, tn), BF16)] * 3
    return pl.pallas_call(
        _merge_kernel,
        grid=(nb, m // tm),
        in_specs=[
            pl.BlockSpec((tm, aw), lambda j, i: (i, 0)),
            pl.BlockSpec((tm, sw), lambda j, i: (i, 0)),
            pl.BlockSpec((aw, tn), lambda j, i: (0, j)),
            pl.BlockSpec((sw, tn), lambda j, i: (0, j)),
            pl.BlockSpec((tm, tn), lambda j, i: (i, gb0 + j)),
            pl.BlockSpec((tm, tn), lambda j, i: (i, gb0 + nb + j)),
        ],
        out_specs=pl.BlockSpec((tm, tn), lambda j, i: (i, j)),
        out_shape=jax.ShapeDtypeStruct((m, d), BF16),
        scratch_shapes=[pltpu.VMEM((aw, tn), BF16), pltpu.VMEM((sw, tn), BF16)],
        compiler_params=_params(("arbitrary", "arbitrary"),
                                _vmem_estimate(blocks, [((tm, tn), F32)] * 4 + [((aw, tn), BF16), ((sw, tn), BF16)])),
        name="merge",
    )(xa, xs, w_pa, w_ps, acts, acts)


def _out_kernel(mg_ref, w_ref, x_ref, g_ref, o_ref, ssq_ref, *, n_col_blocks, tn):
    j = pl.program_id(1)
    blk = x_ref[...] + jnp.dot(mg_ref[...], w_ref[...], preferred_element_type=F32)
    o_ref[:, pl.ds(pl.multiple_of(j * tn, tn), tn)] = blk
    part = jnp.sum(blk * blk, axis=-1, keepdims=True)

    @pl.when(j == 0)
    def _():
        ssq_ref[...] = part

    @pl.when(j > 0)
    def _():
        ssq_ref[...] += part

    @pl.when(j == n_col_blocks - 1)
    def _():
        scale = lax.rsqrt(ssq_ref[...] / (n_col_blocks * tn) + NORM_EPS)
        o_ref[...] = o_ref[...] * scale * g_ref[...]


def _out(merged, row_block0, w_out, x, final_g, tm, tn):
    mx, d = x.shape
    nb = d // tn
    blocks = [((tm, d), BF16), ((d, tn), BF16), ((tm, tn), F32), ((tm, d), F32)]
    return pl.pallas_call(
        functools.partial(_out_kernel, n_col_blocks=nb, tn=tn),
        grid=(mx // tm, nb),
        in_specs=[
            pl.BlockSpec((tm, d), lambda i, j: (row_block0 + i, 0)),
            pl.BlockSpec((d, tn), lambda i, j: (0, j)),
            pl.BlockSpec((tm, tn), lambda i, j: (i, j)),
            pl.BlockSpec((1, d), lambda i, j: (0, 0)),
        ],
        out_specs=pl.BlockSpec((tm, d), lambda i, j: (i, 0)),
        out_shape=jax.ShapeDtypeStruct((mx, d), F32),
        scratch_shapes=[pltpu.VMEM((tm, 1), F32)],
        compiler_params=_params(("arbitrary", "arbitrary"), _vmem_estimate(blocks, [((tm, tn), F32)] * 3 + [((tm, LANES), F32)])),
        name="out_norm",
    )(merged, w_out, x, final_g.reshape(1, d))


def _rope_tables(positions):
    half = HEAD_DIM // 2
    inv_freq = ROPE_THETA ** (-jnp.arange(half, dtype=F32) / half)
    ang = positions.astype(F32)[:, None] * inv_freq[None, :]
    cos, sin = jnp.cos(ang), jnp.sin(ang)
    reps = LANES // HEAD_DIM
    return jnp.tile(jnp.concatenate([cos, cos], axis=1), (1, reps)), jnp.tile(jnp.concatenate([-sin, sin], axis=1), (1, reps))


def kernel(x_prompt, x_sample, cache_k, cache_v, state_ssm_re, state_ssm_im, norm_g, w_in, sink, lambda_re, lambda_im,
           log_dt, b_re, b_im, c_re, c_im, d_skip, w_glu, b_glu, w_pa, w_ps, w_out, final_g):
    depth = norm_g.shape[0]
    assert depth == 1, "one trunk layer"
    batch, seq, d = x_prompt.shape
    dec_batch, dec_seq, _ = x_sample.shape
    aw = w_pa.shape[1]
    sw = w_ps.shape[1]
    nh = aw // HEAD_DIM
    g = max(1, nh // GQA_GROUPING)
    rep = nh // g
    kvw = g * HEAD_DIM
    ng, p = lambda_re.shape[1:]
    assert dec_seq == CHUNK and cache_k.shape[2] == WINDOW and seq % (2 * CHUNK) == 0
    assert kvw % LANES == 0 and rep % 2 == 0 and sw == ng * SSM_GROUP and (1 << SSM_POW_BITS) == SSM_CHUNK
    assert 2 * p == LANES and sw % LANES == 0
    mp, ms = batch * seq, dec_batch * dec_seq
    m = mp + ms
    widths = (aw, 2 * kvw, aw, sw, sw, 2 * d)
    assert sum(widths) == w_in.shape[2]
    c_q, c_kv, c_za, c_u, c_zs, c_gate = (sum(widths[:n]) for n in range(len(widths)))
    tm = _pick(math.gcd(mp, ms), (512, 256, 128))
    tmp = max(t for t in range(16, 1153, 16) if m % t == 0)
    tn = _pick(math.gcd(c_kv, c_za, c_u, c_zs, c_gate, 2 * d), (512, 256, 128))

    xp = x_prompt.reshape(mp, d)
    xs = x_sample.reshape(ms, d)
    w_in2 = w_in.reshape(d, w_in.shape[2])
    positions = jnp.concatenate([jnp.tile(jnp.arange(seq, dtype=jnp.int32), batch),
                                 jnp.tile(PAST_LEN + jnp.arange(dec_seq, dtype=jnp.int32), dec_batch)])
    cos, sin = _rope_tables(positions)

    h = _rmsnorm(xp, xs, norm_g[0], tm)
    q_hm = _proj_q(h, w_in2, cos, sin, aw, tmp, tn)
    k_f, v_f, k_hm, v_hm = _proj_kv(h, w_in2, cos, sin, c_kv, kvw, tmp)
    tna = _pick(math.gcd(aw, sw), (1024, 512, 256))
    acts = _proj_act(h, w_in2, [(c_za, aw, "silu"), (c_zs, sw, "silu"), (c_gate, 2 * d, "sigmoid")], tmp, tna, "proj_acts")
    u = _proj_act(h, w_in2, [(c_u, sw, "none")], tmp, tna, "proj_u", out_dtype=F32)

    to_heads = lambda c: jnp.transpose(c[0], (2, 0, 1, 3)).astype(BF16)
    new_rows = lambda a: a[:, mp:].reshape(g, dec_batch, dec_seq, HEAD_DIM)
    ks = jnp.concatenate([to_heads(cache_k), new_rows(k_hm)], axis=2).reshape(g, dec_batch * (WINDOW + CHUNK), HEAD_DIM)
    vs = jnp.concatenate([to_heads(cache_v), new_rows(v_hm)], axis=2).reshape(g, dec_batch * (WINDOW + CHUNK), HEAD_DIM)
    sink_rows = jnp.repeat(sink[0].reshape(g, rep), CHUNK, axis=1).reshape(g, 1, rep * CHUNK)
    xa, w_out_b = _attention(q_hm, k_hm, v_hm, ks, vs, sink_rows, acts, w_out.reshape(d, d), batch, seq, dec_batch, aw)

    met, ft, a1, a2, a2s = _ssm_params(lambda_re[0], lambda_im[0], log_dt[0], b_re[0], b_im[0], c_re[0], c_im[0], d_skip[0])
    h0 = jnp.concatenate([state_ssm_re[0], state_ssm_im[0]], axis=-1).reshape(dec_batch, ng * 2 * p)
    y, fin = _ssm(u, met, ft, a1, a2, a2s, h0, batch, seq // SSM_CHUNK, dec_batch, dec_seq // SSM_CHUNK)

    tmm = tmp
    x_ssm = _glu(y, w_glu.reshape(sw, 2 * sw), b_glu.reshape(1, 2 * sw), acts, aw, tmm, tn)
    merged = _merge(xa, x_ssm, w_pa.reshape(aw, d), w_ps.reshape(sw, d), acts, aw + sw, tmm, tn)
    tno = _pick(d, (1024, 512, 256))
    y_prompt = _out(merged, 0, w_out_b, xp, final_g, tm, tno).reshape(batch, seq, d)
    y_sample = _out(merged, mp // tm, w_out_b, xs, final_g, tm, tno).reshape(dec_batch, dec_seq, d)

    keep = min(WINDOW, seq)
    last_rows = lambda a: jnp.stack([a[(b + 1) * seq - keep:(b + 1) * seq] for b in range(batch)]).reshape(1, batch, keep, g, HEAD_DIM)
    dec_rows = lambda a: a[mp:].reshape(dec_batch, dec_seq, g, HEAD_DIM)[None]
    fin = fin.reshape(batch + dec_batch, ng, 2, p)
    return (y_prompt, y_sample, last_rows(k_f), last_rows(v_f), fin[:batch, :, 0][None], fin[:batch, :, 1][None],
            dec_rows(k_f), dec_rows(v_f), fin[batch:, :, 0][None], fin[batch:, :, 1][None])
```

```python
import functools
import math

import jax
import jax.numpy as jnp
from jax import lax
from jax.experimental import pallas as pl
from jax.experimental.pallas import tpu as pltpu

CHUNK = 64
WINDOW = 128
HEAD_DIM = 64
GQA_GROUPING = 8
SSM_GROUP = 16
PAST_LEN = 1024
ROPE_THETA = 10000.0
NORM_EPS = 1e-5
LAMBDA_RE_MAX = -1e-4
LOG2_E = math.log2(math.e)

SSM_CHUNK = 16
SSM_POW_BITS = 4
LANES = 128
V7X_VMEM_BYTES = 64 * 1024 * 1024
BF16 = jnp.bfloat16
F32 = jnp.float32


def _pick(n, prefs):
    for p in prefs:
        if n % p == 0:
            return p
    raise ValueError(f"no tile in {prefs} divides {n}")


def _params(sem, vmem_bytes):
    limit = min(int(vmem_bytes), V7X_VMEM_BYTES - 4 * 1024 * 1024)
    return pltpu.CompilerParams(dimension_semantics=sem, vmem_limit_bytes=limit)


def _sigmoid(x):
    return 0.5 * jnp.tanh(0.5 * x) + 0.5


def _nbytes(shape, dtype):
    return math.prod(shape) * jnp.dtype(dtype).itemsize


def _vmem_estimate(blocks, temps=()):
    return 2 * (2 * sum(_nbytes(s, d) for s, d in blocks) + sum(_nbytes(s, d) for s, d in temps))


def _rmsnorm_kernel(xp_ref, xs_ref, g_ref, o_ref, *, n_prompt_blocks):
    i = pl.program_id(0)

    def norm(x_ref):
        x = x_ref[...]
        y = x * lax.rsqrt(jnp.mean(x * x, axis=-1, keepdims=True) + NORM_EPS)
        o_ref[...] = (y * g_ref[...]).astype(o_ref.dtype)

    @pl.when(i < n_prompt_blocks)
    def _():
        norm(xp_ref)

    @pl.when(i >= n_prompt_blocks)
    def _():
        norm(xs_ref)


def _rmsnorm(xp, xs, g, tm):
    mp, d = xp.shape
    ms = xs.shape[0]
    npb, nsb = mp // tm, ms // tm
    return pl.pallas_call(
        functools.partial(_rmsnorm_kernel, n_prompt_blocks=npb),
        grid=(npb + nsb,),
        in_specs=[
            pl.BlockSpec((tm, d), lambda i: (jnp.minimum(i, npb - 1), 0)),
            pl.BlockSpec((tm, d), lambda i: (jnp.maximum(i - npb, 0), 0)),
            pl.BlockSpec((1, d), lambda i: (0, 0)),
        ],
        out_specs=pl.BlockSpec((tm, d), lambda i: (i, 0)),
        out_shape=jax.ShapeDtypeStruct((mp + ms, d), BF16),
        compiler_params=_params(("arbitrary",), _vmem_estimate([((tm, d), F32)] * 2 + [((tm, d), BF16)], [((tm, d), F32)])),
        name="rmsnorm_in",
    )(xp, xs, g.reshape(1, d))


def _rope(acc, cos_ref, sin_ref):
    tm, tn = acc.shape
    reps = tn // LANES
    cos = jnp.tile(cos_ref[...], (1, reps))
    sin = jnp.tile(sin_ref[...], (1, reps))
    lane = lax.broadcasted_iota(jnp.int32, (tm, tn), 1)
    low = (lane % HEAD_DIM) < (HEAD_DIM // 2)
    partner = jnp.where(low, pltpu.roll(acc, tn - HEAD_DIM // 2, 1), pltpu.roll(acc, HEAD_DIM // 2, 1))
    return acc * cos + partner * sin


def _store_heads(o_ref, rows, val):
    for h in range(val.shape[1] // HEAD_DIM):
        o_ref[h, rows, :] = val[:, h * HEAD_DIM:(h + 1) * HEAD_DIM].astype(o_ref.dtype)


def _cast_weight(w_ref, wb_ref):
    @pl.when(pl.program_id(1) == 0)
    def _():
        wb_ref[...] = w_ref[...].astype(wb_ref.dtype)


def _stream_weights(w_hbm, wbuf_ref, stage_ref, sem_ref, *, col_of_block, n_blocks, n_chunks):
    d, tn = wbuf_ref.shape[1:]
    ck = d // n_chunks
    total = n_blocks * n_chunks
    j = pl.program_id(0)
    t = j * n_chunks + pl.program_id(1)

    def aligned(x, a):
        return x if isinstance(x, int) else pl.multiple_of(x, a)

    def chunk_copy(blk, chunk, slot):
        src = w_hbm.at[pl.ds(aligned(chunk * ck, ck), ck), pl.ds(aligned(col_of_block(blk), LANES), tn)]
        return pltpu.make_async_copy(src, stage_ref.at[slot], sem_ref.at[slot])

    def generation(s):
        s = jnp.asarray(s, jnp.int32)
        k = s + n_chunks - 1
        past = (k > total - 1).astype(jnp.int32)
        k = jnp.minimum(k, total - 1)
        blk, chunk = k // n_chunks, k % n_chunks
        return chunk_copy(blk, chunk, s % 2), (blk + past) % 2, chunk

    def land(copy, half, chunk, slot):
        copy.wait()
        wbuf_ref[half, pl.ds(aligned(chunk * ck, ck), ck), :] = stage_ref[slot].astype(wbuf_ref.dtype)

    @pl.when(t == 0)
    def _():
        slot_of = lambda c: (c + n_chunks - 1) % 2
        first = [chunk_copy(0, c, slot_of(c)) for c in range(n_chunks)]
        first[0].start()
        for c in range(n_chunks - 1):
            first[c + 1].start()
            land(first[c], 0, c, slot_of(c))

    @pl.when(t + 1 < total)
    def _():
        generation(t + 1)[0].start()

    copy, half, chunk = generation(t)
    land(copy, half, chunk, t % 2)
    return wbuf_ref.at[j % 2]


def _row_parts(tm, n=2):
    n = n if tm % (16 * n) == 0 and tm >= 1024 else 1
    return [slice(r * (tm // n), (r + 1) * (tm // n)) for r in range(n)]


def _proj_q_kernel(h_ref, w_ref, cos_ref, sin_ref, q_ref, wb_ref):
    _cast_weight(w_ref, wb_ref)
    for rows in _row_parts(h_ref.shape[0], 4):
        acc = jnp.dot(h_ref[rows, :], wb_ref[...], preferred_element_type=F32)
        _store_heads(q_ref, rows, _rope(acc, cos_ref.at[rows, :], sin_ref.at[rows, :]) * (HEAD_DIM ** -0.5 * LOG2_E))


def _proj_kv_kernel(h_ref, w_ref, cos_ref, sin_ref, kf_ref, vf_ref, kh_ref, vh_ref, wb_ref, *, kvw):
    _cast_weight(w_ref, wb_ref)
    for rows in _row_parts(h_ref.shape[0], 4):
        acc = jnp.dot(h_ref[rows, :], wb_ref[...], preferred_element_type=F32)
        k = _rope(acc[:, :kvw], cos_ref.at[rows, :], sin_ref.at[rows, :])
        v = acc[:, kvw:]
        kf_ref[rows, :] = k
        vf_ref[rows, :] = v
        _store_heads(kh_ref, rows, k)
        _store_heads(vh_ref, rows, v)


def _proj_act_kernel(h_ref, w_hbm, o_ref, wbuf_ref, stage_ref, sem_ref, *, segments, n_chunks):
    tn = o_ref.shape[1]
    n_blocks = sum(s[1] for s in segments)
    in_seg = lambda blk, s: (blk >= s[0]) & (blk < s[0] + s[1])

    def col_of_block(blk):
        if isinstance(blk, int):
            return next(s[2] + (blk - s[0]) * tn for s in segments if s[0] <= blk < s[0] + s[1])
        return sum(jnp.where(in_seg(blk, s), s[2] + (blk - s[0]) * tn, 0) for s in segments)

    wb_ref = _stream_weights(w_hbm, wbuf_ref, stage_ref, sem_ref, col_of_block=col_of_block, n_blocks=n_blocks, n_chunks=n_chunks)
    acts = {s[3] for s in segments}
    assert acts in ({"none"}, {"silu"}, {"sigmoid"}, {"silu", "sigmoid"})
    j = pl.program_id(0)
    for rows in _row_parts(h_ref.shape[0]):
        acc = jnp.dot(h_ref[rows, :], wb_ref[...], preferred_element_type=F32)
        if acts != {"none"}:
            sig = _sigmoid(acc)
            if acts == {"silu"}:
                acc = acc * sig
            elif acts == {"sigmoid"}:
                acc = sig
            else:
                is_silu = functools.reduce(lambda a, b: a | b, [in_seg(j, s) for s in segments if s[3] == "silu"])
                acc = jnp.where(is_silu, acc * sig, sig)
        o_ref[rows, :] = acc.astype(o_ref.dtype)


def _proj_specs(d, tm, tn, col0):
    assert col0 % tn == 0
    cb0 = col0 // tn
    return [pl.BlockSpec((tm, d), lambda j, i: (i, 0)), pl.BlockSpec((d, tn), lambda j, i: (0, cb0 + j))]


def _proj_vmem(d, tm, tn, outs):
    return _vmem_estimate([((tm, d), BF16), ((d, tn), F32)] + outs, [((tm, tn), F32)] * 3 + [((d, tn), BF16)])


def _proj_q(h, w, cos, sin, aw, tm, tn):
    m, d = h.shape
    nh = aw // HEAD_DIM
    tab = pl.BlockSpec((tm, LANES), lambda j, i: (i, 0))
    return pl.pallas_call(
        _proj_q_kernel,
        grid=(aw // tn, m // tm),
        in_specs=_proj_specs(d, tm, tn, 0) + [tab, tab],
        out_specs=pl.BlockSpec((tn // HEAD_DIM, tm, HEAD_DIM), lambda j, i: (j, i, 0)),
        out_shape=jax.ShapeDtypeStruct((nh, m, HEAD_DIM), BF16),
        scratch_shapes=[pltpu.VMEM((d, tn), BF16)],
        compiler_params=_params(("arbitrary", "arbitrary"), _proj_vmem(d, tm, tn, [((tm, 2 * tn), BF16), ((tm, 2 * LANES), F32)])),
        name="proj_q",
    )(h, w, cos, sin)


def _proj_kv(h, w, cos, sin, col0, kvw, tm):
    m, d = h.shape
    g = kvw // HEAD_DIM
    tn = 2 * kvw
    tab = pl.BlockSpec((tm, LANES), lambda j, i: (i, 0))
    flat = pl.BlockSpec((tm, kvw), lambda j, i: (i, 0))
    heads = pl.BlockSpec((g, tm, HEAD_DIM), lambda j, i: (0, i, 0))
    return pl.pallas_call(
        functools.partial(_proj_kv_kernel, kvw=kvw),
        grid=(1, m // tm),
        in_specs=_proj_specs(d, tm, tn, col0) + [tab, tab],
        out_specs=[flat, flat, heads, heads],
        out_shape=[jax.ShapeDtypeStruct((m, kvw), F32)] * 2 + [jax.ShapeDtypeStruct((g, m, HEAD_DIM), BF16)] * 2,
        scratch_shapes=[pltpu.VMEM((d, tn), BF16)],
        compiler_params=_params(("arbitrary", "arbitrary"), _proj_vmem(d, tm, tn, [((tm, 2 * tn), F32), ((tm, 2 * tn), BF16), ((tm, 2 * LANES), F32)])),
        name="proj_kv",
    )(h, w, cos, sin)


def _proj_act(h, w, runs, tm, tn, name, out_dtype=BF16):
    m, d = h.shape
    n_chunks = m // tm
    segments, b0 = [], 0
    for col0, width, act in runs:
        assert width % tn == 0 and col0 % LANES == 0
        segments.append((b0, width // tn, col0, act))
        b0 += width // tn
    n_blocks, ncols = b0, b0 * tn
    assert d % (8 * n_chunks) == 0
    scratch = [((2, d, tn), BF16), ((2, d // n_chunks, tn), F32)]
    return pl.pallas_call(
        functools.partial(_proj_act_kernel, segments=tuple(segments), n_chunks=n_chunks),
        grid=(n_blocks, n_chunks),
        in_specs=[pl.BlockSpec((tm, d), lambda j, i: (i, 0)), pl.BlockSpec(memory_space=pl.ANY)],
        out_specs=pl.BlockSpec((tm, tn), lambda j, i: (i, j)),
        out_shape=jax.ShapeDtypeStruct((m, ncols), out_dtype),
        scratch_shapes=[pltpu.VMEM(s, t) for s, t in scratch] + [pltpu.SemaphoreType.DMA((2,))],
        compiler_params=_params(("arbitrary", "arbitrary"),
                                _vmem_estimate([((tm, d), BF16), ((tm, tn), out_dtype)], scratch + [((tm // len(_row_parts(tm)), tn), F32)] * 3)),
        name=name,
    )(h, w)


def _attn_weights(q, kw, sink, n_valid):
    st = lax.dot_general(kw, q, (((1,), (1,)), ((), ())), preferred_element_type=F32)
    if n_valid is not None:
        row = lax.broadcasted_iota(jnp.int32, st.shape, 0)
        st = jnp.where(row < n_valid, st, -jnp.inf)
    m = jnp.maximum(jnp.max(st, axis=0, keepdims=True), sink)
    e = jnp.exp2(st - m)
    return e.astype(BF16), jnp.sum(e, axis=0, keepdims=True) + jnp.exp2(sink - m)


def _attn_values(vw, e, denom):
    ot = jnp.dot(vw.astype(F32).T.astype(BF16), e, preferred_element_type=F32) / denom
    return ot.T


def _store_unit(o_ref, za_ref, row0, o, rep):
    for r in range(0, rep, 2):
        pair = jnp.concatenate([o[r * CHUNK:(r + 1) * CHUNK], o[(r + 1) * CHUNK:(r + 2) * CHUNK]], axis=1)
        gate = za_ref[row0:row0 + CHUNK, r * HEAD_DIM:(r + 2) * HEAD_DIM].astype(F32)
        o_ref[row0:row0 + CHUNK, r * HEAD_DIM:(r + 2) * HEAD_DIM] = (pair * gate).astype(o_ref.dtype)


def _attn_kernel(q_ref, k_ref, v_ref, ks_ref, vs_ref, sink_ref, za_ref, wo_ref, o_ref, wob_ref,
                 *, n_units, rep, steps_per_stream, n_prompt_steps, n_cast_steps):
    step = pl.program_id(1)
    win = WINDOW + CHUNK
    sink = sink_ref[0] * LOG2_E

    @pl.when(pl.program_id(0) * (n_prompt_steps + 1) + step < n_cast_steps)
    def _():
        wob_ref[...] = wo_ref[...].astype(wob_ref.dtype)

    def run(windows):
        staged = []
        for c, (kw, vw, n_valid) in enumerate(windows):
            q = q_ref[:, c * CHUNK:(c + 1) * CHUNK, :].reshape(rep * CHUNK, HEAD_DIM)
            staged.append((vw,) + _attn_weights(q, kw, sink, n_valid))
        for c in range(n_units):
            _store_unit(o_ref, za_ref, c * CHUNK, _attn_values(*staged[c]), rep)

    @pl.when(step < n_prompt_steps)
    def _():
        cb = step % steps_per_stream
        windows = []
        for c in range(n_units):
            start = pl.multiple_of(jnp.maximum(cb * n_units + c - WINDOW // CHUNK, 0) * CHUNK, CHUNK)
            n_valid = jnp.where(cb == 0, (c + 1) * CHUNK, win) if c < WINDOW // CHUNK else None
            windows.append((k_ref[0, pl.ds(start, win), :], v_ref[0, pl.ds(start, win), :], n_valid))
        run(windows)

    @pl.when(step == n_prompt_steps)
    def _():
        run([(ks_ref[0, b * win:(b + 1) * win, :], vs_ref[0, b * win:(b + 1) * win, :], None) for b in range(n_units)])


def _attention(q_hm, k_hm, v_hm, ks, vs, sink_rows, za, w_out, batch, seq, dec_batch, aw):
    nh, m, _ = q_hm.shape
    g = k_hm.shape[0]
    rep = nh // g
    win = WINDOW + CHUNK
    n_units = dec_batch
    rows = n_units * CHUNK
    assert seq % rows == 0 and n_units >= WINDOW // CHUNK
    steps_per_stream = seq // rows
    n_prompt_steps = batch * steps_per_stream
    blocks = ([((rep, rows, LANES), BF16)] + [((seq, LANES), BF16)] * 2 + [((dec_batch * win, LANES), BF16)] * 2
              + [((rows, rep * HEAD_DIM), BF16)] * 2)
    temps = [((rep * CHUNK, 2 * LANES), F32)] * (4 * n_units)
    stream = lambda gi, s: (gi, jnp.minimum(s // steps_per_stream, batch - 1), 0)
    dk, dn = w_out.shape
    n_cast_steps = 16
    assert g * (n_prompt_steps + 1) >= n_cast_steps and dk % (16 * n_cast_steps) == 0
    cast_rows = dk // n_cast_steps
    cast_block = pl.BlockSpec((cast_rows, dn), lambda gi, s: (jnp.minimum(gi * (n_prompt_steps + 1) + s, n_cast_steps - 1), 0))
    blocks += [((cast_rows, dn), F32), ((cast_rows, dn), BF16)]
    return pl.pallas_call(
        functools.partial(_attn_kernel, n_units=n_units, rep=rep, steps_per_stream=steps_per_stream, n_prompt_steps=n_prompt_steps,
                          n_cast_steps=n_cast_steps),
        grid=(g, n_prompt_steps + 1),
        in_specs=[
            pl.BlockSpec((rep, rows, HEAD_DIM), lambda gi, s: (gi, s, 0)),
            pl.BlockSpec((1, seq, HEAD_DIM), stream),
            pl.BlockSpec((1, seq, HEAD_DIM), stream),
            pl.BlockSpec((1, dec_batch * win, HEAD_DIM), lambda gi, s: (gi, 0, 0)),
            pl.BlockSpec((1, dec_batch * win, HEAD_DIM), lambda gi, s: (gi, 0, 0)),
            pl.BlockSpec((1, 1, rep * CHUNK), lambda gi, s: (gi, 0, 0)),
            pl.BlockSpec((rows, rep * HEAD_DIM), lambda gi, s: (s, gi)),
            cast_block,
        ],
        out_specs=[pl.BlockSpec((rows, rep * HEAD_DIM), lambda gi, s: (s, gi)), cast_block],
        out_shape=[jax.ShapeDtypeStruct((m, aw), BF16), jax.ShapeDtypeStruct((dk, dn), BF16)],
        compiler_params=_params(("arbitrary",) * 2, _vmem_estimate(blocks, temps)),
        name="attention",
    )(q_hm, k_hm, v_hm, ks, vs, sink_rows, za, w_out)


def _ssm_disc_kernel(lre_ref, lim_ref, ldt_ref, are_ref, aim_ref, dre_ref, dim_ref, fre_ref, fim_ref):
    lr = jnp.minimum(lre_ref[...], LAMBDA_RE_MAX)
    li = lim_ref[...]
    dt = jnp.exp(ldt_ref[...])
    mag = jnp.exp(lr * dt)
    a_re = mag * jnp.cos(li * dt)
    a_im = mag * jnp.sin(li * dt)
    den = lr * lr + li * li
    nr = a_re - 1.0
    fre_ref[...] = (nr * lr + a_im * li) / den
    fim_ref[...] = (a_im * lr - nr * li) / den
    are_ref[...] = a_re
    aim_ref[...] = a_im
    for _ in range(SSM_POW_BITS):
        a_re, a_im = a_re * a_re - a_im * a_im, 2.0 * a_re * a_im
    dre_ref[...] = a_re
    dim_ref[...] = a_im


def _cmul(ar, ai, br, bi):
    return ar * br - ai * bi, ar * bi + ai * br


def _ssm_build_kernel(are_ref, aim_ref, dre_ref, dim_ref, fre_ref, fim_ref, btr_ref, bti_ref, cr_ref, ci_ref, dv_ref,
                      met_ref, ft_ref, a1_ref, a2_ref, a2s_ref, *, groups):
    lc = SSM_CHUNK * SSM_GROUP
    sub = lax.broadcasted_iota(jnp.int32, (SSM_GROUP, lc), 0)
    lane = lax.broadcasted_iota(jnp.int32, (SSM_GROUP, lc), 1)
    for gi in range(groups):
        row = slice(gi, gi + 1)
        a_re, a_im = are_ref[row, :], aim_ref[row, :]
        pw = [(jnp.ones_like(a_re), jnp.zeros_like(a_re))]
        for _ in range(SSM_CHUNK):
            pw.append(_cmul(pw[-1][0], pw[-1][1], a_re, a_im))
        c_re, c_im = cr_ref[gi], ci_ref[gi]
        wt = [_cmul(pr, pi, c_re, c_im) for pr, pi in pw]
        wt_re = jnp.concatenate([w[0] for w in wt[:SSM_CHUNK]], axis=0)
        wt_im = jnp.concatenate([w[1] for w in wt[:SSM_CHUNK]], axis=0)
        et_re = jnp.concatenate([w[0] for w in wt[1:]], axis=0)
        et_im = jnp.concatenate([w[1] for w in wt[1:]], axis=0)
        bb_re, bb_im = _cmul(fre_ref[row, :], fim_ref[row, :], btr_ref[gi], bti_ref[gi])
        r0 = lax.dot_general(jnp.concatenate([bb_re, bb_im], axis=1), jnp.concatenate([wt_re, -wt_im], axis=1),
                             (((1,), (1,)), ((), ())), preferred_element_type=F32, precision=lax.Precision.HIGHEST)
        r0 = r0 + jnp.where(sub == lane, dv_ref[gi], 0.0)
        rows = [r0] + [jnp.where(lane >= s * SSM_GROUP, pltpu.roll(r0, s * SSM_GROUP, 1), 0.0) for s in range(1, SSM_CHUNK)]
        mt = jnp.concatenate(rows, axis=0).T
        met_ref[gi] = jnp.concatenate([mt, et_re, -et_im], axis=1).astype(met_ref.dtype)
        fb = [_cmul(pw[SSM_CHUNK - 1 - s][0], pw[SSM_CHUNK - 1 - s][1], bb_re, bb_im) for s in range(SSM_CHUNK)]
        f_all = jnp.concatenate([jnp.concatenate([x[0] for x in fb], axis=0), jnp.concatenate([x[1] for x in fb], axis=0)], axis=1)
        ft_ref[gi] = f_all.T.astype(ft_ref.dtype)
        d_re, d_im = dre_ref[row, :], dim_ref[row, :]
        a1_ref[gi] = jnp.concatenate([d_re, d_re], axis=1)
        a2_ref[gi] = jnp.concatenate([-d_im, d_im], axis=1)
        a2s_ref[gi] = jnp.concatenate([d_im, -d_im], axis=1)


def _ssm_params(lambda_re, lambda_im, log_dt, b_re, b_im, c_re, c_im, d_skip):
    ng, p = lambda_re.shape
    lc = SSM_CHUNK * SSM_GROUP
    full = pl.BlockSpec((ng, p), lambda: (0, 0))
    disc = pl.pallas_call(
        _ssm_disc_kernel,
        in_specs=[full, full, pl.BlockSpec((ng, 1), lambda: (0, 0))],
        out_specs=[full] * 6,
        out_shape=[jax.ShapeDtypeStruct((ng, p), F32)] * 6,
        name="ssm_disc",
    )(lambda_re, lambda_im, log_dt.reshape(ng, 1))
    gb = _pick(ng, (8,))
    bt_re = jnp.swapaxes(b_re, 1, 2)
    bt_im = jnp.swapaxes(b_im, 1, 2)
    dvec = jnp.pad(d_skip, ((0, 0), (0, lc - SSM_GROUP))).reshape(ng, 1, lc)
    rows = pl.BlockSpec((gb, p), lambda i: (i, 0))
    mats = pl.BlockSpec((gb, SSM_GROUP, p), lambda i: (i, 0, 0))
    dec = pl.BlockSpec((gb, 1, 2 * p), lambda i: (i, 0, 0))
    blocks = ([((gb, LANES), F32)] * 6 + [((gb, SSM_GROUP, LANES), F32)] * 4 + [((gb, 8, lc), F32)]
              + [((gb, lc, lc + 2 * p), BF16), ((gb, 2 * p, lc), BF16)] + [((gb, 8, LANES), F32)] * 3)
    met, ft, a1, a2, a2s = pl.pallas_call(
        functools.partial(_ssm_build_kernel, groups=gb),
        grid=(ng // gb,),
        in_specs=[rows] * 6 + [mats] * 4 + [pl.BlockSpec((gb, 1, lc), lambda i: (i, 0, 0))],
        out_specs=[pl.BlockSpec((gb, lc, lc + 2 * p), lambda i: (i, 0, 0)), pl.BlockSpec((gb, 2 * p, lc), lambda i: (i, 0, 0)), dec, dec, dec],
        out_shape=[jax.ShapeDtypeStruct((ng, lc, lc + 2 * p), BF16), jax.ShapeDtypeStruct((ng, 2 * p, lc), BF16)]
        + [jax.ShapeDtypeStruct((ng, 1, 2 * p), F32)] * 3,
        compiler_params=_params(("arbitrary",), _vmem_estimate(blocks, [((lc, lc + 2 * p), F32)] * 8)),
        name="ssm_build",
    )(*disc, bt_re, bt_im, c_re, c_im, dvec)
    flat = lambda a: a.reshape(1, ng * 2 * p)
    return met, ft, flat(a1), flat(a2), flat(a2s)


def _chunk_blocks(nk):
    return [(k0, min(LANES, nk - k0)) for k0 in range(0, nk, LANES)]


def _pad_rows(x):
    n = x.shape[0]
    return x if n == LANES else jnp.concatenate([x, jnp.zeros((LANES - n, x.shape[1]), x.dtype)], axis=0)


def _ssm_inputs_phase(u_ref, ft_ref, rhs_ref, s_ref, *, nk, p):
    groups = LANES // SSM_GROUP
    blocks = _chunk_blocks(nk)
    xt = [[_pad_rows(u_ref[pl.ds(k0 * SSM_CHUNK + s, n, stride=SSM_CHUNK), :].astype(BF16)).T for k0, n in blocks]
          for s in range(SSM_CHUNK)]
    for gi in range(groups):
        rhs = jnp.concatenate([jnp.concatenate([xt[s][kb][gi * SSM_GROUP:(gi + 1) * SSM_GROUP, :] for kb in range(len(blocks))], axis=1)
                               for s in range(SSM_CHUNK)], axis=0)
        rhs_ref[gi] = rhs
        st = jnp.dot(ft_ref[gi], rhs, preferred_element_type=F32)
        for kb, (k0, n) in enumerate(blocks):
            s_ref[k0:k0 + n, gi * 2 * p:(gi + 1) * 2 * p] = st[:, kb * LANES:(kb + 1) * LANES].T[:n, :]


def _ssm_outputs_phase(rhs_ref, h_ref, met_ref, y_ref, *, nk, p):
    groups = LANES // SSM_GROUP
    blocks = _chunk_blocks(nk)
    zt = [[None] * groups for _ in range(SSM_CHUNK)]
    for gi in range(groups):
        ht = jnp.concatenate([_pad_rows(h_ref[k0:k0 + n, gi * 2 * p:(gi + 1) * 2 * p]).T for k0, n in blocks], axis=1)
        rhs = jnp.concatenate([rhs_ref[gi], ht.astype(BF16)], axis=0)
        yt = jnp.dot(met_ref[gi], rhs, preferred_element_type=F32)
        for t in range(SSM_CHUNK):
            zt[t][gi] = yt[t * SSM_GROUP:(t + 1) * SSM_GROUP, :]
    for t in range(SSM_CHUNK):
        z = jnp.concatenate(zt[t], axis=0)
        for kb, (k0, n) in enumerate(blocks):
            y_ref[pl.ds(k0 * SSM_CHUNK + t, n, stride=SSM_CHUNK), :] = z[:, kb * LANES:(kb + 1) * LANES].T[:n, :]


def _ssm_scan_phase(s_ref, h0_ref, a1_ref, a2_ref, a2s_ref, h_ref, fin_ref, ss_ref, *, batch, kp, dec_batch, ks, p):
    def swap_halves(x):
        lane = lax.broadcasted_iota(jnp.int32, x.shape, 1)
        return jnp.where(lane % (2 * p) < p, pltpu.roll(x, x.shape[1] - p, 1), pltpu.roll(x, p, 1))

    ss_ref[...] = swap_halves(s_ref[...])
    a1, a2, a2s = a1_ref[...], a2_ref[...], a2s_ref[...]
    zero = jnp.zeros_like(a1)

    def step(k, h, hs):
        h_ref[pl.ds(k, 1), :] = h
        s = s_ref[pl.ds(k, 1), :]
        ss = ss_ref[pl.ds(k, 1), :]
        return a1 * h + a2 * hs + s, a1 * hs + a2s * h + ss

    def body(k, carry):
        out = []
        for b in range(batch):
            out.extend(step(b * kp + k, carry[2 * b], carry[2 * b + 1]))
        return tuple(out)

    fin = lax.fori_loop(0, kp, body, (zero,) * (2 * batch))
    for b in range(batch):
        fin_ref[b:b + 1, :] = fin[2 * b]
    h0s_all = swap_halves(h0_ref[...])
    for b in range(dec_batch):
        h, hs = h0_ref[b:b + 1, :], h0s_all[b:b + 1, :]
        for k in range(ks):
            h, hs = step(batch * kp + b * ks + k, h, hs)
        fin_ref[batch + b:batch + b + 1, :] = h


def _ssm_kernel(u_ref, ft_ref, met_ref, h0_ref, a1_ref, a2_ref, a2s_ref, y_ref, fin_ref, rhs_ref, s_ref, ss_ref, h_ref,
                *, batch, kp, dec_batch, ks, p):
    nk = batch * kp + dec_batch * ks
    _ssm_inputs_phase(u_ref, ft_ref, rhs_ref, s_ref, nk=nk, p=p)
    _ssm_scan_phase(s_ref, h0_ref, a1_ref, a2_ref, a2s_ref, h_ref, fin_ref, ss_ref, batch=batch, kp=kp, dec_batch=dec_batch, ks=ks, p=p)
    _ssm_outputs_phase(rhs_ref, h_ref, met_ref, y_ref, nk=nk, p=p)


def _ssm(u, met, ft, a1, a2, a2s, h0, batch, kp, dec_batch, ks):
    m, sw = u.shape
    ng, p2, lc = ft.shape
    p = p2 // 2
    nk = m // SSM_CHUNK
    assert nk == batch * kp + dec_batch * ks and (nk % LANES) % 8 == 0
    nk_lanes = -(-nk // LANES) * LANES
    groups = LANES // SSM_GROUP
    lb = groups * p2
    nseq = batch + dec_batch
    u_spec = pl.BlockSpec((m, LANES), lambda i: (0, i))
    rowb = pl.BlockSpec((1, lb), lambda i: (0, i))
    blocks = [((m, LANES), F32)] * 2 + [((groups, p2, lc), BF16), ((groups, lc, lc + p2), BF16), ((nseq, lb), F32)]
    scratch = [((groups, lc, nk_lanes), BF16)] + [((nk, lb), F32)] * 3
    return pl.pallas_call(
        functools.partial(_ssm_kernel, batch=batch, kp=kp, dec_batch=dec_batch, ks=ks, p=p),
        grid=(ng // groups,),
        in_specs=[u_spec, pl.BlockSpec((groups, p2, lc), lambda i: (i, 0, 0)), pl.BlockSpec((groups, lc, lc + p2), lambda i: (i, 0, 0)),
                  pl.BlockSpec((dec_batch, lb), lambda i: (0, i)), rowb, rowb, rowb],
        out_specs=[u_spec, pl.BlockSpec((nseq, lb), lambda i: (0, i))],
        out_shape=[jax.ShapeDtypeStruct((m, sw), F32), jax.ShapeDtypeStruct((nseq, ng * p2), F32)],
        scratch_shapes=[pltpu.VMEM(s, d) for s, d in scratch],
        compiler_params=_params(("arbitrary",), _vmem_estimate(blocks, scratch + [((lc + p2, nk_lanes), BF16), ((lc, nk_lanes), F32)])),
        name="ssm",
    )(u, ft, met, h0, a1, a2, a2s)


def _weight_stream_scratch(k, tn, n_chunks):
    assert k % (8 * n_chunks) == 0
    shapes = [((2, k, tn), BF16), ((2, k // n_chunks, tn), F32)]
    return shapes, [pltpu.VMEM(s, t) for s, t in shapes] + [pltpu.SemaphoreType.DMA((2,))]


def _glu_kernel(y_ref, w_hbm, ba_ref, bg_ref, zs_ref, o_ref, wa_buf, wa_stage, wa_sem, wg_buf, wg_stage, wg_sem, *, n_blocks, n_chunks):
    tn = o_ref.shape[1]
    stream = functools.partial(_stream_weights, w_hbm, n_blocks=n_blocks, n_chunks=n_chunks)
    wa_ref = stream(wa_buf, wa_stage, wa_sem, col_of_block=lambda blk: blk * tn)
    wg_ref = stream(wg_buf, wg_stage, wg_sem, col_of_block=lambda blk: (n_blocks + blk) * tn)
    y = y_ref[...].astype(BF16)
    a = jnp.dot(y, wa_ref[...], preferred_element_type=F32) + ba_ref[...]
    g = jnp.dot(y, wg_ref[...], preferred_element_type=F32) + bg_ref[...]
    o_ref[...] = (a * _sigmoid(g) * zs_ref[...].astype(F32)).astype(o_ref.dtype)


def _glu(y, w_glu, b_glu, acts, zs_col0, tm, tn):
    m, sw = y.shape
    nb, n_chunks = sw // tn, m // tm
    zb0 = zs_col0 // tn
    shapes, scratch = _weight_stream_scratch(sw, tn, n_chunks)
    blocks = [((tm, sw), F32), ((tm, tn), BF16), ((tm, tn), BF16)]
    return pl.pallas_call(
        functools.partial(_glu_kernel, n_blocks=nb, n_chunks=n_chunks),
        grid=(nb, n_chunks),
        in_specs=[
            pl.BlockSpec((tm, sw), lambda j, i: (i, 0)),
            pl.BlockSpec(memory_space=pl.ANY),
            pl.BlockSpec((1, tn), lambda j, i: (0, j)),
            pl.BlockSpec((1, tn), lambda j, i: (0, nb + j)),
            pl.BlockSpec((tm, tn), lambda j, i: (i, zb0 + j)),
        ],
        out_specs=pl.BlockSpec((tm, tn), lambda j, i: (i, j)),
        out_shape=jax.ShapeDtypeStruct((m, sw), BF16),
        scratch_shapes=scratch * 2,
        compiler_params=_params(("arbitrary", "arbitrary"), _vmem_estimate(blocks, shapes * 2 + [((tm, tn), F32)] * 4)),
        name="glu",
    )(y, w_glu, b_glu, b_glu, acts)


def _merge_kernel(xa_ref, xs_ref, wpa_hbm, wps_hbm, ga_ref, gs_ref, o_ref, wa_buf, wa_stage, wa_sem, ws_buf, ws_stage, ws_sem,
                  *, n_blocks, n_chunks):
    tn = o_ref.shape[1]
    col = lambda blk: blk * tn
    wpa_ref = _stream_weights(wpa_hbm, wa_buf, wa_stage, wa_sem, col_of_block=col, n_blocks=n_blocks, n_chunks=n_chunks)
    wps_ref = _stream_weights(wps_hbm, ws_buf, ws_stage, ws_sem, col_of_block=col, n_blocks=n_blocks, n_chunks=n_chunks)
    br_a = jnp.dot(xa_ref[...], wpa_ref[...], preferred_element_type=F32)
    br_s = jnp.dot(xs_ref[...], wps_ref[...], preferred_element_type=F32)
    o_ref[...] = (ga_ref[...].astype(F32) * br_a + gs_ref[...].astype(F32) * br_s).astype(o_ref.dtype)


def _merge(xa, xs, w_pa, w_ps, acts, gate_col0, tm, tn):
    m, aw = xa.shape
    sw = xs.shape[1]
    d = w_pa.shape[1]
    nb, n_chunks = d // tn, m // tm
    gb0 = gate_col0 // tn
    shapes_a, scratch_a = _weight_stream_scratch(aw, tn, n_chunks)
    shapes_s, scratch_s = _weight_stream_scratch(sw, tn, n_chunks)
    blocks = [((tm, aw), BF16), ((tm, sw), BF16)] + [((tm, tn), BF16)] * 3
    return pl.pallas_call(
        functools.partial(_merge_kernel, n_blocks=nb, n_chunks=n_chunks),
        grid=(nb, n_chunks),
        in_specs=[
            pl.BlockSpec((tm, aw), lambda j, i: (i, 0)),
            pl.BlockSpec((tm, sw), lambda j, i: (i, 0)),
            pl.BlockSpec(memory_space=pl.ANY),
            pl.BlockSpec(memory_space=pl.ANY),
            pl.BlockSpec((tm, tn), lambda j, i: (i, gb0 + j)),
            pl.BlockSpec((tm, tn), lambda j, i: (i, gb0 + nb + j)),
        ],
        out_specs=pl.BlockSpec((tm, tn), lambda j, i: (i, j)),
        out_shape=jax.ShapeDtypeStruct((m, d), BF16),
        scratch_shapes=scratch_a + scratch_s,
        compiler_params=_params(("arbitrary", "arbitrary"), _vmem_estimate(blocks, shapes_a + shapes_s + [((tm, tn), F32)] * 4)),
        name="merge",
    )(xa, xs, w_pa, w_ps, acts, acts)


def _out_kernel(mg_ref, w_ref, x_ref, g_ref, o_ref, ssq_ref, *, n_col_blocks, tn):
    j = pl.program_id(1)
    blk = x_ref[...] + jnp.dot(mg_ref[...], w_ref[...], preferred_element_type=F32)
    o_ref[:, pl.ds(pl.multiple_of(j * tn, tn), tn)] = blk
    part = jnp.sum(blk * blk, axis=-1, keepdims=True)

    @pl.when(j == 0)
    def _():
        ssq_ref[...] = part

    @pl.when(j > 0)
    def _():
        ssq_ref[...] += part

    @pl.when(j == n_col_blocks - 1)
    def _():
        scale = lax.rsqrt(ssq_ref[...] / (n_col_blocks * tn) + NORM_EPS)
        o_ref[...] = o_ref[...] * scale * g_ref[...]


def _out(merged, row_block0, w_out, x, final_g, tm, tn):
    mx, d = x.shape
    nb = d // tn
    blocks = [((tm, d), BF16), ((d, tn), BF16), ((tm, tn), F32), ((tm, d), F32)]
    return pl.pallas_call(
        functools.partial(_out_kernel, n_col_blocks=nb, tn=tn),
        grid=(mx // tm, nb),
        in_specs=[
            pl.BlockSpec((tm, d), lambda i, j: (row_block0 + i, 0)),
            pl.BlockSpec((d, tn), lambda i, j: (0, j)),
            pl.BlockSpec((tm, tn), lambda i, j: (i, j)),
            pl.BlockSpec((1, d), lambda i, j: (0, 0)),
        ],
        out_specs=pl.BlockSpec((tm, d), lambda i, j: (i, 0)),
        out_shape=jax.ShapeDtypeStruct((mx, d), F32),
        scratch_shapes=[pltpu.VMEM((tm, 1), F32)],
        compiler_params=_params(("arbitrary", "arbitrary"), _vmem_estimate(blocks, [((tm, tn), F32)] * 3 + [((tm, LANES), F32)])),
        name="out_norm",
    )(merged, w_out, x, final_g.reshape(1, d))


def _rope_tables(positions):
    half = HEAD_DIM // 2
    inv_freq = ROPE_THETA ** (-jnp.arange(half, dtype=F32) / half)
    ang = positions.astype(F32)[:, None] * inv_freq[None, :]
    cos, sin = jnp.cos(ang), jnp.sin(ang)
    reps = LANES // HEAD_DIM
    return jnp.tile(jnp.concatenate([cos, cos], axis=1), (1, reps)), jnp.tile(jnp.concatenate([-sin, sin], axis=1), (1, reps))


def kernel(x_prompt, x_sample, cache_k, cache_v, state_ssm_re, state_ssm_im, norm_g, w_in, sink, lambda_re, lambda_im,
           log_dt, b_re, b_im, c_re, c_im, d_skip, w_glu, b_glu, w_pa, w_ps, w_out, final_g):
    depth = norm_g.shape[0]
    assert depth == 1, "one trunk layer"
    batch, seq, d = x_prompt.shape
    dec_batch, dec_seq, _ = x_sample.shape
    aw = w_pa.shape[1]
    sw = w_ps.shape[1]
    nh = aw // HEAD_DIM
    g = max(1, nh // GQA_GROUPING)
    rep = nh // g
    kvw = g * HEAD_DIM
    ng, p = lambda_re.shape[1:]
    assert dec_seq == CHUNK and cache_k.shape[2] == WINDOW and seq % (2 * CHUNK) == 0
    assert kvw % LANES == 0 and rep % 2 == 0 and sw == ng * SSM_GROUP and (1 << SSM_POW_BITS) == SSM_CHUNK
    assert 2 * p == LANES and sw % LANES == 0
    mp, ms = batch * seq, dec_batch * dec_seq
    m = mp + ms
    widths = (aw, 2 * kvw, aw, sw, sw, 2 * d)
    assert sum(widths) == w_in.shape[2]
    c_q, c_kv, c_za, c_u, c_zs, c_gate = (sum(widths[:n]) for n in range(len(widths)))
    tm = _pick(math.gcd(mp, ms), (512, 256, 128))
    tmp = max(t for t in range(16, 1153, 16) if m % t == 0)
    tn = _pick(math.gcd(c_kv, c_za, c_u, c_zs, c_gate, 2 * d), (512, 256, 128))

    xp = x_prompt.reshape(mp, d)
    xs = x_sample.reshape(ms, d)
    w_in2 = w_in.reshape(d, w_in.shape[2])
    positions = jnp.concatenate([jnp.tile(jnp.arange(seq, dtype=jnp.int32), batch),
                                 jnp.tile(PAST_LEN + jnp.arange(dec_seq, dtype=jnp.int32), dec_batch)])
    cos, sin = _rope_tables(positions)

    h = _rmsnorm(xp, xs, norm_g[0], tm)
    q_hm = _proj_q(h, w_in2, cos, sin, aw, tmp, tn)
    k_f, v_f, k_hm, v_hm = _proj_kv(h, w_in2, cos, sin, c_kv, kvw, tmp)
    tna = _pick(math.gcd(aw, sw), (1024, 512, 256))
    acts = _proj_act(h, w_in2, [(c_za, aw, "silu"), (c_zs, sw, "silu"), (c_gate, 2 * d, "sigmoid")], tmp, tna, "proj_acts")
    u = _proj_act(h, w_in2, [(c_u, sw, "none")], tmp, tna, "proj_u", out_dtype=F32)

    to_heads = lambda c: jnp.transpose(c[0], (2, 0, 1, 3)).astype(BF16)
    new_rows = lambda a: a[:, mp:].reshape(g, dec_batch, dec_seq, HEAD_DIM)
    ks = jnp.concatenate([to_heads(cache_k), new_rows(k_hm)], axis=2).reshape(g, dec_batch * (WINDOW + CHUNK), HEAD_DIM)
    vs = jnp.concatenate([to_heads(cache_v), new_rows(v_hm)], axis=2).reshape(g, dec_batch * (WINDOW + CHUNK), HEAD_DIM)
    sink_rows = jnp.repeat(sink[0].reshape(g, rep), CHUNK, axis=1).reshape(g, 1, rep * CHUNK)
    xa, w_out_b = _attention(q_hm, k_hm, v_hm, ks, vs, sink_rows, acts, w_out.reshape(d, d), batch, seq, dec_batch, aw)

    met, ft, a1, a2, a2s = _ssm_params(lambda_re[0], lambda_im[0], log_dt[0], b_re[0], b_im[0], c_re[0], c_im[0], d_skip[0])
    h0 = jnp.concatenate([state_ssm_re[0], state_ssm_im[0]], axis=-1).reshape(dec_batch, ng * 2 * p)
    y, fin = _ssm(u, met, ft, a1, a2, a2s, h0, batch, seq // SSM_CHUNK, dec_batch, dec_seq // SSM_CHUNK)

    tmm = max(t for t in range(16, 577, 16) if m % t == 0)
    x_ssm = _glu(y, w_glu.reshape(sw, 2 * sw), b_glu.reshape(1, 2 * sw), acts, aw, tmm, tna)
    merged = _merge(xa, x_ssm, w_pa.reshape(aw, d), w_ps.reshape(sw, d), acts, aw + sw, tmm, tna)
    tno = _pick(d, (1024, 512, 256))
    y_prompt = _out(merged, 0, w_out_b, xp, final_g, tm, tno).reshape(batch, seq, d)
    y_sample = _out(merged, mp // tm, w_out_b, xs, final_g, tm, tno).reshape(dec_batch, dec_seq, d)

    keep = min(WINDOW, seq)
    last_rows = lambda a: jnp.stack([a[(b + 1) * seq - keep:(b + 1) * seq] for b in range(batch)]).reshape(1, batch, keep, g, HEAD_DIM)
    dec_rows = lambda a: a[mp:].reshape(dec_batch, dec_seq, g, HEAD_DIM)[None]
    fin = fin.reshape(batch + dec_batch, ng, 2, p)
    return (y_prompt, y_sample, last_rows(k_f), last_rows(v_f), fin[:batch, :, 0][None], fin[:batch, :, 1][None],
            dec_rows(k_f), dec_rows(v_f), fin[batch:, :, 0][None], fin[batch:, :, 1][None])
```

```python
import functools
import math

import jax
import jax.numpy as jnp
from jax import lax
from jax.experimental import pallas as pl
from jax.experimental.pallas import tpu as pltpu

CHUNK = 64
WINDOW = 128
HEAD_DIM = 64
GQA_GROUPING = 8
SSM_GROUP = 16
PAST_LEN = 1024
ROPE_THETA = 10000.0
NORM_EPS = 1e-5
LAMBDA_RE_MAX = -1e-4
LOG2_E = math.log2(math.e)

SSM_CHUNK = 16
SSM_POW_BITS = 4
LANES = 128
V7X_VMEM_BYTES = 64 * 1024 * 1024
BF16 = jnp.bfloat16
F32 = jnp.float32


def _pick(n, prefs):
    for p in prefs:
        if n % p == 0:
            return p
    raise ValueError(f"no tile in {prefs} divides {n}")


def _params(sem, vmem_bytes):
    limit = min(int(vmem_bytes), V7X_VMEM_BYTES - 4 * 1024 * 1024)
    return pltpu.CompilerParams(dimension_semantics=sem, vmem_limit_bytes=limit)


def _sigmoid(x):
    return 0.5 * jnp.tanh(0.5 * x) + 0.5


def _nbytes(shape, dtype):
    return math.prod(shape) * jnp.dtype(dtype).itemsize


def _vmem_estimate(blocks, temps=()):
    return 2 * (2 * sum(_nbytes(s, d) for s, d in blocks) + sum(_nbytes(s, d) for s, d in temps))


def _rmsnorm_kernel(xp_ref, xs_ref, g_ref, o_ref, *, n_prompt_blocks):
    i = pl.program_id(0)

    def norm(x_ref):
        x = x_ref[...]
        y = x * lax.rsqrt(jnp.mean(x * x, axis=-1, keepdims=True) + NORM_EPS)
        o_ref[...] = (y * g_ref[...]).astype(o_ref.dtype)

    @pl.when(i < n_prompt_blocks)
    def _():
        norm(xp_ref)

    @pl.when(i >= n_prompt_blocks)
    def _():
        norm(xs_ref)


def _rmsnorm(xp, xs, g, tm):
    mp, d = xp.shape
    ms = xs.shape[0]
    npb, nsb = mp // tm, ms // tm
    return pl.pallas_call(
        functools.partial(_rmsnorm_kernel, n_prompt_blocks=npb),
        grid=(npb + nsb,),
        in_specs=[
            pl.BlockSpec((tm, d), lambda i: (jnp.minimum(i, npb - 1), 0)),
            pl.BlockSpec((tm, d), lambda i: (jnp.maximum(i - npb, 0), 0)),
            pl.BlockSpec((1, d), lambda i: (0, 0)),
        ],
        out_specs=pl.BlockSpec((tm, d), lambda i: (i, 0)),
        out_shape=jax.ShapeDtypeStruct((mp + ms, d), BF16),
        compiler_params=_params(("arbitrary",), _vmem_estimate([((tm, d), F32)] * 2 + [((tm, d), BF16)], [((tm, d), F32)])),
        name="rmsnorm_in",
    )(xp, xs, g.reshape(1, d))


def _rope(acc, cos_ref, sin_ref):
    tm, tn = acc.shape
    reps = tn // LANES
    cos = jnp.tile(cos_ref[...], (1, reps))
    sin = jnp.tile(sin_ref[...], (1, reps))
    lane = lax.broadcasted_iota(jnp.int32, (tm, tn), 1)
    low = (lane % HEAD_DIM) < (HEAD_DIM // 2)
    partner = jnp.where(low, pltpu.roll(acc, tn - HEAD_DIM // 2, 1), pltpu.roll(acc, HEAD_DIM // 2, 1))
    return acc * cos + partner * sin


def _store_heads(o_ref, rows, val):
    for h in range(val.shape[1] // HEAD_DIM):
        o_ref[h, rows, :] = val[:, h * HEAD_DIM:(h + 1) * HEAD_DIM].astype(o_ref.dtype)


def _cast_weight(w_ref, wb_ref):
    @pl.when(pl.program_id(1) == 0)
    def _():
        wb_ref[...] = w_ref[...].astype(wb_ref.dtype)


def _stream_weights(w_hbm, wbuf_ref, stage_ref, sem_ref, *, col_of_block, n_blocks, n_chunks):
    d, tn = wbuf_ref.shape[1:]
    ck = d // n_chunks
    total = n_blocks * n_chunks
    j = pl.program_id(0)
    t = j * n_chunks + pl.program_id(1)

    def aligned(x, a):
        return x if isinstance(x, int) else pl.multiple_of(x, a)

    def chunk_copy(blk, chunk, slot):
        src = w_hbm.at[pl.ds(aligned(chunk * ck, ck), ck), pl.ds(aligned(col_of_block(blk), LANES), tn)]
        return pltpu.make_async_copy(src, stage_ref.at[slot], sem_ref.at[slot])

    def generation(s):
        s = jnp.asarray(s, jnp.int32)
        k = s + n_chunks - 1
        past = (k > total - 1).astype(jnp.int32)
        k = jnp.minimum(k, total - 1)
        blk, chunk = k // n_chunks, k % n_chunks
        return chunk_copy(blk, chunk, s % 2), (blk + past) % 2, chunk

    def land(copy, half, chunk, slot):
        copy.wait()
        wbuf_ref[half, pl.ds(aligned(chunk * ck, ck), ck), :] = stage_ref[slot].astype(wbuf_ref.dtype)

    @pl.when(t == 0)
    def _():
        slot_of = lambda c: (c + n_chunks - 1) % 2
        first = [chunk_copy(0, c, slot_of(c)) for c in range(n_chunks)]
        first[0].start()
        for c in range(n_chunks - 1):
            first[c + 1].start()
            land(first[c], 0, c, slot_of(c))

    @pl.when(t + 1 < total)
    def _():
        generation(t + 1)[0].start()

    copy, half, chunk = generation(t)
    land(copy, half, chunk, t % 2)
    return wbuf_ref.at[j % 2]


def _row_parts(tm, n=2):
    n = n if tm % (16 * n) == 0 and tm >= 1024 else 1
    return [slice(r * (tm // n), (r + 1) * (tm // n)) for r in range(n)]


def _proj_q_kernel(h_ref, w_ref, cos_ref, sin_ref, q_ref, wb_ref):
    _cast_weight(w_ref, wb_ref)
    for rows in _row_parts(h_ref.shape[0], 4):
        acc = jnp.dot(h_ref[rows, :], wb_ref[...], preferred_element_type=F32)
        _store_heads(q_ref, rows, _rope(acc, cos_ref.at[rows, :], sin_ref.at[rows, :]) * (HEAD_DIM ** -0.5 * LOG2_E))


def _proj_kv_kernel(h_ref, w_ref, cos_ref, sin_ref, kf_ref, vf_ref, kh_ref, vh_ref, wb_ref, *, kvw):
    _cast_weight(w_ref, wb_ref)
    for rows in _row_parts(h_ref.shape[0], 4):
        acc = jnp.dot(h_ref[rows, :], wb_ref[...], preferred_element_type=F32)
        k = _rope(acc[:, :kvw], cos_ref.at[rows, :], sin_ref.at[rows, :])
        v = acc[:, kvw:]
        kf_ref[rows, :] = k
        vf_ref[rows, :] = v
        _store_heads(kh_ref, rows, k)
        _store_heads(vh_ref, rows, v)


def _proj_act_kernel(h_ref, w_hbm, o_ref, wbuf_ref, stage_ref, sem_ref, *, segments, n_chunks):
    tn = o_ref.shape[1]
    n_blocks = sum(s[1] for s in segments)
    in_seg = lambda blk, s: (blk >= s[0]) & (blk < s[0] + s[1])

    def col_of_block(blk):
        if isinstance(blk, int):
            return next(s[2] + (blk - s[0]) * tn for s in segments if s[0] <= blk < s[0] + s[1])
        return sum(jnp.where(in_seg(blk, s), s[2] + (blk - s[0]) * tn, 0) for s in segments)

    wb_ref = _stream_weights(w_hbm, wbuf_ref, stage_ref, sem_ref, col_of_block=col_of_block, n_blocks=n_blocks, n_chunks=n_chunks)
    acts = {s[3] for s in segments}
    assert acts in ({"none"}, {"silu"}, {"sigmoid"}, {"silu", "sigmoid"})
    j = pl.program_id(0)
    for rows in _row_parts(h_ref.shape[0]):
        acc = jnp.dot(h_ref[rows, :], wb_ref[...], preferred_element_type=F32)
        if acts != {"none"}:
            sig = _sigmoid(acc)
            if acts == {"silu"}:
                acc = acc * sig
            elif acts == {"sigmoid"}:
                acc = sig
            else:
                is_silu = functools.reduce(lambda a, b: a | b, [in_seg(j, s) for s in segments if s[3] == "silu"])
                acc = jnp.where(is_silu, acc * sig, sig)
        o_ref[rows, :] = acc.astype(o_ref.dtype)


def _proj_specs(d, tm, tn, col0):
    assert col0 % tn == 0
    cb0 = col0 // tn
    return [pl.BlockSpec((tm, d), lambda j, i: (i, 0)), pl.BlockSpec((d, tn), lambda j, i: (0, cb0 + j))]


def _proj_vmem(d, tm, tn, outs):
    return _vmem_estimate([((tm, d), BF16), ((d, tn), F32)] + outs, [((tm, tn), F32)] * 3 + [((d, tn), BF16)])


def _proj_q(h, w, cos, sin, aw, tm, tn):
    m, d = h.shape
    nh = aw // HEAD_DIM
    tab = pl.BlockSpec((tm, LANES), lambda j, i: (i, 0))
    return pl.pallas_call(
        _proj_q_kernel,
        grid=(aw // tn, m // tm),
        in_specs=_proj_specs(d, tm, tn, 0) + [tab, tab],
        out_specs=pl.BlockSpec((tn // HEAD_DIM, tm, HEAD_DIM), lambda j, i: (j, i, 0)),
        out_shape=jax.ShapeDtypeStruct((nh, m, HEAD_DIM), BF16),
        scratch_shapes=[pltpu.VMEM((d, tn), BF16)],
        compiler_params=_params(("arbitrary", "arbitrary"), _proj_vmem(d, tm, tn, [((tm, 2 * tn), BF16), ((tm, 2 * LANES), F32)])),
        name="proj_q",
    )(h, w, cos, sin)


def _proj_kv(h, w, cos, sin, col0, kvw, tm):
    m, d = h.shape
    g = kvw // HEAD_DIM
    tn = 2 * kvw
    tab = pl.BlockSpec((tm, LANES), lambda j, i: (i, 0))
    flat = pl.BlockSpec((tm, kvw), lambda j, i: (i, 0))
    heads = pl.BlockSpec((g, tm, HEAD_DIM), lambda j, i: (0, i, 0))
    return pl.pallas_call(
        functools.partial(_proj_kv_kernel, kvw=kvw),
        grid=(1, m // tm),
        in_specs=_proj_specs(d, tm, tn, col0) + [tab, tab],
        out_specs=[flat, flat, heads, heads],
        out_shape=[jax.ShapeDtypeStruct((m, kvw), F32)] * 2 + [jax.ShapeDtypeStruct((g, m, HEAD_DIM), BF16)] * 2,
        scratch_shapes=[pltpu.VMEM((d, tn), BF16)],
        compiler_params=_params(("arbitrary", "arbitrary"), _proj_vmem(d, tm, tn, [((tm, 2 * tn), F32), ((tm, 2 * tn), BF16), ((tm, 2 * LANES), F32)])),
        name="proj_kv",
    )(h, w, cos, sin)


def _proj_act(h, w, runs, tm, tn, name, out_dtype=BF16):
    m, d = h.shape
    n_chunks = m // tm
    segments, b0 = [], 0
    for col0, width, act in runs:
        assert width % tn == 0 and col0 % LANES == 0
        segments.append((b0, width // tn, col0, act))
        b0 += width // tn
    n_blocks, ncols = b0, b0 * tn
    assert d % (8 * n_chunks) == 0
    scratch = [((2, d, tn), BF16), ((2, d // n_chunks, tn), F32)]
    return pl.pallas_call(
        functools.partial(_proj_act_kernel, segments=tuple(segments), n_chunks=n_chunks),
        grid=(n_blocks, n_chunks),
        in_specs=[pl.BlockSpec((tm, d), lambda j, i: (i, 0)), pl.BlockSpec(memory_space=pl.ANY)],
        out_specs=pl.BlockSpec((tm, tn), lambda j, i: (i, j)),
        out_shape=jax.ShapeDtypeStruct((m, ncols), out_dtype),
        scratch_shapes=[pltpu.VMEM(s, t) for s, t in scratch] + [pltpu.SemaphoreType.DMA((2,))],
        compiler_params=_params(("arbitrary", "arbitrary"),
                                _vmem_estimate([((tm, d), BF16), ((tm, tn), out_dtype)], scratch + [((tm // len(_row_parts(tm)), tn), F32)] * 3)),
        name=name,
    )(h, w)


def _attn_weights(q, kw, sink, n_valid):
    st = lax.dot_general(kw, q, (((1,), (1,)), ((), ())), preferred_element_type=F32)
    if n_valid is not None:
        row = lax.broadcasted_iota(jnp.int32, st.shape, 0)
        st = jnp.where(row < n_valid, st, -jnp.inf)
    m = jnp.maximum(jnp.max(st, axis=0, keepdims=True), sink)
    e = jnp.exp2(st - m)
    return e.astype(BF16), jnp.sum(e, axis=0, keepdims=True) + jnp.exp2(sink - m)


def _attn_values(vw, e, denom):
    ot = jnp.dot(vw.astype(F32).T.astype(BF16), e, preferred_element_type=F32) / denom
    return ot.T


def _store_unit(o_ref, za_ref, row0, o, rep):
    for r in range(0, rep, 2):
        pair = jnp.concatenate([o[r * CHUNK:(r + 1) * CHUNK], o[(r + 1) * CHUNK:(r + 2) * CHUNK]], axis=1)
        gate = za_ref[row0:row0 + CHUNK, r * HEAD_DIM:(r + 2) * HEAD_DIM].astype(F32)
        o_ref[row0:row0 + CHUNK, r * HEAD_DIM:(r + 2) * HEAD_DIM] = (pair * gate).astype(o_ref.dtype)


def _attn_kernel(q_ref, k_ref, v_ref, ks_ref, vs_ref, sink_ref, za_ref, wo_ref, o_ref, wob_ref,
                 *, n_units, rep, steps_per_stream, n_prompt_steps, n_cast_steps):
    step = pl.program_id(1)
    win = WINDOW + CHUNK
    sink = sink_ref[0] * LOG2_E

    @pl.when(pl.program_id(0) * (n_prompt_steps + 1) + step < n_cast_steps)
    def _():
        wob_ref[...] = wo_ref[...].astype(wob_ref.dtype)

    def run(windows):
        staged = []
        for c, (kw, vw, n_valid) in enumerate(windows):
            q = q_ref[:, c * CHUNK:(c + 1) * CHUNK, :].reshape(rep * CHUNK, HEAD_DIM)
            staged.append((vw,) + _attn_weights(q, kw, sink, n_valid))
        for c in range(n_units):
            _store_unit(o_ref, za_ref, c * CHUNK, _attn_values(*staged[c]), rep)

    @pl.when(step < n_prompt_steps)
    def _():
        cb = step % steps_per_stream
        windows = []
        for c in range(n_units):
            start = pl.multiple_of(jnp.maximum(cb * n_units + c - WINDOW // CHUNK, 0) * CHUNK, CHUNK)
            n_valid = jnp.where(cb == 0, (c + 1) * CHUNK, win) if c < WINDOW // CHUNK else None
            windows.append((k_ref[0, pl.ds(start, win), :], v_ref[0, pl.ds(start, win), :], n_valid))
        run(windows)

    @pl.when(step == n_prompt_steps)
    def _():
        run([(ks_ref[0, b * win:(b + 1) * win, :], vs_ref[0, b * win:(b + 1) * win, :], None) for b in range(n_units)])


def _attention(q_hm, k_hm, v_hm, ks, vs, sink_rows, za, w_out, batch, seq, dec_batch, aw):
    nh, m, _ = q_hm.shape
    g = k_hm.shape[0]
    rep = nh // g
    win = WINDOW + CHUNK
    n_units = dec_batch
    rows = n_units * CHUNK
    assert seq % rows == 0 and n_units >= WINDOW // CHUNK
    steps_per_stream = seq // rows
    n_prompt_steps = batch * steps_per_stream
    blocks = ([((rep, rows, LANES), BF16)] + [((seq, LANES), BF16)] * 2 + [((dec_batch * win, LANES), BF16)] * 2
              + [((rows, rep * HEAD_DIM), BF16)] * 2)
    temps = [((rep * CHUNK, 2 * LANES), F32)] * (4 * n_units)
    stream = lambda gi, s: (gi, jnp.minimum(s // steps_per_stream, batch - 1), 0)
    dk, dn = w_out.shape
    n_cast_steps = 16
    assert g * (n_prompt_steps + 1) >= n_cast_steps and dk % (16 * n_cast_steps) == 0
    cast_rows = dk // n_cast_steps
    cast_block = pl.BlockSpec((cast_rows, dn), lambda gi, s: (jnp.minimum(gi * (n_prompt_steps + 1) + s, n_cast_steps - 1), 0))
    blocks += [((cast_rows, dn), F32), ((cast_rows, dn), BF16)]
    return pl.pallas_call(
        functools.partial(_attn_kernel, n_units=n_units, rep=rep, steps_per_stream=steps_per_stream, n_prompt_steps=n_prompt_steps,
                          n_cast_steps=n_cast_steps),
        grid=(g, n_prompt_steps + 1),
        in_specs=[
            pl.BlockSpec((rep, rows, HEAD_DIM), lambda gi, s: (gi, s, 0)),
            pl.BlockSpec((1, seq, HEAD_DIM), stream),
            pl.BlockSpec((1, seq, HEAD_DIM), stream),
            pl.BlockSpec((1, dec_batch * win, HEAD_DIM), lambda gi, s: (gi, 0, 0)),
            pl.BlockSpec((1, dec_batch * win, HEAD_DIM), lambda gi, s: (gi, 0, 0)),
            pl.BlockSpec((1, 1, rep * CHUNK), lambda gi, s: (gi, 0, 0)),
            pl.BlockSpec((rows, rep * HEAD_DIM), lambda gi, s: (s, gi)),
            cast_block,
        ],
        out_specs=[pl.BlockSpec((rows, rep * HEAD_DIM), lambda gi, s: (s, gi)), cast_block],
        out_shape=[jax.ShapeDtypeStruct((m, aw), BF16), jax.ShapeDtypeStruct((dk, dn), BF16)],
        compiler_params=_params(("arbitrary",) * 2, _vmem_estimate(blocks, temps)),
        name="attention",
    )(q_hm, k_hm, v_hm, ks, vs, sink_rows, za, w_out)


def _ssm_disc_kernel(lre_ref, lim_ref, ldt_ref, are_ref, aim_ref, dre_ref, dim_ref, fre_ref, fim_ref):
    lr = jnp.minimum(lre_ref[...], LAMBDA_RE_MAX)
    li = lim_ref[...]
    dt = jnp.exp(ldt_ref[...])
    mag = jnp.exp(lr * dt)
    a_re = mag * jnp.cos(li * dt)
    a_im = mag * jnp.sin(li * dt)
    den = lr * lr + li * li
    nr = a_re - 1.0
    fre_ref[...] = (nr * lr + a_im * li) / den
    fim_ref[...] = (a_im * lr - nr * li) / den
    are_ref[...] = a_re
    aim_ref[...] = a_im
    for _ in range(SSM_POW_BITS):
        a_re, a_im = a_re * a_re - a_im * a_im, 2.0 * a_re * a_im
    dre_ref[...] = a_re
    dim_ref[...] = a_im


def _cmul(ar, ai, br, bi):
    return ar * br - ai * bi, ar * bi + ai * br


def _ssm_build_kernel(are_ref, aim_ref, dre_ref, dim_ref, fre_ref, fim_ref, btr_ref, bti_ref, cr_ref, ci_ref, dv_ref,
                      met_ref, ft_ref, a1_ref, a2_ref, a2s_ref, *, groups):
    lc = SSM_CHUNK * SSM_GROUP
    sub = lax.broadcasted_iota(jnp.int32, (SSM_GROUP, lc), 0)
    lane = lax.broadcasted_iota(jnp.int32, (SSM_GROUP, lc), 1)
    for gi in range(groups):
        row = slice(gi, gi + 1)
        a_re, a_im = are_ref[row, :], aim_ref[row, :]
        pw = [(jnp.ones_like(a_re), jnp.zeros_like(a_re))]
        for _ in range(SSM_CHUNK):
            pw.append(_cmul(pw[-1][0], pw[-1][1], a_re, a_im))
        c_re, c_im = cr_ref[gi], ci_ref[gi]
        wt = [_cmul(pr, pi, c_re, c_im) for pr, pi in pw]
        wt_re = jnp.concatenate([w[0] for w in wt[:SSM_CHUNK]], axis=0)
        wt_im = jnp.concatenate([w[1] for w in wt[:SSM_CHUNK]], axis=0)
        et_re = jnp.concatenate([w[0] for w in wt[1:]], axis=0)
        et_im = jnp.concatenate([w[1] for w in wt[1:]], axis=0)
        bb_re, bb_im = _cmul(fre_ref[row, :], fim_ref[row, :], btr_ref[gi], bti_ref[gi])
        r0 = lax.dot_general(jnp.concatenate([bb_re, bb_im], axis=1), jnp.concatenate([wt_re, -wt_im], axis=1),
                             (((1,), (1,)), ((), ())), preferred_element_type=F32, precision=lax.Precision.HIGHEST)
        r0 = r0 + jnp.where(sub == lane, dv_ref[gi], 0.0)
        rows = [r0] + [jnp.where(lane >= s * SSM_GROUP, pltpu.roll(r0, s * SSM_GROUP, 1), 0.0) for s in range(1, SSM_CHUNK)]
        mt = jnp.concatenate(rows, axis=0).T
        met_ref[gi] = jnp.concatenate([mt, et_re, -et_im], axis=1).astype(met_ref.dtype)
        fb = [_cmul(pw[SSM_CHUNK - 1 - s][0], pw[SSM_CHUNK - 1 - s][1], bb_re, bb_im) for s in range(SSM_CHUNK)]
        f_all = jnp.concatenate([jnp.concatenate([x[0] for x in fb], axis=0), jnp.concatenate([x[1] for x in fb], axis=0)], axis=1)
        ft_ref[gi] = f_all.T.astype(ft_ref.dtype)
        d_re, d_im = dre_ref[row, :], dim_ref[row, :]
        a1_ref[gi] = jnp.concatenate([d_re, d_re], axis=1)
        a2_ref[gi] = jnp.concatenate([-d_im, d_im], axis=1)
        a2s_ref[gi] = jnp.concatenate([d_im, -d_im], axis=1)


def _ssm_params(lambda_re, lambda_im, log_dt, b_re, b_im, c_re, c_im, d_skip):
    ng, p = lambda_re.shape
    lc = SSM_CHUNK * SSM_GROUP
    full = pl.BlockSpec((ng, p), lambda: (0, 0))
    disc = pl.pallas_call(
        _ssm_disc_kernel,
        in_specs=[full, full, pl.BlockSpec((ng, 1), lambda: (0, 0))],
        out_specs=[full] * 6,
        out_shape=[jax.ShapeDtypeStruct((ng, p), F32)] * 6,
        name="ssm_disc",
    )(lambda_re, lambda_im, log_dt.reshape(ng, 1))
    gb = _pick(ng, (8,))
    bt_re = jnp.swapaxes(b_re, 1, 2)
    bt_im = jnp.swapaxes(b_im, 1, 2)
    dvec = jnp.pad(d_skip, ((0, 0), (0, lc - SSM_GROUP))).reshape(ng, 1, lc)
    rows = pl.BlockSpec((gb, p), lambda i: (i, 0))
    mats = pl.BlockSpec((gb, SSM_GROUP, p), lambda i: (i, 0, 0))
    dec = pl.BlockSpec((gb, 1, 2 * p), lambda i: (i, 0, 0))
    blocks = ([((gb, LANES), F32)] * 6 + [((gb, SSM_GROUP, LANES), F32)] * 4 + [((gb, 8, lc), F32)]
              + [((gb, lc, lc + 2 * p), BF16), ((gb, 2 * p, lc), BF16)] + [((gb, 8, LANES), F32)] * 3)
    met, ft, a1, a2, a2s = pl.pallas_call(
        functools.partial(_ssm_build_kernel, groups=gb),
        grid=(ng // gb,),
        in_specs=[rows] * 6 + [mats] * 4 + [pl.BlockSpec((gb, 1, lc), lambda i: (i, 0, 0))],
        out_specs=[pl.BlockSpec((gb, lc, lc + 2 * p), lambda i: (i, 0, 0)), pl.BlockSpec((gb, 2 * p, lc), lambda i: (i, 0, 0)), dec, dec, dec],
        out_shape=[jax.ShapeDtypeStruct((ng, lc, lc + 2 * p), BF16), jax.ShapeDtypeStruct((ng, 2 * p, lc), BF16)]
        + [jax.ShapeDtypeStruct((ng, 1, 2 * p), F32)] * 3,
        compiler_params=_params(("arbitrary",), _vmem_estimate(blocks, [((lc, lc + 2 * p), F32)] * 8)),
        name="ssm_build",
    )(*disc, bt_re, bt_im, c_re, c_im, dvec)
    flat = lambda a: a.reshape(1, ng * 2 * p)
    return met, ft, flat(a1), flat(a2), flat(a2s)


def _chunk_blocks(nk):
    return [(k0, min(LANES, nk - k0)) for k0 in range(0, nk, LANES)]


def _pad_rows(x):
    n = x.shape[0]
    return x if n == LANES else jnp.concatenate([x, jnp.zeros((LANES - n, x.shape[1]), x.dtype)], axis=0)


def _ssm_inputs_phase(u_ref, ft_ref, rhs_ref, s_ref, *, nk, p):
    groups = LANES // SSM_GROUP
    blocks = _chunk_blocks(nk)
    xt = [[_pad_rows(u_ref[pl.ds(k0 * SSM_CHUNK + s, n, stride=SSM_CHUNK), :].astype(BF16)).T for k0, n in blocks]
          for s in range(SSM_CHUNK)]
    for gi in range(groups):
        rhs = jnp.concatenate([jnp.concatenate([xt[s][kb][gi * SSM_GROUP:(gi + 1) * SSM_GROUP, :] for kb in range(len(blocks))], axis=1)
                               for s in range(SSM_CHUNK)], axis=0)
        rhs_ref[gi] = rhs
        st = jnp.dot(ft_ref[gi], rhs, preferred_element_type=F32)
        for kb, (k0, n) in enumerate(blocks):
            s_ref[k0:k0 + n, gi * 2 * p:(gi + 1) * 2 * p] = st[:, kb * LANES:(kb + 1) * LANES].T[:n, :]


def _ssm_outputs_phase(rhs_ref, h_ref, met_ref, y_ref, *, nk, p):
    groups = LANES // SSM_GROUP
    blocks = _chunk_blocks(nk)
    zt = [[None] * groups for _ in range(SSM_CHUNK)]
    for gi in range(groups):
        ht = jnp.concatenate([_pad_rows(h_ref[k0:k0 + n, gi * 2 * p:(gi + 1) * 2 * p]).T for k0, n in blocks], axis=1)
        rhs = jnp.concatenate([rhs_ref[gi], ht.astype(BF16)], axis=0)
        yt = jnp.dot(met_ref[gi], rhs, preferred_element_type=F32)
        for t in range(SSM_CHUNK):
            zt[t][gi] = yt[t * SSM_GROUP:(t + 1) * SSM_GROUP, :]
    for t in range(SSM_CHUNK):
        z = jnp.concatenate(zt[t], axis=0)
        for kb, (k0, n) in enumerate(blocks):
            y_ref[pl.ds(k0 * SSM_CHUNK + t, n, stride=SSM_CHUNK), :] = z[:, kb * LANES:(kb + 1) * LANES].T[:n, :]


def _ssm_scan_phase(s_ref, h0_ref, a1_ref, a2_ref, a2s_ref, h_ref, fin_ref, ss_ref, *, batch, kp, dec_batch, ks, p):
    def swap_halves(x):
        lane = lax.broadcasted_iota(jnp.int32, x.shape, 1)
        return jnp.where(lane % (2 * p) < p, pltpu.roll(x, x.shape[1] - p, 1), pltpu.roll(x, p, 1))

    ss_ref[...] = swap_halves(s_ref[...])
    a1, a2, a2s = a1_ref[...], a2_ref[...], a2s_ref[...]
    zero = jnp.zeros_like(a1)

    def step(k, h, hs):
        h_ref[pl.ds(k, 1), :] = h
        s = s_ref[pl.ds(k, 1), :]
        ss = ss_ref[pl.ds(k, 1), :]
        return a1 * h + a2 * hs + s, a1 * hs + a2s * h + ss

    def body(k, carry):
        out = []
        for b in range(batch):
            out.extend(step(b * kp + k, carry[2 * b], carry[2 * b + 1]))
        return tuple(out)

    fin = lax.fori_loop(0, kp, body, (zero,) * (2 * batch))
    for b in range(batch):
        fin_ref[b:b + 1, :] = fin[2 * b]
    h0s_all = swap_halves(h0_ref[...])
    for b in range(dec_batch):
        h, hs = h0_ref[b:b + 1, :], h0s_all[b:b + 1, :]
        for k in range(ks):
            h, hs = step(batch * kp + b * ks + k, h, hs)
        fin_ref[batch + b:batch + b + 1, :] = h


def _ssm_kernel(u_ref, ft_ref, met_ref, h0_ref, a1_ref, a2_ref, a2s_ref, y_ref, fin_ref, rhs_ref, s_ref, ss_ref, h_ref,
                *, batch, kp, dec_batch, ks, p):
    nk = batch * kp + dec_batch * ks
    _ssm_inputs_phase(u_ref, ft_ref, rhs_ref, s_ref, nk=nk, p=p)
    _ssm_scan_phase(s_ref, h0_ref, a1_ref, a2_ref, a2s_ref, h_ref, fin_ref, ss_ref, batch=batch, kp=kp, dec_batch=dec_batch, ks=ks, p=p)
    _ssm_outputs_phase(rhs_ref, h_ref, met_ref, y_ref, nk=nk, p=p)


def _ssm(u, met, ft, a1, a2, a2s, h0, batch, kp, dec_batch, ks):
    m, sw = u.shape
    ng, p2, lc = ft.shape
    p = p2 // 2
    nk = m // SSM_CHUNK
    assert nk == batch * kp + dec_batch * ks and (nk % LANES) % 8 == 0
    nk_lanes = -(-nk // LANES) * LANES
    groups = LANES // SSM_GROUP
    lb = groups * p2
    nseq = batch + dec_batch
    u_spec = pl.BlockSpec((m, LANES), lambda i: (0, i))
    rowb = pl.BlockSpec((1, lb), lambda i: (0, i))
    blocks = [((m, LANES), F32)] * 2 + [((groups, p2, lc), BF16), ((groups, lc, lc + p2), BF16), ((nseq, lb), F32)]
    scratch = [((groups, lc, nk_lanes), BF16)] + [((nk, lb), F32)] * 3
    return pl.pallas_call(
        functools.partial(_ssm_kernel, batch=batch, kp=kp, dec_batch=dec_batch, ks=ks, p=p),
        grid=(ng // groups,),
        in_specs=[u_spec, pl.BlockSpec((groups, p2, lc), lambda i: (i, 0, 0)), pl.BlockSpec((groups, lc, lc + p2), lambda i: (i, 0, 0)),
                  pl.BlockSpec((dec_batch, lb), lambda i: (0, i)), rowb, rowb, rowb],
        out_specs=[u_spec, pl.BlockSpec((nseq, lb), lambda i: (0, i))],
        out_shape=[jax.ShapeDtypeStruct((m, sw), F32), jax.ShapeDtypeStruct((nseq, ng * p2), F32)],
        scratch_shapes=[pltpu.VMEM(s, d) for s, d in scratch],
        compiler_params=_params(("arbitrary",), _vmem_estimate(blocks, scratch + [((lc + p2, nk_lanes), BF16), ((lc, nk_lanes), F32)])),
        name="ssm",
    )(u, ft, met, h0, a1, a2, a2s)


def _weight_stream_scratch(k, tn, n_chunks):
    assert k % (8 * n_chunks) == 0
    shapes = [((2, k, tn), BF16), ((2, k // n_chunks, tn), F32)]
    return shapes, [pltpu.VMEM(s, t) for s, t in shapes] + [pltpu.SemaphoreType.DMA((2,))]


def _glu_kernel(y_ref, w_hbm, ba_ref, bg_ref, zs_ref, o_ref, wa_buf, wa_stage, wa_sem, wg_buf, wg_stage, wg_sem, *, n_blocks, n_chunks):
    tn = o_ref.shape[1]
    stream = functools.partial(_stream_weights, w_hbm, n_blocks=n_blocks, n_chunks=n_chunks)
    wa_ref = stream(wa_buf, wa_stage, wa_sem, col_of_block=lambda blk: blk * tn)
    wg_ref = stream(wg_buf, wg_stage, wg_sem, col_of_block=lambda blk: (n_blocks + blk) * tn)
    y = y_ref[...].astype(BF16)
    a = jnp.dot(y, wa_ref[...], preferred_element_type=F32) + ba_ref[...]
    g = jnp.dot(y, wg_ref[...], preferred_element_type=F32) + bg_ref[...]
    o_ref[...] = (a * _sigmoid(g) * zs_ref[...].astype(F32)).astype(o_ref.dtype)


def _glu(y, w_glu, b_glu, acts, zs_col0, tm, tn):
    m, sw = y.shape
    nb, n_chunks = sw // tn, m // tm
    zb0 = zs_col0 // tn
    shapes, scratch = _weight_stream_scratch(sw, tn, n_chunks)
    blocks = [((tm, sw), F32), ((tm, tn), BF16), ((tm, tn), BF16)]
    return pl.pallas_call(
        functools.partial(_glu_kernel, n_blocks=nb, n_chunks=n_chunks),
        grid=(nb, n_chunks),
        in_specs=[
            pl.BlockSpec((tm, sw), lambda j, i: (i, 0)),
            pl.BlockSpec(memory_space=pl.ANY),
            pl.BlockSpec((1, tn), lambda j, i: (0, j)),
            pl.BlockSpec((1, tn), lambda j, i: (0, nb + j)),
            pl.BlockSpec((tm, tn), lambda j, i: (i, zb0 + j)),
        ],
        out_specs=pl.BlockSpec((tm, tn), lambda j, i: (i, j)),
        out_shape=jax.ShapeDtypeStruct((m, sw), BF16),
        scratch_shapes=scratch * 2,
        compiler_params=_params(("arbitrary", "arbitrary"), _vmem_estimate(blocks, shapes * 2 + [((tm, tn), F32)] * 4)),
        name="glu",
    )(y, w_glu, b_glu, b_glu, acts)


def _merge_kernel(xa_ref, xs_ref, wpa_hbm, wps_hbm, ga_ref, gs_ref, o_ref, wa_buf, wa_stage, wa_sem, ws_buf, ws_stage, ws_sem,
                  *, n_blocks, n_chunks):
    tn = o_ref.shape[1]
    col = lambda blk: blk * tn
    wpa_ref = _stream_weights(wpa_hbm, wa_buf, wa_stage, wa_sem, col_of_block=col, n_blocks=n_blocks, n_chunks=n_chunks)
    wps_ref = _stream_weights(wps_hbm, ws_buf, ws_stage, ws_sem, col_of_block=col, n_blocks=n_blocks, n_chunks=n_chunks)
    br_a = jnp.dot(xa_ref[...], wpa_ref[...], preferred_element_type=F32)
    br_s = jnp.dot(xs_ref[...], wps_ref[...], preferred_element_type=F32)
    o_ref[...] = (ga_ref[...].astype(F32) * br_a + gs_ref[...].astype(F32) * br_s).astype(o_ref.dtype)


def _merge(xa, xs, w_pa, w_ps, acts, gate_col0, tm, tn):
    m, aw = xa.shape
    sw = xs.shape[1]
    d = w_pa.shape[1]
    nb, n_chunks = d // tn, m // tm
    gb0 = gate_col0 // tn
    shapes_a, scratch_a = _weight_stream_scratch(aw, tn, n_chunks)
    shapes_s, scratch_s = _weight_stream_scratch(sw, tn, n_chunks)
    blocks = [((tm, aw), BF16), ((tm, sw), BF16)] + [((tm, tn), BF16)] * 3
    return pl.pallas_call(
        functools.partial(_merge_kernel, n_blocks=nb, n_chunks=n_chunks),
        grid=(nb, n_chunks),
        in_specs=[
            pl.BlockSpec((tm, aw), lambda j, i: (i, 0)),
            pl.BlockSpec((tm, sw), lambda j, i: (i, 0)),
            pl.BlockSpec(memory_space=pl.ANY),
            pl.BlockSpec(memory_space=pl.ANY),
            pl.BlockSpec((tm, tn), lambda j, i: (i, gb0 + j)),
            pl.BlockSpec((tm, tn), lambda j, i: (i, gb0 + nb + j)),
        ],
        out_specs=pl.BlockSpec((tm, tn), lambda j, i: (i, j)),
        out_shape=jax.ShapeDtypeStruct((m, d), BF16),
        scratch_shapes=scratch_a + scratch_s,
        compiler_params=_params(("arbitrary", "arbitrary"), _vmem_estimate(blocks, shapes_a + shapes_s + [((tm, tn), F32)] * 4)),
        name="merge",
    )(xa, xs, w_pa, w_ps, acts, acts)


def _out_kernel(mg_ref, w_ref, x_ref, g_ref, o_ref, ssq_ref, *, n_col_blocks, tn):
    j = pl.program_id(1)
    blk = x_ref[...] + jnp.dot(mg_ref[...], w_ref[...], preferred_element_type=F32)
    o_ref[:, pl.ds(pl.multiple_of(j * tn, tn), tn)] = blk
    part = jnp.sum(blk * blk, axis=-1, keepdims=True)

    @pl.when(j == 0)
    def _():
        ssq_ref[...] = part

    @pl.when(j > 0)
    def _():
        ssq_ref[...] += part

    @pl.when(j == n_col_blocks - 1)
    def _():
        scale = lax.rsqrt(ssq_ref[...] / (n_col_blocks * tn) + NORM_EPS)
        o_ref[...] = o_ref[...] * scale * g_ref[...]


def _out(merged, row_block0, w_out, x, final_g, tm, tn):
    mx, d = x.shape
    nb = d // tn
    blocks = [((tm, d), BF16), ((d, tn), BF16), ((tm, tn), F32), ((tm, d), F32)]
    return pl.pallas_call(
        functools.partial(_out_kernel, n_col_blocks=nb, tn=tn),
        grid=(mx // tm, nb),
        in_specs=[
            pl.BlockSpec((tm, d), lambda i, j: (row_block0 + i, 0)),
            pl.BlockSpec((d, tn), lambda i, j: (0, j)),
            pl.BlockSpec((tm, tn), lambda i, j: (i, j)),
            pl.BlockSpec((1, d), lambda i, j: (0, 0)),
        ],
        out_specs=pl.BlockSpec((tm, d), lambda i, j: (i, 0)),
        out_shape=jax.ShapeDtypeStruct((mx, d), F32),
        scratch_shapes=[pltpu.VMEM((tm, 1), F32)],
        compiler_params=_params(("arbitrary", "arbitrary"), _vmem_estimate(blocks, [((tm, tn), F32)] * 3 + [((tm, LANES), F32)])),
        name="out_norm",
    )(merged, w_out, x, final_g.reshape(1, d))


def _rope_tables(positions):
    half = HEAD_DIM // 2
    inv_freq = ROPE_THETA ** (-jnp.arange(half, dtype=F32) / half)
    ang = positions.astype(F32)[:, None] * inv_freq[None, :]
    cos, sin = jnp.cos(ang), jnp.sin(ang)
    reps = LANES // HEAD_DIM
    return jnp.tile(jnp.concatenate([cos, cos], axis=1), (1, reps)), jnp.tile(jnp.concatenate([-sin, sin], axis=1), (1, reps))


def kernel(x_prompt, x_sample, cache_k, cache_v, state_ssm_re, state_ssm_im, norm_g, w_in, sink, lambda_re, lambda_im,
           log_dt, b_re, b_im, c_re, c_im, d_skip, w_glu, b_glu, w_pa, w_ps, w_out, final_g):
    depth = norm_g.shape[0]
    assert depth == 1, "one trunk layer"
    batch, seq, d = x_prompt.shape
    dec_batch, dec_seq, _ = x_sample.shape
    aw = w_pa.shape[1]
    sw = w_ps.shape[1]
    nh = aw // HEAD_DIM
    g = max(1, nh // GQA_GROUPING)
    rep = nh // g
    kvw = g * HEAD_DIM
    ng, p = lambda_re.shape[1:]
    assert dec_seq == CHUNK and cache_k.shape[2] == WINDOW and seq % (2 * CHUNK) == 0
    assert kvw % LANES == 0 and rep % 2 == 0 and sw == ng * SSM_GROUP and (1 << SSM_POW_BITS) == SSM_CHUNK
    assert 2 * p == LANES and sw % LANES == 0
    mp, ms = batch * seq, dec_batch * dec_seq
    m = mp + ms
    widths = (aw, 2 * kvw, aw, sw, sw, 2 * d)
    assert sum(widths) == w_in.shape[2]
    c_q, c_kv, c_za, c_u, c_zs, c_gate = (sum(widths[:n]) for n in range(len(widths)))
    tm = _pick(math.gcd(mp, ms), (512, 256, 128))
    tmp = max(t for t in range(16, 1153, 16) if m % t == 0)
    tn = _pick(math.gcd(c_kv, c_za, c_u, c_zs, c_gate, 2 * d), (512, 256, 128))

    xp = x_prompt.reshape(mp, d)
    xs = x_sample.reshape(ms, d)
    w_in2 = w_in.reshape(d, w_in.shape[2])
    positions = jnp.concatenate([jnp.tile(jnp.arange(seq, dtype=jnp.int32), batch),
                                 jnp.tile(PAST_LEN + jnp.arange(dec_seq, dtype=jnp.int32), dec_batch)])
    cos, sin = _rope_tables(positions)

    h = _rmsnorm(xp, xs, norm_g[0], tm)
    q_hm = _proj_q(h, w_in2, cos, sin, aw, tmp, tn)
    k_f, v_f, k_hm, v_hm = _proj_kv(h, w_in2, cos, sin, c_kv, kvw, tmp)
    tna = _pick(math.gcd(aw, sw), (1024, 512, 256))
    acts = _proj_act(h, w_in2, [(c_za, aw, "silu"), (c_zs, sw, "silu"), (c_gate, 2 * d, "sigmoid")], tmp, tna, "proj_acts")
    u = _proj_act(h, w_in2, [(c_u, sw, "none")], tmp, tna, "proj_u", out_dtype=F32)

    to_heads = lambda c: jnp.transpose(c[0], (2, 0, 1, 3)).astype(BF16)
    new_rows = lambda a: a[:, mp:].reshape(g, dec_batch, dec_seq, HEAD_DIM)
    ks = jnp.concatenate([to_heads(cache_k), new_rows(k_hm)], axis=2).reshape(g, dec_batch * (WINDOW + CHUNK), HEAD_DIM)
    vs = jnp.concatenate([to_heads(cache_v), new_rows(v_hm)], axis=2).reshape(g, dec_batch * (WINDOW + CHUNK), HEAD_DIM)
    sink_rows = jnp.repeat(sink[0].reshape(g, rep), CHUNK, axis=1).reshape(g, 1, rep * CHUNK)
    xa, w_out_b = _attention(q_hm, k_hm, v_hm, ks, vs, sink_rows, acts, w_out.reshape(d, d), batch, seq, dec_batch, aw)

    met, ft, a1, a2, a2s = _ssm_params(lambda_re[0], lambda_im[0], log_dt[0], b_re[0], b_im[0], c_re[0], c_im[0], d_skip[0])
    h0 = jnp.concatenate([state_ssm_re[0], state_ssm_im[0]], axis=-1).reshape(dec_batch, ng * 2 * p)
    y, fin = _ssm(u, met, ft, a1, a2, a2s, h0, batch, seq // SSM_CHUNK, dec_batch, dec_seq // SSM_CHUNK)

    tmm = tmp
    x_ssm = _glu(y, w_glu.reshape(sw, 2 * sw), b_glu.reshape(1, 2 * sw), acts, aw, tmm, tna)
    merged = _merge(xa, x_ssm, w_pa.reshape(aw, d), w_ps.reshape(sw, d), acts, aw + sw, tmm, tna)
    tno = _pick(d, (1024, 512, 256))
    y_prompt = _out(merged, 0, w_out_b, xp, final_g, tm, tno).reshape(batch, seq, d)
    y_sample = _out(merged, mp // tm, w_out_b, xs, final_g, tm, tno).reshape(dec_batch, dec_seq, d)

    keep = min(WINDOW, seq)
    last_rows = lambda a: jnp.stack([a[(b + 1) * seq - keep:(b + 1) * seq] for b in range(batch)]).reshape(1, batch, keep, g, HEAD_DIM)
    dec_rows = lambda a: a[mp:].reshape(dec_batch, dec_seq, g, HEAD_DIM)[None]
    fin = fin.reshape(batch + dec_batch, ng, 2, p)
    return (y_prompt, y_sample, last_rows(k_f), last_rows(v_f), fin[:batch, :, 0][None], fin[:batch, :, 1][None],
            dec_rows(k_f), dec_rows(v_f), fin[batch:, :, 0][None], fin[batch:, :, 1][None])
```

```python
import functools
import math

import jax
import jax.numpy as jnp
from jax import lax
from jax.experimental import pallas as pl
from jax.experimental.pallas import tpu as pltpu

CHUNK = 64
WINDOW = 128
HEAD_DIM = 64
GQA_GROUPING = 8
SSM_GROUP = 16
PAST_LEN = 1024
ROPE_THETA = 10000.0
NORM_EPS = 1e-5
LAMBDA_RE_MAX = -1e-4
LOG2_E = math.log2(math.e)

SSM_CHUNK = 16
SSM_POW_BITS = 4
LANES = 128
V7X_VMEM_BYTES = 64 * 1024 * 1024
BF16 = jnp.bfloat16
F32 = jnp.float32


def _pick(n, prefs):
    for p in prefs:
        if n % p == 0:
            return p
    raise ValueError(f"no tile in {prefs} divides {n}")


def _params(sem, vmem_bytes):
    limit = min(int(vmem_bytes), V7X_VMEM_BYTES - 4 * 1024 * 1024)
    return pltpu.CompilerParams(dimension_semantics=sem, vmem_limit_bytes=limit)


def _sigmoid(x):
    return 0.5 * jnp.tanh(0.5 * x) + 0.5


def _nbytes(shape, dtype):
    return math.prod(shape) * jnp.dtype(dtype).itemsize


def _vmem_estimate(blocks, temps=()):
    return 2 * (2 * sum(_nbytes(s, d) for s, d in blocks) + sum(_nbytes(s, d) for s, d in temps))


def _norm_kv_kernel(xp_ref, xs_ref, g_ref, w_ref, cos_ref, sin_ref, h_ref, kf_ref, vf_ref, kh_ref, vh_ref, wb_ref, *, n_prompt_blocks, kvw):
    i = pl.program_id(0)

    @pl.when(i == 0)
    def _():
        wb_ref[...] = w_ref[...].astype(wb_ref.dtype)

    tm = h_ref.shape[0]
    n_parts = 2 if tm % 32 == 0 else 1
    for r in range(n_parts):
        rows = slice(r * (tm // n_parts), (r + 1) * (tm // n_parts))
        x = jnp.where(i < n_prompt_blocks, xp_ref[rows, :], xs_ref[rows, :])
        y = x * lax.rsqrt(jnp.mean(x * x, axis=-1, keepdims=True) + NORM_EPS)
        h = (y * g_ref[...]).astype(h_ref.dtype)
        h_ref[rows, :] = h
        acc = jnp.dot(h, wb_ref[...], preferred_element_type=F32)
        k = _rope(acc[:, :kvw], cos_ref.at[rows, :], sin_ref.at[rows, :])
        v = acc[:, kvw:]
        kf_ref[rows, :] = k
        vf_ref[rows, :] = v
        _store_heads(kh_ref, rows, k)
        _store_heads(vh_ref, rows, v)


def _norm_kv(xp, xs, norm_g, w, cos, sin, col0, kvw, tm):
    mp, d = xp.shape
    ms = xs.shape[0]
    npb, nsb = mp // tm, ms // tm
    m = mp + ms
    g = kvw // HEAD_DIM
    tn = 2 * kvw
    assert col0 % tn == 0
    once = pl.Buffered(1)
    tab = pl.BlockSpec((tm, LANES), lambda i: (i, 0))
    flat = pl.BlockSpec((tm, kvw), lambda i: (i, 0))
    heads = pl.BlockSpec((g, tm, HEAD_DIM), lambda i: (0, i, 0))
    xs_spec = (pl.BlockSpec((tm, d), lambda i: (0, 0), pipeline_mode=once) if nsb == 1
               else pl.BlockSpec((tm, d), lambda i: (jnp.maximum(i - npb, 0), 0)))
    blocks = [((tm, d), F32), ((tm, d), BF16), ((tm, 2 * tn), F32), ((tm, 2 * tn), BF16), ((tm, 2 * LANES), F32)]
    temps = [((tm, d), F32)] * 2 + [((d, tn), F32), ((d, tn), BF16)]
    return pl.pallas_call(
        functools.partial(_norm_kv_kernel, n_prompt_blocks=npb, kvw=kvw),
        grid=(npb + nsb,),
        in_specs=[
            pl.BlockSpec((tm, d), lambda i: (jnp.minimum(i, npb - 1), 0)),
            xs_spec,
            pl.BlockSpec((1, d), lambda i: (0, 0)),
            pl.BlockSpec((d, tn), lambda i: (0, col0 // tn), pipeline_mode=once),
            tab, tab,
        ],
        out_specs=[pl.BlockSpec((tm, d), lambda i: (i, 0)), flat, flat, heads, heads],
        out_shape=[jax.ShapeDtypeStruct((m, d), BF16)] + [jax.ShapeDtypeStruct((m, kvw), F32)] * 2
        + [jax.ShapeDtypeStruct((g, m, HEAD_DIM), BF16)] * 2,
        scratch_shapes=[pltpu.VMEM((d, tn), BF16)],
        compiler_params=_params(("arbitrary",), _vmem_estimate(blocks, temps)),
        name="norm_kv",
    )(xp, xs, norm_g.reshape(1, d), w, cos, sin)


def _rope(acc, cos_ref, sin_ref):
    tm, tn = acc.shape
    reps = tn // LANES
    cos = jnp.tile(cos_ref[...], (1, reps))
    sin = jnp.tile(sin_ref[...], (1, reps))
    lane = lax.broadcasted_iota(jnp.int32, (tm, tn), 1)
    low = (lane % HEAD_DIM) < (HEAD_DIM // 2)
    partner = jnp.where(low, pltpu.roll(acc, tn - HEAD_DIM // 2, 1), pltpu.roll(acc, HEAD_DIM // 2, 1))
    return acc * cos + partner * sin


def _store_heads(o_ref, rows, val):
    for h in range(val.shape[1] // HEAD_DIM):
        o_ref[h, rows, :] = val[:, h * HEAD_DIM:(h + 1) * HEAD_DIM].astype(o_ref.dtype)


def _cast_weight(w_ref, wb_ref):
    @pl.when(pl.program_id(1) == 0)
    def _():
        wb_ref[...] = w_ref[...].astype(wb_ref.dtype)


def _stream_weights(w_hbm, wbuf_ref, stage_ref, sem_ref, *, col_of_block, n_blocks, n_chunks):
    d, tn = wbuf_ref.shape[1:]
    ck = d // n_chunks
    total = n_blocks * n_chunks
    j = pl.program_id(0)
    t = j * n_chunks + pl.program_id(1)

    def aligned(x, a):
        return x if isinstance(x, int) else pl.multiple_of(x, a)

    def chunk_copy(blk, chunk, slot):
        src = w_hbm.at[pl.ds(aligned(chunk * ck, ck), ck), pl.ds(aligned(col_of_block(blk), LANES), tn)]
        return pltpu.make_async_copy(src, stage_ref.at[slot], sem_ref.at[slot])

    def generation(s):
        s = jnp.asarray(s, jnp.int32)
        k = s + n_chunks - 1
        past = (k > total - 1).astype(jnp.int32)
        k = jnp.minimum(k, total - 1)
        blk, chunk = k // n_chunks, k % n_chunks
        return chunk_copy(blk, chunk, s % 2), (blk + past) % 2, chunk

    def land(copy, half, chunk, slot):
        copy.wait()
        wbuf_ref[half, pl.ds(aligned(chunk * ck, ck), ck), :] = stage_ref[slot].astype(wbuf_ref.dtype)

    @pl.when(t == 0)
    def _():
        slot_of = lambda c: (c + n_chunks - 1) % 2
        first = [chunk_copy(0, c, slot_of(c)) for c in range(n_chunks)]
        first[0].start()
        for c in range(n_chunks - 1):
            first[c + 1].start()
            land(first[c], 0, c, slot_of(c))

    @pl.when(t + 1 < total)
    def _():
        generation(t + 1)[0].start()

    copy, half, chunk = generation(t)
    land(copy, half, chunk, t % 2)
    return wbuf_ref.at[j % 2]


def _row_parts(tm, n=2):
    n = n if tm % (16 * n) == 0 and tm >= 1024 else 1
    return [slice(r * (tm // n), (r + 1) * (tm // n)) for r in range(n)]


def _proj_q_kernel(h_ref, w_ref, cos_ref, sin_ref, q_ref, wb_ref):
    _cast_weight(w_ref, wb_ref)
    for rows in _row_parts(h_ref.shape[0], 4):
        acc = jnp.dot(h_ref[rows, :], wb_ref[...], preferred_element_type=F32)
        _store_heads(q_ref, rows, _rope(acc, cos_ref.at[rows, :], sin_ref.at[rows, :]) * (HEAD_DIM ** -0.5 * LOG2_E))


def _proj_act_kernel(h_ref, w_hbm, o_ref, wbuf_ref, stage_ref, sem_ref, *, segments, n_chunks):
    tn = o_ref.shape[1]
    n_blocks = sum(s[1] for s in segments)
    in_seg = lambda blk, s: (blk >= s[0]) & (blk < s[0] + s[1])

    def col_of_block(blk):
        if isinstance(blk, int):
            return next(s[2] + (blk - s[0]) * tn for s in segments if s[0] <= blk < s[0] + s[1])
        return sum(jnp.where(in_seg(blk, s), s[2] + (blk - s[0]) * tn, 0) for s in segments)

    wb_ref = _stream_weights(w_hbm, wbuf_ref, stage_ref, sem_ref, col_of_block=col_of_block, n_blocks=n_blocks, n_chunks=n_chunks)
    acts = {s[3] for s in segments}
    assert acts in ({"none"}, {"silu"}, {"sigmoid"}, {"silu", "sigmoid"})
    j = pl.program_id(0)
    for rows in _row_parts(h_ref.shape[0]):
        acc = jnp.dot(h_ref[rows, :], wb_ref[...], preferred_element_type=F32)
        if acts != {"none"}:
            sig = _sigmoid(acc)
            if acts == {"silu"}:
                acc = acc * sig
            elif acts == {"sigmoid"}:
                acc = sig
            else:
                is_silu = functools.reduce(lambda a, b: a | b, [in_seg(j, s) for s in segments if s[3] == "silu"])
                acc = jnp.where(is_silu, acc * sig, sig)
        o_ref[rows, :] = acc.astype(o_ref.dtype)


def _proj_specs(d, tm, tn, col0):
    assert col0 % tn == 0
    cb0 = col0 // tn
    return [pl.BlockSpec((tm, d), lambda j, i: (i, 0)), pl.BlockSpec((d, tn), lambda j, i: (0, cb0 + j))]


def _proj_vmem(d, tm, tn, outs):
    return _vmem_estimate([((tm, d), BF16), ((d, tn), F32)] + outs, [((tm, tn), F32)] * 3 + [((d, tn), BF16)])


def _proj_q(h, w, cos, sin, aw, tm, tn):
    m, d = h.shape
    nh = aw // HEAD_DIM
    tab = pl.BlockSpec((tm, LANES), lambda j, i: (i, 0))
    return pl.pallas_call(
        _proj_q_kernel,
        grid=(aw // tn, m // tm),
        in_specs=_proj_specs(d, tm, tn, 0) + [tab, tab],
        out_specs=pl.BlockSpec((tn // HEAD_DIM, tm, HEAD_DIM), lambda j, i: (j, i, 0)),
        out_shape=jax.ShapeDtypeStruct((nh, m, HEAD_DIM), BF16),
        scratch_shapes=[pltpu.VMEM((d, tn), BF16)],
        compiler_params=_params(("arbitrary", "arbitrary"), _proj_vmem(d, tm, tn, [((tm, 2 * tn), BF16), ((tm, 2 * LANES), F32)])),
        name="proj_q",
    )(h, w, cos, sin)


def _proj_act(h, w, runs, tm, tn, name, out_dtype=BF16):
    m, d = h.shape
    n_chunks = m // tm
    segments, b0 = [], 0
    for col0, width, act in runs:
        assert width % tn == 0 and col0 % LANES == 0
        segments.append((b0, width // tn, col0, act))
        b0 += width // tn
    n_blocks, ncols = b0, b0 * tn
    assert d % (8 * n_chunks) == 0
    scratch = [((2, d, tn), BF16), ((2, d // n_chunks, tn), F32)]
    return pl.pallas_call(
        functools.partial(_proj_act_kernel, segments=tuple(segments), n_chunks=n_chunks),
        grid=(n_blocks, n_chunks),
        in_specs=[pl.BlockSpec((tm, d), lambda j, i: (i, 0)), pl.BlockSpec(memory_space=pl.ANY)],
        out_specs=pl.BlockSpec((tm, tn), lambda j, i: (i, j)),
        out_shape=jax.ShapeDtypeStruct((m, ncols), out_dtype),
        scratch_shapes=[pltpu.VMEM(s, t) for s, t in scratch] + [pltpu.SemaphoreType.DMA((2,))],
        compiler_params=_params(("arbitrary", "arbitrary"),
                                _vmem_estimate([((tm, d), BF16), ((tm, tn), out_dtype)], scratch + [((tm // len(_row_parts(tm)), tn), F32)] * 3)),
        name=name,
    )(h, w)


def _attn_weights(q, kw, sink, n_valid):
    st = lax.dot_general(kw, q, (((1,), (1,)), ((), ())), preferred_element_type=F32)
    if n_valid is not None:
        row = lax.broadcasted_iota(jnp.int32, st.shape, 0)
        st = jnp.where(row < n_valid, st, -jnp.inf)
    m = jnp.maximum(jnp.max(st, axis=0, keepdims=True), sink)
    e = jnp.exp2(st - m)
    return e.astype(BF16), jnp.sum(e, axis=0, keepdims=True) + jnp.exp2(sink - m)


def _attn_values(vw, e, denom):
    ot = jnp.dot(vw.astype(F32).T.astype(BF16), e, preferred_element_type=F32) / denom
    return ot.T


def _store_unit(o_ref, za_ref, row0, o, rep):
    for r in range(0, rep, 2):
        pair = jnp.concatenate([o[r * CHUNK:(r + 1) * CHUNK], o[(r + 1) * CHUNK:(r + 2) * CHUNK]], axis=1)
        gate = za_ref[row0:row0 + CHUNK, r * HEAD_DIM:(r + 2) * HEAD_DIM].astype(F32)
        o_ref[row0:row0 + CHUNK, r * HEAD_DIM:(r + 2) * HEAD_DIM] = (pair * gate).astype(o_ref.dtype)


def _attn_kernel(q_ref, k_ref, v_ref, ks_ref, vs_ref, sink_ref, za_ref, wo_ref, o_ref, wob_ref,
                 *, n_units, rep, steps_per_stream, n_prompt_steps, n_cast_steps):
    step = pl.program_id(1)
    win = WINDOW + CHUNK
    sink = sink_ref[0] * LOG2_E

    @pl.when(pl.program_id(0) * (n_prompt_steps + 1) + step < n_cast_steps)
    def _():
        wob_ref[...] = wo_ref[...].astype(wob_ref.dtype)

    def run(windows):
        staged = []
        for c, (kw, vw, n_valid) in enumerate(windows):
            q = q_ref[:, c * CHUNK:(c + 1) * CHUNK, :].reshape(rep * CHUNK, HEAD_DIM)
            staged.append((vw,) + _attn_weights(q, kw, sink, n_valid))
        for c in range(n_units):
            _store_unit(o_ref, za_ref, c * CHUNK, _attn_values(*staged[c]), rep)

    @pl.when(step < n_prompt_steps)
    def _():
        cb = step % steps_per_stream
        windows = []
        for c in range(n_units):
            start = pl.multiple_of(jnp.maximum(cb * n_units + c - WINDOW // CHUNK, 0) * CHUNK, CHUNK)
            n_valid = jnp.where(cb == 0, (c + 1) * CHUNK, win) if c < WINDOW // CHUNK else None
            windows.append((k_ref[0, pl.ds(start, win), :], v_ref[0, pl.ds(start, win), :], n_valid))
        run(windows)

    @pl.when(step == n_prompt_steps)
    def _():
        run([(ks_ref[0, b * win:(b + 1) * win, :], vs_ref[0, b * win:(b + 1) * win, :], None) for b in range(n_units)])


def _attention(q_hm, k_hm, v_hm, ks, vs, sink_rows, za, w_out, batch, seq, dec_batch, aw):
    nh, m, _ = q_hm.shape
    g = k_hm.shape[0]
    rep = nh // g
    win = WINDOW + CHUNK
    n_units = dec_batch
    rows = n_units * CHUNK
    assert seq % rows == 0 and n_units >= WINDOW // CHUNK
    steps_per_stream = seq // rows
    n_prompt_steps = batch * steps_per_stream
    blocks = ([((rep, rows, LANES), BF16)] + [((seq, LANES), BF16)] * 2 + [((dec_batch * win, LANES), BF16)] * 2
              + [((rows, rep * HEAD_DIM), BF16)] * 2)
    temps = [((rep * CHUNK, 2 * LANES), F32)] * (4 * n_units)
    stream = lambda gi, s: (gi, jnp.minimum(s // steps_per_stream, batch - 1), 0)
    dk, dn = w_out.shape
    n_cast_steps = 16
    assert g * (n_prompt_steps + 1) >= n_cast_steps and dk % (16 * n_cast_steps) == 0
    cast_rows = dk // n_cast_steps
    cast_block = pl.BlockSpec((cast_rows, dn), lambda gi, s: (jnp.minimum(gi * (n_prompt_steps + 1) + s, n_cast_steps - 1), 0))
    blocks += [((cast_rows, dn), F32), ((cast_rows, dn), BF16)]
    return pl.pallas_call(
        functools.partial(_attn_kernel, n_units=n_units, rep=rep, steps_per_stream=steps_per_stream, n_prompt_steps=n_prompt_steps,
                          n_cast_steps=n_cast_steps),
        grid=(g, n_prompt_steps + 1),
        in_specs=[
            pl.BlockSpec((rep, rows, HEAD_DIM), lambda gi, s: (gi, s, 0)),
            pl.BlockSpec((1, seq, HEAD_DIM), stream),
            pl.BlockSpec((1, seq, HEAD_DIM), stream),
            pl.BlockSpec((1, dec_batch * win, HEAD_DIM), lambda gi, s: (gi, 0, 0)),
            pl.BlockSpec((1, dec_batch * win, HEAD_DIM), lambda gi, s: (gi, 0, 0)),
            pl.BlockSpec((1, 1, rep * CHUNK), lambda gi, s: (gi, 0, 0)),
            pl.BlockSpec((rows, rep * HEAD_DIM), lambda gi, s: (s, gi)),
            cast_block,
        ],
        out_specs=[pl.BlockSpec((rows, rep * HEAD_DIM), lambda gi, s: (s, gi)), cast_block],
        out_shape=[jax.ShapeDtypeStruct((m, aw), BF16), jax.ShapeDtypeStruct((dk, dn), BF16)],
        compiler_params=_params(("arbitrary",) * 2, _vmem_estimate(blocks, temps)),
        name="attention",
    )(q_hm, k_hm, v_hm, ks, vs, sink_rows, za, w_out)


def _ssm_disc_kernel(lre_ref, lim_ref, ldt_ref, are_ref, aim_ref, dre_ref, dim_ref, fre_ref, fim_ref):
    lr = jnp.minimum(lre_ref[...], LAMBDA_RE_MAX)
    li = lim_ref[...]
    dt = jnp.exp(ldt_ref[...])
    mag = jnp.exp(lr * dt)
    a_re = mag * jnp.cos(li * dt)
    a_im = mag * jnp.sin(li * dt)
    den = lr * lr + li * li
    nr = a_re - 1.0
    fre_ref[...] = (nr * lr + a_im * li) / den
    fim_ref[...] = (a_im * lr - nr * li) / den
    are_ref[...] = a_re
    aim_ref[...] = a_im
    for _ in range(SSM_POW_BITS):
        a_re, a_im = a_re * a_re - a_im * a_im, 2.0 * a_re * a_im
    dre_ref[...] = a_re
    dim_ref[...] = a_im


def _cmul(ar, ai, br, bi):
    return ar * br - ai * bi, ar * bi + ai * br


def _ssm_build_kernel(are_ref, aim_ref, dre_ref, dim_ref, fre_ref, fim_ref, btr_ref, bti_ref, cr_ref, ci_ref, dv_ref,
                      met_ref, ft_ref, a1_ref, a2_ref, a2s_ref, *, groups):
    lc = SSM_CHUNK * SSM_GROUP
    sub = lax.broadcasted_iota(jnp.int32, (SSM_GROUP, lc), 0)
    lane = lax.broadcasted_iota(jnp.int32, (SSM_GROUP, lc), 1)
    for gi in range(groups):
        row = slice(gi, gi + 1)
        a_re, a_im = are_ref[row, :], aim_ref[row, :]
        pw = [(jnp.ones_like(a_re), jnp.zeros_like(a_re))]
        for _ in range(SSM_CHUNK):
            pw.append(_cmul(pw[-1][0], pw[-1][1], a_re, a_im))
        c_re, c_im = cr_ref[gi], ci_ref[gi]
        wt = [_cmul(pr, pi, c_re, c_im) for pr, pi in pw]
        wt_re = jnp.concatenate([w[0] for w in wt[:SSM_CHUNK]], axis=0)
        wt_im = jnp.concatenate([w[1] for w in wt[:SSM_CHUNK]], axis=0)
        et_re = jnp.concatenate([w[0] for w in wt[1:]], axis=0)
        et_im = jnp.concatenate([w[1] for w in wt[1:]], axis=0)
        bb_re, bb_im = _cmul(fre_ref[row, :], fim_ref[row, :], btr_ref[gi], bti_ref[gi])
        r0 = lax.dot_general(jnp.concatenate([bb_re, bb_im], axis=1), jnp.concatenate([wt_re, -wt_im], axis=1),
                             (((1,), (1,)), ((), ())), preferred_element_type=F32, precision=lax.Precision.HIGHEST)
        r0 = r0 + jnp.where(sub == lane, dv_ref[gi], 0.0)
        rows = [r0] + [jnp.where(lane >= s * SSM_GROUP, pltpu.roll(r0, s * SSM_GROUP, 1), 0.0) for s in range(1, SSM_CHUNK)]
        mt = jnp.concatenate(rows, axis=0).T
        met_ref[gi] = jnp.concatenate([mt, et_re, -et_im], axis=1).astype(met_ref.dtype)
        fb = [_cmul(pw[SSM_CHUNK - 1 - s][0], pw[SSM_CHUNK - 1 - s][1], bb_re, bb_im) for s in range(SSM_CHUNK)]
        f_all = jnp.concatenate([jnp.concatenate([x[0] for x in fb], axis=0), jnp.concatenate([x[1] for x in fb], axis=0)], axis=1)
        ft_ref[gi] = f_all.T.astype(ft_ref.dtype)
        d_re, d_im = dre_ref[row, :], dim_ref[row, :]
        a1_ref[gi] = jnp.concatenate([d_re, d_re], axis=1)
        a2_ref[gi] = jnp.concatenate([-d_im, d_im], axis=1)
        a2s_ref[gi] = jnp.concatenate([d_im, -d_im], axis=1)


def _ssm_params(lambda_re, lambda_im, log_dt, b_re, b_im, c_re, c_im, d_skip):
    ng, p = lambda_re.shape
    lc = SSM_CHUNK * SSM_GROUP
    full = pl.BlockSpec((ng, p), lambda: (0, 0))
    disc = pl.pallas_call(
        _ssm_disc_kernel,
        in_specs=[full, full, pl.BlockSpec((ng, 1), lambda: (0, 0))],
        out_specs=[full] * 6,
        out_shape=[jax.ShapeDtypeStruct((ng, p), F32)] * 6,
        name="ssm_disc",
    )(lambda_re, lambda_im, log_dt.reshape(ng, 1))
    gb = _pick(ng, (8,))
    bt_re = jnp.swapaxes(b_re, 1, 2)
    bt_im = jnp.swapaxes(b_im, 1, 2)
    dvec = jnp.pad(d_skip, ((0, 0), (0, lc - SSM_GROUP))).reshape(ng, 1, lc)
    rows = pl.BlockSpec((gb, p), lambda i: (i, 0))
    mats = pl.BlockSpec((gb, SSM_GROUP, p), lambda i: (i, 0, 0))
    dec = pl.BlockSpec((gb, 1, 2 * p), lambda i: (i, 0, 0))
    blocks = ([((gb, LANES), F32)] * 6 + [((gb, SSM_GROUP, LANES), F32)] * 4 + [((gb, 8, lc), F32)]
              + [((gb, lc, lc + 2 * p), BF16), ((gb, 2 * p, lc), BF16)] + [((gb, 8, LANES), F32)] * 3)
    met, ft, a1, a2, a2s = pl.pallas_call(
        functools.partial(_ssm_build_kernel, groups=gb),
        grid=(ng // gb,),
        in_specs=[rows] * 6 + [mats] * 4 + [pl.BlockSpec((gb, 1, lc), lambda i: (i, 0, 0))],
        out_specs=[pl.BlockSpec((gb, lc, lc + 2 * p), lambda i: (i, 0, 0)), pl.BlockSpec((gb, 2 * p, lc), lambda i: (i, 0, 0)), dec, dec, dec],
        out_shape=[jax.ShapeDtypeStruct((ng, lc, lc + 2 * p), BF16), jax.ShapeDtypeStruct((ng, 2 * p, lc), BF16)]
        + [jax.ShapeDtypeStruct((ng, 1, 2 * p), F32)] * 3,
        compiler_params=_params(("arbitrary",), _vmem_estimate(blocks, [((lc, lc + 2 * p), F32)] * 8)),
        name="ssm_build",
    )(*disc, bt_re, bt_im, c_re, c_im, dvec)
    flat = lambda a: a.reshape(1, ng * 2 * p)
    return met, ft, flat(a1), flat(a2), flat(a2s)


def _chunk_blocks(nk):
    return [(k0, min(LANES, nk - k0)) for k0 in range(0, nk, LANES)]


def _pad_rows(x):
    n = x.shape[0]
    return x if n == LANES else jnp.concatenate([x, jnp.zeros((LANES - n, x.shape[1]), x.dtype)], axis=0)


def _ssm_inputs_phase(u_ref, ft_ref, rhs_ref, s_ref, *, nk, p):
    groups = LANES // SSM_GROUP
    blocks = _chunk_blocks(nk)
    xt = [[_pad_rows(u_ref[pl.ds(k0 * SSM_CHUNK + s, n, stride=SSM_CHUNK), :].astype(BF16)).T for k0, n in blocks]
          for s in range(SSM_CHUNK)]
    for gi in range(groups):
        rhs = jnp.concatenate([jnp.concatenate([xt[s][kb][gi * SSM_GROUP:(gi + 1) * SSM_GROUP, :] for kb in range(len(blocks))], axis=1)
                               for s in range(SSM_CHUNK)], axis=0)
        rhs_ref[gi] = rhs
        st = jnp.dot(ft_ref[gi], rhs, preferred_element_type=F32)
        for kb, (k0, n) in enumerate(blocks):
            s_ref[k0:k0 + n, gi * 2 * p:(gi + 1) * 2 * p] = st[:, kb * LANES:(kb + 1) * LANES].T[:n, :]


def _ssm_outputs_phase(rhs_ref, h_ref, met_ref, y_ref, *, nk, p):
    groups = LANES // SSM_GROUP
    blocks = _chunk_blocks(nk)
    zt = [[None] * groups for _ in range(SSM_CHUNK)]
    for gi in range(groups):
        ht = jnp.concatenate([_pad_rows(h_ref[k0:k0 + n, gi * 2 * p:(gi + 1) * 2 * p]).T for k0, n in blocks], axis=1)
        rhs = jnp.concatenate([rhs_ref[gi], ht.astype(BF16)], axis=0)
        yt = jnp.dot(met_ref[gi], rhs, preferred_element_type=F32)
        for t in range(SSM_CHUNK):
            zt[t][gi] = yt[t * SSM_GROUP:(t + 1) * SSM_GROUP, :]
    for t in range(SSM_CHUNK):
        z = jnp.concatenate(zt[t], axis=0)
        for kb, (k0, n) in enumerate(blocks):
            y_ref[pl.ds(k0 * SSM_CHUNK + t, n, stride=SSM_CHUNK), :] = z[:, kb * LANES:(kb + 1) * LANES].T[:n, :]


def _ssm_scan_phase(s_ref, h0_ref, a1_ref, a2_ref, a2s_ref, h_ref, fin_ref, ss_ref, *, batch, kp, dec_batch, ks, p):
    def swap_halves(x):
        lane = lax.broadcasted_iota(jnp.int32, x.shape, 1)
        return jnp.where(lane % (2 * p) < p, pltpu.roll(x, x.shape[1] - p, 1), pltpu.roll(x, p, 1))

    ss_ref[...] = swap_halves(s_ref[...])
    a1, a2, a2s = a1_ref[...], a2_ref[...], a2s_ref[...]
    zero = jnp.zeros_like(a1)

    def step(k, h, hs):
        h_ref[pl.ds(k, 1), :] = h
        s = s_ref[pl.ds(k, 1), :]
        ss = ss_ref[pl.ds(k, 1), :]
        return a1 * h + a2 * hs + s, a1 * hs + a2s * h + ss

    def body(k, carry):
        out = []
        for b in range(batch):
            out.extend(step(b * kp + k, carry[2 * b], carry[2 * b + 1]))
        return tuple(out)

    fin = lax.fori_loop(0, kp, body, (zero,) * (2 * batch))
    for b in range(batch):
        fin_ref[b:b + 1, :] = fin[2 * b]
    h0s_all = swap_halves(h0_ref[...])
    for b in range(dec_batch):
        h, hs = h0_ref[b:b + 1, :], h0s_all[b:b + 1, :]
        for k in range(ks):
            h, hs = step(batch * kp + b * ks + k, h, hs)
        fin_ref[batch + b:batch + b + 1, :] = h


def _ssm_kernel(u_ref, ft_ref, met_ref, h0_ref, a1_ref, a2_ref, a2s_ref, y_ref, fin_ref, rhs_ref, s_ref, ss_ref, h_ref,
                *, batch, kp, dec_batch, ks, p):
    nk = batch * kp + dec_batch * ks
    _ssm_inputs_phase(u_ref, ft_ref, rhs_ref, s_ref, nk=nk, p=p)
    _ssm_scan_phase(s_ref, h0_ref, a1_ref, a2_ref, a2s_ref, h_ref, fin_ref, ss_ref, batch=batch, kp=kp, dec_batch=dec_batch, ks=ks, p=p)
    _ssm_outputs_phase(rhs_ref, h_ref, met_ref, y_ref, nk=nk, p=p)


def _ssm(u, met, ft, a1, a2, a2s, h0, batch, kp, dec_batch, ks):
    m, sw = u.shape
    ng, p2, lc = ft.shape
    p = p2 // 2
    nk = m // SSM_CHUNK
    assert nk == batch * kp + dec_batch * ks and (nk % LANES) % 8 == 0
    nk_lanes = -(-nk // LANES) * LANES
    groups = LANES // SSM_GROUP
    lb = groups * p2
    nseq = batch + dec_batch
    u_spec = pl.BlockSpec((m, LANES), lambda i: (0, i))
    rowb = pl.BlockSpec((1, lb), lambda i: (0, i))
    blocks = [((m, LANES), F32)] * 2 + [((groups, p2, lc), BF16), ((groups, lc, lc + p2), BF16), ((nseq, lb), F32)]
    scratch = [((groups, lc, nk_lanes), BF16)] + [((nk, lb), F32)] * 3
    return pl.pallas_call(
        functools.partial(_ssm_kernel, batch=batch, kp=kp, dec_batch=dec_batch, ks=ks, p=p),
        grid=(ng // groups,),
        in_specs=[u_spec, pl.BlockSpec((groups, p2, lc), lambda i: (i, 0, 0)), pl.BlockSpec((groups, lc, lc + p2), lambda i: (i, 0, 0)),
                  pl.BlockSpec((dec_batch, lb), lambda i: (0, i)), rowb, rowb, rowb],
        out_specs=[u_spec, pl.BlockSpec((nseq, lb), lambda i: (0, i))],
        out_shape=[jax.ShapeDtypeStruct((m, sw), F32), jax.ShapeDtypeStruct((nseq, ng * p2), F32)],
        scratch_shapes=[pltpu.VMEM(s, d) for s, d in scratch],
        compiler_params=_params(("arbitrary",), _vmem_estimate(blocks, scratch + [((lc + p2, nk_lanes), BF16), ((lc, nk_lanes), F32)])),
        name="ssm",
    )(u, ft, met, h0, a1, a2, a2s)


def _weight_stream_scratch(k, tn, n_chunks):
    assert k % (8 * n_chunks) == 0
    shapes = [((2, k, tn), BF16), ((2, k // n_chunks, tn), F32)]
    return shapes, [pltpu.VMEM(s, t) for s, t in shapes] + [pltpu.SemaphoreType.DMA((2,))]


def _glu_kernel(y_ref, w_hbm, ba_ref, bg_ref, zs_ref, o_ref, wa_buf, wa_stage, wa_sem, wg_buf, wg_stage, wg_sem, *, n_blocks, n_chunks):
    tn = o_ref.shape[1]
    stream = functools.partial(_stream_weights, w_hbm, n_blocks=n_blocks, n_chunks=n_chunks)
    wa_ref = stream(wa_buf, wa_stage, wa_sem, col_of_block=lambda blk: blk * tn)
    wg_ref = stream(wg_buf, wg_stage, wg_sem, col_of_block=lambda blk: (n_blocks + blk) * tn)
    y = y_ref[...].astype(BF16)
    a = jnp.dot(y, wa_ref[...], preferred_element_type=F32) + ba_ref[...]
    g = jnp.dot(y, wg_ref[...], preferred_element_type=F32) + bg_ref[...]
    o_ref[...] = (a * _sigmoid(g) * zs_ref[...].astype(F32)).astype(o_ref.dtype)


def _glu(y, w_glu, b_glu, acts, zs_col0, tm, tn):
    m, sw = y.shape
    nb, n_chunks = sw // tn, m // tm
    zb0 = zs_col0 // tn
    shapes, scratch = _weight_stream_scratch(sw, tn, n_chunks)
    blocks = [((tm, sw), F32), ((tm, tn), BF16), ((tm, tn), BF16)]
    return pl.pallas_call(
        functools.partial(_glu_kernel, n_blocks=nb, n_chunks=n_chunks),
        grid=(nb, n_chunks),
        in_specs=[
            pl.BlockSpec((tm, sw), lambda j, i: (i, 0)),
            pl.BlockSpec(memory_space=pl.ANY),
            pl.BlockSpec((1, tn), lambda j, i: (0, j)),
            pl.BlockSpec((1, tn), lambda j, i: (0, nb + j)),
            pl.BlockSpec((tm, tn), lambda j, i: (i, zb0 + j)),
        ],
        out_specs=pl.BlockSpec((tm, tn), lambda j, i: (i, j)),
        out_shape=jax.ShapeDtypeStruct((m, sw), BF16),
        scratch_shapes=scratch * 2,
        compiler_params=_params(("arbitrary", "arbitrary"), _vmem_estimate(blocks, shapes * 2 + [((tm, tn), F32)] * 4)),
        name="glu",
    )(y, w_glu, b_glu, b_glu, acts)


def _merge_kernel(xa_ref, xs_ref, wpa_hbm, wps_hbm, ga_ref, gs_ref, o_ref, wa_buf, wa_stage, wa_sem, ws_buf, ws_stage, ws_sem,
                  *, n_blocks, n_chunks):
    tn = o_ref.shape[1]
    col = lambda blk: blk * tn
    wpa_ref = _stream_weights(wpa_hbm, wa_buf, wa_stage, wa_sem, col_of_block=col, n_blocks=n_blocks, n_chunks=n_chunks)
    wps_ref = _stream_weights(wps_hbm, ws_buf, ws_stage, ws_sem, col_of_block=col, n_blocks=n_blocks, n_chunks=n_chunks)
    br_a = jnp.dot(xa_ref[...], wpa_ref[...], preferred_element_type=F32)
    br_s = jnp.dot(xs_ref[...], wps_ref[...], preferred_element_type=F32)
    o_ref[...] = (ga_ref[...].astype(F32) * br_a + gs_ref[...].astype(F32) * br_s).astype(o_ref.dtype)


def _merge(xa, xs, w_pa, w_ps, acts, gate_col0, tm, tn):
    m, aw = xa.shape
    sw = xs.shape[1]
    d = w_pa.shape[1]
    nb, n_chunks = d // tn, m // tm
    gb0 = gate_col0 // tn
    shapes_a, scratch_a = _weight_stream_scratch(aw, tn, n_chunks)
    shapes_s, scratch_s = _weight_stream_scratch(sw, tn, n_chunks)
    blocks = [((tm, aw), BF16), ((tm, sw), BF16)] + [((tm, tn), BF16)] * 3
    return pl.pallas_call(
        functools.partial(_merge_kernel, n_blocks=nb, n_chunks=n_chunks),
        grid=(nb, n_chunks),
        in_specs=[
            pl.BlockSpec((tm, aw), lambda j, i: (i, 0)),
            pl.BlockSpec((tm, sw), lambda j, i: (i, 0)),
            pl.BlockSpec(memory_space=pl.ANY),
            pl.BlockSpec(memory_space=pl.ANY),
            pl.BlockSpec((tm, tn), lambda j, i: (i, gb0 + j)),
            pl.BlockSpec((tm, tn), lambda j, i: (i, gb0 + nb + j)),
        ],
        out_specs=pl.BlockSpec((tm, tn), lambda j, i: (i, j)),
        out_shape=jax.ShapeDtypeStruct((m, d), BF16),
        scratch_shapes=scratch_a + scratch_s,
        compiler_params=_params(("arbitrary", "arbitrary"), _vmem_estimate(blocks, shapes_a + shapes_s + [((tm, tn), F32)] * 4)),
        name="merge",
    )(xa, xs, w_pa, w_ps, acts, acts)


def _out_kernel(mg_ref, w_ref, x_ref, g_ref, o_ref, ssq_ref, *, n_col_blocks, tn):
    j = pl.program_id(1)
    blk = x_ref[...] + jnp.dot(mg_ref[...], w_ref[...], preferred_element_type=F32)
    o_ref[:, pl.ds(pl.multiple_of(j * tn, tn), tn)] = blk
    part = jnp.sum(blk * blk, axis=-1, keepdims=True)

    @pl.when(j == 0)
    def _():
        ssq_ref[...] = part

    @pl.when(j > 0)
    def _():
        ssq_ref[...] += part

    @pl.when(j == n_col_blocks - 1)
    def _():
        scale = lax.rsqrt(ssq_ref[...] / (n_col_blocks * tn) + NORM_EPS)
        o_ref[...] = o_ref[...] * scale * g_ref[...]


def _out(merged, row_block0, w_out, x, final_g, tm, tn):
    mx, d = x.shape
    nb = d // tn
    blocks = [((tm, d), BF16), ((d, tn), BF16), ((tm, tn), F32), ((tm, d), F32)]
    return pl.pallas_call(
        functools.partial(_out_kernel, n_col_blocks=nb, tn=tn),
        grid=(mx // tm, nb),
        in_specs=[
            pl.BlockSpec((tm, d), lambda i, j: (row_block0 + i, 0)),
            pl.BlockSpec((d, tn), lambda i, j: (0, j)),
            pl.BlockSpec((tm, tn), lambda i, j: (i, j)),
            pl.BlockSpec((1, d), lambda i, j: (0, 0)),
        ],
        out_specs=pl.BlockSpec((tm, d), lambda i, j: (i, 0)),
        out_shape=jax.ShapeDtypeStruct((mx, d), F32),
        scratch_shapes=[pltpu.VMEM((tm, 1), F32)],
        compiler_params=_params(("arbitrary", "arbitrary"), _vmem_estimate(blocks, [((tm, tn), F32)] * 3 + [((tm, LANES), F32)])),
        name="out_norm",
    )(merged, w_out, x, final_g.reshape(1, d))


def _rope_tables(positions):
    half = HEAD_DIM // 2
    inv_freq = ROPE_THETA ** (-jnp.arange(half, dtype=F32) / half)
    ang = positions.astype(F32)[:, None] * inv_freq[None, :]
    cos, sin = jnp.cos(ang), jnp.sin(ang)
    reps = LANES // HEAD_DIM
    return jnp.tile(jnp.concatenate([cos, cos], axis=1), (1, reps)), jnp.tile(jnp.concatenate([-sin, sin], axis=1), (1, reps))


def kernel(x_prompt, x_sample, cache_k, cache_v, state_ssm_re, state_ssm_im, norm_g, w_in, sink, lambda_re, lambda_im,
           log_dt, b_re, b_im, c_re, c_im, d_skip, w_glu, b_glu, w_pa, w_ps, w_out, final_g):
    depth = norm_g.shape[0]
    assert depth == 1, "one trunk layer"
    batch, seq, d = x_prompt.shape
    dec_batch, dec_seq, _ = x_sample.shape
    aw = w_pa.shape[1]
    sw = w_ps.shape[1]
    nh = aw // HEAD_DIM
    g = max(1, nh // GQA_GROUPING)
    rep = nh // g
    kvw = g * HEAD_DIM
    ng, p = lambda_re.shape[1:]
    assert dec_seq == CHUNK and cache_k.shape[2] == WINDOW and seq % (2 * CHUNK) == 0
    assert kvw % LANES == 0 and rep % 2 == 0 and sw == ng * SSM_GROUP and (1 << SSM_POW_BITS) == SSM_CHUNK
    assert 2 * p == LANES and sw % LANES == 0
    mp, ms = batch * seq, dec_batch * dec_seq
    m = mp + ms
    widths = (aw, 2 * kvw, aw, sw, sw, 2 * d)
    assert sum(widths) == w_in.shape[2]
    c_q, c_kv, c_za, c_u, c_zs, c_gate = (sum(widths[:n]) for n in range(len(widths)))
    tm = _pick(math.gcd(mp, ms), (512, 256, 128))
    tmp = max(t for t in range(16, 1153, 16) if m % t == 0)
    tn = _pick(math.gcd(c_kv, c_za, c_u, c_zs, c_gate, 2 * d), (512, 256, 128))

    xp = x_prompt.reshape(mp, d)
    xs = x_sample.reshape(ms, d)
    w_in2 = w_in.reshape(d, w_in.shape[2])
    positions = jnp.concatenate([jnp.tile(jnp.arange(seq, dtype=jnp.int32), batch),
                                 jnp.tile(PAST_LEN + jnp.arange(dec_seq, dtype=jnp.int32), dec_batch)])
    cos, sin = _rope_tables(positions)

    h, k_f, v_f, k_hm, v_hm = _norm_kv(xp, xs, norm_g[0], w_in2, cos, sin, c_kv, kvw, tm)
    q_hm = _proj_q(h, w_in2, cos, sin, aw, tmp, tn)
    tna = _pick(math.gcd(aw, sw), (1024, 512, 256))
    acts = _proj_act(h, w_in2, [(c_za, aw, "silu"), (c_zs, sw, "silu"), (c_gate, 2 * d, "sigmoid")], tmp, tna, "proj_acts")
    u = _proj_act(h, w_in2, [(c_u, sw, "none")], tmp, tna, "proj_u", out_dtype=F32)

    to_heads = lambda c: jnp.transpose(c[0], (2, 0, 1, 3)).astype(BF16)
    new_rows = lambda a: a[:, mp:].reshape(g, dec_batch, dec_seq, HEAD_DIM)
    ks = jnp.concatenate([to_heads(cache_k), new_rows(k_hm)], axis=2).reshape(g, dec_batch * (WINDOW + CHUNK), HEAD_DIM)
    vs = jnp.concatenate([to_heads(cache_v), new_rows(v_hm)], axis=2).reshape(g, dec_batch * (WINDOW + CHUNK), HEAD_DIM)
    sink_rows = jnp.repeat(sink[0].reshape(g, rep), CHUNK, axis=1).reshape(g, 1, rep * CHUNK)
    xa, w_out_b = _attention(q_hm, k_hm, v_hm, ks, vs, sink_rows, acts, w_out.reshape(d, d), batch, seq, dec_batch, aw)

    met, ft, a1, a2, a2s = _ssm_params(lambda_re[0], lambda_im[0], log_dt[0], b_re[0], b_im[0], c_re[0], c_im[0], d_skip[0])
    h0 = jnp.concatenate([state_ssm_re[0], state_ssm_im[0]], axis=-1).reshape(dec_batch, ng * 2 * p)
    y, fin = _ssm(u, met, ft, a1, a2, a2s, h0, batch, seq // SSM_CHUNK, dec_batch, dec_seq // SSM_CHUNK)

    tmm = tmp
    x_ssm = _glu(y, w_glu.reshape(sw, 2 * sw), b_glu.reshape(1, 2 * sw), acts, aw, tmm, tna)
    merged = _merge(xa, x_ssm, w_pa.reshape(aw, d), w_ps.reshape(sw, d), acts, aw + sw, tmm, tna)
    tno = _pick(d, (1024, 512, 256))
    y_prompt = _out(merged, 0, w_out_b, xp, final_g, tm, tno).reshape(batch, seq, d)
    y_sample = _out(merged, mp // tm, w_out_b, xs, final_g, tm, tno).reshape(dec_batch, dec_seq, d)

    keep = min(WINDOW, seq)
    last_rows = lambda a: jnp.stack([a[(b + 1) * seq - keep:(b + 1) * seq] for b in range(batch)]).reshape(1, batch, keep, g, HEAD_DIM)
    dec_rows = lambda a: a[mp:].reshape(dec_batch, dec_seq, g, HEAD_DIM)[None]
    fin = fin.reshape(batch + dec_batch, ng, 2, p)
    return (y_prompt, y_sample, last_rows(k_f), last_rows(v_f), fin[:batch, :, 0][None], fin[:batch, :, 1][None],
            dec_rows(k_f), dec_rows(v_f), fin[batch:, :, 0][None], fin[batch:, :, 1][None])
```

```python
import functools
import math

import jax
import jax.numpy as jnp
from jax import lax
from jax.experimental import pallas as pl
from jax.experimental.pallas import tpu as pltpu

CHUNK = 64
WINDOW = 128
HEAD_DIM = 64
GQA_GROUPING = 8
SSM_GROUP = 16
PAST_LEN = 1024
ROPE_THETA = 10000.0
NORM_EPS = 1e-5
LAMBDA_RE_MAX = -1e-4
LOG2_E = math.log2(math.e)

SSM_CHUNK = 16
SSM_POW_BITS = 4
LANES = 128
V7X_VMEM_BYTES = 64 * 1024 * 1024
BF16 = jnp.bfloat16
F32 = jnp.float32


def _pick(n, prefs):
    for p in prefs:
        if n % p == 0:
            return p
    raise ValueError(f"no tile in {prefs} divides {n}")


def _params(sem, vmem_bytes):
    limit = min(int(vmem_bytes), V7X_VMEM_BYTES - 4 * 1024 * 1024)
    return pltpu.CompilerParams(dimension_semantics=sem, vmem_limit_bytes=limit)


def _sigmoid(x):
    return 0.5 * jnp.tanh(0.5 * x) + 0.5


def _nbytes(shape, dtype):
    return math.prod(shape) * jnp.dtype(dtype).itemsize


def _vmem_estimate(blocks, temps=()):
    return 2 * (2 * sum(_nbytes(s, d) for s, d in blocks) + sum(_nbytes(s, d) for s, d in temps))


def _norm_kv_kernel(xp_ref, xs_ref, g_ref, w_ref, cos_ref, sin_ref, h_ref, kf_ref, vf_ref, kh_ref, vh_ref, wb_ref, *, n_prompt_blocks, kvw):
    i = pl.program_id(0)

    @pl.when(i == 0)
    def _():
        wb_ref[...] = w_ref[...].astype(wb_ref.dtype)

    tm = h_ref.shape[0]
    n_parts = 2 if tm % 32 == 0 else 1
    for r in range(n_parts):
        rows = slice(r * (tm // n_parts), (r + 1) * (tm // n_parts))
        x = jnp.where(i < n_prompt_blocks, xp_ref[rows, :], xs_ref[rows, :])
        y = x * lax.rsqrt(jnp.mean(x * x, axis=-1, keepdims=True) + NORM_EPS)
        h = (y * g_ref[...]).astype(h_ref.dtype)
        h_ref[rows, :] = h
        acc = jnp.dot(h, wb_ref[...], preferred_element_type=F32)
        k = _rope(acc[:, :kvw], cos_ref.at[rows, :], sin_ref.at[rows, :])
        v = acc[:, kvw:]
        kf_ref[rows, :] = k
        vf_ref[rows, :] = v
        _store_heads(kh_ref, rows, k)
        _store_heads(vh_ref, rows, v)


def _norm_kv(xp, xs, norm_g, w, cos, sin, col0, kvw, tm):
    mp, d = xp.shape
    ms = xs.shape[0]
    npb, nsb = mp // tm, ms // tm
    m = mp + ms
    g = kvw // HEAD_DIM
    tn = 2 * kvw
    assert col0 % tn == 0
    once = pl.Buffered(1)
    tab = pl.BlockSpec((tm, LANES), lambda i: (i, 0))
    flat = pl.BlockSpec((tm, kvw), lambda i: (i, 0))
    heads = pl.BlockSpec((g, tm, HEAD_DIM), lambda i: (0, i, 0))
    xs_spec = (pl.BlockSpec((tm, d), lambda i: (0, 0), pipeline_mode=once) if nsb == 1
               else pl.BlockSpec((tm, d), lambda i: (jnp.maximum(i - npb, 0), 0)))
    blocks = [((tm, d), F32), ((tm, d), BF16), ((tm, 2 * tn), F32), ((tm, 2 * tn), BF16), ((tm, 2 * LANES), F32)]
    temps = [((tm, d), F32)] * 2 + [((d, tn), F32), ((d, tn), BF16)]
    return pl.pallas_call(
        functools.partial(_norm_kv_kernel, n_prompt_blocks=npb, kvw=kvw),
        grid=(npb + nsb,),
        in_specs=[
            pl.BlockSpec((tm, d), lambda i: (jnp.minimum(i, npb - 1), 0)),
            xs_spec,
            pl.BlockSpec((1, d), lambda i: (0, 0)),
            pl.BlockSpec((d, tn), lambda i: (0, col0 // tn), pipeline_mode=once),
            tab, tab,
        ],
        out_specs=[pl.BlockSpec((tm, d), lambda i: (i, 0)), flat, flat, heads, heads],
        out_shape=[jax.ShapeDtypeStruct((m, d), BF16)] + [jax.ShapeDtypeStruct((m, kvw), F32)] * 2
        + [jax.ShapeDtypeStruct((g, m, HEAD_DIM), BF16)] * 2,
        scratch_shapes=[pltpu.VMEM((d, tn), BF16)],
        compiler_params=_params(("arbitrary",), _vmem_estimate(blocks, temps)),
        name="norm_kv",
    )(xp, xs, norm_g.reshape(1, d), w, cos, sin)


def _rope(acc, cos_ref, sin_ref):
    tm, tn = acc.shape
    reps = tn // LANES
    cos = jnp.tile(cos_ref[...], (1, reps))
    sin = jnp.tile(sin_ref[...], (1, reps))
    lane = lax.broadcasted_iota(jnp.int32, (tm, tn), 1)
    low = (lane % HEAD_DIM) < (HEAD_DIM // 2)
    partner = jnp.where(low, pltpu.roll(acc, tn - HEAD_DIM // 2, 1), pltpu.roll(acc, HEAD_DIM // 2, 1))
    return acc * cos + partner * sin


def _store_heads(o_ref, rows, val):
    for h in range(val.shape[1] // HEAD_DIM):
        o_ref[h, rows, :] = val[:, h * HEAD_DIM:(h + 1) * HEAD_DIM].astype(o_ref.dtype)


def _cast_weight(w_ref, wb_ref):
    @pl.when(pl.program_id(1) == 0)
    def _():
        wb_ref[...] = w_ref[...].astype(wb_ref.dtype)


def _stream_weights(w_hbm, wbuf_ref, stage_ref, sem_ref, *, col_of_block, n_blocks, n_chunks):
    d, tn = wbuf_ref.shape[1:]
    ck = d // n_chunks
    total = n_blocks * n_chunks
    j = pl.program_id(0)
    t = j * n_chunks + pl.program_id(1)

    def aligned(x, a):
        return x if isinstance(x, int) else pl.multiple_of(x, a)

    def chunk_copy(blk, chunk, slot):
        src = w_hbm.at[pl.ds(aligned(chunk * ck, ck), ck), pl.ds(aligned(col_of_block(blk), LANES), tn)]
        return pltpu.make_async_copy(src, stage_ref.at[slot], sem_ref.at[slot])

    def generation(s):
        s = jnp.asarray(s, jnp.int32)
        k = s + n_chunks - 1
        past = (k > total - 1).astype(jnp.int32)
        k = jnp.minimum(k, total - 1)
        blk, chunk = k // n_chunks, k % n_chunks
        return chunk_copy(blk, chunk, s % 2), (blk + past) % 2, chunk

    def land(copy, half, chunk, slot):
        copy.wait()
        wbuf_ref[half, pl.ds(aligned(chunk * ck, ck), ck), :] = stage_ref[slot].astype(wbuf_ref.dtype)

    @pl.when(t == 0)
    def _():
        slot_of = lambda c: (c + n_chunks - 1) % 2
        first = [chunk_copy(0, c, slot_of(c)) for c in range(n_chunks)]
        first[0].start()
        for c in range(n_chunks - 1):
            first[c + 1].start()
            land(first[c], 0, c, slot_of(c))

    @pl.when(t + 1 < total)
    def _():
        generation(t + 1)[0].start()

    copy, half, chunk = generation(t)
    land(copy, half, chunk, t % 2)
    return wbuf_ref.at[j % 2]


def _row_parts(tm, n=2):
    n = n if tm % (16 * n) == 0 and tm >= 1024 else 1
    return [slice(r * (tm // n), (r + 1) * (tm // n)) for r in range(n)]


def _proj_q_kernel(h_ref, w_ref, cos_ref, sin_ref, q_ref, wb_ref):
    _cast_weight(w_ref, wb_ref)
    for rows in _row_parts(h_ref.shape[0], 4):
        acc = jnp.dot(h_ref[rows, :], wb_ref[...], preferred_element_type=F32)
        _store_heads(q_ref, rows, _rope(acc, cos_ref.at[rows, :], sin_ref.at[rows, :]) * (HEAD_DIM ** -0.5 * LOG2_E))


def _proj_act_kernel(h_ref, w_hbm, o_ref, wbuf_ref, stage_ref, sem_ref, *, segments, n_chunks):
    tn = o_ref.shape[1]
    n_blocks = sum(s[1] for s in segments)
    in_seg = lambda blk, s: (blk >= s[0]) & (blk < s[0] + s[1])

    def col_of_block(blk):
        if isinstance(blk, int):
            return next(s[2] + (blk - s[0]) * tn for s in segments if s[0] <= blk < s[0] + s[1])
        return sum(jnp.where(in_seg(blk, s), s[2] + (blk - s[0]) * tn, 0) for s in segments)

    wb_ref = _stream_weights(w_hbm, wbuf_ref, stage_ref, sem_ref, col_of_block=col_of_block, n_blocks=n_blocks, n_chunks=n_chunks)
    acts = {s[3] for s in segments}
    assert acts in ({"none"}, {"silu"}, {"sigmoid"}, {"silu", "sigmoid"})
    j = pl.program_id(0)
    for rows in _row_parts(h_ref.shape[0]):
        acc = jnp.dot(h_ref[rows, :], wb_ref[...], preferred_element_type=F32)
        if acts != {"none"}:
            sig = _sigmoid(acc)
            if acts == {"silu"}:
                acc = acc * sig
            elif acts == {"sigmoid"}:
                acc = sig
            else:
                is_silu = functools.reduce(lambda a, b: a | b, [in_seg(j, s) for s in segments if s[3] == "silu"])
                acc = jnp.where(is_silu, acc * sig, sig)
        o_ref[rows, :] = acc.astype(o_ref.dtype)


def _proj_specs(d, tm, tn, col0):
    assert col0 % tn == 0
    cb0 = col0 // tn
    return [pl.BlockSpec((tm, d), lambda j, i: (i, 0)), pl.BlockSpec((d, tn), lambda j, i: (0, cb0 + j))]


def _proj_vmem(d, tm, tn, outs):
    return _vmem_estimate([((tm, d), BF16), ((d, tn), F32)] + outs, [((tm, tn), F32)] * 3 + [((d, tn), BF16)])


def _proj_q(h, w, cos, sin, aw, tm, tn):
    m, d = h.shape
    nh = aw // HEAD_DIM
    tab = pl.BlockSpec((tm, LANES), lambda j, i: (i, 0))
    return pl.pallas_call(
        _proj_q_kernel,
        grid=(aw // tn, m // tm),
        in_specs=_proj_specs(d, tm, tn, 0) + [tab, tab],
        out_specs=pl.BlockSpec((tn // HEAD_DIM, tm, HEAD_DIM), lambda j, i: (j, i, 0)),
        out_shape=jax.ShapeDtypeStruct((nh, m, HEAD_DIM), BF16),
        scratch_shapes=[pltpu.VMEM((d, tn), BF16)],
        compiler_params=_params(("arbitrary", "arbitrary"), _proj_vmem(d, tm, tn, [((tm, 2 * tn), BF16), ((tm, 2 * LANES), F32)])),
        name="proj_q",
    )(h, w, cos, sin)


def _proj_act(h, w, runs, tm, tn, name, out_dtype=BF16):
    m, d = h.shape
    n_chunks = m // tm
    segments, b0 = [], 0
    for col0, width, act in runs:
        assert width % tn == 0 and col0 % LANES == 0
        segments.append((b0, width // tn, col0, act))
        b0 += width // tn
    n_blocks, ncols = b0, b0 * tn
    assert d % (8 * n_chunks) == 0
    scratch = [((2, d, tn), BF16), ((2, d // n_chunks, tn), F32)]
    return pl.pallas_call(
        functools.partial(_proj_act_kernel, segments=tuple(segments), n_chunks=n_chunks),
        grid=(n_blocks, n_chunks),
        in_specs=[pl.BlockSpec((tm, d), lambda j, i: (i, 0)), pl.BlockSpec(memory_space=pl.ANY)],
        out_specs=pl.BlockSpec((tm, tn), lambda j, i: (i, j)),
        out_shape=jax.ShapeDtypeStruct((m, ncols), out_dtype),
        scratch_shapes=[pltpu.VMEM(s, t) for s, t in scratch] + [pltpu.SemaphoreType.DMA((2,))],
        compiler_params=_params(("arbitrary", "arbitrary"),
                                _vmem_estimate([((tm, d), BF16), ((tm, tn), out_dtype)], scratch + [((tm // len(_row_parts(tm)), tn), F32)] * 3)),
        name=name,
    )(h, w)


def _attn_weights(q, kw, sink, n_valid):
    st = lax.dot_general(kw, q, (((1,), (1,)), ((), ())), preferred_element_type=F32)
    if n_valid is not None:
        row = lax.broadcasted_iota(jnp.int32, st.shape, 0)
        st = jnp.where(row < n_valid, st, -jnp.inf)
    m = jnp.maximum(jnp.max(st, axis=0, keepdims=True), sink)
    e = jnp.exp2(st - m)
    return e.astype(BF16), jnp.sum(e, axis=0, keepdims=True) + jnp.exp2(sink - m)


def _attn_values(vw, e, denom):
    ot = jnp.dot(vw.astype(F32).T.astype(BF16), e, preferred_element_type=F32) / denom
    return ot.T


def _store_unit(o_ref, za_ref, row0, col0, o, rep):
    for r in range(0, rep, 2):
        pair = jnp.concatenate([o[r * CHUNK:(r + 1) * CHUNK], o[(r + 1) * CHUNK:(r + 2) * CHUNK]], axis=1)
        cols = slice(col0 + r * HEAD_DIM, col0 + (r + 2) * HEAD_DIM)
        gate = za_ref[row0:row0 + CHUNK, cols].astype(F32)
        o_ref[row0:row0 + CHUNK, cols] = (pair * gate).astype(o_ref.dtype)


def _attn_kernel(q_ref, k_ref, v_ref, ks_ref, vs_ref, sink_ref, za_ref, wo_ref, o_ref, wob_ref,
                 *, n_units, n_heads, rep, steps_per_stream, n_prompt_steps, n_cast_steps):
    step = pl.program_id(1)
    win = WINDOW + CHUNK

    @pl.when(pl.program_id(0) * (n_prompt_steps + 1) + step < n_cast_steps)
    def _():
        wob_ref[...] = wo_ref[...].astype(wob_ref.dtype)

    def run(windows):
        staged = []
        for hd, c, kw, vw, n_valid in windows:
            q = q_ref[hd * rep:(hd + 1) * rep, c * CHUNK:(c + 1) * CHUNK, :].reshape(rep * CHUNK, HEAD_DIM)
            staged.append((hd, c, vw) + _attn_weights(q, kw, sink_ref[hd] * LOG2_E, n_valid))
        for hd, c, vw, e, denom in staged:
            _store_unit(o_ref, za_ref, c * CHUNK, hd * rep * HEAD_DIM, _attn_values(vw, e, denom), rep)

    @pl.when(step < n_prompt_steps)
    def _():
        cb = step % steps_per_stream
        windows = []
        for hd in range(n_heads):
            for c in range(n_units):
                start = pl.multiple_of(jnp.maximum(cb * n_units + c - WINDOW // CHUNK, 0) * CHUNK, CHUNK)
                n_valid = jnp.where(cb == 0, (c + 1) * CHUNK, win) if c < WINDOW // CHUNK else None
                windows.append((hd, c, k_ref[hd, pl.ds(start, win), :], v_ref[hd, pl.ds(start, win), :], n_valid))
        run(windows)

    @pl.when(step == n_prompt_steps)
    def _():
        run([(hd, b, ks_ref[hd, b * win:(b + 1) * win, :], vs_ref[hd, b * win:(b + 1) * win, :], None)
             for hd in range(n_heads) for b in range(n_units)])


def _attention(q_hm, k_hm, v_hm, ks, vs, sink_rows, za, w_out, batch, seq, dec_batch, aw):
    nh, m, _ = q_hm.shape
    g = k_hm.shape[0]
    rep = nh // g
    win = WINDOW + CHUNK
    n_units = dec_batch
    rows = n_units * CHUNK
    assert seq % rows == 0 and n_units >= WINDOW // CHUNK
    steps_per_stream = seq // rows
    n_prompt_steps = batch * steps_per_stream
    nhd = 2 if g % 2 == 0 else 1
    blocks = ([((nhd * rep, rows, LANES), BF16)] + [((nhd * seq, LANES), BF16)] * 2 + [((nhd * dec_batch * win, LANES), BF16)] * 2
              + [((rows, nhd * rep * HEAD_DIM), BF16)] * 2)
    temps = [((rep * CHUNK, 2 * LANES), F32)] * (4 * n_units * nhd)
    stream = lambda gi, s: (gi, jnp.minimum(s // steps_per_stream, batch - 1), 0)
    dk, dn = w_out.shape
    n_cast_steps = 16
    assert (g // nhd) * (n_prompt_steps + 1) >= n_cast_steps and dk % (16 * n_cast_steps) == 0
    cast_rows = dk // n_cast_steps
    cast_block = pl.BlockSpec((cast_rows, dn), lambda gi, s: (jnp.minimum(gi * (n_prompt_steps + 1) + s, n_cast_steps - 1), 0))
    blocks += [((cast_rows, dn), F32), ((cast_rows, dn), BF16)]
    return pl.pallas_call(
        functools.partial(_attn_kernel, n_units=n_units, n_heads=nhd, rep=rep, steps_per_stream=steps_per_stream,
                          n_prompt_steps=n_prompt_steps, n_cast_steps=n_cast_steps),
        grid=(g // nhd, n_prompt_steps + 1),
        in_specs=[
            pl.BlockSpec((nhd * rep, rows, HEAD_DIM), lambda gi, s: (gi, s, 0)),
            pl.BlockSpec((nhd, seq, HEAD_DIM), stream),
            pl.BlockSpec((nhd, seq, HEAD_DIM), stream),
            pl.BlockSpec((nhd, dec_batch * win, HEAD_DIM), lambda gi, s: (gi, 0, 0)),
            pl.BlockSpec((nhd, dec_batch * win, HEAD_DIM), lambda gi, s: (gi, 0, 0)),
            pl.BlockSpec((nhd, 1, rep * CHUNK), lambda gi, s: (gi, 0, 0)),
            pl.BlockSpec((rows, nhd * rep * HEAD_DIM), lambda gi, s: (s, gi)),
            cast_block,
        ],
        out_specs=[pl.BlockSpec((rows, nhd * rep * HEAD_DIM), lambda gi, s: (s, gi)), cast_block],
        out_shape=[jax.ShapeDtypeStruct((m, aw), BF16), jax.ShapeDtypeStruct((dk, dn), BF16)],
        compiler_params=_params(("arbitrary",) * 2, _vmem_estimate(blocks, temps)),
        name="attention",
    )(q_hm, k_hm, v_hm, ks, vs, sink_rows, za, w_out)


def _ssm_disc_kernel(lre_ref, lim_ref, ldt_ref, are_ref, aim_ref, dre_ref, dim_ref, fre_ref, fim_ref):
    lr = jnp.minimum(lre_ref[...], LAMBDA_RE_MAX)
    li = lim_ref[...]
    dt = jnp.exp(ldt_ref[...])
    mag = jnp.exp(lr * dt)
    a_re = mag * jnp.cos(li * dt)
    a_im = mag * jnp.sin(li * dt)
    den = lr * lr + li * li
    nr = a_re - 1.0
    fre_ref[...] = (nr * lr + a_im * li) / den
    fim_ref[...] = (a_im * lr - nr * li) / den
    are_ref[...] = a_re
    aim_ref[...] = a_im
    for _ in range(SSM_POW_BITS):
        a_re, a_im = a_re * a_re - a_im * a_im, 2.0 * a_re * a_im
    dre_ref[...] = a_re
    dim_ref[...] = a_im


def _cmul(ar, ai, br, bi):
    return ar * br - ai * bi, ar * bi + ai * br


def _ssm_build_kernel(are_ref, aim_ref, dre_ref, dim_ref, fre_ref, fim_ref, btr_ref, bti_ref, cr_ref, ci_ref, dv_ref,
                      met_ref, ft_ref, a1_ref, a2_ref, a2s_ref, *, groups):
    lc = SSM_CHUNK * SSM_GROUP
    sub = lax.broadcasted_iota(jnp.int32, (SSM_GROUP, lc), 0)
    lane = lax.broadcasted_iota(jnp.int32, (SSM_GROUP, lc), 1)
    for gi in range(groups):
        row = slice(gi, gi + 1)
        a_re, a_im = are_ref[row, :], aim_ref[row, :]
        pw = [(jnp.ones_like(a_re), jnp.zeros_like(a_re))]
        for _ in range(SSM_CHUNK):
            pw.append(_cmul(pw[-1][0], pw[-1][1], a_re, a_im))
        c_re, c_im = cr_ref[gi], ci_ref[gi]
        wt = [_cmul(pr, pi, c_re, c_im) for pr, pi in pw]
        wt_re = jnp.concatenate([w[0] for w in wt[:SSM_CHUNK]], axis=0)
        wt_im = jnp.concatenate([w[1] for w in wt[:SSM_CHUNK]], axis=0)
        et_re = jnp.concatenate([w[0] for w in wt[1:]], axis=0)
        et_im = jnp.concatenate([w[1] for w in wt[1:]], axis=0)
        bb_re, bb_im = _cmul(fre_ref[row, :], fim_ref[row, :], btr_ref[gi], bti_ref[gi])
        r0 = lax.dot_general(jnp.concatenate([bb_re, bb_im], axis=1), jnp.concatenate([wt_re, -wt_im], axis=1),
                             (((1,), (1,)), ((), ())), preferred_element_type=F32, precision=lax.Precision.HIGHEST)
        r0 = r0 + jnp.where(sub == lane, dv_ref[gi], 0.0)
        rows = [r0] + [jnp.where(lane >= s * SSM_GROUP, pltpu.roll(r0, s * SSM_GROUP, 1), 0.0) for s in range(1, SSM_CHUNK)]
        mt = jnp.concatenate(rows, axis=0).T
        met_ref[gi] = jnp.concatenate([mt, et_re, -et_im], axis=1).astype(met_ref.dtype)
        fb = [_cmul(pw[SSM_CHUNK - 1 - s][0], pw[SSM_CHUNK - 1 - s][1], bb_re, bb_im) for s in range(SSM_CHUNK)]
        f_all = jnp.concatenate([jnp.concatenate([x[0] for x in fb], axis=0), jnp.concatenate([x[1] for x in fb], axis=0)], axis=1)
        ft_ref[gi] = f_all.T.astype(ft_ref.dtype)
        d_re, d_im = dre_ref[row, :], dim_ref[row, :]
        a1_ref[gi] = jnp.concatenate([d_re, d_re], axis=1)
        a2_ref[gi] = jnp.concatenate([-d_im, d_im], axis=1)
        a2s_ref[gi] = jnp.concatenate([d_im, -d_im], axis=1)


def _ssm_params(lambda_re, lambda_im, log_dt, b_re, b_im, c_re, c_im, d_skip):
    ng, p = lambda_re.shape
    lc = SSM_CHUNK * SSM_GROUP
    full = pl.BlockSpec((ng, p), lambda: (0, 0))
    disc = pl.pallas_call(
        _ssm_disc_kernel,
        in_specs=[full, full, pl.BlockSpec((ng, 1), lambda: (0, 0))],
        out_specs=[full] * 6,
        out_shape=[jax.ShapeDtypeStruct((ng, p), F32)] * 6,
        name="ssm_disc",
    )(lambda_re, lambda_im, log_dt.reshape(ng, 1))
    gb = _pick(ng, (8,))
    bt_re = jnp.swapaxes(b_re, 1, 2)
    bt_im = jnp.swapaxes(b_im, 1, 2)
    dvec = jnp.pad(d_skip, ((0, 0), (0, lc - SSM_GROUP))).reshape(ng, 1, lc)
    rows = pl.BlockSpec((gb, p), lambda i: (i, 0))
    mats = pl.BlockSpec((gb, SSM_GROUP, p), lambda i: (i, 0, 0))
    dec = pl.BlockSpec((gb, 1, 2 * p), lambda i: (i, 0, 0))
    blocks = ([((gb, LANES), F32)] * 6 + [((gb, SSM_GROUP, LANES), F32)] * 4 + [((gb, 8, lc), F32)]
              + [((gb, lc, lc + 2 * p), BF16), ((gb, 2 * p, lc), BF16)] + [((gb, 8, LANES), F32)] * 3)
    met, ft, a1, a2, a2s = pl.pallas_call(
        functools.partial(_ssm_build_kernel, groups=gb),
        grid=(ng // gb,),
        in_specs=[rows] * 6 + [mats] * 4 + [pl.BlockSpec((gb, 1, lc), lambda i: (i, 0, 0))],
        out_specs=[pl.BlockSpec((gb, lc, lc + 2 * p), lambda i: (i, 0, 0)), pl.BlockSpec((gb, 2 * p, lc), lambda i: (i, 0, 0)), dec, dec, dec],
        out_shape=[jax.ShapeDtypeStruct((ng, lc, lc + 2 * p), BF16), jax.ShapeDtypeStruct((ng, 2 * p, lc), BF16)]
        + [jax.ShapeDtypeStruct((ng, 1, 2 * p), F32)] * 3,
        compiler_params=_params(("arbitrary",), _vmem_estimate(blocks, [((lc, lc + 2 * p), F32)] * 8)),
        name="ssm_build",
    )(*disc, bt_re, bt_im, c_re, c_im, dvec)
    flat = lambda a: a.reshape(1, ng * 2 * p)
    return met, ft, flat(a1), flat(a2), flat(a2s)


def _chunk_blocks(nk):
    return [(k0, min(LANES, nk - k0)) for k0 in range(0, nk, LANES)]


def _pad_rows(x):
    n = x.shape[0]
    return x if n == LANES else jnp.concatenate([x, jnp.zeros((LANES - n, x.shape[1]), x.dtype)], axis=0)


def _ssm_inputs_phase(u_ref, ft_ref, rhs_ref, s_ref, *, nk, p):
    groups = LANES // SSM_GROUP
    blocks = _chunk_blocks(nk)
    xt = [[_pad_rows(u_ref[pl.ds(k0 * SSM_CHUNK + s, n, stride=SSM_CHUNK), :].astype(BF16)).T for k0, n in blocks]
          for s in range(SSM_CHUNK)]
    for gi in range(groups):
        rhs = jnp.concatenate([jnp.concatenate([xt[s][kb][gi * SSM_GROUP:(gi + 1) * SSM_GROUP, :] for kb in range(len(blocks))], axis=1)
                               for s in range(SSM_CHUNK)], axis=0)
        rhs_ref[gi] = rhs
        st = jnp.dot(ft_ref[gi], rhs, preferred_element_type=F32)
        for kb, (k0, n) in enumerate(blocks):
            s_ref[k0:k0 + n, gi * 2 * p:(gi + 1) * 2 * p] = st[:, kb * LANES:(kb + 1) * LANES].T[:n, :]


def _ssm_outputs_phase(rhs_ref, h_ref, met_ref, y_ref, *, nk, p):
    groups = LANES // SSM_GROUP
    blocks = _chunk_blocks(nk)
    zt = [[None] * groups for _ in range(SSM_CHUNK)]
    for gi in range(groups):
        ht = jnp.concatenate([_pad_rows(h_ref[k0:k0 + n, gi * 2 * p:(gi + 1) * 2 * p]).T for k0, n in blocks], axis=1)
        rhs = jnp.concatenate([rhs_ref[gi], ht.astype(BF16)], axis=0)
        yt = jnp.dot(met_ref[gi], rhs, preferred_element_type=F32)
        for t in range(SSM_CHUNK):
            zt[t][gi] = yt[t * SSM_GROUP:(t + 1) * SSM_GROUP, :]
    for t in range(SSM_CHUNK):
        z = jnp.concatenate(zt[t], axis=0)
        for kb, (k0, n) in enumerate(blocks):
            y_ref[pl.ds(k0 * SSM_CHUNK + t, n, stride=SSM_CHUNK), :] = z[:, kb * LANES:(kb + 1) * LANES].T[:n, :]


def _ssm_scan_phase(s_ref, h0_ref, a1_ref, a2_ref, a2s_ref, h_ref, fin_ref, ss_ref, *, batch, kp, dec_batch, ks, p):
    def swap_halves(x):
        lane = lax.broadcasted_iota(jnp.int32, x.shape, 1)
        return jnp.where(lane % (2 * p) < p, pltpu.roll(x, x.shape[1] - p, 1), pltpu.roll(x, p, 1))

    ss_ref[...] = swap_halves(s_ref[...])
    a1, a2, a2s = a1_ref[...], a2_ref[...], a2s_ref[...]
    zero = jnp.zeros_like(a1)

    def step(k, h, hs):
        h_ref[pl.ds(k, 1), :] = h
        s = s_ref[pl.ds(k, 1), :]
        ss = ss_ref[pl.ds(k, 1), :]
        return a1 * h + a2 * hs + s, a1 * hs + a2s * h + ss

    def body(k, carry):
        out = []
        for b in range(batch):
            out.extend(step(b * kp + k, carry[2 * b], carry[2 * b + 1]))
        return tuple(out)

    fin = lax.fori_loop(0, kp, body, (zero,) * (2 * batch))
    for b in range(batch):
        fin_ref[b:b + 1, :] = fin[2 * b]
    h0s_all = swap_halves(h0_ref[...])
    for b in range(dec_batch):
        h, hs = h0_ref[b:b + 1, :], h0s_all[b:b + 1, :]
        for k in range(ks):
            h, hs = step(batch * kp + b * ks + k, h, hs)
        fin_ref[batch + b:batch + b + 1, :] = h


def _ssm_kernel(u_ref, ft_ref, met_ref, h0_ref, a1_ref, a2_ref, a2s_ref, y_ref, fin_ref, rhs_ref, s_ref, ss_ref, h_ref,
                *, batch, kp, dec_batch, ks, p):
    nk = batch * kp + dec_batch * ks
    _ssm_inputs_phase(u_ref, ft_ref, rhs_ref, s_ref, nk=nk, p=p)
    _ssm_scan_phase(s_ref, h0_ref, a1_ref, a2_ref, a2s_ref, h_ref, fin_ref, ss_ref, batch=batch, kp=kp, dec_batch=dec_batch, ks=ks, p=p)
    _ssm_outputs_phase(rhs_ref, h_ref, met_ref, y_ref, nk=nk, p=p)


def _ssm(u, met, ft, a1, a2, a2s, h0, batch, kp, dec_batch, ks):
    m, sw = u.shape
    ng, p2, lc = ft.shape
    p = p2 // 2
    nk = m // SSM_CHUNK
    assert nk == batch * kp + dec_batch * ks and (nk % LANES) % 8 == 0
    nk_lanes = -(-nk // LANES) * LANES
    groups = LANES // SSM_GROUP
    lb = groups * p2
    nseq = batch + dec_batch
    u_spec = pl.BlockSpec((m, LANES), lambda i: (0, i))
    rowb = pl.BlockSpec((1, lb), lambda i: (0, i))
    blocks = [((m, LANES), F32)] * 2 + [((groups, p2, lc), BF16), ((groups, lc, lc + p2), BF16), ((nseq, lb), F32)]
    scratch = [((groups, lc, nk_lanes), BF16)] + [((nk, lb), F32)] * 3
    return pl.pallas_call(
        functools.partial(_ssm_kernel, batch=batch, kp=kp, dec_batch=dec_batch, ks=ks, p=p),
        grid=(ng // groups,),
        in_specs=[u_spec, pl.BlockSpec((groups, p2, lc), lambda i: (i, 0, 0)), pl.BlockSpec((groups, lc, lc + p2), lambda i: (i, 0, 0)),
                  pl.BlockSpec((dec_batch, lb), lambda i: (0, i)), rowb, rowb, rowb],
        out_specs=[u_spec, pl.BlockSpec((nseq, lb), lambda i: (0, i))],
        out_shape=[jax.ShapeDtypeStruct((m, sw), F32), jax.ShapeDtypeStruct((nseq, ng * p2), F32)],
        scratch_shapes=[pltpu.VMEM(s, d) for s, d in scratch],
        compiler_params=_params(("arbitrary",), _vmem_estimate(blocks, scratch + [((lc + p2, nk_lanes), BF16), ((lc, nk_lanes), F32)])),
        name="ssm",
    )(u, ft, met, h0, a1, a2, a2s)


def _weight_stream_scratch(k, tn, n_chunks):
    assert k % (8 * n_chunks) == 0
    shapes = [((2, k, tn), BF16), ((2, k // n_chunks, tn), F32)]
    return shapes, [pltpu.VMEM(s, t) for s, t in shapes] + [pltpu.SemaphoreType.DMA((2,))]


def _glu_kernel(y_ref, w_hbm, ba_ref, bg_ref, zs_ref, o_ref, wa_buf, wa_stage, wa_sem, wg_buf, wg_stage, wg_sem, *, n_blocks, n_chunks):
    tn = o_ref.shape[1]
    stream = functools.partial(_stream_weights, w_hbm, n_blocks=n_blocks, n_chunks=n_chunks)
    wa_ref = stream(wa_buf, wa_stage, wa_sem, col_of_block=lambda blk: blk * tn)
    wg_ref = stream(wg_buf, wg_stage, wg_sem, col_of_block=lambda blk: (n_blocks + blk) * tn)
    y = y_ref[...].astype(BF16)
    a = jnp.dot(y, wa_ref[...], preferred_element_type=F32) + ba_ref[...]
    g = jnp.dot(y, wg_ref[...], preferred_element_type=F32) + bg_ref[...]
    o_ref[...] = (a * _sigmoid(g) * zs_ref[...].astype(F32)).astype(o_ref.dtype)


def _glu(y, w_glu, b_glu, acts, zs_col0, tm, tn):
    m, sw = y.shape
    nb, n_chunks = sw // tn, m // tm
    zb0 = zs_col0 // tn
    shapes, scratch = _weight_stream_scratch(sw, tn, n_chunks)
    blocks = [((tm, sw), F32), ((tm, tn), BF16), ((tm, tn), BF16)]
    return pl.pallas_call(
        functools.partial(_glu_kernel, n_blocks=nb, n_chunks=n_chunks),
        grid=(nb, n_chunks),
        in_specs=[
            pl.BlockSpec((tm, sw), lambda j, i: (i, 0)),
            pl.BlockSpec(memory_space=pl.ANY),
            pl.BlockSpec((1, tn), lambda j, i: (0, j)),
            pl.BlockSpec((1, tn), lambda j, i: (0, nb + j)),
            pl.BlockSpec((tm, tn), lambda j, i: (i, zb0 + j)),
        ],
        out_specs=pl.BlockSpec((tm, tn), lambda j, i: (i, j)),
        out_shape=jax.ShapeDtypeStruct((m, sw), BF16),
        scratch_shapes=scratch * 2,
        compiler_params=_params(("arbitrary", "arbitrary"), _vmem_estimate(blocks, shapes * 2 + [((tm, tn), F32)] * 4)),
        name="glu",
    )(y, w_glu, b_glu, b_glu, acts)


def _merge_kernel(xa_ref, xs_ref, wpa_hbm, wps_hbm, ga_ref, gs_ref, o_ref, wa_buf, wa_stage, wa_sem, ws_buf, ws_stage, ws_sem,
                  *, n_blocks, n_chunks):
    tn = o_ref.shape[1]
    col = lambda blk: blk * tn
    wpa_ref = _stream_weights(wpa_hbm, wa_buf, wa_stage, wa_sem, col_of_block=col, n_blocks=n_blocks, n_chunks=n_chunks)
    wps_ref = _stream_weights(wps_hbm, ws_buf, ws_stage, ws_sem, col_of_block=col, n_blocks=n_blocks, n_chunks=n_chunks)
    br_a = jnp.dot(xa_ref[...], wpa_ref[...], preferred_element_type=F32)
    br_s = jnp.dot(xs_ref[...], wps_ref[...], preferred_element_type=F32)
    o_ref[...] = (ga_ref[...].astype(F32) * br_a + gs_ref[...].astype(F32) * br_s).astype(o_ref.dtype)


def _merge(xa, xs, w_pa, w_ps, acts, gate_col0, tm, tn):
    m, aw = xa.shape
    sw = xs.shape[1]
    d = w_pa.shape[1]
    nb, n_chunks = d // tn, m // tm
    gb0 = gate_col0 // tn
    shapes_a, scratch_a = _weight_stream_scratch(aw, tn, n_chunks)
    shapes_s, scratch_s = _weight_stream_scratch(sw, tn, n_chunks)
    blocks = [((tm, aw), BF16), ((tm, sw), BF16)] + [((tm, tn), BF16)] * 3
    return pl.pallas_call(
        functools.partial(_merge_kernel, n_blocks=nb, n_chunks=n_chunks),
        grid=(nb, n_chunks),
        in_specs=[
            pl.BlockSpec((tm, aw), lambda j, i: (i, 0)),
            pl.BlockSpec((tm, sw), lambda j, i: (i, 0)),
            pl.BlockSpec(memory_space=pl.ANY),
            pl.BlockSpec(memory_space=pl.ANY),
            pl.BlockSpec((tm, tn), lambda j, i: (i, gb0 + j)),
            pl.BlockSpec((tm, tn), lambda j, i: (i, gb0 + nb + j)),
        ],
        out_specs=pl.BlockSpec((tm, tn), lambda j, i: (i, j)),
        out_shape=jax.ShapeDtypeStruct((m, d), BF16),
        scratch_shapes=scratch_a + scratch_s,
        compiler_params=_params(("arbitrary", "arbitrary"), _vmem_estimate(blocks, shapes_a + shapes_s + [((tm, tn), F32)] * 4)),
        name="merge",
    )(xa, xs, w_pa, w_ps, acts, acts)


def _out_kernel(mg_ref, w_ref, x_ref, g_ref, o_ref, ssq_ref, *, n_col_blocks, tn):
    j = pl.program_id(1)
    blk = x_ref[...] + jnp.dot(mg_ref[...], w_ref[...], preferred_element_type=F32)
    o_ref[:, pl.ds(pl.multiple_of(j * tn, tn), tn)] = blk
    part = jnp.sum(blk * blk, axis=-1, keepdims=True)

    @pl.when(j == 0)
    def _():
        ssq_ref[...] = part

    @pl.when(j > 0)
    def _():
        ssq_ref[...] += part

    @pl.when(j == n_col_blocks - 1)
    def _():
        scale = lax.rsqrt(ssq_ref[...] / (n_col_blocks * tn) + NORM_EPS)
        o_ref[...] = o_ref[...] * scale * g_ref[...]


def _out(merged, row_block0, w_out, x, final_g, tm, tn):
    mx, d = x.shape
    nb = d // tn
    blocks = [((tm, d), BF16), ((d, tn), BF16), ((tm, tn), F32), ((tm, d), F32)]
    return pl.pallas_call(
        functools.partial(_out_kernel, n_col_blocks=nb, tn=tn),
        grid=(mx // tm, nb),
        in_specs=[
            pl.BlockSpec((tm, d), lambda i, j: (row_block0 + i, 0)),
            pl.BlockSpec((d, tn), lambda i, j: (0, j)),
            pl.BlockSpec((tm, tn), lambda i, j: (i, j)),
            pl.BlockSpec((1, d), lambda i, j: (0, 0)),
        ],
        out_specs=pl.BlockSpec((tm, d), lambda i, j: (i, 0)),
        out_shape=jax.ShapeDtypeStruct((mx, d), F32),
        scratch_shapes=[pltpu.VMEM((tm, 1), F32)],
        compiler_params=_params(("arbitrary", "arbitrary"), _vmem_estimate(blocks, [((tm, tn), F32)] * 3 + [((tm, LANES), F32)])),
        name="out_norm",
    )(merged, w_out, x, final_g.reshape(1, d))


def _rope_tables(positions):
    half = HEAD_DIM // 2
    inv_freq = ROPE_THETA ** (-jnp.arange(half, dtype=F32) / half)
    ang = positions.astype(F32)[:, None] * inv_freq[None, :]
    cos, sin = jnp.cos(ang), jnp.sin(ang)
    reps = LANES // HEAD_DIM
    return jnp.tile(jnp.concatenate([cos, cos], axis=1), (1, reps)), jnp.tile(jnp.concatenate([-sin, sin], axis=1), (1, reps))


def kernel(x_prompt, x_sample, cache_k, cache_v, state_ssm_re, state_ssm_im, norm_g, w_in, sink, lambda_re, lambda_im,
           log_dt, b_re, b_im, c_re, c_im, d_skip, w_glu, b_glu, w_pa, w_ps, w_out, final_g):
    depth = norm_g.shape[0]
    assert depth == 1, "one trunk layer"
    batch, seq, d = x_prompt.shape
    dec_batch, dec_seq, _ = x_sample.shape
    aw = w_pa.shape[1]
    sw = w_ps.shape[1]
    nh = aw // HEAD_DIM
    g = max(1, nh // GQA_GROUPING)
    rep = nh // g
    kvw = g * HEAD_DIM
    ng, p = lambda_re.shape[1:]
    assert dec_seq == CHUNK and cache_k.shape[2] == WINDOW and seq % (2 * CHUNK) == 0
    assert kvw % LANES == 0 and rep % 2 == 0 and sw == ng * SSM_GROUP and (1 << SSM_POW_BITS) == SSM_CHUNK
    assert 2 * p == LANES and sw % LANES == 0
    mp, ms = batch * seq, dec_batch * dec_seq
    m = mp + ms
    widths = (aw, 2 * kvw, aw, sw, sw, 2 * d)
    assert sum(widths) == w_in.shape[2]
    c_q, c_kv, c_za, c_u, c_zs, c_gate = (sum(widths[:n]) for n in range(len(widths)))
    tm = _pick(math.gcd(mp, ms), (512, 256, 128))
    tmp = max(t for t in range(16, 1153, 16) if m % t == 0)
    tn = _pick(math.gcd(c_kv, c_za, c_u, c_zs, c_gate, 2 * d), (512, 256, 128))

    xp = x_prompt.reshape(mp, d)
    xs = x_sample.reshape(ms, d)
    w_in2 = w_in.reshape(d, w_in.shape[2])
    positions = jnp.concatenate([jnp.tile(jnp.arange(seq, dtype=jnp.int32), batch),
                                 jnp.tile(PAST_LEN + jnp.arange(dec_seq, dtype=jnp.int32), dec_batch)])
    cos, sin = _rope_tables(positions)

    h, k_f, v_f, k_hm, v_hm = _norm_kv(xp, xs, norm_g[0], w_in2, cos, sin, c_kv, kvw, tm)
    q_hm = _proj_q(h, w_in2, cos, sin, aw, tmp, tn)
    tna = _pick(math.gcd(aw, sw), (1024, 512, 256))
    acts = _proj_act(h, w_in2, [(c_za, aw, "silu"), (c_zs, sw, "silu"), (c_gate, 2 * d, "sigmoid")], tmp, tna, "proj_acts")
    u = _proj_act(h, w_in2, [(c_u, sw, "none")], tmp, tna, "proj_u", out_dtype=F32)

    to_heads = lambda c: jnp.transpose(c[0], (2, 0, 1, 3)).astype(BF16)
    new_rows = lambda a: a[:, mp:].reshape(g, dec_batch, dec_seq, HEAD_DIM)
    ks = jnp.concatenate([to_heads(cache_k), new_rows(k_hm)], axis=2).reshape(g, dec_batch * (WINDOW + CHUNK), HEAD_DIM)
    vs = jnp.concatenate([to_heads(cache_v), new_rows(v_hm)], axis=2).reshape(g, dec_batch * (WINDOW + CHUNK), HEAD_DIM)
    sink_rows = jnp.repeat(sink[0].reshape(g, rep), CHUNK, axis=1).reshape(g, 1, rep * CHUNK)
    xa, w_out_b = _attention(q_hm, k_hm, v_hm, ks, vs, sink_rows, acts, w_out.reshape(d, d), batch, seq, dec_batch, aw)

    met, ft, a1, a2, a2s = _ssm_params(lambda_re[0], lambda_im[0], log_dt[0], b_re[0], b_im[0], c_re[0], c_im[0], d_skip[0])
    h0 = jnp.concatenate([state_ssm_re[0], state_ssm_im[0]], axis=-1).reshape(dec_batch, ng * 2 * p)
    y, fin = _ssm(u, met, ft, a1, a2, a2s, h0, batch, seq // SSM_CHUNK, dec_batch, dec_seq // SSM_CHUNK)

    tmm = tmp
    x_ssm = _glu(y, w_glu.reshape(sw, 2 * sw), b_glu.reshape(1, 2 * sw), acts, aw, tmm, tna)
    merged = _merge(xa, x_ssm, w_pa.reshape(aw, d), w_ps.reshape(sw, d), acts, aw + sw, tmm, tna)
    tno = _pick(d, (1024, 512, 256))
    y_prompt = _out(merged, 0, w_out_b, xp, final_g, tm, tno).reshape(batch, seq, d)
    y_sample = _out(merged, mp // tm, w_out_b, xs, final_g, tm, tno).reshape(dec_batch, dec_seq, d)

    keep = min(WINDOW, seq)
    last_rows = lambda a: jnp.stack([a[(b + 1) * seq - keep:(b + 1) * seq] for b in range(batch)]).reshape(1, batch, keep, g, HEAD_DIM)
    dec_rows = lambda a: a[mp:].reshape(dec_batch, dec_seq, g, HEAD_DIM)[None]
    fin = fin.reshape(batch + dec_batch, ng, 2, p)
    return (y_prompt, y_sample, last_rows(k_f), last_rows(v_f), fin[:batch, :, 0][None], fin[:batch, :, 1][None],
            dec_rows(k_f), dec_rows(v_f), fin[batch:, :, 0][None], fin[batch:, :, 1][None])
```

```python
import functools
import math

import jax
import jax.numpy as jnp
from jax import lax
from jax.experimental import pallas as pl
from jax.experimental.pallas import tpu as pltpu

CHUNK = 64
WINDOW = 128
HEAD_DIM = 64
GQA_GROUPING = 8
SSM_GROUP = 16
PAST_LEN = 1024
ROPE_THETA = 10000.0
NORM_EPS = 1e-5
LAMBDA_RE_MAX = -1e-4
LOG2_E = math.log2(math.e)

SSM_CHUNK = 16
SSM_POW_BITS = 4
LANES = 128
V7X_VMEM_BYTES = 64 * 1024 * 1024
BF16 = jnp.bfloat16
F32 = jnp.float32


def _pick(n, prefs):
    for p in prefs:
        if n % p == 0:
            return p
    raise ValueError(f"no tile in {prefs} divides {n}")


def _params(sem, vmem_bytes):
    limit = min(int(vmem_bytes), V7X_VMEM_BYTES - 4 * 1024 * 1024)
    return pltpu.CompilerParams(dimension_semantics=sem, vmem_limit_bytes=limit)


def _sigmoid(x):
    return 0.5 * jnp.tanh(0.5 * x) + 0.5


def _nbytes(shape, dtype):
    return math.prod(shape) * jnp.dtype(dtype).itemsize


def _vmem_estimate(blocks, temps=()):
    return 2 * (2 * sum(_nbytes(s, d) for s, d in blocks) + sum(_nbytes(s, d) for s, d in temps))


def _norm_kv_kernel(xp_ref, xs_ref, g_ref, w_ref, cos_ref, sin_ref, h_ref, kf_ref, vf_ref, kh_ref, vh_ref, wb_ref, *, n_prompt_blocks, kvw):
    i = pl.program_id(0)

    @pl.when(i == 0)
    def _():
        wb_ref[...] = w_ref[...].astype(wb_ref.dtype)

    tm = h_ref.shape[0]
    n_parts = 2 if tm % 32 == 0 else 1
    for r in range(n_parts):
        rows = slice(r * (tm // n_parts), (r + 1) * (tm // n_parts))
        x = jnp.where(i < n_prompt_blocks, xp_ref[rows, :], xs_ref[rows, :])
        y = x * lax.rsqrt(jnp.mean(x * x, axis=-1, keepdims=True) + NORM_EPS)
        h = (y * g_ref[...]).astype(h_ref.dtype)
        h_ref[rows, :] = h
        acc = jnp.dot(h, wb_ref[...], preferred_element_type=F32)
        k = _rope(acc[:, :kvw], cos_ref.at[rows, :], sin_ref.at[rows, :])
        v = acc[:, kvw:]
        kf_ref[rows, :] = k
        vf_ref[rows, :] = v
        _store_heads(kh_ref, rows, k)
        _store_heads(vh_ref, rows, v)


def _norm_kv(xp, xs, norm_g, w, cos, sin, col0, kvw, tm):
    mp, d = xp.shape
    ms = xs.shape[0]
    npb, nsb = mp // tm, ms // tm
    m = mp + ms
    g = kvw // HEAD_DIM
    tn = 2 * kvw
    assert col0 % tn == 0
    once = pl.Buffered(1)
    tab = pl.BlockSpec((tm, LANES), lambda i: (i, 0))
    flat = pl.BlockSpec((tm, kvw), lambda i: (i, 0))
    heads = pl.BlockSpec((g, tm, HEAD_DIM), lambda i: (0, i, 0))
    xs_spec = (pl.BlockSpec((tm, d), lambda i: (0, 0), pipeline_mode=once) if nsb == 1
               else pl.BlockSpec((tm, d), lambda i: (jnp.maximum(i - npb, 0), 0)))
    blocks = [((tm, d), F32), ((tm, d), BF16), ((tm, 2 * tn), F32), ((tm, 2 * tn), BF16), ((tm, 2 * LANES), F32)]
    temps = [((tm, d), F32)] * 2 + [((d, tn), F32), ((d, tn), BF16)]
    return pl.pallas_call(
        functools.partial(_norm_kv_kernel, n_prompt_blocks=npb, kvw=kvw),
        grid=(npb + nsb,),
        in_specs=[
            pl.BlockSpec((tm, d), lambda i: (jnp.minimum(i, npb - 1), 0)),
            xs_spec,
            pl.BlockSpec((1, d), lambda i: (0, 0)),
            pl.BlockSpec((d, tn), lambda i: (0, col0 // tn), pipeline_mode=once),
            tab, tab,
        ],
        out_specs=[pl.BlockSpec((tm, d), lambda i: (i, 0)), flat, flat, heads, heads],
        out_shape=[jax.ShapeDtypeStruct((m, d), BF16)] + [jax.ShapeDtypeStruct((m, kvw), F32)] * 2
        + [jax.ShapeDtypeStruct((g, m, HEAD_DIM), BF16)] * 2,
        scratch_shapes=[pltpu.VMEM((d, tn), BF16)],
        compiler_params=_params(("arbitrary",), _vmem_estimate(blocks, temps)),
        name="norm_kv",
    )(xp, xs, norm_g.reshape(1, d), w, cos, sin)


def _rope(acc, cos_ref, sin_ref):
    tm, tn = acc.shape
    reps = tn // LANES
    cos = jnp.tile(cos_ref[...], (1, reps))
    sin = jnp.tile(sin_ref[...], (1, reps))
    lane = lax.broadcasted_iota(jnp.int32, (tm, tn), 1)
    low = (lane % HEAD_DIM) < (HEAD_DIM // 2)
    partner = jnp.where(low, pltpu.roll(acc, tn - HEAD_DIM // 2, 1), pltpu.roll(acc, HEAD_DIM // 2, 1))
    return acc * cos + partner * sin


def _store_heads(o_ref, rows, val):
    for h in range(val.shape[1] // HEAD_DIM):
        o_ref[h, rows, :] = val[:, h * HEAD_DIM:(h + 1) * HEAD_DIM].astype(o_ref.dtype)


def _cast_weight(w_ref, wb_ref):
    @pl.when(pl.program_id(1) == 0)
    def _():
        wb_ref[...] = w_ref[...].astype(wb_ref.dtype)


def _stream_weights(w_hbm, wbuf_ref, stage_ref, sem_ref, *, col_of_block, n_blocks, n_chunks):
    d, tn = wbuf_ref.shape[1:]
    ck = d // n_chunks
    total = n_blocks * n_chunks
    j = pl.program_id(0)
    t = j * n_chunks + pl.program_id(1)

    def aligned(x, a):
        return x if isinstance(x, int) else pl.multiple_of(x, a)

    def chunk_copy(blk, chunk, slot):
        src = w_hbm.at[pl.ds(aligned(chunk * ck, ck), ck), pl.ds(aligned(col_of_block(blk), LANES), tn)]
        return pltpu.make_async_copy(src, stage_ref.at[slot], sem_ref.at[slot])

    def generation(s):
        s = jnp.asarray(s, jnp.int32)
        k = s + n_chunks - 1
        past = (k > total - 1).astype(jnp.int32)
        k = jnp.minimum(k, total - 1)
        blk, chunk = k // n_chunks, k % n_chunks
        return chunk_copy(blk, chunk, s % 2), (blk + past) % 2, chunk

    def land(copy, half, chunk, slot):
        copy.wait()
        wbuf_ref[half, pl.ds(aligned(chunk * ck, ck), ck), :] = stage_ref[slot].astype(wbuf_ref.dtype)

    @pl.when(t == 0)
    def _():
        slot_of = lambda c: (c + n_chunks - 1) % 2
        first = [chunk_copy(0, c, slot_of(c)) for c in range(n_chunks)]
        first[0].start()
        for c in range(n_chunks - 1):
            first[c + 1].start()
            land(first[c], 0, c, slot_of(c))

    @pl.when(t + 1 < total)
    def _():
        generation(t + 1)[0].start()

    copy, half, chunk = generation(t)
    land(copy, half, chunk, t % 2)
    return wbuf_ref.at[j % 2]


def _row_parts(tm, n=2):
    n = n if tm % (16 * n) == 0 and tm >= 1024 else 1
    return [slice(r * (tm // n), (r + 1) * (tm // n)) for r in range(n)]


def _proj_q_kernel(h_ref, w_ref, cos_ref, sin_ref, q_ref, wb_ref):
    _cast_weight(w_ref, wb_ref)
    for rows in _row_parts(h_ref.shape[0], 4):
        acc = jnp.dot(h_ref[rows, :], wb_ref[...], preferred_element_type=F32)
        _store_heads(q_ref, rows, _rope(acc, cos_ref.at[rows, :], sin_ref.at[rows, :]) * (HEAD_DIM ** -0.5 * LOG2_E))


def _proj_act_kernel(h_ref, w_hbm, o_ref, wbuf_ref, stage_ref, sem_ref, *, segments, n_chunks):
    tn = o_ref.shape[1]
    n_blocks = sum(s[1] for s in segments)
    in_seg = lambda blk, s: (blk >= s[0]) & (blk < s[0] + s[1])

    def col_of_block(blk):
        if isinstance(blk, int):
            return next(s[2] + (blk - s[0]) * tn for s in segments if s[0] <= blk < s[0] + s[1])
        return sum(jnp.where(in_seg(blk, s), s[2] + (blk - s[0]) * tn, 0) for s in segments)

    wb_ref = _stream_weights(w_hbm, wbuf_ref, stage_ref, sem_ref, col_of_block=col_of_block, n_blocks=n_blocks, n_chunks=n_chunks)
    acts = {s[3] for s in segments}
    assert acts in ({"none"}, {"silu"}, {"sigmoid"}, {"silu", "sigmoid"})
    j = pl.program_id(0)
    for rows in _row_parts(h_ref.shape[0]):
        acc = jnp.dot(h_ref[rows, :], wb_ref[...], preferred_element_type=F32)
        if acts != {"none"}:
            sig = _sigmoid(acc)
            if acts == {"silu"}:
                acc = acc * sig
            elif acts == {"sigmoid"}:
                acc = sig
            else:
                is_silu = functools.reduce(lambda a, b: a | b, [in_seg(j, s) for s in segments if s[3] == "silu"])
                acc = jnp.where(is_silu, acc * sig, sig)
        o_ref[rows, :] = acc.astype(o_ref.dtype)


def _proj_specs(d, tm, tn, col0):
    assert col0 % tn == 0
    cb0 = col0 // tn
    return [pl.BlockSpec((tm, d), lambda j, i: (i, 0)), pl.BlockSpec((d, tn), lambda j, i: (0, cb0 + j))]


def _proj_vmem(d, tm, tn, outs):
    return _vmem_estimate([((tm, d), BF16), ((d, tn), F32)] + outs, [((tm, tn), F32)] * 3 + [((d, tn), BF16)])


def _proj_q(h, w, cos, sin, aw, tm, tn):
    m, d = h.shape
    nh = aw // HEAD_DIM
    tab = pl.BlockSpec((tm, LANES), lambda j, i: (i, 0))
    return pl.pallas_call(
        _proj_q_kernel,
        grid=(aw // tn, m // tm),
        in_specs=_proj_specs(d, tm, tn, 0) + [tab, tab],
        out_specs=pl.BlockSpec((tn // HEAD_DIM, tm, HEAD_DIM), lambda j, i: (j, i, 0)),
        out_shape=jax.ShapeDtypeStruct((nh, m, HEAD_DIM), BF16),
        scratch_shapes=[pltpu.VMEM((d, tn), BF16)],
        compiler_params=_params(("arbitrary", "arbitrary"), _proj_vmem(d, tm, tn, [((tm, 2 * tn), BF16), ((tm, 2 * LANES), F32)])),
        name="proj_q",
    )(h, w, cos, sin)


def _proj_act(h, w, runs, tm, tn, name, out_dtype=BF16):
    m, d = h.shape
    n_chunks = m // tm
    segments, b0 = [], 0
    for col0, width, act in runs:
        assert width % tn == 0 and col0 % LANES == 0
        segments.append((b0, width // tn, col0, act))
        b0 += width // tn
    n_blocks, ncols = b0, b0 * tn
    assert d % (8 * n_chunks) == 0
    scratch = [((2, d, tn), BF16), ((2, d // n_chunks, tn), F32)]
    return pl.pallas_call(
        functools.partial(_proj_act_kernel, segments=tuple(segments), n_chunks=n_chunks),
        grid=(n_blocks, n_chunks),
        in_specs=[pl.BlockSpec((tm, d), lambda j, i: (i, 0)), pl.BlockSpec(memory_space=pl.ANY)],
        out_specs=pl.BlockSpec((tm, tn), lambda j, i: (i, j)),
        out_shape=jax.ShapeDtypeStruct((m, ncols), out_dtype),
        scratch_shapes=[pltpu.VMEM(s, t) for s, t in scratch] + [pltpu.SemaphoreType.DMA((2,))],
        compiler_params=_params(("arbitrary", "arbitrary"),
                                _vmem_estimate([((tm, d), BF16), ((tm, tn), out_dtype)], scratch + [((tm // len(_row_parts(tm)), tn), F32)] * 3)),
        name=name,
    )(h, w)


def _attn_weights(q, kw, sink, n_valid):
    st = lax.dot_general(kw, q, (((1,), (1,)), ((), ())), preferred_element_type=F32)
    if n_valid is not None:
        row = lax.broadcasted_iota(jnp.int32, st.shape, 0)
        st = jnp.where(row < n_valid, st, -jnp.inf)
    m = jnp.maximum(jnp.max(st, axis=0, keepdims=True), sink)
    e = jnp.exp2(st - m)
    return e.astype(BF16), jnp.sum(e, axis=0, keepdims=True) + jnp.exp2(sink - m)


def _attn_values(vw, e, denom):
    ot = jnp.dot(vw.astype(F32).T.astype(BF16), e, preferred_element_type=F32) / denom
    return ot.T


def _store_unit(o_ref, za_ref, row0, col0, o, rep):
    for r in range(0, rep, 2):
        pair = jnp.concatenate([o[r * CHUNK:(r + 1) * CHUNK], o[(r + 1) * CHUNK:(r + 2) * CHUNK]], axis=1)
        cols = slice(col0 + r * HEAD_DIM, col0 + (r + 2) * HEAD_DIM)
        gate = za_ref[row0:row0 + CHUNK, cols].astype(F32)
        o_ref[row0:row0 + CHUNK, cols] = (pair * gate).astype(o_ref.dtype)


def _attn_kernel(q_ref, k_ref, v_ref, ks_ref, vs_ref, sink_ref, za_ref, wo_ref, o_ref, wob_ref,
                 *, n_units, n_heads, rep, steps_per_stream, n_prompt_steps, n_cast_steps):
    step = pl.program_id(1)
    win = WINDOW + CHUNK

    @pl.when(pl.program_id(0) * (n_prompt_steps + 1) + step < n_cast_steps)
    def _():
        wob_ref[...] = wo_ref[...].astype(wob_ref.dtype)

    def run(windows):
        staged = []
        for hd, c, kw, vw, n_valid in windows:
            q = q_ref[hd * rep:(hd + 1) * rep, c * CHUNK:(c + 1) * CHUNK, :].reshape(rep * CHUNK, HEAD_DIM)
            staged.append((hd, c, vw) + _attn_weights(q, kw, sink_ref[hd] * LOG2_E, n_valid))
        for hd, c, vw, e, denom in staged:
            _store_unit(o_ref, za_ref, c * CHUNK, hd * rep * HEAD_DIM, _attn_values(vw, e, denom), rep)

    @pl.when(step < n_prompt_steps)
    def _():
        cb = step % steps_per_stream
        windows = []
        for hd in range(n_heads):
            for c in range(n_units):
                start = pl.multiple_of(jnp.maximum(cb * n_units + c - WINDOW // CHUNK, 0) * CHUNK, CHUNK)
                n_valid = jnp.where(cb == 0, (c + 1) * CHUNK, win) if c < WINDOW // CHUNK else None
                windows.append((hd, c, k_ref[hd, pl.ds(start, win), :], v_ref[hd, pl.ds(start, win), :], n_valid))
        run(windows)

    @pl.when(step == n_prompt_steps)
    def _():
        run([(hd, b, ks_ref[hd, b * win:(b + 1) * win, :], vs_ref[hd, b * win:(b + 1) * win, :], None)
             for hd in range(n_heads) for b in range(n_units)])


def _attention(q_hm, k_hm, v_hm, ks, vs, sink_rows, za, w_out, batch, seq, dec_batch, aw):
    nh, m, _ = q_hm.shape
    g = k_hm.shape[0]
    rep = nh // g
    win = WINDOW + CHUNK
    n_units = dec_batch
    rows = n_units * CHUNK
    assert seq % rows == 0 and n_units >= WINDOW // CHUNK
    steps_per_stream = seq // rows
    n_prompt_steps = batch * steps_per_stream
    nhd = 2 if g % 2 == 0 else 1
    blocks = ([((nhd * rep, rows, LANES), BF16)] + [((nhd * seq, LANES), BF16)] * 2 + [((nhd * dec_batch * win, LANES), BF16)] * 2
              + [((rows, nhd * rep * HEAD_DIM), BF16)] * 2)
    temps = [((rep * CHUNK, 2 * LANES), F32)] * (4 * n_units * nhd)
    stream = lambda gi, s: (gi, jnp.minimum(s // steps_per_stream, batch - 1), 0)
    dk, dn = w_out.shape
    n_steps = (g // nhd) * (n_prompt_steps + 1)
    n_cast_steps = max(c for c in (32, 16, 8) if c <= n_steps)
    assert dk % (16 * n_cast_steps) == 0
    cast_rows = dk // n_cast_steps
    cast_block = pl.BlockSpec((cast_rows, dn), lambda gi, s: (jnp.minimum(gi * (n_prompt_steps + 1) + s, n_cast_steps - 1), 0))
    blocks += [((cast_rows, dn), F32), ((cast_rows, dn), BF16)]
    return pl.pallas_call(
        functools.partial(_attn_kernel, n_units=n_units, n_heads=nhd, rep=rep, steps_per_stream=steps_per_stream,
                          n_prompt_steps=n_prompt_steps, n_cast_steps=n_cast_steps),
        grid=(g // nhd, n_prompt_steps + 1),
        in_specs=[
            pl.BlockSpec((nhd * rep, rows, HEAD_DIM), lambda gi, s: (gi, s, 0)),
            pl.BlockSpec((nhd, seq, HEAD_DIM), stream),
            pl.BlockSpec((nhd, seq, HEAD_DIM), stream),
            pl.BlockSpec((nhd, dec_batch * win, HEAD_DIM), lambda gi, s: (gi, 0, 0)),
            pl.BlockSpec((nhd, dec_batch * win, HEAD_DIM), lambda gi, s: (gi, 0, 0)),
            pl.BlockSpec((nhd, 1, rep * CHUNK), lambda gi, s: (gi, 0, 0)),
            pl.BlockSpec((rows, nhd * rep * HEAD_DIM), lambda gi, s: (s, gi)),
            cast_block,
        ],
        out_specs=[pl.BlockSpec((rows, nhd * rep * HEAD_DIM), lambda gi, s: (s, gi)), cast_block],
        out_shape=[jax.ShapeDtypeStruct((m, aw), BF16), jax.ShapeDtypeStruct((dk, dn), BF16)],
        compiler_params=_params(("arbitrary",) * 2, _vmem_estimate(blocks, temps)),
        name="attention",
    )(q_hm, k_hm, v_hm, ks, vs, sink_rows, za, w_out)


def _ssm_disc_kernel(lre_ref, lim_ref, ldt_ref, are_ref, aim_ref, dre_ref, dim_ref, fre_ref, fim_ref):
    lr = jnp.minimum(lre_ref[...], LAMBDA_RE_MAX)
    li = lim_ref[...]
    dt = jnp.exp(ldt_ref[...])
    mag = jnp.exp(lr * dt)
    a_re = mag * jnp.cos(li * dt)
    a_im = mag * jnp.sin(li * dt)
    den = lr * lr + li * li
    nr = a_re - 1.0
    fre_ref[...] = (nr * lr + a_im * li) / den
    fim_ref[...] = (a_im * lr - nr * li) / den
    are_ref[...] = a_re
    aim_ref[...] = a_im
    for _ in range(SSM_POW_BITS):
        a_re, a_im = a_re * a_re - a_im * a_im, 2.0 * a_re * a_im
    dre_ref[...] = a_re
    dim_ref[...] = a_im


def _cmul(ar, ai, br, bi):
    return ar * br - ai * bi, ar * bi + ai * br


def _ssm_build_kernel(are_ref, aim_ref, dre_ref, dim_ref, fre_ref, fim_ref, btr_ref, bti_ref, cr_ref, ci_ref, dv_ref,
                      met_ref, ft_ref, a1_ref, a2_ref, a2s_ref, *, groups):
    lc = SSM_CHUNK * SSM_GROUP
    sub = lax.broadcasted_iota(jnp.int32, (SSM_GROUP, lc), 0)
    lane = lax.broadcasted_iota(jnp.int32, (SSM_GROUP, lc), 1)
    for gi in range(groups):
        row = slice(gi, gi + 1)
        a_re, a_im = are_ref[row, :], aim_ref[row, :]
        pw = [(jnp.ones_like(a_re), jnp.zeros_like(a_re))]
        for _ in range(SSM_CHUNK):
            pw.append(_cmul(pw[-1][0], pw[-1][1], a_re, a_im))
        c_re, c_im = cr_ref[gi], ci_ref[gi]
        wt = [_cmul(pr, pi, c_re, c_im) for pr, pi in pw]
        wt_re = jnp.concatenate([w[0] for w in wt[:SSM_CHUNK]], axis=0)
        wt_im = jnp.concatenate([w[1] for w in wt[:SSM_CHUNK]], axis=0)
        et_re = jnp.concatenate([w[0] for w in wt[1:]], axis=0)
        et_im = jnp.concatenate([w[1] for w in wt[1:]], axis=0)
        bb_re, bb_im = _cmul(fre_ref[row, :], fim_ref[row, :], btr_ref[gi], bti_ref[gi])
        r0 = lax.dot_general(jnp.concatenate([bb_re, bb_im], axis=1), jnp.concatenate([wt_re, -wt_im], axis=1),
                             (((1,), (1,)), ((), ())), preferred_element_type=F32, precision=lax.Precision.HIGHEST)
        r0 = r0 + jnp.where(sub == lane, dv_ref[gi], 0.0)
        rows = [r0] + [jnp.where(lane >= s * SSM_GROUP, pltpu.roll(r0, s * SSM_GROUP, 1), 0.0) for s in range(1, SSM_CHUNK)]
        mt = jnp.concatenate(rows, axis=0).T
        met_ref[gi] = jnp.concatenate([mt, et_re, -et_im], axis=1).astype(met_ref.dtype)
        fb = [_cmul(pw[SSM_CHUNK - 1 - s][0], pw[SSM_CHUNK - 1 - s][1], bb_re, bb_im) for s in range(SSM_CHUNK)]
        f_all = jnp.concatenate([jnp.concatenate([x[0] for x in fb], axis=0), jnp.concatenate([x[1] for x in fb], axis=0)], axis=1)
        ft_ref[gi] = f_all.T.astype(ft_ref.dtype)
        d_re, d_im = dre_ref[row, :], dim_ref[row, :]
        a1_ref[gi] = jnp.concatenate([d_re, d_re], axis=1)
        a2_ref[gi] = jnp.concatenate([-d_im, d_im], axis=1)
        a2s_ref[gi] = jnp.concatenate([d_im, -d_im], axis=1)


def _ssm_params(lambda_re, lambda_im, log_dt, b_re, b_im, c_re, c_im, d_skip):
    ng, p = lambda_re.shape
    lc = SSM_CHUNK * SSM_GROUP
    full = pl.BlockSpec((ng, p), lambda: (0, 0))
    disc = pl.pallas_call(
        _ssm_disc_kernel,
        in_specs=[full, full, pl.BlockSpec((ng, 1), lambda: (0, 0))],
        out_specs=[full] * 6,
        out_shape=[jax.ShapeDtypeStruct((ng, p), F32)] * 6,
        name="ssm_disc",
    )(lambda_re, lambda_im, log_dt.reshape(ng, 1))
    gb = _pick(ng, (8,))
    bt_re = jnp.swapaxes(b_re, 1, 2)
    bt_im = jnp.swapaxes(b_im, 1, 2)
    dvec = jnp.pad(d_skip, ((0, 0), (0, lc - SSM_GROUP))).reshape(ng, 1, lc)
    rows = pl.BlockSpec((gb, p), lambda i: (i, 0))
    mats = pl.BlockSpec((gb, SSM_GROUP, p), lambda i: (i, 0, 0))
    dec = pl.BlockSpec((gb, 1, 2 * p), lambda i: (i, 0, 0))
    blocks = ([((gb, LANES), F32)] * 6 + [((gb, SSM_GROUP, LANES), F32)] * 4 + [((gb, 8, lc), F32)]
              + [((gb, lc, lc + 2 * p), BF16), ((gb, 2 * p, lc), BF16)] + [((gb, 8, LANES), F32)] * 3)
    met, ft, a1, a2, a2s = pl.pallas_call(
        functools.partial(_ssm_build_kernel, groups=gb),
        grid=(ng // gb,),
        in_specs=[rows] * 6 + [mats] * 4 + [pl.BlockSpec((gb, 1, lc), lambda i: (i, 0, 0))],
        out_specs=[pl.BlockSpec((gb, lc, lc + 2 * p), lambda i: (i, 0, 0)), pl.BlockSpec((gb, 2 * p, lc), lambda i: (i, 0, 0)), dec, dec, dec],
        out_shape=[jax.ShapeDtypeStruct((ng, lc, lc + 2 * p), BF16), jax.ShapeDtypeStruct((ng, 2 * p, lc), BF16)]
        + [jax.ShapeDtypeStruct((ng, 1, 2 * p), F32)] * 3,
        compiler_params=_params(("arbitrary",), _vmem_estimate(blocks, [((lc, lc + 2 * p), F32)] * 8)),
        name="ssm_build",
    )(*disc, bt_re, bt_im, c_re, c_im, dvec)
    flat = lambda a: a.reshape(1, ng * 2 * p)
    return met, ft, flat(a1), flat(a2), flat(a2s)


def _chunk_blocks(nk):
    return [(k0, min(LANES, nk - k0)) for k0 in range(0, nk, LANES)]


def _pad_rows(x):
    n = x.shape[0]
    return x if n == LANES else jnp.concatenate([x, jnp.zeros((LANES - n, x.shape[1]), x.dtype)], axis=0)


def _ssm_inputs_phase(u_ref, ft_ref, rhs_ref, s_ref, *, nk, p):
    groups = LANES // SSM_GROUP
    blocks = _chunk_blocks(nk)
    xt = [[_pad_rows(u_ref[pl.ds(k0 * SSM_CHUNK + s, n, stride=SSM_CHUNK), :].astype(BF16)).T for k0, n in blocks]
          for s in range(SSM_CHUNK)]
    for gi in range(groups):
        rhs = jnp.concatenate([jnp.concatenate([xt[s][kb][gi * SSM_GROUP:(gi + 1) * SSM_GROUP, :] for kb in range(len(blocks))], axis=1)
                               for s in range(SSM_CHUNK)], axis=0)
        rhs_ref[gi] = rhs
        st = jnp.dot(ft_ref[gi], rhs, preferred_element_type=F32)
        for kb, (k0, n) in enumerate(blocks):
            s_ref[k0:k0 + n, gi * 2 * p:(gi + 1) * 2 * p] = st[:, kb * LANES:(kb + 1) * LANES].T[:n, :]


def _ssm_outputs_phase(rhs_ref, h_ref, met_ref, y_ref, *, nk, p):
    groups = LANES // SSM_GROUP
    blocks = _chunk_blocks(nk)
    zt = [[None] * groups for _ in range(SSM_CHUNK)]
    for gi in range(groups):
        ht = jnp.concatenate([_pad_rows(h_ref[k0:k0 + n, gi * 2 * p:(gi + 1) * 2 * p]).T for k0, n in blocks], axis=1)
        rhs = jnp.concatenate([rhs_ref[gi], ht.astype(BF16)], axis=0)
        yt = jnp.dot(met_ref[gi], rhs, preferred_element_type=F32)
        for t in range(SSM_CHUNK):
            zt[t][gi] = yt[t * SSM_GROUP:(t + 1) * SSM_GROUP, :]
    for t in range(SSM_CHUNK):
        z = jnp.concatenate(zt[t], axis=0)
        for kb, (k0, n) in enumerate(blocks):
            y_ref[pl.ds(k0 * SSM_CHUNK + t, n, stride=SSM_CHUNK), :] = z[:, kb * LANES:(kb + 1) * LANES].T[:n, :]


def _ssm_scan_phase(s_ref, h0_ref, a1_ref, a2_ref, a2s_ref, h_ref, fin_ref, ss_ref, *, batch, kp, dec_batch, ks, p):
    def swap_halves(x):
        lane = lax.broadcasted_iota(jnp.int32, x.shape, 1)
        return jnp.where(lane % (2 * p) < p, pltpu.roll(x, x.shape[1] - p, 1), pltpu.roll(x, p, 1))

    ss_ref[...] = swap_halves(s_ref[...])
    a1, a2, a2s = a1_ref[...], a2_ref[...], a2s_ref[...]
    zero = jnp.zeros_like(a1)

    def step(k, h, hs):
        h_ref[pl.ds(k, 1), :] = h
        s = s_ref[pl.ds(k, 1), :]
        ss = ss_ref[pl.ds(k, 1), :]
        return a1 * h + a2 * hs + s, a1 * hs + a2s * h + ss

    def body(k, carry):
        out = []
        for b in range(batch):
            out.extend(step(b * kp + k, carry[2 * b], carry[2 * b + 1]))
        return tuple(out)

    fin = lax.fori_loop(0, kp, body, (zero,) * (2 * batch), unroll=4 if kp % 4 == 0 else 1)
    for b in range(batch):
        fin_ref[b:b + 1, :] = fin[2 * b]
    h0s_all = swap_halves(h0_ref[...])
    for b in range(dec_batch):
        h, hs = h0_ref[b:b + 1, :], h0s_all[b:b + 1, :]
        for k in range(ks):
            h, hs = step(batch * kp + b * ks + k, h, hs)
        fin_ref[batch + b:batch + b + 1, :] = h


def _ssm_kernel(u_ref, ft_ref, met_ref, h0_ref, a1_ref, a2_ref, a2s_ref, y_ref, fin_ref, rhs_ref, s_ref, ss_ref, h_ref,
                *, batch, kp, dec_batch, ks, p):
    nk = batch * kp + dec_batch * ks
    _ssm_inputs_phase(u_ref, ft_ref, rhs_ref, s_ref, nk=nk, p=p)
    _ssm_scan_phase(s_ref, h0_ref, a1_ref, a2_ref, a2s_ref, h_ref, fin_ref, ss_ref, batch=batch, kp=kp, dec_batch=dec_batch, ks=ks, p=p)
    _ssm_outputs_phase(rhs_ref, h_ref, met_ref, y_ref, nk=nk, p=p)


def _ssm(u, met, ft, a1, a2, a2s, h0, batch, kp, dec_batch, ks):
    m, sw = u.shape
    ng, p2, lc = ft.shape
    p = p2 // 2
    nk = m // SSM_CHUNK
    assert nk == batch * kp + dec_batch * ks and (nk % LANES) % 8 == 0
    nk_lanes = -(-nk // LANES) * LANES
    groups = LANES // SSM_GROUP
    lb = groups * p2
    nseq = batch + dec_batch
    u_spec = pl.BlockSpec((m, LANES), lambda i: (0, i))
    rowb = pl.BlockSpec((1, lb), lambda i: (0, i))
    blocks = [((m, LANES), F32)] * 2 + [((groups, p2, lc), BF16), ((groups, lc, lc + p2), BF16), ((nseq, lb), F32)]
    scratch = [((groups, lc, nk_lanes), BF16)] + [((nk, lb), F32)] * 3
    return pl.pallas_call(
        functools.partial(_ssm_kernel, batch=batch, kp=kp, dec_batch=dec_batch, ks=ks, p=p),
        grid=(ng // groups,),
        in_specs=[u_spec, pl.BlockSpec((groups, p2, lc), lambda i: (i, 0, 0)), pl.BlockSpec((groups, lc, lc + p2), lambda i: (i, 0, 0)),
                  pl.BlockSpec((dec_batch, lb), lambda i: (0, i)), rowb, rowb, rowb],
        out_specs=[u_spec, pl.BlockSpec((nseq, lb), lambda i: (0, i))],
        out_shape=[jax.ShapeDtypeStruct((m, sw), F32), jax.ShapeDtypeStruct((nseq, ng * p2), F32)],
        scratch_shapes=[pltpu.VMEM(s, d) for s, d in scratch],
        compiler_params=_params(("arbitrary",), _vmem_estimate(blocks, scratch + [((lc + p2, nk_lanes), BF16), ((lc, nk_lanes), F32)])),
        name="ssm",
    )(u, ft, met, h0, a1, a2, a2s)


def _weight_stream_scratch(k, tn, n_chunks):
    assert k % (8 * n_chunks) == 0
    shapes = [((2, k, tn), BF16), ((2, k // n_chunks, tn), F32)]
    return shapes, [pltpu.VMEM(s, t) for s, t in shapes] + [pltpu.SemaphoreType.DMA((2,))]


def _glu_kernel(y_ref, w_hbm, ba_ref, bg_ref, zs_ref, o_ref, wa_buf, wa_stage, wa_sem, wg_buf, wg_stage, wg_sem, *, n_blocks, n_chunks):
    tn = o_ref.shape[1]
    stream = functools.partial(_stream_weights, w_hbm, n_blocks=n_blocks, n_chunks=n_chunks)
    wa_ref = stream(wa_buf, wa_stage, wa_sem, col_of_block=lambda blk: blk * tn)
    wg_ref = stream(wg_buf, wg_stage, wg_sem, col_of_block=lambda blk: (n_blocks + blk) * tn)
    y = y_ref[...].astype(BF16)
    a = jnp.dot(y, wa_ref[...], preferred_element_type=F32) + ba_ref[...]
    g = jnp.dot(y, wg_ref[...], preferred_element_type=F32) + bg_ref[...]
    o_ref[...] = (a * _sigmoid(g) * zs_ref[...].astype(F32)).astype(o_ref.dtype)


def _glu(y, w_glu, b_glu, acts, zs_col0, tm, tn):
    m, sw = y.shape
    nb, n_chunks = sw // tn, m // tm
    zb0 = zs_col0 // tn
    shapes, scratch = _weight_stream_scratch(sw, tn, n_chunks)
    blocks = [((tm, sw), F32), ((tm, tn), BF16), ((tm, tn), BF16)]
    return pl.pallas_call(
        functools.partial(_glu_kernel, n_blocks=nb, n_chunks=n_chunks),
        grid=(nb, n_chunks),
        in_specs=[
            pl.BlockSpec((tm, sw), lambda j, i: (i, 0)),
            pl.BlockSpec(memory_space=pl.ANY),
            pl.BlockSpec((1, tn), lambda j, i: (0, j)),
            pl.BlockSpec((1, tn), lambda j, i: (0, nb + j)),
            pl.BlockSpec((tm, tn), lambda j, i: (i, zb0 + j)),
        ],
        out_specs=pl.BlockSpec((tm, tn), lambda j, i: (i, j)),
        out_shape=jax.ShapeDtypeStruct((m, sw), BF16),
        scratch_shapes=scratch * 2,
        compiler_params=_params(("arbitrary", "arbitrary"), _vmem_estimate(blocks, shapes * 2 + [((tm, tn), F32)] * 4)),
        name="glu",
    )(y, w_glu, b_glu, b_glu, acts)


def _merge_kernel(xa_ref, xs_ref, wpa_hbm, wps_hbm, ga_ref, gs_ref, o_ref, wa_buf, wa_stage, wa_sem, ws_buf, ws_stage, ws_sem,
                  *, n_blocks, n_chunks):
    tn = o_ref.shape[1]
    col = lambda blk: blk * tn
    wpa_ref = _stream_weights(wpa_hbm, wa_buf, wa_stage, wa_sem, col_of_block=col, n_blocks=n_blocks, n_chunks=n_chunks)
    wps_ref = _stream_weights(wps_hbm, ws_buf, ws_stage, ws_sem, col_of_block=col, n_blocks=n_blocks, n_chunks=n_chunks)
    br_a = jnp.dot(xa_ref[...], wpa_ref[...], preferred_element_type=F32)
    br_s = jnp.dot(xs_ref[...], wps_ref[...], preferred_element_type=F32)
    o_ref[...] = (ga_ref[...].astype(F32) * br_a + gs_ref[...].astype(F32) * br_s).astype(o_ref.dtype)


def _merge(xa, xs, w_pa, w_ps, acts, gate_col0, tm, tn):
    m, aw = xa.shape
    sw = xs.shape[1]
    d = w_pa.shape[1]
    nb, n_chunks = d // tn, m // tm
    gb0 = gate_col0 // tn
    shapes_a, scratch_a = _weight_stream_scratch(aw, tn, n_chunks)
    shapes_s, scratch_s = _weight_stream_scratch(sw, tn, n_chunks)
    blocks = [((tm, aw), BF16), ((tm, sw), BF16)] + [((tm, tn), BF16)] * 3
    return pl.pallas_call(
        functools.partial(_merge_kernel, n_blocks=nb, n_chunks=n_chunks),
        grid=(nb, n_chunks),
        in_specs=[
            pl.BlockSpec((tm, aw), lambda j, i: (i, 0)),
            pl.BlockSpec((tm, sw), lambda j, i: (i, 0)),
            pl.BlockSpec(memory_space=pl.ANY),
            pl.BlockSpec(memory_space=pl.ANY),
            pl.BlockSpec((tm, tn), lambda j, i: (i, gb0 + j)),
            pl.BlockSpec((tm, tn), lambda j, i: (i, gb0 + nb + j)),
        ],
        out_specs=pl.BlockSpec((tm, tn), lambda j, i: (i, j)),
        out_shape=jax.ShapeDtypeStruct((m, d), BF16),
        scratch_shapes=scratch_a + scratch_s,
        compiler_params=_params(("arbitrary", "arbitrary"), _vmem_estimate(blocks, shapes_a + shapes_s + [((tm, tn), F32)] * 4)),
        name="merge",
    )(xa, xs, w_pa, w_ps, acts, acts)


def _out_kernel(mg_ref, w_ref, x_ref, g_ref, o_ref, ssq_ref, *, n_col_blocks, tn):
    j = pl.program_id(1)
    blk = x_ref[...] + jnp.dot(mg_ref[...], w_ref[...], preferred_element_type=F32)
    o_ref[:, pl.ds(pl.multiple_of(j * tn, tn), tn)] = blk
    part = jnp.sum(blk * blk, axis=-1, keepdims=True)

    @pl.when(j == 0)
    def _():
        ssq_ref[...] = part

    @pl.when(j > 0)
    def _():
        ssq_ref[...] += part

    @pl.when(j == n_col_blocks - 1)
    def _():
        scale = lax.rsqrt(ssq_ref[...] / (n_col_blocks * tn) + NORM_EPS)
        o_ref[...] = o_ref[...] * scale * g_ref[...]


def _out(merged, row_block0, w_out, x, final_g, tm, tn):
    mx, d = x.shape
    nb = d // tn
    blocks = [((tm, d), BF16), ((d, tn), BF16), ((tm, tn), F32), ((tm, d), F32)]
    return pl.pallas_call(
        functools.partial(_out_kernel, n_col_blocks=nb, tn=tn),
        grid=(mx // tm, nb),
        in_specs=[
            pl.BlockSpec((tm, d), lambda i, j: (row_block0 + i, 0)),
            pl.BlockSpec((d, tn), lambda i, j: (0, j)),
            pl.BlockSpec((tm, tn), lambda i, j: (i, j)),
            pl.BlockSpec((1, d), lambda i, j: (0, 0)),
        ],
        out_specs=pl.BlockSpec((tm, d), lambda i, j: (i, 0)),
        out_shape=jax.ShapeDtypeStruct((mx, d), F32),
        scratch_shapes=[pltpu.VMEM((tm, 1), F32)],
        compiler_params=_params(("arbitrary", "arbitrary"), _vmem_estimate(blocks, [((tm, tn), F32)] * 3 + [((tm, LANES), F32)])),
        name="out_norm",
    )(merged, w_out, x, final_g.reshape(1, d))


def _rope_tables(positions):
    half = HEAD_DIM // 2
    inv_freq = ROPE_THETA ** (-jnp.arange(half, dtype=F32) / half)
    ang = positions.astype(F32)[:, None] * inv_freq[None, :]
    cos, sin = jnp.cos(ang), jnp.sin(ang)
    reps = LANES // HEAD_DIM
    return jnp.tile(jnp.concatenate([cos, cos], axis=1), (1, reps)), jnp.tile(jnp.concatenate([-sin, sin], axis=1), (1, reps))


def kernel(x_prompt, x_sample, cache_k, cache_v, state_ssm_re, state_ssm_im, norm_g, w_in, sink, lambda_re, lambda_im,
           log_dt, b_re, b_im, c_re, c_im, d_skip, w_glu, b_glu, w_pa, w_ps, w_out, final_g):
    depth = norm_g.shape[0]
    assert depth == 1, "one trunk layer"
    batch, seq, d = x_prompt.shape
    dec_batch, dec_seq, _ = x_sample.shape
    aw = w_pa.shape[1]
    sw = w_ps.shape[1]
    nh = aw // HEAD_DIM
    g = max(1, nh // GQA_GROUPING)
    rep = nh // g
    kvw = g * HEAD_DIM
    ng, p = lambda_re.shape[1:]
    assert dec_seq == CHUNK and cache_k.shape[2] == WINDOW and seq % (2 * CHUNK) == 0
    assert kvw % LANES == 0 and rep % 2 == 0 and sw == ng * SSM_GROUP and (1 << SSM_POW_BITS) == SSM_CHUNK
    assert 2 * p == LANES and sw % LANES == 0
    mp, ms = batch * seq, dec_batch * dec_seq
    m = mp + ms
    widths = (aw, 2 * kvw, aw, sw, sw, 2 * d)
    assert sum(widths) == w_in.shape[2]
    c_q, c_kv, c_za, c_u, c_zs, c_gate = (sum(widths[:n]) for n in range(len(widths)))
    tm = _pick(math.gcd(mp, ms), (512, 256, 128))
    tmp = max(t for t in range(16, 1153, 16) if m % t == 0)
    tn = _pick(math.gcd(c_kv, c_za, c_u, c_zs, c_gate, 2 * d), (512, 256, 128))

    xp = x_prompt.reshape(mp, d)
    xs = x_sample.reshape(ms, d)
    w_in2 = w_in.reshape(d, w_in.shape[2])
    positions = jnp.concatenate([jnp.tile(jnp.arange(seq, dtype=jnp.int32), batch),
                                 jnp.tile(PAST_LEN + jnp.arange(dec_seq, dtype=jnp.int32), dec_batch)])
    cos, sin = _rope_tables(positions)

    h, k_f, v_f, k_hm, v_hm = _norm_kv(xp, xs, norm_g[0], w_in2, cos, sin, c_kv, kvw, tm)
    q_hm = _proj_q(h, w_in2, cos, sin, aw, tmp, tn)
    tna = _pick(math.gcd(aw, sw), (1024, 512, 256))
    acts = _proj_act(h, w_in2, [(c_za, aw, "silu"), (c_zs, sw, "silu"), (c_gate, 2 * d, "sigmoid")], tmp, tna, "proj_acts")
    u = _proj_act(h, w_in2, [(c_u, sw, "none")], tmp, tna, "proj_u", out_dtype=F32)

    to_heads = lambda c: jnp.transpose(c[0], (2, 0, 1, 3)).astype(BF16)
    new_rows = lambda a: a[:, mp:].reshape(g, dec_batch, dec_seq, HEAD_DIM)
    ks = jnp.concatenate([to_heads(cache_k), new_rows(k_hm)], axis=2).reshape(g, dec_batch * (WINDOW + CHUNK), HEAD_DIM)
    vs = jnp.concatenate([to_heads(cache_v), new_rows(v_hm)], axis=2).reshape(g, dec_batch * (WINDOW + CHUNK), HEAD_DIM)
    sink_rows = jnp.repeat(sink[0].reshape(g, rep), CHUNK, axis=1).reshape(g, 1, rep * CHUNK)
    xa, w_out_b = _attention(q_hm, k_hm, v_hm, ks, vs, sink_rows, acts, w_out.reshape(d, d), batch, seq, dec_batch, aw)

    met, ft, a1, a2, a2s = _ssm_params(lambda_re[0], lambda_im[0], log_dt[0], b_re[0], b_im[0], c_re[0], c_im[0], d_skip[0])
    h0 = jnp.concatenate([state_ssm_re[0], state_ssm_im[0]], axis=-1).reshape(dec_batch, ng * 2 * p)
    y, fin = _ssm(u, met, ft, a1, a2, a2s, h0, batch, seq // SSM_CHUNK, dec_batch, dec_seq // SSM_CHUNK)

    tmm = tmp
    x_ssm = _glu(y, w_glu.reshape(sw, 2 * sw), b_glu.reshape(1, 2 * sw), acts, aw, tmm, tna)
    merged = _merge(xa, x_ssm, w_pa.reshape(aw, d), w_ps.reshape(sw, d), acts, aw + sw, tmm, tna)
    tno = _pick(d, (1024, 512, 256))
    y_prompt = _out(merged, 0, w_out_b, xp, final_g, tm, tno).reshape(batch, seq, d)
    y_sample = _out(merged, mp // tm, w_out_b, xs, final_g, tm, tno).reshape(dec_batch, dec_seq, d)

    keep = min(WINDOW, seq)
    last_rows = lambda a: jnp.stack([a[(b + 1) * seq - keep:(b + 1) * seq] for b in range(batch)]).reshape(1, batch, keep, g, HEAD_DIM)
    dec_rows = lambda a: a[mp:].reshape(dec_batch, dec_seq, g, HEAD_DIM)[None]
    fin = fin.reshape(batch + dec_batch, ng, 2, p)
    return (y_prompt, y_sample, last_rows(k_f), last_rows(v_f), fin[:batch, :, 0][None], fin[:batch, :, 1][None],
            dec_rows(k_f), dec_rows(v_f), fin[batch:, :, 0][None], fin[batch:, :, 1][None])
```

```python
import functools
import math

import jax
import jax.numpy as jnp
from jax import lax
from jax.experimental import pallas as pl
from jax.experimental.pallas import tpu as pltpu

CHUNK = 64
WINDOW = 128
HEAD_DIM = 64
GQA_GROUPING = 8
SSM_GROUP = 16
PAST_LEN = 1024
ROPE_THETA = 10000.0
NORM_EPS = 1e-5
LAMBDA_RE_MAX = -1e-4
LOG2_E = math.log2(math.e)

SSM_CHUNK = 16
SSM_POW_BITS = 4
LANES = 128
V7X_VMEM_BYTES = 64 * 1024 * 1024
V7X_VMEM_RESERVED_BYTES = 4 * 1024 * 1024
BF16 = jnp.bfloat16
F32 = jnp.float32


def _pick(n, prefs):
    for p in prefs:
        if n % p == 0:
            return p
    raise ValueError(f"no tile in {prefs} divides {n}")


def _params(sem, vmem_bytes):
    limit = min(int(vmem_bytes), V7X_VMEM_BYTES - V7X_VMEM_RESERVED_BYTES)
    return pltpu.CompilerParams(dimension_semantics=sem, vmem_limit_bytes=limit)


def _sigmoid(x):
    return 0.5 * jnp.tanh(0.5 * x) + 0.5


def _nbytes(shape, dtype):
    return math.prod(shape) * jnp.dtype(dtype).itemsize


def _vmem_estimate(blocks, temps=()):
    return 2 * (2 * sum(_nbytes(s, d) for s, d in blocks) + sum(_nbytes(s, d) for s, d in temps))


def _norm_kv_kernel(xp_ref, xs_ref, g_ref, w_ref, cos_ref, sin_ref, h_ref, kf_ref, vf_ref, kh_ref, vh_ref, wb_ref, *, n_prompt_blocks, kvw):
    i = pl.program_id(0)

    @pl.when(i == 0)
    def _():
        wb_ref[...] = w_ref[...].astype(wb_ref.dtype)

    tm = h_ref.shape[0]
    n_parts = 2 if tm % 32 == 0 else 1
    for r in range(n_parts):
        rows = slice(r * (tm // n_parts), (r + 1) * (tm // n_parts))
        x = jnp.where(i < n_prompt_blocks, xp_ref[rows, :], xs_ref[rows, :])
        y = x * lax.rsqrt(jnp.mean(x * x, axis=-1, keepdims=True) + NORM_EPS)
        h = (y * g_ref[...]).astype(h_ref.dtype)
        h_ref[rows, :] = h
        acc = jnp.dot(h, wb_ref[...], preferred_element_type=F32)
        k = _rope(acc[:, :kvw], cos_ref.at[rows, :], sin_ref.at[rows, :])
        v = acc[:, kvw:]
        kf_ref[rows, :] = k
        vf_ref[rows, :] = v
        _store_heads(kh_ref, rows, k)
        _store_heads(vh_ref, rows, v)


def _norm_kv(xp, xs, norm_g, w, cos, sin, col0, kvw, tm):
    mp, d = xp.shape
    ms = xs.shape[0]
    npb, nsb = mp // tm, ms // tm
    m = mp + ms
    g = kvw // HEAD_DIM
    tn = 2 * kvw
    assert col0 % tn == 0
    once = pl.Buffered(1)
    tab = pl.BlockSpec((tm, LANES), lambda i: (i, 0))
    flat = pl.BlockSpec((tm, kvw), lambda i: (i, 0))
    heads = pl.BlockSpec((g, tm, HEAD_DIM), lambda i: (0, i, 0))
    xs_spec = (pl.BlockSpec((tm, d), lambda i: (0, 0), pipeline_mode=once) if nsb == 1
               else pl.BlockSpec((tm, d), lambda i: (jnp.maximum(i - npb, 0), 0)))
    blocks = [((tm, d), F32), ((tm, d), BF16), ((tm, 2 * tn), F32), ((tm, 2 * tn), BF16), ((tm, 2 * LANES), F32)]
    temps = [((tm, d), F32)] * 2 + [((d, tn), F32), ((d, tn), BF16)]
    return pl.pallas_call(
        functools.partial(_norm_kv_kernel, n_prompt_blocks=npb, kvw=kvw),
        grid=(npb + nsb,),
        in_specs=[
            pl.BlockSpec((tm, d), lambda i: (jnp.minimum(i, npb - 1), 0)),
            xs_spec,
            pl.BlockSpec((1, d), lambda i: (0, 0)),
            pl.BlockSpec((d, tn), lambda i: (0, col0 // tn), pipeline_mode=once),
            tab, tab,
        ],
        out_specs=[pl.BlockSpec((tm, d), lambda i: (i, 0)), flat, flat, heads, heads],
        out_shape=[jax.ShapeDtypeStruct((m, d), BF16)] + [jax.ShapeDtypeStruct((m, kvw), F32)] * 2
        + [jax.ShapeDtypeStruct((g, m, HEAD_DIM), BF16)] * 2,
        scratch_shapes=[pltpu.VMEM((d, tn), BF16)],
        compiler_params=_params(("arbitrary",), _vmem_estimate(blocks, temps)),
        name="norm_kv",
    )(xp, xs, norm_g.reshape(1, d), w, cos, sin)


def _rope(acc, cos_ref, sin_ref):
    tm, tn = acc.shape
    reps = tn // LANES
    cos = jnp.tile(cos_ref[...], (1, reps))
    sin = jnp.tile(sin_ref[...], (1, reps))
    lane = lax.broadcasted_iota(jnp.int32, (tm, tn), 1)
    low = (lane % HEAD_DIM) < (HEAD_DIM // 2)
    partner = jnp.where(low, pltpu.roll(acc, tn - HEAD_DIM // 2, 1), pltpu.roll(acc, HEAD_DIM // 2, 1))
    return acc * cos + partner * sin


def _store_heads(o_ref, rows, val):
    for h in range(val.shape[1] // HEAD_DIM):
        o_ref[h, rows, :] = val[:, h * HEAD_DIM:(h + 1) * HEAD_DIM].astype(o_ref.dtype)


def _cast_weight(w_ref, wb_ref):
    @pl.when(pl.program_id(1) == 0)
    def _():
        wb_ref[...] = w_ref[...].astype(wb_ref.dtype)


def _stream_weights(w_hbm, wbuf_ref, stage_ref, sem_ref, *, col_of_block, n_blocks, n_chunks):
    d, tn = wbuf_ref.shape[1:]
    ck = d // n_chunks
    total = n_blocks * n_chunks
    j = pl.program_id(0)
    t = j * n_chunks + pl.program_id(1)

    def aligned(x, a):
        return x if isinstance(x, int) else pl.multiple_of(x, a)

    def chunk_copy(blk, chunk, slot):
        src = w_hbm.at[pl.ds(aligned(chunk * ck, ck), ck), pl.ds(aligned(col_of_block(blk), LANES), tn)]
        return pltpu.make_async_copy(src, stage_ref.at[slot], sem_ref.at[slot])

    def generation(s):
        s = jnp.asarray(s, jnp.int32)
        k = s + n_chunks - 1
        past = (k > total - 1).astype(jnp.int32)
        k = jnp.minimum(k, total - 1)
        blk, chunk = k // n_chunks, k % n_chunks
        return chunk_copy(blk, chunk, s % 2), (blk + past) % 2, chunk

    def land(copy, half, chunk, slot):
        copy.wait()
        wbuf_ref[half, pl.ds(aligned(chunk * ck, ck), ck), :] = stage_ref[slot].astype(wbuf_ref.dtype)

    @pl.when(t == 0)
    def _():
        slot_of = lambda c: (c + n_chunks - 1) % 2
        first = [chunk_copy(0, c, slot_of(c)) for c in range(n_chunks)]
        first[0].start()
        for c in range(n_chunks - 1):
            first[c + 1].start()
            land(first[c], 0, c, slot_of(c))

    @pl.when(t + 1 < total)
    def _():
        generation(t + 1)[0].start()

    copy, half, chunk = generation(t)
    land(copy, half, chunk, t % 2)
    return wbuf_ref.at[j % 2]


def _weight_stream_scratch(k, tn, n_chunks):
    assert k % (8 * n_chunks) == 0
    shapes = [((2, k, tn), BF16), ((2, k // n_chunks, tn), F32)]
    return shapes, [pltpu.VMEM(s, t) for s, t in shapes] + [pltpu.SemaphoreType.DMA((2,))]


def _row_parts(tm, n=2):
    n = n if tm % (16 * n) == 0 and tm >= 1024 else 1
    return [slice(r * (tm // n), (r + 1) * (tm // n)) for r in range(n)]


def _proj_q_kernel(h_ref, w_ref, cos_ref, sin_ref, q_ref, wb_ref):
    _cast_weight(w_ref, wb_ref)
    for rows in _row_parts(h_ref.shape[0], 4):
        acc = jnp.dot(h_ref[rows, :], wb_ref[...], preferred_element_type=F32)
        _store_heads(q_ref, rows, _rope(acc, cos_ref.at[rows, :], sin_ref.at[rows, :]) * (HEAD_DIM ** -0.5 * LOG2_E))


def _proj_act_kernel(h_ref, w_hbm, o_ref, wbuf_ref, stage_ref, sem_ref, *, segments, n_chunks):
    tn = o_ref.shape[1]
    n_blocks = sum(s[1] for s in segments)
    in_seg = lambda blk, s: (blk >= s[0]) & (blk < s[0] + s[1])

    def col_of_block(blk):
        if isinstance(blk, int):
            return next(s[2] + (blk - s[0]) * tn for s in segments if s[0] <= blk < s[0] + s[1])
        return sum(jnp.where(in_seg(blk, s), s[2] + (blk - s[0]) * tn, 0) for s in segments)

    wb_ref = _stream_weights(w_hbm, wbuf_ref, stage_ref, sem_ref, col_of_block=col_of_block, n_blocks=n_blocks, n_chunks=n_chunks)
    acts = {s[3] for s in segments}
    assert acts in ({"none"}, {"silu"}, {"sigmoid"}, {"silu", "sigmoid"})
    j = pl.program_id(0)
    for rows in _row_parts(h_ref.shape[0]):
        acc = jnp.dot(h_ref[rows, :], wb_ref[...], preferred_element_type=F32)
        if acts != {"none"}:
            sig = _sigmoid(acc)
            if acts == {"silu"}:
                acc = acc * sig
            elif acts == {"sigmoid"}:
                acc = sig
            else:
                is_silu = functools.reduce(lambda a, b: a | b, [in_seg(j, s) for s in segments if s[3] == "silu"])
                acc = jnp.where(is_silu, acc * sig, sig)
        o_ref[rows, :] = acc.astype(o_ref.dtype)


def _proj_specs(d, tm, tn, col0):
    assert col0 % tn == 0
    cb0 = col0 // tn
    return [pl.BlockSpec((tm, d), lambda j, i: (i, 0)), pl.BlockSpec((d, tn), lambda j, i: (0, cb0 + j))]


def _proj_vmem(d, tm, tn, outs):
    return _vmem_estimate([((tm, d), BF16), ((d, tn), F32)] + outs, [((tm, tn), F32)] * 3 + [((d, tn), BF16)])


def _proj_q(h, w, cos, sin, aw, tm, tn):
    m, d = h.shape
    nh = aw // HEAD_DIM
    tab = pl.BlockSpec((tm, LANES), lambda j, i: (i, 0))
    return pl.pallas_call(
        _proj_q_kernel,
        grid=(aw // tn, m // tm),
        in_specs=_proj_specs(d, tm, tn, 0) + [tab, tab],
        out_specs=pl.BlockSpec((tn // HEAD_DIM, tm, HEAD_DIM), lambda j, i: (j, i, 0)),
        out_shape=jax.ShapeDtypeStruct((nh, m, HEAD_DIM), BF16),
        scratch_shapes=[pltpu.VMEM((d, tn), BF16)],
        compiler_params=_params(("arbitrary", "arbitrary"), _proj_vmem(d, tm, tn, [((tm, 2 * tn), BF16), ((tm, 2 * LANES), F32)])),
        name="proj_q",
    )(h, w, cos, sin)


def _proj_act(h, w, runs, tm, tn, name, out_dtype=BF16):
    m, d = h.shape
    n_chunks = m // tm
    segments, b0 = [], 0
    for col0, width, act in runs:
        assert width % tn == 0 and col0 % LANES == 0
        segments.append((b0, width // tn, col0, act))
        b0 += width // tn
    n_blocks, ncols = b0, b0 * tn
    scratch, scratch_shapes = _weight_stream_scratch(d, tn, n_chunks)
    return pl.pallas_call(
        functools.partial(_proj_act_kernel, segments=tuple(segments), n_chunks=n_chunks),
        grid=(n_blocks, n_chunks),
        in_specs=[pl.BlockSpec((tm, d), lambda j, i: (i, 0)), pl.BlockSpec(memory_space=pl.ANY)],
        out_specs=pl.BlockSpec((tm, tn), lambda j, i: (i, j)),
        out_shape=jax.ShapeDtypeStruct((m, ncols), out_dtype),
        scratch_shapes=scratch_shapes,
        compiler_params=_params(("arbitrary", "arbitrary"),
                                _vmem_estimate([((tm, d), BF16), ((tm, tn), out_dtype)], scratch + [((tm // len(_row_parts(tm)), tn), F32)] * 3)),
        name=name,
    )(h, w)


def _attn_weights(q, kw, sink, n_valid):
    st = lax.dot_general(kw, q, (((1,), (1,)), ((), ())), preferred_element_type=F32)
    if n_valid is not None:
        row = lax.broadcasted_iota(jnp.int32, st.shape, 0)
        st = jnp.where(row < n_valid, st, -jnp.inf)
    m = jnp.maximum(jnp.max(st, axis=0, keepdims=True), sink)
    e = jnp.exp2(st - m)
    return e.astype(BF16), jnp.sum(e, axis=0, keepdims=True) + jnp.exp2(sink - m)


def _attn_values(vw, e, denom):
    ot = jnp.dot(vw.astype(F32).T.astype(BF16), e, preferred_element_type=F32) / denom
    return ot.T


def _store_unit(o_ref, za_ref, row0, col0, o, rep):
    for r in range(0, rep, 2):
        pair = jnp.concatenate([o[r * CHUNK:(r + 1) * CHUNK], o[(r + 1) * CHUNK:(r + 2) * CHUNK]], axis=1)
        cols = slice(col0 + r * HEAD_DIM, col0 + (r + 2) * HEAD_DIM)
        gate = za_ref[row0:row0 + CHUNK, cols].astype(F32)
        o_ref[row0:row0 + CHUNK, cols] = (pair * gate).astype(o_ref.dtype)


def _attn_kernel(q_ref, k_ref, v_ref, ks_ref, vs_ref, sink_ref, za_ref, wo_ref, o_ref, wob_ref,
                 *, n_units, n_heads, rep, steps_per_stream, n_prompt_steps, n_cast_steps):
    step = pl.program_id(1)
    win = WINDOW + CHUNK

    @pl.when(pl.program_id(0) * (n_prompt_steps + 1) + step < n_cast_steps)
    def _():
        wob_ref[...] = wo_ref[...].astype(wob_ref.dtype)

    def run(windows):
        staged = []
        for hd, c, kw, vw, n_valid in windows:
            q = q_ref[hd * rep:(hd + 1) * rep, c * CHUNK:(c + 1) * CHUNK, :].reshape(rep * CHUNK, HEAD_DIM)
            staged.append((hd, c, vw) + _attn_weights(q, kw, sink_ref[hd] * LOG2_E, n_valid))
        for hd, c, vw, e, denom in staged:
            _store_unit(o_ref, za_ref, c * CHUNK, hd * rep * HEAD_DIM, _attn_values(vw, e, denom), rep)

    @pl.when(step < n_prompt_steps)
    def _():
        cb = step % steps_per_stream
        windows = []
        for hd in range(n_heads):
            for c in range(n_units):
                start = pl.multiple_of(jnp.maximum(cb * n_units + c - WINDOW // CHUNK, 0) * CHUNK, CHUNK)
                n_valid = jnp.where(cb == 0, (c + 1) * CHUNK, win) if c < WINDOW // CHUNK else None
                windows.append((hd, c, k_ref[hd, pl.ds(start, win), :], v_ref[hd, pl.ds(start, win), :], n_valid))
        run(windows)

    @pl.when(step == n_prompt_steps)
    def _():
        run([(hd, b, ks_ref[hd, b * win:(b + 1) * win, :], vs_ref[hd, b * win:(b + 1) * win, :], None)
             for hd in range(n_heads) for b in range(n_units)])


def _attention(q_hm, k_hm, v_hm, ks, vs, sink_rows, za, w_out, batch, seq, dec_batch, aw):
    nh, m, _ = q_hm.shape
    g = k_hm.shape[0]
    rep = nh // g
    win = WINDOW + CHUNK
    n_units = dec_batch
    rows = n_units * CHUNK
    assert seq % rows == 0 and n_units >= WINDOW // CHUNK
    steps_per_stream = seq // rows
    n_prompt_steps = batch * steps_per_stream
    nhd = 2 if g % 2 == 0 else 1
    blocks = ([((nhd * rep, rows, LANES), BF16)] + [((nhd * seq, LANES), BF16)] * 2 + [((nhd * dec_batch * win, LANES), BF16)] * 2
              + [((rows, nhd * rep * HEAD_DIM), BF16)] * 2)
    temps = [((rep * CHUNK, 2 * LANES), F32)] * (4 * n_units * nhd)
    stream = lambda gi, s: (gi, jnp.minimum(s // steps_per_stream, batch - 1), 0)
    dk, dn = w_out.shape
    n_steps = (g // nhd) * (n_prompt_steps + 1)
    n_cast_steps = max(c for c in (32, 16, 8) if c <= n_steps)
    assert dk % (16 * n_cast_steps) == 0
    cast_rows = dk // n_cast_steps
    cast_block = pl.BlockSpec((cast_rows, dn), lambda gi, s: (jnp.minimum(gi * (n_prompt_steps + 1) + s, n_cast_steps - 1), 0))
    blocks += [((cast_rows, dn), F32), ((cast_rows, dn), BF16)]
    return pl.pallas_call(
        functools.partial(_attn_kernel, n_units=n_units, n_heads=nhd, rep=rep, steps_per_stream=steps_per_stream,
                          n_prompt_steps=n_prompt_steps, n_cast_steps=n_cast_steps),
        grid=(g // nhd, n_prompt_steps + 1),
        in_specs=[
            pl.BlockSpec((nhd * rep, rows, HEAD_DIM), lambda gi, s: (gi, s, 0)),
            pl.BlockSpec((nhd, seq, HEAD_DIM), stream),
            pl.BlockSpec((nhd, seq, HEAD_DIM), stream),
            pl.BlockSpec((nhd, dec_batch * win, HEAD_DIM), lambda gi, s: (gi, 0, 0)),
            pl.BlockSpec((nhd, dec_batch * win, HEAD_DIM), lambda gi, s: (gi, 0, 0)),
            pl.BlockSpec((nhd, 1, rep * CHUNK), lambda gi, s: (gi, 0, 0)),
            pl.BlockSpec((rows, nhd * rep * HEAD_DIM), lambda gi, s: (s, gi)),
            cast_block,
        ],
        out_specs=[pl.BlockSpec((rows, nhd * rep * HEAD_DIM), lambda gi, s: (s, gi)), cast_block],
        out_shape=[jax.ShapeDtypeStruct((m, aw), BF16), jax.ShapeDtypeStruct((dk, dn), BF16)],
        compiler_params=_params(("arbitrary",) * 2, _vmem_estimate(blocks, temps)),
        name="attention",
    )(q_hm, k_hm, v_hm, ks, vs, sink_rows, za, w_out)


def _ssm_disc_kernel(lre_ref, lim_ref, ldt_ref, are_ref, aim_ref, dre_ref, dim_ref, fre_ref, fim_ref):
    lr = jnp.minimum(lre_ref[...], LAMBDA_RE_MAX)
    li = lim_ref[...]
    dt = jnp.exp(ldt_ref[...])
    mag = jnp.exp(lr * dt)
    a_re = mag * jnp.cos(li * dt)
    a_im = mag * jnp.sin(li * dt)
    den = lr * lr + li * li
    nr = a_re - 1.0
    fre_ref[...] = (nr * lr + a_im * li) / den
    fim_ref[...] = (a_im * lr - nr * li) / den
    are_ref[...] = a_re
    aim_ref[...] = a_im
    for _ in range(SSM_POW_BITS):
        a_re, a_im = a_re * a_re - a_im * a_im, 2.0 * a_re * a_im
    dre_ref[...] = a_re
    dim_ref[...] = a_im


def _cmul(ar, ai, br, bi):
    return ar * br - ai * bi, ar * bi + ai * br


def _ssm_build_kernel(are_ref, aim_ref, dre_ref, dim_ref, fre_ref, fim_ref, btr_ref, bti_ref, cr_ref, ci_ref, dv_ref,
                      met_ref, ft_ref, a1_ref, a2_ref, a2s_ref, *, groups):
    lc = SSM_CHUNK * SSM_GROUP
    sub = lax.broadcasted_iota(jnp.int32, (SSM_GROUP, lc), 0)
    lane = lax.broadcasted_iota(jnp.int32, (SSM_GROUP, lc), 1)
    for gi in range(groups):
        row = slice(gi, gi + 1)
        a_re, a_im = are_ref[row, :], aim_ref[row, :]
        pw = [(jnp.ones_like(a_re), jnp.zeros_like(a_re))]
        for _ in range(SSM_CHUNK):
            pw.append(_cmul(pw[-1][0], pw[-1][1], a_re, a_im))
        c_re, c_im = cr_ref[gi], ci_ref[gi]
        wt = [_cmul(pr, pi, c_re, c_im) for pr, pi in pw]
        wt_re = jnp.concatenate([w[0] for w in wt[:SSM_CHUNK]], axis=0)
        wt_im = jnp.concatenate([w[1] for w in wt[:SSM_CHUNK]], axis=0)
        et_re = jnp.concatenate([w[0] for w in wt[1:]], axis=0)
        et_im = jnp.concatenate([w[1] for w in wt[1:]], axis=0)
        bb_re, bb_im = _cmul(fre_ref[row, :], fim_ref[row, :], btr_ref[gi], bti_ref[gi])
        r0 = lax.dot_general(jnp.concatenate([bb_re, bb_im], axis=1), jnp.concatenate([wt_re, -wt_im], axis=1),
                             (((1,), (1,)), ((), ())), preferred_element_type=F32, precision=lax.Precision.HIGHEST)
        r0 = r0 + jnp.where(sub == lane, dv_ref[gi], 0.0)
        rows = [r0] + [jnp.where(lane >= s * SSM_GROUP, pltpu.roll(r0, s * SSM_GROUP, 1), 0.0) for s in range(1, SSM_CHUNK)]
        mt = jnp.concatenate(rows, axis=0).T
        met_ref[gi] = jnp.concatenate([mt, et_re, -et_im], axis=1).astype(met_ref.dtype)
        fb = [_cmul(pw[SSM_CHUNK - 1 - s][0], pw[SSM_CHUNK - 1 - s][1], bb_re, bb_im) for s in range(SSM_CHUNK)]
        f_all = jnp.concatenate([jnp.concatenate([x[0] for x in fb], axis=0), jnp.concatenate([x[1] for x in fb], axis=0)], axis=1)
        ft_ref[gi] = f_all.T.astype(ft_ref.dtype)
        d_re, d_im = dre_ref[row, :], dim_ref[row, :]
        a1_ref[gi] = jnp.concatenate([d_re, d_re], axis=1)
        a2_ref[gi] = jnp.concatenate([-d_im, d_im], axis=1)
        a2s_ref[gi] = jnp.concatenate([d_im, -d_im], axis=1)


def _ssm_params(lambda_re, lambda_im, log_dt, b_re, b_im, c_re, c_im, d_skip):
    ng, p = lambda_re.shape
    lc = SSM_CHUNK * SSM_GROUP
    full = pl.BlockSpec((ng, p), lambda: (0, 0))
    disc = pl.pallas_call(
        _ssm_disc_kernel,
        in_specs=[full, full, pl.BlockSpec((ng, 1), lambda: (0, 0))],
        out_specs=[full] * 6,
        out_shape=[jax.ShapeDtypeStruct((ng, p), F32)] * 6,
        name="ssm_disc",
    )(lambda_re, lambda_im, log_dt.reshape(ng, 1))
    gb = _pick(ng, (8,))
    bt_re = jnp.swapaxes(b_re, 1, 2)
    bt_im = jnp.swapaxes(b_im, 1, 2)
    dvec = jnp.pad(d_skip, ((0, 0), (0, lc - SSM_GROUP))).reshape(ng, 1, lc)
    rows = pl.BlockSpec((gb, p), lambda i: (i, 0))
    mats = pl.BlockSpec((gb, SSM_GROUP, p), lambda i: (i, 0, 0))
    dec = pl.BlockSpec((gb, 1, 2 * p), lambda i: (i, 0, 0))
    blocks = ([((gb, LANES), F32)] * 6 + [((gb, SSM_GROUP, LANES), F32)] * 4 + [((gb, 8, lc), F32)]
              + [((gb, lc, lc + 2 * p), BF16), ((gb, 2 * p, lc), BF16)] + [((gb, 8, LANES), F32)] * 3)
    met, ft, a1, a2, a2s = pl.pallas_call(
        functools.partial(_ssm_build_kernel, groups=gb),
        grid=(ng // gb,),
        in_specs=[rows] * 6 + [mats] * 4 + [pl.BlockSpec((gb, 1, lc), lambda i: (i, 0, 0))],
        out_specs=[pl.BlockSpec((gb, lc, lc + 2 * p), lambda i: (i, 0, 0)), pl.BlockSpec((gb, 2 * p, lc), lambda i: (i, 0, 0)), dec, dec, dec],
        out_shape=[jax.ShapeDtypeStruct((ng, lc, lc + 2 * p), BF16), jax.ShapeDtypeStruct((ng, 2 * p, lc), BF16)]
        + [jax.ShapeDtypeStruct((ng, 1, 2 * p), F32)] * 3,
        compiler_params=_params(("arbitrary",), _vmem_estimate(blocks, [((lc, lc + 2 * p), F32)] * 8)),
        name="ssm_build",
    )(*disc, bt_re, bt_im, c_re, c_im, dvec)
    flat = lambda a: a.reshape(1, ng * 2 * p)
    return met, ft, flat(a1), flat(a2), flat(a2s)


def _chunk_blocks(nk):
    return [(k0, min(LANES, nk - k0)) for k0 in range(0, nk, LANES)]


def _pad_rows(x):
    n = x.shape[0]
    return x if n == LANES else jnp.concatenate([x, jnp.zeros((LANES - n, x.shape[1]), x.dtype)], axis=0)


def _ssm_inputs_phase(u_ref, ft_ref, rhs_ref, s_ref, *, nk, p):
    groups = LANES // SSM_GROUP
    blocks = _chunk_blocks(nk)
    xt = [[_pad_rows(u_ref[pl.ds(k0 * SSM_CHUNK + s, n, stride=SSM_CHUNK), :].astype(BF16)).T for k0, n in blocks]
          for s in range(SSM_CHUNK)]
    for gi in range(groups):
        rhs = jnp.concatenate([jnp.concatenate([xt[s][kb][gi * SSM_GROUP:(gi + 1) * SSM_GROUP, :] for kb in range(len(blocks))], axis=1)
                               for s in range(SSM_CHUNK)], axis=0)
        rhs_ref[gi] = rhs
        st = jnp.dot(ft_ref[gi], rhs, preferred_element_type=F32)
        for kb, (k0, n) in enumerate(blocks):
            s_ref[k0:k0 + n, gi * 2 * p:(gi + 1) * 2 * p] = st[:, kb * LANES:(kb + 1) * LANES].T[:n, :]


def _ssm_outputs_phase(rhs_ref, h_ref, met_ref, y_ref, *, nk, p):
    groups = LANES // SSM_GROUP
    blocks = _chunk_blocks(nk)
    zt = [[None] * groups for _ in range(SSM_CHUNK)]
    for gi in range(groups):
        ht = jnp.concatenate([_pad_rows(h_ref[k0:k0 + n, gi * 2 * p:(gi + 1) * 2 * p]).T for k0, n in blocks], axis=1)
        rhs = jnp.concatenate([rhs_ref[gi], ht.astype(BF16)], axis=0)
        yt = jnp.dot(met_ref[gi], rhs, preferred_element_type=F32)
        for t in range(SSM_CHUNK):
            zt[t][gi] = yt[t * SSM_GROUP:(t + 1) * SSM_GROUP, :]
    for t in range(SSM_CHUNK):
        z = jnp.concatenate(zt[t], axis=0)
        for kb, (k0, n) in enumerate(blocks):
            y_ref[pl.ds(k0 * SSM_CHUNK + t, n, stride=SSM_CHUNK), :] = z[:, kb * LANES:(kb + 1) * LANES].T[:n, :]


def _ssm_scan_phase(s_ref, h0_ref, a1_ref, a2_ref, a2s_ref, h_ref, fin_ref, ss_ref, *, batch, kp, dec_batch, ks, p):
    def swap_halves(x):
        lane = lax.broadcasted_iota(jnp.int32, x.shape, 1)
        return jnp.where(lane % (2 * p) < p, pltpu.roll(x, x.shape[1] - p, 1), pltpu.roll(x, p, 1))

    ss_ref[...] = swap_halves(s_ref[...])
    a1, a2, a2s = a1_ref[...], a2_ref[...], a2s_ref[...]
    zero = jnp.zeros_like(a1)

    def step(k, h, hs):
        h_ref[pl.ds(k, 1), :] = h
        s = s_ref[pl.ds(k, 1), :]
        ss = ss_ref[pl.ds(k, 1), :]
        return a1 * h + a2 * hs + s, a1 * hs + a2s * h + ss

    def body(k, carry):
        out = []
        for b in range(batch):
            out.extend(step(b * kp + k, carry[2 * b], carry[2 * b + 1]))
        return tuple(out)

    fin = lax.fori_loop(0, kp, body, (zero,) * (2 * batch), unroll=4 if kp % 4 == 0 else 1)
    for b in range(batch):
        fin_ref[b:b + 1, :] = fin[2 * b]
    h0s_all = swap_halves(h0_ref[...])
    for b in range(dec_batch):
        h, hs = h0_ref[b:b + 1, :], h0s_all[b:b + 1, :]
        for k in range(ks):
            h, hs = step(batch * kp + b * ks + k, h, hs)
        fin_ref[batch + b:batch + b + 1, :] = h


def _ssm_kernel(u_ref, ft_ref, met_ref, h0_ref, a1_ref, a2_ref, a2s_ref, y_ref, fin_ref, rhs_ref, s_ref, ss_ref, h_ref,
                *, batch, kp, dec_batch, ks, p):
    nk = batch * kp + dec_batch * ks
    _ssm_inputs_phase(u_ref, ft_ref, rhs_ref, s_ref, nk=nk, p=p)
    _ssm_scan_phase(s_ref, h0_ref, a1_ref, a2_ref, a2s_ref, h_ref, fin_ref, ss_ref, batch=batch, kp=kp, dec_batch=dec_batch, ks=ks, p=p)
    _ssm_outputs_phase(rhs_ref, h_ref, met_ref, y_ref, nk=nk, p=p)


def _ssm(u, met, ft, a1, a2, a2s, h0, batch, kp, dec_batch, ks):
    m, sw = u.shape
    ng, p2, lc = ft.shape
    p = p2 // 2
    nk = m // SSM_CHUNK
    assert nk == batch * kp + dec_batch * ks and (nk % LANES) % 8 == 0
    nk_lanes = -(-nk // LANES) * LANES
    groups = LANES // SSM_GROUP
    lb = groups * p2
    nseq = batch + dec_batch
    u_spec = pl.BlockSpec((m, LANES), lambda i: (0, i))
    rowb = pl.BlockSpec((1, lb), lambda i: (0, i))
    blocks = [((m, LANES), F32)] * 2 + [((groups, p2, lc), BF16), ((groups, lc, lc + p2), BF16), ((nseq, lb), F32)]
    scratch = [((groups, lc, nk_lanes), BF16)] + [((nk, lb), F32)] * 3
    return pl.pallas_call(
        functools.partial(_ssm_kernel, batch=batch, kp=kp, dec_batch=dec_batch, ks=ks, p=p),
        grid=(ng // groups,),
        in_specs=[u_spec, pl.BlockSpec((groups, p2, lc), lambda i: (i, 0, 0)), pl.BlockSpec((groups, lc, lc + p2), lambda i: (i, 0, 0)),
                  pl.BlockSpec((dec_batch, lb), lambda i: (0, i)), rowb, rowb, rowb],
        out_specs=[u_spec, pl.BlockSpec((nseq, lb), lambda i: (0, i))],
        out_shape=[jax.ShapeDtypeStruct((m, sw), F32), jax.ShapeDtypeStruct((nseq, ng * p2), F32)],
        scratch_shapes=[pltpu.VMEM(s, d) for s, d in scratch],
        compiler_params=_params(("arbitrary",), _vmem_estimate(blocks, scratch + [((lc + p2, nk_lanes), BF16), ((lc, nk_lanes), F32)])),
        name="ssm",
    )(u, ft, met, h0, a1, a2, a2s)


def _glu_kernel(y_ref, w_hbm, ba_ref, bg_ref, zs_ref, o_ref, wa_buf, wa_stage, wa_sem, wg_buf, wg_stage, wg_sem, *, n_blocks, n_chunks):
    tn = o_ref.shape[1]
    stream = functools.partial(_stream_weights, w_hbm, n_blocks=n_blocks, n_chunks=n_chunks)
    wa_ref = stream(wa_buf, wa_stage, wa_sem, col_of_block=lambda blk: blk * tn)
    wg_ref = stream(wg_buf, wg_stage, wg_sem, col_of_block=lambda blk: (n_blocks + blk) * tn)
    for rows in _row_parts(y_ref.shape[0]):
        y = y_ref[rows, :].astype(BF16)
        a = jnp.dot(y, wa_ref[...], preferred_element_type=F32) + ba_ref[...]
        g = jnp.dot(y, wg_ref[...], preferred_element_type=F32) + bg_ref[...]
        o_ref[rows, :] = (a * _sigmoid(g) * zs_ref[rows, :].astype(F32)).astype(o_ref.dtype)


def _glu(y, w_glu, b_glu, acts, zs_col0, tm, tn):
    m, sw = y.shape
    nb, n_chunks = sw // tn, m // tm
    zb0 = zs_col0 // tn
    shapes, scratch = _weight_stream_scratch(sw, tn, n_chunks)
    blocks = [((tm, sw), F32), ((tm, tn), BF16), ((tm, tn), BF16)]
    return pl.pallas_call(
        functools.partial(_glu_kernel, n_blocks=nb, n_chunks=n_chunks),
        grid=(nb, n_chunks),
        in_specs=[
            pl.BlockSpec((tm, sw), lambda j, i: (i, 0)),
            pl.BlockSpec(memory_space=pl.ANY),
            pl.BlockSpec((1, tn), lambda j, i: (0, j)),
            pl.BlockSpec((1, tn), lambda j, i: (0, nb + j)),
            pl.BlockSpec((tm, tn), lambda j, i: (i, zb0 + j)),
        ],
        out_specs=pl.BlockSpec((tm, tn), lambda j, i: (i, j)),
        out_shape=jax.ShapeDtypeStruct((m, sw), BF16),
        scratch_shapes=scratch * 2,
        compiler_params=_params(("arbitrary", "arbitrary"), _vmem_estimate(blocks, shapes * 2 + [((tm, tn), F32)] * 4)),
        name="glu",
    )(y, w_glu, b_glu, b_glu, acts)


def _merge_kernel(xa_ref, xs_ref, wpa_hbm, wps_hbm, ga_ref, gs_ref, o_ref, wa_buf, wa_stage, wa_sem, ws_buf, ws_stage, ws_sem,
                  *, n_blocks, n_chunks):
    tn = o_ref.shape[1]
    col = lambda blk: blk * tn
    wpa_ref = _stream_weights(wpa_hbm, wa_buf, wa_stage, wa_sem, col_of_block=col, n_blocks=n_blocks, n_chunks=n_chunks)
    wps_ref = _stream_weights(wps_hbm, ws_buf, ws_stage, ws_sem, col_of_block=col, n_blocks=n_blocks, n_chunks=n_chunks)
    for rows in _row_parts(xa_ref.shape[0]):
        br_a = jnp.dot(xa_ref[rows, :], wpa_ref[...], preferred_element_type=F32)
        br_s = jnp.dot(xs_ref[rows, :], wps_ref[...], preferred_element_type=F32)
        o_ref[rows, :] = (ga_ref[rows, :].astype(F32) * br_a + gs_ref[rows, :].astype(F32) * br_s).astype(o_ref.dtype)


def _merge(xa, xs, w_pa, w_ps, acts, gate_col0, tm, tn):
    m, aw = xa.shape
    sw = xs.shape[1]
    d = w_pa.shape[1]
    nb, n_chunks = d // tn, m // tm
    gb0 = gate_col0 // tn
    shapes_a, scratch_a = _weight_stream_scratch(aw, tn, n_chunks)
    shapes_s, scratch_s = _weight_stream_scratch(sw, tn, n_chunks)
    blocks = [((tm, aw), BF16), ((tm, sw), BF16)] + [((tm, tn), BF16)] * 3
    return pl.pallas_call(
        functools.partial(_merge_kernel, n_blocks=nb, n_chunks=n_chunks),
        grid=(nb, n_chunks),
        in_specs=[
            pl.BlockSpec((tm, aw), lambda j, i: (i, 0)),
            pl.BlockSpec((tm, sw), lambda j, i: (i, 0)),
            pl.BlockSpec(memory_space=pl.ANY),
            pl.BlockSpec(memory_space=pl.ANY),
            pl.BlockSpec((tm, tn), lambda j, i: (i, gb0 + j)),
            pl.BlockSpec((tm, tn), lambda j, i: (i, gb0 + nb + j)),
        ],
        out_specs=pl.BlockSpec((tm, tn), lambda j, i: (i, j)),
        out_shape=jax.ShapeDtypeStruct((m, d), BF16),
        scratch_shapes=scratch_a + scratch_s,
        compiler_params=_params(("arbitrary", "arbitrary"), _vmem_estimate(blocks, shapes_a + shapes_s + [((tm, tn), F32)] * 4)),
        name="merge",
    )(xa, xs, w_pa, w_ps, acts, acts)


def _out_kernel(mg_ref, w_ref, x_ref, g_ref, o_ref, ssq_ref, *, n_col_blocks, tn):
    j = pl.program_id(1)
    blk = x_ref[...] + jnp.dot(mg_ref[...], w_ref[...], preferred_element_type=F32)
    o_ref[:, pl.ds(pl.multiple_of(j * tn, tn), tn)] = blk
    part = jnp.sum(blk * blk, axis=-1, keepdims=True)

    @pl.when(j == 0)
    def _():
        ssq_ref[...] = part

    @pl.when(j > 0)
    def _():
        ssq_ref[...] += part

    @pl.when(j == n_col_blocks - 1)
    def _():
        scale = lax.rsqrt(ssq_ref[...] / (n_col_blocks * tn) + NORM_EPS)
        o_ref[...] = o_ref[...] * scale * g_ref[...]


def _out(merged, row_block0, w_out, x, final_g, tm, tn):
    mx, d = x.shape
    nb = d // tn
    blocks = [((tm, d), BF16), ((d, tn), BF16), ((tm, tn), F32), ((tm, d), F32)]
    return pl.pallas_call(
        functools.partial(_out_kernel, n_col_blocks=nb, tn=tn),
        grid=(mx // tm, nb),
        in_specs=[
            pl.BlockSpec((tm, d), lambda i, j: (row_block0 + i, 0)),
            pl.BlockSpec((d, tn), lambda i, j: (0, j)),
            pl.BlockSpec((tm, tn), lambda i, j: (i, j)),
            pl.BlockSpec((1, d), lambda i, j: (0, 0)),
        ],
        out_specs=pl.BlockSpec((tm, d), lambda i, j: (i, 0)),
        out_shape=jax.ShapeDtypeStruct((mx, d), F32),
        scratch_shapes=[pltpu.VMEM((tm, 1), F32)],
        compiler_params=_params(("arbitrary", "arbitrary"), _vmem_estimate(blocks, [((tm, tn), F32)] * 3 + [((tm, LANES), F32)])),
        name="out_norm",
    )(merged, w_out, x, final_g.reshape(1, d))


def _rope_tables(positions):
    half = HEAD_DIM // 2
    inv_freq = ROPE_THETA ** (-jnp.arange(half, dtype=F32) / half)
    ang = positions.astype(F32)[:, None] * inv_freq[None, :]
    cos, sin = jnp.cos(ang), jnp.sin(ang)
    reps = LANES // HEAD_DIM
    return jnp.tile(jnp.concatenate([cos, cos], axis=1), (1, reps)), jnp.tile(jnp.concatenate([-sin, sin], axis=1), (1, reps))


def kernel(x_prompt, x_sample, cache_k, cache_v, state_ssm_re, state_ssm_im, norm_g, w_in, sink, lambda_re, lambda_im,
           log_dt, b_re, b_im, c_re, c_im, d_skip, w_glu, b_glu, w_pa, w_ps, w_out, final_g):
    depth = norm_g.shape[0]
    assert depth == 1, "one trunk layer"
    batch, seq, d = x_prompt.shape
    dec_batch, dec_seq, _ = x_sample.shape
    aw = w_pa.shape[1]
    sw = w_ps.shape[1]
    nh = aw // HEAD_DIM
    g = max(1, nh // GQA_GROUPING)
    rep = nh // g
    kvw = g * HEAD_DIM
    ng, p = lambda_re.shape[1:]
    assert dec_seq == CHUNK and cache_k.shape[2] == WINDOW and seq % (2 * CHUNK) == 0
    assert kvw % LANES == 0 and rep % 2 == 0 and sw == ng * SSM_GROUP and (1 << SSM_POW_BITS) == SSM_CHUNK
    assert 2 * p == LANES and sw % LANES == 0
    mp, ms = batch * seq, dec_batch * dec_seq
    m = mp + ms
    widths = (aw, 2 * kvw, aw, sw, sw, 2 * d)
    assert sum(widths) == w_in.shape[2]
    c_q, c_kv, c_za, c_u, c_zs, c_gate = (sum(widths[:n]) for n in range(len(widths)))
    tm = _pick(math.gcd(mp, ms), (512, 256, 128))
    tmp = max(t for t in range(16, 1153, 16) if m % t == 0)
    tn = _pick(math.gcd(c_kv, c_za, c_u, c_zs, c_gate, 2 * d), (512, 256, 128))

    xp = x_prompt.reshape(mp, d)
    xs = x_sample.reshape(ms, d)
    w_in2 = w_in.reshape(d, w_in.shape[2])
    positions = jnp.concatenate([jnp.tile(jnp.arange(seq, dtype=jnp.int32), batch),
                                 jnp.tile(PAST_LEN + jnp.arange(dec_seq, dtype=jnp.int32), dec_batch)])
    cos, sin = _rope_tables(positions)

    h, k_f, v_f, k_hm, v_hm = _norm_kv(xp, xs, norm_g[0], w_in2, cos, sin, c_kv, kvw, tm)
    q_hm = _proj_q(h, w_in2, cos, sin, aw, tmp, tn)
    tna = _pick(math.gcd(aw, sw), (1024, 512, 256))
    acts = _proj_act(h, w_in2, [(c_za, aw, "silu"), (c_zs, sw, "silu"), (c_gate, 2 * d, "sigmoid")], tmp, tna, "proj_acts")
    u = _proj_act(h, w_in2, [(c_u, sw, "none")], tmp, tna, "proj_u", out_dtype=F32)

    to_heads = lambda c: jnp.transpose(c[0], (2, 0, 1, 3)).astype(BF16)
    new_rows = lambda a: a[:, mp:].reshape(g, dec_batch, dec_seq, HEAD_DIM)
    ks = jnp.concatenate([to_heads(cache_k), new_rows(k_hm)], axis=2).reshape(g, dec_batch * (WINDOW + CHUNK), HEAD_DIM)
    vs = jnp.concatenate([to_heads(cache_v), new_rows(v_hm)], axis=2).reshape(g, dec_batch * (WINDOW + CHUNK), HEAD_DIM)
    sink_rows = jnp.repeat(sink[0].reshape(g, rep), CHUNK, axis=1).reshape(g, 1, rep * CHUNK)
    xa, w_out_b = _attention(q_hm, k_hm, v_hm, ks, vs, sink_rows, acts, w_out.reshape(d, d), batch, seq, dec_batch, aw)

    met, ft, a1, a2, a2s = _ssm_params(lambda_re[0], lambda_im[0], log_dt[0], b_re[0], b_im[0], c_re[0], c_im[0], d_skip[0])
    h0 = jnp.concatenate([state_ssm_re[0], state_ssm_im[0]], axis=-1).reshape(dec_batch, ng * 2 * p)
    y, fin = _ssm(u, met, ft, a1, a2, a2s, h0, batch, seq // SSM_CHUNK, dec_batch, dec_seq // SSM_CHUNK)

    tmm = tmp
    x_ssm = _glu(y, w_glu.reshape(sw, 2 * sw), b_glu.reshape(1, 2 * sw), acts, aw, tmm, tna)
    merged = _merge(xa, x_ssm, w_pa.reshape(aw, d), w_ps.reshape(sw, d), acts, aw + sw, tmm, tna)
    tno = _pick(d, (1024, 512, 256))
    y_prompt = _out(merged, 0, w_out_b, xp, final_g, tm, tno).reshape(batch, seq, d)
    y_sample = _out(merged, mp // tm, w_out_b, xs, final_g, tm, tno).reshape(dec_batch, dec_seq, d)

    keep = min(WINDOW, seq)
    last_rows = lambda a: jnp.stack([a[(b + 1) * seq - keep:(b + 1) * seq] for b in range(batch)]).reshape(1, batch, keep, g, HEAD_DIM)
    dec_rows = lambda a: a[mp:].reshape(dec_batch, dec_seq, g, HEAD_DIM)[None]
    fin = fin.reshape(batch + dec_batch, ng, 2, p)
    return (y_prompt, y_sample, last_rows(k_f), last_rows(v_f), fin[:batch, :, 0][None], fin[:batch, :, 1][None],
            dec_rows(k_f), dec_rows(v_f), fin[batch:, :, 0][None], fin[batch:, :, 1][None])
```

```python
import functools
import math

import jax
import jax.numpy as jnp
from jax import lax
from jax.experimental import pallas as pl
from jax.experimental.pallas import tpu as pltpu

CHUNK = 64
WINDOW = 128
HEAD_DIM = 64
GQA_GROUPING = 8
SSM_GROUP = 16
PAST_LEN = 1024
ROPE_THETA = 10000.0
NORM_EPS = 1e-5
LAMBDA_RE_MAX = -1e-4
LOG2_E = math.log2(math.e)

SSM_CHUNK = 16
SSM_POW_BITS = 4
LANES = 128
V7X_VMEM_BYTES = 64 * 1024 * 1024
V7X_VMEM_RESERVED_BYTES = 4 * 1024 * 1024
BF16 = jnp.bfloat16
F32 = jnp.float32


def _pick(n, prefs):
    for p in prefs:
        if n % p == 0:
            return p
    raise ValueError(f"no tile in {prefs} divides {n}")


def _params(sem, vmem_bytes):
    limit = min(int(vmem_bytes), V7X_VMEM_BYTES - V7X_VMEM_RESERVED_BYTES)
    return pltpu.CompilerParams(dimension_semantics=sem, vmem_limit_bytes=limit)


def _sigmoid(x):
    return 0.5 * jnp.tanh(0.5 * x) + 0.5


def _nbytes(shape, dtype):
    return math.prod(shape) * jnp.dtype(dtype).itemsize


def _vmem_estimate(blocks, temps=()):
    return 2 * (2 * sum(_nbytes(s, d) for s, d in blocks) + sum(_nbytes(s, d) for s, d in temps))


def _norm_kv_kernel(xp_ref, xs_ref, g_ref, w_ref, cos_ref, sin_ref, h_ref, kf_ref, vf_ref, kh_ref, vh_ref, wb_ref, *, n_prompt_blocks, kvw):
    i = pl.program_id(0)

    @pl.when(i == 0)
    def _():
        wb_ref[...] = w_ref[...].astype(wb_ref.dtype)

    tm = h_ref.shape[0]
    n_parts = 2 if tm % 32 == 0 else 1

    def run(x_ref):
        for r in range(n_parts):
            rows = slice(r * (tm // n_parts), (r + 1) * (tm // n_parts))
            x = x_ref[rows, :]
            y = x * lax.rsqrt(jnp.mean(x * x, axis=-1, keepdims=True) + NORM_EPS)
            h = (y * g_ref[...]).astype(h_ref.dtype)
            h_ref[rows, :] = h
            acc = jnp.dot(h, wb_ref[...], preferred_element_type=F32)
            k = _rope(acc[:, :kvw], cos_ref.at[rows, :], sin_ref.at[rows, :])
            v = acc[:, kvw:]
            kf_ref[rows, :] = k
            vf_ref[rows, :] = v
            _store_heads(kh_ref, rows, k)
            _store_heads(vh_ref, rows, v)

    @pl.when(i < n_prompt_blocks)
    def _():
        run(xp_ref)

    @pl.when(i >= n_prompt_blocks)
    def _():
        run(xs_ref)


def _norm_kv(xp, xs, norm_g, w, cos, sin, col0, kvw, tm):
    mp, d = xp.shape
    ms = xs.shape[0]
    npb, nsb = mp // tm, ms // tm
    m = mp + ms
    g = kvw // HEAD_DIM
    tn = 2 * kvw
    assert col0 % tn == 0
    once = pl.Buffered(1)
    tab = pl.BlockSpec((tm, LANES), lambda i: (i, 0))
    flat = pl.BlockSpec((tm, kvw), lambda i: (i, 0))
    heads = pl.BlockSpec((g, tm, HEAD_DIM), lambda i: (0, i, 0))
    xs_spec = (pl.BlockSpec((tm, d), lambda i: (0, 0), pipeline_mode=once) if nsb == 1
               else pl.BlockSpec((tm, d), lambda i: (jnp.maximum(i - npb, 0), 0)))
    blocks = [((tm, d), F32), ((tm, d), BF16), ((tm, 2 * tn), F32), ((tm, 2 * tn), BF16), ((tm, 2 * LANES), F32)]
    temps = [((tm, d), F32)] * 2 + [((d, tn), F32), ((d, tn), BF16)]
    return pl.pallas_call(
        functools.partial(_norm_kv_kernel, n_prompt_blocks=npb, kvw=kvw),
        grid=(npb + nsb,),
        in_specs=[
            pl.BlockSpec((tm, d), lambda i: (jnp.minimum(i, npb - 1), 0)),
            xs_spec,
            pl.BlockSpec((1, d), lambda i: (0, 0)),
            pl.BlockSpec((d, tn), lambda i: (0, col0 // tn), pipeline_mode=once),
            tab, tab,
        ],
        out_specs=[pl.BlockSpec((tm, d), lambda i: (i, 0)), flat, flat, heads, heads],
        out_shape=[jax.ShapeDtypeStruct((m, d), BF16)] + [jax.ShapeDtypeStruct((m, kvw), F32)] * 2
        + [jax.ShapeDtypeStruct((g, m, HEAD_DIM), BF16)] * 2,
        scratch_shapes=[pltpu.VMEM((d, tn), BF16)],
        compiler_params=_params(("arbitrary",), _vmem_estimate(blocks, temps)),
        name="norm_kv",
    )(xp, xs, norm_g.reshape(1, d), w, cos, sin)


def _rope(acc, cos_ref, sin_ref):
    tm, tn = acc.shape
    reps = tn // LANES
    cos = jnp.tile(cos_ref[...], (1, reps))
    sin = jnp.tile(sin_ref[...], (1, reps))
    lane = lax.broadcasted_iota(jnp.int32, (tm, tn), 1)
    low = (lane % HEAD_DIM) < (HEAD_DIM // 2)
    partner = jnp.where(low, pltpu.roll(acc, tn - HEAD_DIM // 2, 1), pltpu.roll(acc, HEAD_DIM // 2, 1))
    return acc * cos + partner * sin


def _store_heads(o_ref, rows, val):
    for h in range(val.shape[1] // HEAD_DIM):
        o_ref[h, rows, :] = val[:, h * HEAD_DIM:(h + 1) * HEAD_DIM].astype(o_ref.dtype)


def _cast_weight(w_ref, wb_ref):
    @pl.when(pl.program_id(1) == 0)
    def _():
        wb_ref[...] = w_ref[...].astype(wb_ref.dtype)


def _stream_weights(w_hbm, wbuf_ref, stage_ref, sem_ref, *, col_of_block, n_blocks, n_chunks):
    d, tn = wbuf_ref.shape[1:]
    ck = d // n_chunks
    total = n_blocks * n_chunks
    j = pl.program_id(0)
    t = j * n_chunks + pl.program_id(1)

    def aligned(x, a):
        return x if isinstance(x, int) else pl.multiple_of(x, a)

    def chunk_copy(blk, chunk, slot):
        src = w_hbm.at[pl.ds(aligned(chunk * ck, ck), ck), pl.ds(aligned(col_of_block(blk), LANES), tn)]
        return pltpu.make_async_copy(src, stage_ref.at[slot], sem_ref.at[slot])

    def generation(s):
        s = jnp.asarray(s, jnp.int32)
        k = s + n_chunks - 1
        past = (k > total - 1).astype(jnp.int32)
        k = jnp.minimum(k, total - 1)
        blk, chunk = k // n_chunks, k % n_chunks
        return chunk_copy(blk, chunk, s % 2), (blk + past) % 2, chunk

    def land(copy, half, chunk, slot):
        copy.wait()
        wbuf_ref[half, pl.ds(aligned(chunk * ck, ck), ck), :] = stage_ref[slot].astype(wbuf_ref.dtype)

    @pl.when(t == 0)
    def _():
        slot_of = lambda c: (c + n_chunks - 1) % 2
        first = [chunk_copy(0, c, slot_of(c)) for c in range(n_chunks)]
        first[0].start()
        for c in range(n_chunks - 1):
            first[c + 1].start()
            land(first[c], 0, c, slot_of(c))

    @pl.when(t + 1 < total)
    def _():
        generation(t + 1)[0].start()

    copy, half, chunk = generation(t)
    land(copy, half, chunk, t % 2)
    return wbuf_ref.at[j % 2]


def _weight_stream_scratch(k, tn, n_chunks):
    assert k % (8 * n_chunks) == 0
    shapes = [((2, k, tn), BF16), ((2, k // n_chunks, tn), F32)]
    return shapes, [pltpu.VMEM(s, t) for s, t in shapes] + [pltpu.SemaphoreType.DMA((2,))]


def _row_parts(tm, n=2):
    n = n if tm % (16 * n) == 0 and tm >= 1024 else 1
    return [slice(r * (tm // n), (r + 1) * (tm // n)) for r in range(n)]


def _proj_q_kernel(h_ref, w_ref, cos_ref, sin_ref, q_ref, wb_ref):
    _cast_weight(w_ref, wb_ref)
    for rows in _row_parts(h_ref.shape[0], 4):
        acc = jnp.dot(h_ref[rows, :], wb_ref[...], preferred_element_type=F32)
        _store_heads(q_ref, rows, _rope(acc, cos_ref.at[rows, :], sin_ref.at[rows, :]) * (HEAD_DIM ** -0.5 * LOG2_E))


def _proj_act_kernel(h_ref, w_hbm, o_ref, wbuf_ref, stage_ref, sem_ref, *, segments, n_chunks):
    tn = o_ref.shape[1]
    n_blocks = sum(s[1] for s in segments)
    in_seg = lambda blk, s: (blk >= s[0]) & (blk < s[0] + s[1])

    def col_of_block(blk):
        if isinstance(blk, int):
            return next(s[2] + (blk - s[0]) * tn for s in segments if s[0] <= blk < s[0] + s[1])
        return sum(jnp.where(in_seg(blk, s), s[2] + (blk - s[0]) * tn, 0) for s in segments)

    wb_ref = _stream_weights(w_hbm, wbuf_ref, stage_ref, sem_ref, col_of_block=col_of_block, n_blocks=n_blocks, n_chunks=n_chunks)
    acts = {s[3] for s in segments}
    assert acts in ({"none"}, {"silu"}, {"sigmoid"}, {"silu", "sigmoid"})
    j = pl.program_id(0)
    for rows in _row_parts(h_ref.shape[0]):
        acc = jnp.dot(h_ref[rows, :], wb_ref[...], preferred_element_type=F32)
        if acts != {"none"}:
            sig = _sigmoid(acc)
            if acts == {"silu"}:
                acc = acc * sig
            elif acts == {"sigmoid"}:
                acc = sig
            else:
                is_silu = functools.reduce(lambda a, b: a | b, [in_seg(j, s) for s in segments if s[3] == "silu"])
                acc = jnp.where(is_silu, acc * sig, sig)
        o_ref[rows, :] = acc.astype(o_ref.dtype)


def _proj_specs(d, tm, tn, col0):
    assert col0 % tn == 0
    cb0 = col0 // tn
    return [pl.BlockSpec((tm, d), lambda j, i: (i, 0)), pl.BlockSpec((d, tn), lambda j, i: (0, cb0 + j))]


def _proj_vmem(d, tm, tn, outs):
    return _vmem_estimate([((tm, d), BF16), ((d, tn), F32)] + outs, [((tm, tn), F32)] * 3 + [((d, tn), BF16)])


def _proj_q(h, w, cos, sin, aw, tm, tn):
    m, d = h.shape
    nh = aw // HEAD_DIM
    tab = pl.BlockSpec((tm, LANES), lambda j, i: (i, 0))
    return pl.pallas_call(
        _proj_q_kernel,
        grid=(aw // tn, m // tm),
        in_specs=_proj_specs(d, tm, tn, 0) + [tab, tab],
        out_specs=pl.BlockSpec((tn // HEAD_DIM, tm, HEAD_DIM), lambda j, i: (j, i, 0)),
        out_shape=jax.ShapeDtypeStruct((nh, m, HEAD_DIM), BF16),
        scratch_shapes=[pltpu.VMEM((d, tn), BF16)],
        compiler_params=_params(("arbitrary", "arbitrary"), _proj_vmem(d, tm, tn, [((tm, 2 * tn), BF16), ((tm, 2 * LANES), F32)])),
        name="proj_q",
    )(h, w, cos, sin)


def _proj_act(h, w, runs, tm, tn, name, out_dtype=BF16):
    m, d = h.shape
    n_chunks = m // tm
    segments, b0 = [], 0
    for col0, width, act in runs:
        assert width % tn == 0 and col0 % LANES == 0
        segments.append((b0, width // tn, col0, act))
        b0 += width // tn
    n_blocks, ncols = b0, b0 * tn
    scratch, scratch_shapes = _weight_stream_scratch(d, tn, n_chunks)
    return pl.pallas_call(
        functools.partial(_proj_act_kernel, segments=tuple(segments), n_chunks=n_chunks),
        grid=(n_blocks, n_chunks),
        in_specs=[pl.BlockSpec((tm, d), lambda j, i: (i, 0)), pl.BlockSpec(memory_space=pl.ANY)],
        out_specs=pl.BlockSpec((tm, tn), lambda j, i: (i, j)),
        out_shape=jax.ShapeDtypeStruct((m, ncols), out_dtype),
        scratch_shapes=scratch_shapes,
        compiler_params=_params(("arbitrary", "arbitrary"),
                                _vmem_estimate([((tm, d), BF16), ((tm, tn), out_dtype)], scratch + [((tm // len(_row_parts(tm)), tn), F32)] * 3)),
        name=name,
    )(h, w)


def _attn_weights(q, kw, sink, n_valid):
    st = lax.dot_general(kw, q, (((1,), (1,)), ((), ())), preferred_element_type=F32)
    if n_valid is not None:
        row = lax.broadcasted_iota(jnp.int32, st.shape, 0)
        st = jnp.where(row < n_valid, st, -jnp.inf)
    m = jnp.maximum(jnp.max(st, axis=0, keepdims=True), sink)
    e = jnp.exp2(st - m)
    return e.astype(BF16), jnp.sum(e, axis=0, keepdims=True) + jnp.exp2(sink - m)


def _attn_values(vw, e, denom):
    ot = jnp.dot(vw.astype(F32).T.astype(BF16), e, preferred_element_type=F32) / denom
    return ot.T


def _store_unit(o_ref, za_ref, row0, col0, o, rep):
    for r in range(0, rep, 2):
        pair = jnp.concatenate([o[r * CHUNK:(r + 1) * CHUNK], o[(r + 1) * CHUNK:(r + 2) * CHUNK]], axis=1)
        cols = slice(col0 + r * HEAD_DIM, col0 + (r + 2) * HEAD_DIM)
        gate = za_ref[row0:row0 + CHUNK, cols].astype(F32)
        o_ref[row0:row0 + CHUNK, cols] = (pair * gate).astype(o_ref.dtype)


def _attn_kernel(q_ref, k_ref, v_ref, ks_ref, vs_ref, sink_ref, za_ref, wo_ref, o_ref, wob_ref,
                 *, n_units, n_heads, rep, steps_per_stream, n_prompt_steps, n_cast_steps):
    step = pl.program_id(1)
    win = WINDOW + CHUNK

    @pl.when(pl.program_id(0) * (n_prompt_steps + 1) + step < n_cast_steps)
    def _():
        wob_ref[...] = wo_ref[...].astype(wob_ref.dtype)

    def run(windows):
        staged = []
        for hd, c, kw, vw, n_valid in windows:
            q = q_ref[hd * rep:(hd + 1) * rep, c * CHUNK:(c + 1) * CHUNK, :].reshape(rep * CHUNK, HEAD_DIM)
            staged.append((hd, c, vw) + _attn_weights(q, kw, sink_ref[hd] * LOG2_E, n_valid))
        for hd, c, vw, e, denom in staged:
            _store_unit(o_ref, za_ref, c * CHUNK, hd * rep * HEAD_DIM, _attn_values(vw, e, denom), rep)

    @pl.when(step < n_prompt_steps)
    def _():
        cb = step % steps_per_stream
        windows = []
        for hd in range(n_heads):
            for c in range(n_units):
                start = pl.multiple_of(jnp.maximum(cb * n_units + c - WINDOW // CHUNK, 0) * CHUNK, CHUNK)
                n_valid = jnp.where(cb == 0, (c + 1) * CHUNK, win) if c < WINDOW // CHUNK else None
                windows.append((hd, c, k_ref[hd, pl.ds(start, win), :], v_ref[hd, pl.ds(start, win), :], n_valid))
        run(windows)

    @pl.when(step == n_prompt_steps)
    def _():
        run([(hd, b, ks_ref[hd, b * win:(b + 1) * win, :], vs_ref[hd, b * win:(b + 1) * win, :], None)
             for hd in range(n_heads) for b in range(n_units)])


def _attention(q_hm, k_hm, v_hm, ks, vs, sink_rows, za, w_out, batch, seq, dec_batch, aw):
    nh, m, _ = q_hm.shape
    g = k_hm.shape[0]
    rep = nh // g
    win = WINDOW + CHUNK
    n_units = dec_batch
    rows = n_units * CHUNK
    assert seq % rows == 0 and n_units >= WINDOW // CHUNK
    steps_per_stream = seq // rows
    n_prompt_steps = batch * steps_per_stream
    nhd = 2 if g % 2 == 0 else 1
    blocks = ([((nhd * rep, rows, LANES), BF16)] + [((nhd * seq, LANES), BF16)] * 2 + [((nhd * dec_batch * win, LANES), BF16)] * 2
              + [((rows, nhd * rep * HEAD_DIM), BF16)] * 2)
    temps = [((rep * CHUNK, 2 * LANES), F32)] * (4 * n_units * nhd)
    stream = lambda gi, s: (gi, jnp.minimum(s // steps_per_stream, batch - 1), 0)
    dk, dn = w_out.shape
    n_steps = (g // nhd) * (n_prompt_steps + 1)
    n_cast_steps = max(c for c in (32, 16, 8) if c <= n_steps)
    assert dk % (16 * n_cast_steps) == 0
    cast_rows = dk // n_cast_steps
    cast_block = pl.BlockSpec((cast_rows, dn), lambda gi, s: (jnp.minimum(gi * (n_prompt_steps + 1) + s, n_cast_steps - 1), 0))
    blocks += [((cast_rows, dn), F32), ((cast_rows, dn), BF16)]
    return pl.pallas_call(
        functools.partial(_attn_kernel, n_units=n_units, n_heads=nhd, rep=rep, steps_per_stream=steps_per_stream,
                          n_prompt_steps=n_prompt_steps, n_cast_steps=n_cast_steps),
        grid=(g // nhd, n_prompt_steps + 1),
        in_specs=[
            pl.BlockSpec((nhd * rep, rows, HEAD_DIM), lambda gi, s: (gi, s, 0)),
            pl.BlockSpec((nhd, seq, HEAD_DIM), stream),
            pl.BlockSpec((nhd, seq, HEAD_DIM), stream),
            pl.BlockSpec((nhd, dec_batch * win, HEAD_DIM), lambda gi, s: (gi, 0, 0)),
            pl.BlockSpec((nhd, dec_batch * win, HEAD_DIM), lambda gi, s: (gi, 0, 0)),
            pl.BlockSpec((nhd, 1, rep * CHUNK), lambda gi, s: (gi, 0, 0)),
            pl.BlockSpec((rows, nhd * rep * HEAD_DIM), lambda gi, s: (s, gi)),
            cast_block,
        ],
        out_specs=[pl.BlockSpec((rows, nhd * rep * HEAD_DIM), lambda gi, s: (s, gi)), cast_block],
        out_shape=[jax.ShapeDtypeStruct((m, aw), BF16), jax.ShapeDtypeStruct((dk, dn), BF16)],
        compiler_params=_params(("arbitrary",) * 2, _vmem_estimate(blocks, temps)),
        name="attention",
    )(q_hm, k_hm, v_hm, ks, vs, sink_rows, za, w_out)


def _ssm_disc_kernel(lre_ref, lim_ref, ldt_ref, are_ref, aim_ref, dre_ref, dim_ref, fre_ref, fim_ref):
    lr = jnp.minimum(lre_ref[...], LAMBDA_RE_MAX)
    li = lim_ref[...]
    dt = jnp.exp(ldt_ref[...])
    mag = jnp.exp(lr * dt)
    a_re = mag * jnp.cos(li * dt)
    a_im = mag * jnp.sin(li * dt)
    den = lr * lr + li * li
    nr = a_re - 1.0
    fre_ref[...] = (nr * lr + a_im * li) / den
    fim_ref[...] = (a_im * lr - nr * li) / den
    are_ref[...] = a_re
    aim_ref[...] = a_im
    for _ in range(SSM_POW_BITS):
        a_re, a_im = a_re * a_re - a_im * a_im, 2.0 * a_re * a_im
    dre_ref[...] = a_re
    dim_ref[...] = a_im


def _cmul(ar, ai, br, bi):
    return ar * br - ai * bi, ar * bi + ai * br


def _ssm_build_kernel(are_ref, aim_ref, dre_ref, dim_ref, fre_ref, fim_ref, btr_ref, bti_ref, cr_ref, ci_ref, dv_ref,
                      met_ref, ft_ref, a1_ref, a2_ref, a2s_ref, *, groups):
    lc = SSM_CHUNK * SSM_GROUP
    sub = lax.broadcasted_iota(jnp.int32, (SSM_GROUP, lc), 0)
    lane = lax.broadcasted_iota(jnp.int32, (SSM_GROUP, lc), 1)
    for gi in range(groups):
        row = slice(gi, gi + 1)
        a_re, a_im = are_ref[row, :], aim_ref[row, :]
        pw = [(jnp.ones_like(a_re), jnp.zeros_like(a_re))]
        for _ in range(SSM_CHUNK):
            pw.append(_cmul(pw[-1][0], pw[-1][1], a_re, a_im))
        c_re, c_im = cr_ref[gi], ci_ref[gi]
        wt = [_cmul(pr, pi, c_re, c_im) for pr, pi in pw]
        wt_re = jnp.concatenate([w[0] for w in wt[:SSM_CHUNK]], axis=0)
        wt_im = jnp.concatenate([w[1] for w in wt[:SSM_CHUNK]], axis=0)
        et_re = jnp.concatenate([w[0] for w in wt[1:]], axis=0)
        et_im = jnp.concatenate([w[1] for w in wt[1:]], axis=0)
        bb_re, bb_im = _cmul(fre_ref[row, :], fim_ref[row, :], btr_ref[gi], bti_ref[gi])
        r0 = lax.dot_general(jnp.concatenate([bb_re, bb_im], axis=1), jnp.concatenate([wt_re, -wt_im], axis=1),
                             (((1,), (1,)), ((), ())), preferred_element_type=F32, precision=lax.Precision.HIGHEST)
        r0 = r0 + jnp.where(sub == lane, dv_ref[gi], 0.0)
        rows = [r0] + [jnp.where(lane >= s * SSM_GROUP, pltpu.roll(r0, s * SSM_GROUP, 1), 0.0) for s in range(1, SSM_CHUNK)]
        mt = jnp.concatenate(rows, axis=0).T
        met_ref[gi] = jnp.concatenate([mt, et_re, -et_im], axis=1).astype(met_ref.dtype)
        fb = [_cmul(pw[SSM_CHUNK - 1 - s][0], pw[SSM_CHUNK - 1 - s][1], bb_re, bb_im) for s in range(SSM_CHUNK)]
        f_all = jnp.concatenate([jnp.concatenate([x[0] for x in fb], axis=0), jnp.concatenate([x[1] for x in fb], axis=0)], axis=1)
        ft_ref[gi] = f_all.T.astype(ft_ref.dtype)
        d_re, d_im = dre_ref[row, :], dim_ref[row, :]
        a1_ref[gi] = jnp.concatenate([d_re, d_re], axis=1)
        a2_ref[gi] = jnp.concatenate([-d_im, d_im], axis=1)
        a2s_ref[gi] = jnp.concatenate([d_im, -d_im], axis=1)


def _ssm_params(lambda_re, lambda_im, log_dt, b_re, b_im, c_re, c_im, d_skip):
    ng, p = lambda_re.shape
    lc = SSM_CHUNK * SSM_GROUP
    full = pl.BlockSpec((ng, p), lambda: (0, 0))
    disc = pl.pallas_call(
        _ssm_disc_kernel,
        in_specs=[full, full, pl.BlockSpec((ng, 1), lambda: (0, 0))],
        out_specs=[full] * 6,
        out_shape=[jax.ShapeDtypeStruct((ng, p), F32)] * 6,
        name="ssm_disc",
    )(lambda_re, lambda_im, log_dt.reshape(ng, 1))
    gb = _pick(ng, (8,))
    bt_re = jnp.swapaxes(b_re, 1, 2)
    bt_im = jnp.swapaxes(b_im, 1, 2)
    dvec = jnp.pad(d_skip, ((0, 0), (0, lc - SSM_GROUP))).reshape(ng, 1, lc)
    rows = pl.BlockSpec((gb, p), lambda i: (i, 0))
    mats = pl.BlockSpec((gb, SSM_GROUP, p), lambda i: (i, 0, 0))
    dec = pl.BlockSpec((gb, 1, 2 * p), lambda i: (i, 0, 0))
    blocks = ([((gb, LANES), F32)] * 6 + [((gb, SSM_GROUP, LANES), F32)] * 4 + [((gb, 8, lc), F32)]
              + [((gb, lc, lc + 2 * p), BF16), ((gb, 2 * p, lc), BF16)] + [((gb, 8, LANES), F32)] * 3)
    met, ft, a1, a2, a2s = pl.pallas_call(
        functools.partial(_ssm_build_kernel, groups=gb),
        grid=(ng // gb,),
        in_specs=[rows] * 6 + [mats] * 4 + [pl.BlockSpec((gb, 1, lc), lambda i: (i, 0, 0))],
        out_specs=[pl.BlockSpec((gb, lc, lc + 2 * p), lambda i: (i, 0, 0)), pl.BlockSpec((gb, 2 * p, lc), lambda i: (i, 0, 0)), dec, dec, dec],
        out_shape=[jax.ShapeDtypeStruct((ng, lc, lc + 2 * p), BF16), jax.ShapeDtypeStruct((ng, 2 * p, lc), BF16)]
        + [jax.ShapeDtypeStruct((ng, 1, 2 * p), F32)] * 3,
        compiler_params=_params(("arbitrary",), _vmem_estimate(blocks, [((lc, lc + 2 * p), F32)] * 8)),
        name="ssm_build",
    )(*disc, bt_re, bt_im, c_re, c_im, dvec)
    flat = lambda a: a.reshape(1, ng * 2 * p)
    return met, ft, flat(a1), flat(a2), flat(a2s)


def _chunk_blocks(nk):
    return [(k0, min(LANES, nk - k0)) for k0 in range(0, nk, LANES)]


def _pad_rows(x):
    n = x.shape[0]
    return x if n == LANES else jnp.concatenate([x, jnp.zeros((LANES - n, x.shape[1]), x.dtype)], axis=0)


def _ssm_inputs_phase(u_ref, ft_ref, rhs_ref, s_ref, *, nk, p):
    groups = LANES // SSM_GROUP
    blocks = _chunk_blocks(nk)
    xt = [[_pad_rows(u_ref[pl.ds(k0 * SSM_CHUNK + s, n, stride=SSM_CHUNK), :].astype(BF16)).T for k0, n in blocks]
          for s in range(SSM_CHUNK)]
    for gi in range(groups):
        rhs = jnp.concatenate([jnp.concatenate([xt[s][kb][gi * SSM_GROUP:(gi + 1) * SSM_GROUP, :] for kb in range(len(blocks))], axis=1)
                               for s in range(SSM_CHUNK)], axis=0)
        rhs_ref[gi] = rhs
        st = jnp.dot(ft_ref[gi], rhs, preferred_element_type=F32)
        for kb, (k0, n) in enumerate(blocks):
            s_ref[k0:k0 + n, gi * 2 * p:(gi + 1) * 2 * p] = st[:, kb * LANES:(kb + 1) * LANES].T[:n, :]


def _ssm_outputs_phase(rhs_ref, h_ref, met_ref, y_ref, *, nk, p):
    groups = LANES // SSM_GROUP
    blocks = _chunk_blocks(nk)
    zt = [[None] * groups for _ in range(SSM_CHUNK)]
    for gi in range(groups):
        ht = jnp.concatenate([_pad_rows(h_ref[k0:k0 + n, gi * 2 * p:(gi + 1) * 2 * p]).T for k0, n in blocks], axis=1)
        rhs = jnp.concatenate([rhs_ref[gi], ht.astype(BF16)], axis=0)
        yt = jnp.dot(met_ref[gi], rhs, preferred_element_type=F32)
        for t in range(SSM_CHUNK):
            zt[t][gi] = yt[t * SSM_GROUP:(t + 1) * SSM_GROUP, :]
    for t in range(SSM_CHUNK):
        z = jnp.concatenate(zt[t], axis=0)
        for kb, (k0, n) in enumerate(blocks):
            y_ref[pl.ds(k0 * SSM_CHUNK + t, n, stride=SSM_CHUNK), :] = z[:, kb * LANES:(kb + 1) * LANES].T[:n, :]


def _ssm_scan_phase(s_ref, h0_ref, a1_ref, a2_ref, a2s_ref, h_ref, fin_ref, ss_ref, *, batch, kp, dec_batch, ks, p):
    def swap_halves(x):
        lane = lax.broadcasted_iota(jnp.int32, x.shape, 1)
        return jnp.where(lane % (2 * p) < p, pltpu.roll(x, x.shape[1] - p, 1), pltpu.roll(x, p, 1))

    ss_ref[...] = swap_halves(s_ref[...])
    a1, a2, a2s = a1_ref[...], a2_ref[...], a2s_ref[...]
    zero = jnp.zeros_like(a1)

    def step(k, h, hs):
        h_ref[pl.ds(k, 1), :] = h
        s = s_ref[pl.ds(k, 1), :]
        ss = ss_ref[pl.ds(k, 1), :]
        return a1 * h + a2 * hs + s, a1 * hs + a2s * h + ss

    def body(k, carry):
        out = []
        for b in range(batch):
            out.extend(step(b * kp + k, carry[2 * b], carry[2 * b + 1]))
        return tuple(out)

    fin = lax.fori_loop(0, kp, body, (zero,) * (2 * batch), unroll=4 if kp % 4 == 0 else 1)
    for b in range(batch):
        fin_ref[b:b + 1, :] = fin[2 * b]
    h0s_all = swap_halves(h0_ref[...])
    for b in range(dec_batch):
        h, hs = h0_ref[b:b + 1, :], h0s_all[b:b + 1, :]
        for k in range(ks):
            h, hs = step(batch * kp + b * ks + k, h, hs)
        fin_ref[batch + b:batch + b + 1, :] = h


def _ssm_kernel(u_ref, ft_ref, met_ref, h0_ref, a1_ref, a2_ref, a2s_ref, y_ref, fin_ref, rhs_ref, s_ref, ss_ref, h_ref,
                *, batch, kp, dec_batch, ks, p):
    nk = batch * kp + dec_batch * ks
    _ssm_inputs_phase(u_ref, ft_ref, rhs_ref, s_ref, nk=nk, p=p)
    _ssm_scan_phase(s_ref, h0_ref, a1_ref, a2_ref, a2s_ref, h_ref, fin_ref, ss_ref, batch=batch, kp=kp, dec_batch=dec_batch, ks=ks, p=p)
    _ssm_outputs_phase(rhs_ref, h_ref, met_ref, y_ref, nk=nk, p=p)


def _ssm(u, met, ft, a1, a2, a2s, h0, batch, kp, dec_batch, ks):
    m, sw = u.shape
    ng, p2, lc = ft.shape
    p = p2 // 2
    nk = m // SSM_CHUNK
    assert nk == batch * kp + dec_batch * ks and (nk % LANES) % 8 == 0
    nk_lanes = -(-nk // LANES) * LANES
    groups = LANES // SSM_GROUP
    lb = groups * p2
    nseq = batch + dec_batch
    u_spec = pl.BlockSpec((m, LANES), lambda i: (0, i))
    rowb = pl.BlockSpec((1, lb), lambda i: (0, i))
    blocks = [((m, LANES), F32)] * 2 + [((groups, p2, lc), BF16), ((groups, lc, lc + p2), BF16), ((nseq, lb), F32)]
    scratch = [((groups, lc, nk_lanes), BF16)] + [((nk, lb), F32)] * 3
    return pl.pallas_call(
        functools.partial(_ssm_kernel, batch=batch, kp=kp, dec_batch=dec_batch, ks=ks, p=p),
        grid=(ng // groups,),
        in_specs=[u_spec, pl.BlockSpec((groups, p2, lc), lambda i: (i, 0, 0)), pl.BlockSpec((groups, lc, lc + p2), lambda i: (i, 0, 0)),
                  pl.BlockSpec((dec_batch, lb), lambda i: (0, i)), rowb, rowb, rowb],
        out_specs=[u_spec, pl.BlockSpec((nseq, lb), lambda i: (0, i))],
        out_shape=[jax.ShapeDtypeStruct((m, sw), F32), jax.ShapeDtypeStruct((nseq, ng * p2), F32)],
        scratch_shapes=[pltpu.VMEM(s, d) for s, d in scratch],
        compiler_params=_params(("arbitrary",), _vmem_estimate(blocks, scratch + [((lc + p2, nk_lanes), BF16), ((lc, nk_lanes), F32)])),
        name="ssm",
    )(u, ft, met, h0, a1, a2, a2s)


def _glu_kernel(y_ref, w_hbm, ba_ref, bg_ref, zs_ref, o_ref, wa_buf, wa_stage, wa_sem, wg_buf, wg_stage, wg_sem, *, n_blocks, n_chunks):
    tn = o_ref.shape[1]
    stream = functools.partial(_stream_weights, w_hbm, n_blocks=n_blocks, n_chunks=n_chunks)
    wa_ref = stream(wa_buf, wa_stage, wa_sem, col_of_block=lambda blk: blk * tn)
    wg_ref = stream(wg_buf, wg_stage, wg_sem, col_of_block=lambda blk: (n_blocks + blk) * tn)
    for rows in _row_parts(y_ref.shape[0]):
        y = y_ref[rows, :].astype(BF16)
        a = jnp.dot(y, wa_ref[...], preferred_element_type=F32) + ba_ref[...]
        g = jnp.dot(y, wg_ref[...], preferred_element_type=F32) + bg_ref[...]
        o_ref[rows, :] = (a * _sigmoid(g) * zs_ref[rows, :].astype(F32)).astype(o_ref.dtype)


def _glu(y, w_glu, b_glu, acts, zs_col0, tm, tn):
    m, sw = y.shape
    nb, n_chunks = sw // tn, m // tm
    zb0 = zs_col0 // tn
    shapes, scratch = _weight_stream_scratch(sw, tn, n_chunks)
    blocks = [((tm, sw), F32), ((tm, tn), BF16), ((tm, tn), BF16)]
    return pl.pallas_call(
        functools.partial(_glu_kernel, n_blocks=nb, n_chunks=n_chunks),
        grid=(nb, n_chunks),
        in_specs=[
            pl.BlockSpec((tm, sw), lambda j, i: (i, 0)),
            pl.BlockSpec(memory_space=pl.ANY),
            pl.BlockSpec((1, tn), lambda j, i: (0, j)),
            pl.BlockSpec((1, tn), lambda j, i: (0, nb + j)),
            pl.BlockSpec((tm, tn), lambda j, i: (i, zb0 + j)),
        ],
        out_specs=pl.BlockSpec((tm, tn), lambda j, i: (i, j)),
        out_shape=jax.ShapeDtypeStruct((m, sw), BF16),
        scratch_shapes=scratch * 2,
        compiler_params=_params(("arbitrary", "arbitrary"), _vmem_estimate(blocks, shapes * 2 + [((tm, tn), F32)] * 4)),
        name="glu",
    )(y, w_glu, b_glu, b_glu, acts)


def _merge_kernel(xa_ref, xs_ref, wpa_hbm, wps_hbm, ga_ref, gs_ref, o_ref, wa_buf, wa_stage, wa_sem, ws_buf, ws_stage, ws_sem,
                  *, n_blocks, n_chunks):
    tn = o_ref.shape[1]
    col = lambda blk: blk * tn
    wpa_ref = _stream_weights(wpa_hbm, wa_buf, wa_stage, wa_sem, col_of_block=col, n_blocks=n_blocks, n_chunks=n_chunks)
    wps_ref = _stream_weights(wps_hbm, ws_buf, ws_stage, ws_sem, col_of_block=col, n_blocks=n_blocks, n_chunks=n_chunks)
    for rows in _row_parts(xa_ref.shape[0]):
        br_a = jnp.dot(xa_ref[rows, :], wpa_ref[...], preferred_element_type=F32)
        br_s = jnp.dot(xs_ref[rows, :], wps_ref[...], preferred_element_type=F32)
        o_ref[rows, :] = (ga_ref[rows, :].astype(F32) * br_a + gs_ref[rows, :].astype(F32) * br_s).astype(o_ref.dtype)


def _merge(xa, xs, w_pa, w_ps, acts, gate_col0, tm, tn):
    m, aw = xa.shape
    sw = xs.shape[1]
    d = w_pa.shape[1]
    nb, n_chunks = d // tn, m // tm
    gb0 = gate_col0 // tn
    shapes_a, scratch_a = _weight_stream_scratch(aw, tn, n_chunks)
    shapes_s, scratch_s = _weight_stream_scratch(sw, tn, n_chunks)
    blocks = [((tm, aw), BF16), ((tm, sw), BF16)] + [((tm, tn), BF16)] * 3
    return pl.pallas_call(
        functools.partial(_merge_kernel, n_blocks=nb, n_chunks=n_chunks),
        grid=(nb, n_chunks),
        in_specs=[
            pl.BlockSpec((tm, aw), lambda j, i: (i, 0)),
            pl.BlockSpec((tm, sw), lambda j, i: (i, 0)),
            pl.BlockSpec(memory_space=pl.ANY),
            pl.BlockSpec(memory_space=pl.ANY),
            pl.BlockSpec((tm, tn), lambda j, i: (i, gb0 + j)),
            pl.BlockSpec((tm, tn), lambda j, i: (i, gb0 + nb + j)),
        ],
        out_specs=pl.BlockSpec((tm, tn), lambda j, i: (i, j)),
        out_shape=jax.ShapeDtypeStruct((m, d), BF16),
        scratch_shapes=scratch_a + scratch_s,
        compiler_params=_params(("arbitrary", "arbitrary"), _vmem_estimate(blocks, shapes_a + shapes_s + [((tm, tn), F32)] * 4)),
        name="merge",
    )(xa, xs, w_pa, w_ps, acts, acts)


def _out_kernel(mg_ref, w_ref, x_ref, g_ref, o_ref, ssq_ref, *, n_col_blocks, tn):
    j = pl.program_id(1)
    blk = x_ref[...] + jnp.dot(mg_ref[...], w_ref[...], preferred_element_type=F32)
    o_ref[:, pl.ds(pl.multiple_of(j * tn, tn), tn)] = blk
    part = jnp.sum(blk * blk, axis=-1, keepdims=True)

    @pl.when(j == 0)
    def _():
        ssq_ref[...] = part

    @pl.when(j > 0)
    def _():
        ssq_ref[...] += part

    @pl.when(j == n_col_blocks - 1)
    def _():
        scale = lax.rsqrt(ssq_ref[...] / (n_col_blocks * tn) + NORM_EPS)
        o_ref[...] = o_ref[...] * scale * g_ref[...]


def _out(merged, row_block0, w_out, x, final_g, tm, tn):
    mx, d = x.shape
    nb = d // tn
    blocks = [((tm, d), BF16), ((d, tn), BF16), ((tm, tn), F32), ((tm, d), F32)]
    return pl.pallas_call(
        functools.partial(_out_kernel, n_col_blocks=nb, tn=tn),
        grid=(mx // tm, nb),
        in_specs=[
            pl.BlockSpec((tm, d), lambda i, j: (row_block0 + i, 0)),
            pl.BlockSpec((d, tn), lambda i, j: (0, j)),
            pl.BlockSpec((tm, tn), lambda i, j: (i, j)),
            pl.BlockSpec((1, d), lambda i, j: (0, 0)),
        ],
        out_specs=pl.BlockSpec((tm, d), lambda i, j: (i, 0)),
        out_shape=jax.ShapeDtypeStruct((mx, d), F32),
        scratch_shapes=[pltpu.VMEM((tm, 1), F32)],
        compiler_params=_params(("arbitrary", "arbitrary"), _vmem_estimate(blocks, [((tm, tn), F32)] * 3 + [((tm, LANES), F32)])),
        name="out_norm",
    )(merged, w_out, x, final_g.reshape(1, d))


def _rope_tables(positions):
    half = HEAD_DIM // 2
    inv_freq = ROPE_THETA ** (-jnp.arange(half, dtype=F32) / half)
    ang = positions.astype(F32)[:, None] * inv_freq[None, :]
    cos, sin = jnp.cos(ang), jnp.sin(ang)
    reps = LANES // HEAD_DIM
    return jnp.tile(jnp.concatenate([cos, cos], axis=1), (1, reps)), jnp.tile(jnp.concatenate([-sin, sin], axis=1), (1, reps))


def kernel(x_prompt, x_sample, cache_k, cache_v, state_ssm_re, state_ssm_im, norm_g, w_in, sink, lambda_re, lambda_im,
           log_dt, b_re, b_im, c_re, c_im, d_skip, w_glu, b_glu, w_pa, w_ps, w_out, final_g):
    depth = norm_g.shape[0]
    assert depth == 1, "one trunk layer"
    batch, seq, d = x_prompt.shape
    dec_batch, dec_seq, _ = x_sample.shape
    aw = w_pa.shape[1]
    sw = w_ps.shape[1]
    nh = aw // HEAD_DIM
    g = max(1, nh // GQA_GROUPING)
    rep = nh // g
    kvw = g * HEAD_DIM
    ng, p = lambda_re.shape[1:]
    assert dec_seq == CHUNK and cache_k.shape[2] == WINDOW and seq % (2 * CHUNK) == 0
    assert kvw % LANES == 0 and rep % 2 == 0 and sw == ng * SSM_GROUP and (1 << SSM_POW_BITS) == SSM_CHUNK
    assert 2 * p == LANES and sw % LANES == 0
    mp, ms = batch * seq, dec_batch * dec_seq
    m = mp + ms
    widths = (aw, 2 * kvw, aw, sw, sw, 2 * d)
    assert sum(widths) == w_in.shape[2]
    c_q, c_kv, c_za, c_u, c_zs, c_gate = (sum(widths[:n]) for n in range(len(widths)))
    tm = _pick(math.gcd(mp, ms), (512, 256, 128))
    tmp = max(t for t in range(16, 1153, 16) if m % t == 0)
    tn = _pick(math.gcd(c_kv, c_za, c_u, c_zs, c_gate, 2 * d), (512, 256, 128))

    xp = x_prompt.reshape(mp, d)
    xs = x_sample.reshape(ms, d)
    w_in2 = w_in.reshape(d, w_in.shape[2])
    positions = jnp.concatenate([jnp.tile(jnp.arange(seq, dtype=jnp.int32), batch),
                                 jnp.tile(PAST_LEN + jnp.arange(dec_seq, dtype=jnp.int32), dec_batch)])
    cos, sin = _rope_tables(positions)

    h, k_f, v_f, k_hm, v_hm = _norm_kv(xp, xs, norm_g[0], w_in2, cos, sin, c_kv, kvw, tm)
    q_hm = _proj_q(h, w_in2, cos, sin, aw, tmp, tn)
    tna = _pick(math.gcd(aw, sw), (1024, 512, 256))
    acts = _proj_act(h, w_in2, [(c_za, aw, "silu"), (c_zs, sw, "silu"), (c_gate, 2 * d, "sigmoid")], tmp, tna, "proj_acts")
    u = _proj_act(h, w_in2, [(c_u, sw, "none")], tmp, tna, "proj_u", out_dtype=F32)

    to_heads = lambda c: jnp.transpose(c[0], (2, 0, 1, 3)).astype(BF16)
    new_rows = lambda a: a[:, mp:].reshape(g, dec_batch, dec_seq, HEAD_DIM)
    ks = jnp.concatenate([to_heads(cache_k), new_rows(k_hm)], axis=2).reshape(g, dec_batch * (WINDOW + CHUNK), HEAD_DIM)
    vs = jnp.concatenate([to_heads(cache_v), new_rows(v_hm)], axis=2).reshape(g, dec_batch * (WINDOW + CHUNK), HEAD_DIM)
    sink_rows = jnp.repeat(sink[0].reshape(g, rep), CHUNK, axis=1).reshape(g, 1, rep * CHUNK)
    xa, w_out_b = _attention(q_hm, k_hm, v_hm, ks, vs, sink_rows, acts, w_out.reshape(d, d), batch, seq, dec_batch, aw)

    met, ft, a1, a2, a2s = _ssm_params(lambda_re[0], lambda_im[0], log_dt[0], b_re[0], b_im[0], c_re[0], c_im[0], d_skip[0])
    h0 = jnp.concatenate([state_ssm_re[0], state_ssm_im[0]], axis=-1).reshape(dec_batch, ng * 2 * p)
    y, fin = _ssm(u, met, ft, a1, a2, a2s, h0, batch, seq // SSM_CHUNK, dec_batch, dec_seq // SSM_CHUNK)

    tmm = tmp
    x_ssm = _glu(y, w_glu.reshape(sw, 2 * sw), b_glu.reshape(1, 2 * sw), acts, aw, tmm, tna)
    merged = _merge(xa, x_ssm, w_pa.reshape(aw, d), w_ps.reshape(sw, d), acts, aw + sw, tmm, tna)
    tno = _pick(d, (1024, 512, 256))
    y_prompt = _out(merged, 0, w_out_b, xp, final_g, tm, tno).reshape(batch, seq, d)
    y_sample = _out(merged, mp // tm, w_out_b, xs, final_g, tm, tno).reshape(dec_batch, dec_seq, d)

    keep = min(WINDOW, seq)
    last_rows = lambda a: jnp.stack([a[(b + 1) * seq - keep:(b + 1) * seq] for b in range(batch)]).reshape(1, batch, keep, g, HEAD_DIM)
    dec_rows = lambda a: a[mp:].reshape(dec_batch, dec_seq, g, HEAD_DIM)[None]
    fin = fin.reshape(batch + dec_batch, ng, 2, p)
    return (y_prompt, y_sample, last_rows(k_f), last_rows(v_f), fin[:batch, :, 0][None], fin[:batch, :, 1][None],
            dec_rows(k_f), dec_rows(v_f), fin[batch:, :, 0][None], fin[batch:, :, 1][None])
```

```python
import functools
import math

import jax
import jax.numpy as jnp
from jax import lax
from jax.experimental import pallas as pl
from jax.experimental.pallas import tpu as pltpu

CHUNK = 64
WINDOW = 128
HEAD_DIM = 64
GQA_GROUPING = 8
SSM_GROUP = 16
PAST_LEN = 1024
ROPE_THETA = 10000.0
NORM_EPS = 1e-5
LAMBDA_RE_MAX = -1e-4
LOG2_E = math.log2(math.e)

SSM_CHUNK = 16
SSM_POW_BITS = 4
LANES = 128
V7X_VMEM_BYTES = 64 * 1024 * 1024
V7X_VMEM_RESERVED_BYTES = 4 * 1024 * 1024
BF16 = jnp.bfloat16
F32 = jnp.float32


def _pick(n, prefs):
    for p in prefs:
        if n % p == 0:
            return p
    raise ValueError(f"no tile in {prefs} divides {n}")


def _params(sem, vmem_bytes):
    limit = min(int(vmem_bytes), V7X_VMEM_BYTES - V7X_VMEM_RESERVED_BYTES)
    return pltpu.CompilerParams(dimension_semantics=sem, vmem_limit_bytes=limit)


def _sigmoid(x):
    return 0.5 * jnp.tanh(0.5 * x) + 0.5


def _nbytes(shape, dtype):
    return math.prod(shape) * jnp.dtype(dtype).itemsize


def _vmem_estimate(blocks, temps=()):
    return 2 * (2 * sum(_nbytes(s, d) for s, d in blocks) + sum(_nbytes(s, d) for s, d in temps))


def _norm_kv_kernel(xp_ref, xs_ref, g_ref, w_ref, cos_ref, sin_ref, h_ref, kf_ref, vf_ref, kh_ref, vh_ref, wb_ref, *, n_prompt_blocks, kvw):
    i = pl.program_id(0)

    @pl.when(i == 0)
    def _():
        wb_ref[...] = w_ref[...].astype(wb_ref.dtype)

    tm = h_ref.shape[0]
    n_parts = 2 if tm % 32 == 0 else 1
    for r in range(n_parts):
        rows = slice(r * (tm // n_parts), (r + 1) * (tm // n_parts))
        x = jnp.where(i < n_prompt_blocks, xp_ref[rows, :], xs_ref[rows, :])
        y = x * lax.rsqrt(jnp.mean(x * x, axis=-1, keepdims=True) + NORM_EPS)
        h = (y * g_ref[...]).astype(h_ref.dtype)
        h_ref[rows, :] = h
        acc = jnp.dot(h, wb_ref[...], preferred_element_type=F32)
        k = _rope(acc[:, :kvw], cos_ref.at[rows, :], sin_ref.at[rows, :])
        v = acc[:, kvw:]
        kf_ref[rows, :] = k
        vf_ref[rows, :] = v
        _store_heads(kh_ref, rows, k)
        _store_heads(vh_ref, rows, v)


def _norm_kv(xp, xs, norm_g, w, cos, sin, col0, kvw, tm):
    mp, d = xp.shape
    ms = xs.shape[0]
    npb, nsb = mp // tm, ms // tm
    m = mp + ms
    g = kvw // HEAD_DIM
    tn = 2 * kvw
    assert col0 % tn == 0
    once = pl.Buffered(1)
    tab = pl.BlockSpec((tm, LANES), lambda i: (i, 0))
    flat = pl.BlockSpec((tm, kvw), lambda i: (i, 0))
    heads = pl.BlockSpec((g, tm, HEAD_DIM), lambda i: (0, i, 0))
    xs_spec = (pl.BlockSpec((tm, d), lambda i: (0, 0), pipeline_mode=once) if nsb == 1
               else pl.BlockSpec((tm, d), lambda i: (jnp.maximum(i - npb, 0), 0)))
    blocks = [((tm, d), F32), ((tm, d), BF16), ((tm, 2 * tn), F32), ((tm, 2 * tn), BF16), ((tm, 2 * LANES), F32)]
    temps = [((tm, d), F32)] * 2 + [((d, tn), F32), ((d, tn), BF16)]
    return pl.pallas_call(
        functools.partial(_norm_kv_kernel, n_prompt_blocks=npb, kvw=kvw),
        grid=(npb + nsb,),
        in_specs=[
            pl.BlockSpec((tm, d), lambda i: (jnp.minimum(i, npb - 1), 0)),
            xs_spec,
            pl.BlockSpec((1, d), lambda i: (0, 0)),
            pl.BlockSpec((d, tn), lambda i: (0, col0 // tn), pipeline_mode=once),
            tab, tab,
        ],
        out_specs=[pl.BlockSpec((tm, d), lambda i: (i, 0)), flat, flat, heads, heads],
        out_shape=[jax.ShapeDtypeStruct((m, d), BF16)] + [jax.ShapeDtypeStruct((m, kvw), F32)] * 2
        + [jax.ShapeDtypeStruct((g, m, HEAD_DIM), BF16)] * 2,
        scratch_shapes=[pltpu.VMEM((d, tn), BF16)],
        compiler_params=_params(("arbitrary",), _vmem_estimate(blocks, temps)),
        name="norm_kv",
    )(xp, xs, norm_g.reshape(1, d), w, cos, sin)


def _rope(acc, cos_ref, sin_ref):
    tm, tn = acc.shape
    reps = tn // LANES
    cos = jnp.tile(cos_ref[...], (1, reps))
    sin = jnp.tile(sin_ref[...], (1, reps))
    lane = lax.broadcasted_iota(jnp.int32, (tm, tn), 1)
    low = (lane % HEAD_DIM) < (HEAD_DIM // 2)
    partner = jnp.where(low, pltpu.roll(acc, tn - HEAD_DIM // 2, 1), pltpu.roll(acc, HEAD_DIM // 2, 1))
    return acc * cos + partner * sin


def _store_heads(o_ref, rows, val):
    for h in range(val.shape[1] // HEAD_DIM):
        o_ref[h, rows, :] = val[:, h * HEAD_DIM:(h + 1) * HEAD_DIM].astype(o_ref.dtype)


def _stream_weights(w_hbm, wbuf_ref, stage_ref, sem_ref, *, col_of_block, n_blocks, n_chunks):
    d, tn = wbuf_ref.shape[1:]
    ck = d // n_chunks
    total = n_blocks * n_chunks
    j = pl.program_id(0)
    t = j * n_chunks + pl.program_id(1)

    def aligned(x, a):
        return x if isinstance(x, int) else pl.multiple_of(x, a)

    def chunk_copy(blk, chunk, slot):
        src = w_hbm.at[pl.ds(aligned(chunk * ck, ck), ck), pl.ds(aligned(col_of_block(blk), LANES), tn)]
        return pltpu.make_async_copy(src, stage_ref.at[slot], sem_ref.at[slot])

    def generation(s):
        s = jnp.asarray(s, jnp.int32)
        k = s + n_chunks - 1
        past = (k > total - 1).astype(jnp.int32)
        k = jnp.minimum(k, total - 1)
        blk, chunk = k // n_chunks, k % n_chunks
        return chunk_copy(blk, chunk, s % 2), (blk + past) % 2, chunk

    def land(copy, half, chunk, slot):
        copy.wait()
        wbuf_ref[half, pl.ds(aligned(chunk * ck, ck), ck), :] = stage_ref[slot].astype(wbuf_ref.dtype)

    @pl.when(t == 0)
    def _():
        slot_of = lambda c: (c + n_chunks - 1) % 2
        first = [chunk_copy(0, c, slot_of(c)) for c in range(n_chunks)]
        first[0].start()
        for c in range(n_chunks - 1):
            first[c + 1].start()
            land(first[c], 0, c, slot_of(c))

    @pl.when(t + 1 < total)
    def _():
        generation(t + 1)[0].start()

    copy, half, chunk = generation(t)
    land(copy, half, chunk, t % 2)
    return wbuf_ref.at[j % 2]


def _weight_stream_scratch(k, tn, n_chunks):
    assert k % (8 * n_chunks) == 0
    shapes = [((2, k, tn), BF16), ((2, k // n_chunks, tn), F32)]
    return shapes, [pltpu.VMEM(s, t) for s, t in shapes] + [pltpu.SemaphoreType.DMA((2,))]


def _row_parts(tm, n=2):
    n = n if tm % (16 * n) == 0 and tm >= 1024 else 1
    return [slice(r * (tm // n), (r + 1) * (tm // n)) for r in range(n)]


def _proj_q_kernel(h_ref, w_hbm, cos_ref, sin_ref, q_ref, wbuf_ref, stage_ref, sem_ref, *, n_blocks, n_chunks):
    tn = wbuf_ref.shape[2]
    wb_ref = _stream_weights(w_hbm, wbuf_ref, stage_ref, sem_ref, col_of_block=lambda blk: blk * tn, n_blocks=n_blocks, n_chunks=n_chunks)
    for rows in _row_parts(h_ref.shape[0], 4):
        acc = jnp.dot(h_ref[rows, :], wb_ref[...], preferred_element_type=F32)
        _store_heads(q_ref, rows, _rope(acc, cos_ref.at[rows, :], sin_ref.at[rows, :]) * (HEAD_DIM ** -0.5 * LOG2_E))


def _proj_act_kernel(h_ref, w_hbm, o_ref, wbuf_ref, stage_ref, sem_ref, *, segments, n_chunks):
    tn = o_ref.shape[1]
    n_blocks = sum(s[1] for s in segments)
    in_seg = lambda blk, s: (blk >= s[0]) & (blk < s[0] + s[1])

    def col_of_block(blk):
        if isinstance(blk, int):
            return next(s[2] + (blk - s[0]) * tn for s in segments if s[0] <= blk < s[0] + s[1])
        return sum(jnp.where(in_seg(blk, s), s[2] + (blk - s[0]) * tn, 0) for s in segments)

    wb_ref = _stream_weights(w_hbm, wbuf_ref, stage_ref, sem_ref, col_of_block=col_of_block, n_blocks=n_blocks, n_chunks=n_chunks)
    acts = {s[3] for s in segments}
    assert acts in ({"none"}, {"silu"}, {"sigmoid"}, {"silu", "sigmoid"})
    j = pl.program_id(0)
    for rows in _row_parts(h_ref.shape[0]):
        acc = jnp.dot(h_ref[rows, :], wb_ref[...], preferred_element_type=F32)
        if acts != {"none"}:
            sig = _sigmoid(acc)
            if acts == {"silu"}:
                acc = acc * sig
            elif acts == {"sigmoid"}:
                acc = sig
            else:
                is_silu = functools.reduce(lambda a, b: a | b, [in_seg(j, s) for s in segments if s[3] == "silu"])
                acc = jnp.where(is_silu, acc * sig, sig)
        o_ref[rows, :] = acc.astype(o_ref.dtype)


def _proj_q(h, w, cos, sin, aw, tm, tn):
    m, d = h.shape
    nh = aw // HEAD_DIM
    tab = pl.BlockSpec((tm, LANES), lambda j, i: (i, 0))
    n_blocks, n_chunks = aw // tn, m // tm
    scratch, scratch_shapes = _weight_stream_scratch(d, tn, n_chunks)
    return pl.pallas_call(
        functools.partial(_proj_q_kernel, n_blocks=n_blocks, n_chunks=n_chunks),
        grid=(n_blocks, n_chunks),
        in_specs=[pl.BlockSpec((tm, d), lambda j, i: (i, 0)), pl.BlockSpec(memory_space=pl.ANY), tab, tab],
        out_specs=pl.BlockSpec((tn // HEAD_DIM, tm, HEAD_DIM), lambda j, i: (j, i, 0)),
        out_shape=jax.ShapeDtypeStruct((nh, m, HEAD_DIM), BF16),
        scratch_shapes=scratch_shapes,
        compiler_params=_params(("arbitrary", "arbitrary"),
                                _vmem_estimate([((tm, d), BF16), ((tm, 2 * tn), BF16), ((tm, 2 * LANES), F32)], scratch + [((tm // 4, tn), F32)] * 4)),
        name="proj_q",
    )(h, w, cos, sin)


def _proj_act(h, w, runs, tm, tn, name, out_dtype=BF16):
    m, d = h.shape
    n_chunks = m // tm
    segments, b0 = [], 0
    for col0, width, act in runs:
        assert width % tn == 0 and col0 % LANES == 0
        segments.append((b0, width // tn, col0, act))
        b0 += width // tn
    n_blocks, ncols = b0, b0 * tn
    scratch, scratch_shapes = _weight_stream_scratch(d, tn, n_chunks)
    return pl.pallas_call(
        functools.partial(_proj_act_kernel, segments=tuple(segments), n_chunks=n_chunks),
        grid=(n_blocks, n_chunks),
        in_specs=[pl.BlockSpec((tm, d), lambda j, i: (i, 0)), pl.BlockSpec(memory_space=pl.ANY)],
        out_specs=pl.BlockSpec((tm, tn), lambda j, i: (i, j)),
        out_shape=jax.ShapeDtypeStruct((m, ncols), out_dtype),
        scratch_shapes=scratch_shapes,
        compiler_params=_params(("arbitrary", "arbitrary"),
                                _vmem_estimate([((tm, d), BF16), ((tm, tn), out_dtype)], scratch + [((tm // len(_row_parts(tm)), tn), F32)] * 3)),
        name=name,
    )(h, w)


def _attn_weights(q, kw, sink, n_valid):
    st = lax.dot_general(kw, q, (((1,), (1,)), ((), ())), preferred_element_type=F32)
    if n_valid is not None:
        row = lax.broadcasted_iota(jnp.int32, st.shape, 0)
        st = jnp.where(row < n_valid, st, -jnp.inf)
    m = jnp.maximum(jnp.max(st, axis=0, keepdims=True), sink)
    e = jnp.exp2(st - m)
    return e.astype(BF16), jnp.sum(e, axis=0, keepdims=True) + jnp.exp2(sink - m)


def _attn_values(vw, e, denom):
    ot = jnp.dot(vw.astype(F32).T.astype(BF16), e, preferred_element_type=F32) / denom
    return ot.T


def _store_unit(o_ref, za_ref, row0, col0, o, rep):
    for r in range(0, rep, 2):
        pair = jnp.concatenate([o[r * CHUNK:(r + 1) * CHUNK], o[(r + 1) * CHUNK:(r + 2) * CHUNK]], axis=1)
        cols = slice(col0 + r * HEAD_DIM, col0 + (r + 2) * HEAD_DIM)
        gate = za_ref[row0:row0 + CHUNK, cols].astype(F32)
        o_ref[row0:row0 + CHUNK, cols] = (pair * gate).astype(o_ref.dtype)


def _attn_kernel(q_ref, k_ref, v_ref, ks_ref, vs_ref, sink_ref, za_ref, wo_ref, o_ref, wob_ref,
                 *, n_units, n_heads, rep, steps_per_stream, n_prompt_steps, n_cast_steps):
    step = pl.program_id(1)
    win = WINDOW + CHUNK

    @pl.when(pl.program_id(0) * (n_prompt_steps + 1) + step < n_cast_steps)
    def _():
        wob_ref[...] = wo_ref[...].astype(wob_ref.dtype)

    def run(windows):
        staged = []
        for hd, c, kw, vw, n_valid in windows:
            q = q_ref[hd * rep:(hd + 1) * rep, c * CHUNK:(c + 1) * CHUNK, :].reshape(rep * CHUNK, HEAD_DIM)
            staged.append((hd, c, vw) + _attn_weights(q, kw, sink_ref[hd] * LOG2_E, n_valid))
        for hd, c, vw, e, denom in staged:
            _store_unit(o_ref, za_ref, c * CHUNK, hd * rep * HEAD_DIM, _attn_values(vw, e, denom), rep)

    @pl.when(step < n_prompt_steps)
    def _():
        cb = step % steps_per_stream
        windows = []
        for hd in range(n_heads):
            for c in range(n_units):
                start = pl.multiple_of(jnp.maximum(cb * n_units + c - WINDOW // CHUNK, 0) * CHUNK, CHUNK)
                n_valid = jnp.where(cb == 0, (c + 1) * CHUNK, win) if c < WINDOW // CHUNK else None
                windows.append((hd, c, k_ref[hd, pl.ds(start, win), :], v_ref[hd, pl.ds(start, win), :], n_valid))
        run(windows)

    @pl.when(step == n_prompt_steps)
    def _():
        run([(hd, b, ks_ref[hd, b * win:(b + 1) * win, :], vs_ref[hd, b * win:(b + 1) * win, :], None)
             for hd in range(n_heads) for b in range(n_units)])


def _attention(q_hm, k_hm, v_hm, ks, vs, sink_rows, za, w_out, batch, seq, dec_batch, aw):
    nh, m, _ = q_hm.shape
    g = k_hm.shape[0]
    rep = nh // g
    win = WINDOW + CHUNK
    n_units = dec_batch
    rows = n_units * CHUNK
    assert seq % rows == 0 and n_units >= WINDOW // CHUNK
    steps_per_stream = seq // rows
    n_prompt_steps = batch * steps_per_stream
    nhd = 2 if g % 2 == 0 else 1
    blocks = ([((nhd * rep, rows, LANES), BF16)] + [((nhd * seq, LANES), BF16)] * 2 + [((nhd * dec_batch * win, LANES), BF16)] * 2
              + [((rows, nhd * rep * HEAD_DIM), BF16)] * 2)
    temps = [((rep * CHUNK, 2 * LANES), F32)] * (4 * n_units * nhd)
    stream = lambda gi, s: (gi, jnp.minimum(s // steps_per_stream, batch - 1), 0)
    dk, dn = w_out.shape
    n_steps = (g // nhd) * (n_prompt_steps + 1)
    n_cast_steps = max(c for c in (32, 16, 8) if c <= n_steps)
    assert dk % (16 * n_cast_steps) == 0
    cast_rows = dk // n_cast_steps
    cast_block = pl.BlockSpec((cast_rows, dn), lambda gi, s: (jnp.minimum(gi * (n_prompt_steps + 1) + s, n_cast_steps - 1), 0))
    blocks += [((cast_rows, dn), F32), ((cast_rows, dn), BF16)]
    return pl.pallas_call(
        functools.partial(_attn_kernel, n_units=n_units, n_heads=nhd, rep=rep, steps_per_stream=steps_per_stream,
                          n_prompt_steps=n_prompt_steps, n_cast_steps=n_cast_steps),
        grid=(g // nhd, n_prompt_steps + 1),
        in_specs=[
            pl.BlockSpec((nhd * rep, rows, HEAD_DIM), lambda gi, s: (gi, s, 0)),
            pl.BlockSpec((nhd, seq, HEAD_DIM), stream),
            pl.BlockSpec((nhd, seq, HEAD_DIM), stream),
            pl.BlockSpec((nhd, dec_batch * win, HEAD_DIM), lambda gi, s: (gi, 0, 0)),
            pl.BlockSpec((nhd, dec_batch * win, HEAD_DIM), lambda gi, s: (gi, 0, 0)),
            pl.BlockSpec((nhd, 1, rep * CHUNK), lambda gi, s: (gi, 0, 0)),
            pl.BlockSpec((rows, nhd * rep * HEAD_DIM), lambda gi, s: (s, gi)),
            cast_block,
        ],
        out_specs=[pl.BlockSpec((rows, nhd * rep * HEAD_DIM), lambda gi, s: (s, gi)), cast_block],
        out_shape=[jax.ShapeDtypeStruct((m, aw), BF16), jax.ShapeDtypeStruct((dk, dn), BF16)],
        compiler_params=_params(("arbitrary",) * 2, _vmem_estimate(blocks, temps)),
        name="attention",
    )(q_hm, k_hm, v_hm, ks, vs, sink_rows, za, w_out)


def _ssm_disc_kernel(lre_ref, lim_ref, ldt_ref, are_ref, aim_ref, dre_ref, dim_ref, fre_ref, fim_ref):
    lr = jnp.minimum(lre_ref[...], LAMBDA_RE_MAX)
    li = lim_ref[...]
    dt = jnp.exp(ldt_ref[...])
    mag = jnp.exp(lr * dt)
    a_re = mag * jnp.cos(li * dt)
    a_im = mag * jnp.sin(li * dt)
    den = lr * lr + li * li
    nr = a_re - 1.0
    fre_ref[...] = (nr * lr + a_im * li) / den
    fim_ref[...] = (a_im * lr - nr * li) / den
    are_ref[...] = a_re
    aim_ref[...] = a_im
    for _ in range(SSM_POW_BITS):
        a_re, a_im = a_re * a_re - a_im * a_im, 2.0 * a_re * a_im
    dre_ref[...] = a_re
    dim_ref[...] = a_im


def _cmul(ar, ai, br, bi):
    return ar * br - ai * bi, ar * bi + ai * br


def _ssm_build_kernel(are_ref, aim_ref, dre_ref, dim_ref, fre_ref, fim_ref, btr_ref, bti_ref, cr_ref, ci_ref, dv_ref,
                      met_ref, ft_ref, a1_ref, a2_ref, a2s_ref, *, groups):
    lc = SSM_CHUNK * SSM_GROUP
    sub = lax.broadcasted_iota(jnp.int32, (SSM_GROUP, lc), 0)
    lane = lax.broadcasted_iota(jnp.int32, (SSM_GROUP, lc), 1)
    for gi in range(groups):
        row = slice(gi, gi + 1)
        a_re, a_im = are_ref[row, :], aim_ref[row, :]
        pw = [(jnp.ones_like(a_re), jnp.zeros_like(a_re))]
        for _ in range(SSM_CHUNK):
            pw.append(_cmul(pw[-1][0], pw[-1][1], a_re, a_im))
        c_re, c_im = cr_ref[gi], ci_ref[gi]
        wt = [_cmul(pr, pi, c_re, c_im) for pr, pi in pw]
        wt_re = jnp.concatenate([w[0] for w in wt[:SSM_CHUNK]], axis=0)
        wt_im = jnp.concatenate([w[1] for w in wt[:SSM_CHUNK]], axis=0)
        et_re = jnp.concatenate([w[0] for w in wt[1:]], axis=0)
        et_im = jnp.concatenate([w[1] for w in wt[1:]], axis=0)
        bb_re, bb_im = _cmul(fre_ref[row, :], fim_ref[row, :], btr_ref[gi], bti_ref[gi])
        r0 = lax.dot_general(jnp.concatenate([bb_re, bb_im], axis=1), jnp.concatenate([wt_re, -wt_im], axis=1),
                             (((1,), (1,)), ((), ())), preferred_element_type=F32, precision=lax.Precision.HIGHEST)
        r0 = r0 + jnp.where(sub == lane, dv_ref[gi], 0.0)
        rows = [r0] + [jnp.where(lane >= s * SSM_GROUP, pltpu.roll(r0, s * SSM_GROUP, 1), 0.0) for s in range(1, SSM_CHUNK)]
        mt = jnp.concatenate(rows, axis=0).T
        met_ref[gi] = jnp.concatenate([mt, et_re, -et_im], axis=1).astype(met_ref.dtype)
        fb = [_cmul(pw[SSM_CHUNK - 1 - s][0], pw[SSM_CHUNK - 1 - s][1], bb_re, bb_im) for s in range(SSM_CHUNK)]
        f_all = jnp.concatenate([jnp.concatenate([x[0] for x in fb], axis=0), jnp.concatenate([x[1] for x in fb], axis=0)], axis=1)
        ft_ref[gi] = f_all.T.astype(ft_ref.dtype)
        d_re, d_im = dre_ref[row, :], dim_ref[row, :]
        a1_ref[gi] = jnp.concatenate([d_re, d_re], axis=1)
        a2_ref[gi] = jnp.concatenate([-d_im, d_im], axis=1)
        a2s_ref[gi] = jnp.concatenate([d_im, -d_im], axis=1)


def _ssm_params(lambda_re, lambda_im, log_dt, b_re, b_im, c_re, c_im, d_skip):
    ng, p = lambda_re.shape
    lc = SSM_CHUNK * SSM_GROUP
    full = pl.BlockSpec((ng, p), lambda: (0, 0))
    disc = pl.pallas_call(
        _ssm_disc_kernel,
        in_specs=[full, full, pl.BlockSpec((ng, 1), lambda: (0, 0))],
        out_specs=[full] * 6,
        out_shape=[jax.ShapeDtypeStruct((ng, p), F32)] * 6,
        name="ssm_disc",
    )(lambda_re, lambda_im, log_dt.reshape(ng, 1))
    gb = _pick(ng, (8,))
    bt_re = jnp.swapaxes(b_re, 1, 2)
    bt_im = jnp.swapaxes(b_im, 1, 2)
    dvec = jnp.pad(d_skip, ((0, 0), (0, lc - SSM_GROUP))).reshape(ng, 1, lc)
    rows = pl.BlockSpec((gb, p), lambda i: (i, 0))
    mats = pl.BlockSpec((gb, SSM_GROUP, p), lambda i: (i, 0, 0))
    dec = pl.BlockSpec((gb, 1, 2 * p), lambda i: (i, 0, 0))
    blocks = ([((gb, LANES), F32)] * 6 + [((gb, SSM_GROUP, LANES), F32)] * 4 + [((gb, 8, lc), F32)]
              + [((gb, lc, lc + 2 * p), BF16), ((gb, 2 * p, lc), BF16)] + [((gb, 8, LANES), F32)] * 3)
    met, ft, a1, a2, a2s = pl.pallas_call(
        functools.partial(_ssm_build_kernel, groups=gb),
        grid=(ng // gb,),
        in_specs=[rows] * 6 + [mats] * 4 + [pl.BlockSpec((gb, 1, lc), lambda i: (i, 0, 0))],
        out_specs=[pl.BlockSpec((gb, lc, lc + 2 * p), lambda i: (i, 0, 0)), pl.BlockSpec((gb, 2 * p, lc), lambda i: (i, 0, 0)), dec, dec, dec],
        out_shape=[jax.ShapeDtypeStruct((ng, lc, lc + 2 * p), BF16), jax.ShapeDtypeStruct((ng, 2 * p, lc), BF16)]
        + [jax.ShapeDtypeStruct((ng, 1, 2 * p), F32)] * 3,
        compiler_params=_params(("arbitrary",), _vmem_estimate(blocks, [((lc, lc + 2 * p), F32)] * 8)),
        name="ssm_build",
    )(*disc, bt_re, bt_im, c_re, c_im, dvec)
    flat = lambda a: a.reshape(1, ng * 2 * p)
    return met, ft, flat(a1), flat(a2), flat(a2s)


def _chunk_blocks(nk):
    return [(k0, min(LANES, nk - k0)) for k0 in range(0, nk, LANES)]


def _pad_rows(x):
    n = x.shape[0]
    return x if n == LANES else jnp.concatenate([x, jnp.zeros((LANES - n, x.shape[1]), x.dtype)], axis=0)


def _ssm_inputs_phase(u_ref, ft_ref, rhs_ref, s_ref, *, nk, p):
    groups = LANES // SSM_GROUP
    blocks = _chunk_blocks(nk)
    xt = [[_pad_rows(u_ref[pl.ds(k0 * SSM_CHUNK + s, n, stride=SSM_CHUNK), :].astype(BF16)).T for k0, n in blocks]
          for s in range(SSM_CHUNK)]
    for gi in range(groups):
        rhs = jnp.concatenate([jnp.concatenate([xt[s][kb][gi * SSM_GROUP:(gi + 1) * SSM_GROUP, :] for kb in range(len(blocks))], axis=1)
                               for s in range(SSM_CHUNK)], axis=0)
        rhs_ref[gi] = rhs
        st = jnp.dot(ft_ref[gi], rhs, preferred_element_type=F32)
        for kb, (k0, n) in enumerate(blocks):
            s_ref[k0:k0 + n, gi * 2 * p:(gi + 1) * 2 * p] = st[:, kb * LANES:(kb + 1) * LANES].T[:n, :]


def _ssm_outputs_phase(rhs_ref, h_ref, met_ref, y_ref, *, nk, p):
    groups = LANES // SSM_GROUP
    blocks = _chunk_blocks(nk)
    zt = [[None] * groups for _ in range(SSM_CHUNK)]
    for gi in range(groups):
        ht = jnp.concatenate([_pad_rows(h_ref[k0:k0 + n, gi * 2 * p:(gi + 1) * 2 * p]).T for k0, n in blocks], axis=1)
        rhs = jnp.concatenate([rhs_ref[gi], ht.astype(BF16)], axis=0)
        yt = jnp.dot(met_ref[gi], rhs, preferred_element_type=F32)
        for t in range(SSM_CHUNK):
            zt[t][gi] = yt[t * SSM_GROUP:(t + 1) * SSM_GROUP, :]
    for t in range(SSM_CHUNK):
        z = jnp.concatenate(zt[t], axis=0)
        for kb, (k0, n) in enumerate(blocks):
            y_ref[pl.ds(k0 * SSM_CHUNK + t, n, stride=SSM_CHUNK), :] = z[:, kb * LANES:(kb + 1) * LANES].T[:n, :]


def _ssm_scan_phase(s_ref, h0_ref, a1_ref, a2_ref, a2s_ref, h_ref, fin_ref, ss_ref, *, batch, kp, dec_batch, ks, p):
    def swap_halves(x):
        lane = lax.broadcasted_iota(jnp.int32, x.shape, 1)
        return jnp.where(lane % (2 * p) < p, pltpu.roll(x, x.shape[1] - p, 1), pltpu.roll(x, p, 1))

    ss_ref[...] = swap_halves(s_ref[...])
    a1, a2, a2s = a1_ref[...], a2_ref[...], a2s_ref[...]
    zero = jnp.zeros_like(a1)

    def step(k, h, hs):
        h_ref[pl.ds(k, 1), :] = h
        s = s_ref[pl.ds(k, 1), :]
        ss = ss_ref[pl.ds(k, 1), :]
        return a1 * h + a2 * hs + s, a1 * hs + a2s * h + ss

    def body(k, carry):
        out = []
        for b in range(batch):
            out.extend(step(b * kp + k, carry[2 * b], carry[2 * b + 1]))
        return tuple(out)

    fin = lax.fori_loop(0, kp, body, (zero,) * (2 * batch), unroll=4 if kp % 4 == 0 else 1)
    for b in range(batch):
        fin_ref[b:b + 1, :] = fin[2 * b]
    h0s_all = swap_halves(h0_ref[...])
    for b in range(dec_batch):
        h, hs = h0_ref[b:b + 1, :], h0s_all[b:b + 1, :]
        for k in range(ks):
            h, hs = step(batch * kp + b * ks + k, h, hs)
        fin_ref[batch + b:batch + b + 1, :] = h


def _ssm_kernel(u_ref, ft_ref, met_ref, h0_ref, a1_ref, a2_ref, a2s_ref, y_ref, fin_ref, rhs_ref, s_ref, ss_ref, h_ref,
                *, batch, kp, dec_batch, ks, p):
    nk = batch * kp + dec_batch * ks
    _ssm_inputs_phase(u_ref, ft_ref, rhs_ref, s_ref, nk=nk, p=p)
    _ssm_scan_phase(s_ref, h0_ref, a1_ref, a2_ref, a2s_ref, h_ref, fin_ref, ss_ref, batch=batch, kp=kp, dec_batch=dec_batch, ks=ks, p=p)
    _ssm_outputs_phase(rhs_ref, h_ref, met_ref, y_ref, nk=nk, p=p)


def _ssm(u, met, ft, a1, a2, a2s, h0, batch, kp, dec_batch, ks):
    m, sw = u.shape
    ng, p2, lc = ft.shape
    p = p2 // 2
    nk = m // SSM_CHUNK
    assert nk == batch * kp + dec_batch * ks and (nk % LANES) % 8 == 0
    nk_lanes = -(-nk // LANES) * LANES
    groups = LANES // SSM_GROUP
    lb = groups * p2
    nseq = batch + dec_batch
    u_spec = pl.BlockSpec((m, LANES), lambda i: (0, i))
    rowb = pl.BlockSpec((1, lb), lambda i: (0, i))
    blocks = [((m, LANES), F32)] * 2 + [((groups, p2, lc), BF16), ((groups, lc, lc + p2), BF16), ((nseq, lb), F32)]
    scratch = [((groups, lc, nk_lanes), BF16)] + [((nk, lb), F32)] * 3
    return pl.pallas_call(
        functools.partial(_ssm_kernel, batch=batch, kp=kp, dec_batch=dec_batch, ks=ks, p=p),
        grid=(ng // groups,),
        in_specs=[u_spec, pl.BlockSpec((groups, p2, lc), lambda i: (i, 0, 0)), pl.BlockSpec((groups, lc, lc + p2), lambda i: (i, 0, 0)),
                  pl.BlockSpec((dec_batch, lb), lambda i: (0, i)), rowb, rowb, rowb],
        out_specs=[u_spec, pl.BlockSpec((nseq, lb), lambda i: (0, i))],
        out_shape=[jax.ShapeDtypeStruct((m, sw), F32), jax.ShapeDtypeStruct((nseq, ng * p2), F32)],
        scratch_shapes=[pltpu.VMEM(s, d) for s, d in scratch],
        compiler_params=_params(("arbitrary",), _vmem_estimate(blocks, scratch + [((lc + p2, nk_lanes), BF16), ((lc, nk_lanes), F32)])),
        name="ssm",
    )(u, ft, met, h0, a1, a2, a2s)


def _glu_kernel(y_ref, w_hbm, ba_ref, bg_ref, zs_ref, o_ref, wa_buf, wa_stage, wa_sem, wg_buf, wg_stage, wg_sem, *, n_blocks, n_chunks):
    tn = o_ref.shape[1]
    stream = functools.partial(_stream_weights, w_hbm, n_blocks=n_blocks, n_chunks=n_chunks)
    wa_ref = stream(wa_buf, wa_stage, wa_sem, col_of_block=lambda blk: blk * tn)
    wg_ref = stream(wg_buf, wg_stage, wg_sem, col_of_block=lambda blk: (n_blocks + blk) * tn)
    for rows in _row_parts(y_ref.shape[0]):
        y = y_ref[rows, :].astype(BF16)
        a = jnp.dot(y, wa_ref[...], preferred_element_type=F32) + ba_ref[...]
        g = jnp.dot(y, wg_ref[...], preferred_element_type=F32) + bg_ref[...]
        o_ref[rows, :] = (a * _sigmoid(g) * zs_ref[rows, :].astype(F32)).astype(o_ref.dtype)


def _glu(y, w_glu, b_glu, acts, zs_col0, tm, tn):
    m, sw = y.shape
    nb, n_chunks = sw // tn, m // tm
    zb0 = zs_col0 // tn
    shapes, scratch = _weight_stream_scratch(sw, tn, n_chunks)
    blocks = [((tm, sw), F32), ((tm, tn), BF16), ((tm, tn), BF16)]
    return pl.pallas_call(
        functools.partial(_glu_kernel, n_blocks=nb, n_chunks=n_chunks),
        grid=(nb, n_chunks),
        in_specs=[
            pl.BlockSpec((tm, sw), lambda j, i: (i, 0)),
            pl.BlockSpec(memory_space=pl.ANY),
            pl.BlockSpec((1, tn), lambda j, i: (0, j)),
            pl.BlockSpec((1, tn), lambda j, i: (0, nb + j)),
            pl.BlockSpec((tm, tn), lambda j, i: (i, zb0 + j)),
        ],
        out_specs=pl.BlockSpec((tm, tn), lambda j, i: (i, j)),
        out_shape=jax.ShapeDtypeStruct((m, sw), BF16),
        scratch_shapes=scratch * 2,
        compiler_params=_params(("arbitrary", "arbitrary"), _vmem_estimate(blocks, shapes * 2 + [((tm, tn), F32)] * 4)),
        name="glu",
    )(y, w_glu, b_glu, b_glu, acts)


def _merge_kernel(xa_ref, xs_ref, wpa_hbm, wps_hbm, ga_ref, gs_ref, o_ref, wa_buf, wa_stage, wa_sem, ws_buf, ws_stage, ws_sem,
                  *, n_blocks, n_chunks):
    tn = o_ref.shape[1]
    col = lambda blk: blk * tn
    wpa_ref = _stream_weights(wpa_hbm, wa_buf, wa_stage, wa_sem, col_of_block=col, n_blocks=n_blocks, n_chunks=n_chunks)
    wps_ref = _stream_weights(wps_hbm, ws_buf, ws_stage, ws_sem, col_of_block=col, n_blocks=n_blocks, n_chunks=n_chunks)
    for rows in _row_parts(xa_ref.shape[0]):
        br_a = jnp.dot(xa_ref[rows, :], wpa_ref[...], preferred_element_type=F32)
        br_s = jnp.dot(xs_ref[rows, :], wps_ref[...], preferred_element_type=F32)
        o_ref[rows, :] = (ga_ref[rows, :].astype(F32) * br_a + gs_ref[rows, :].astype(F32) * br_s).astype(o_ref.dtype)


def _merge(xa, xs, w_pa, w_ps, acts, gate_col0, tm, tn):
    m, aw = xa.shape
    sw = xs.shape[1]
    d = w_pa.shape[1]
    nb, n_chunks = d // tn, m // tm
    gb0 = gate_col0 // tn
    shapes_a, scratch_a = _weight_stream_scratch(aw, tn, n_chunks)
    shapes_s, scratch_s = _weight_stream_scratch(sw, tn, n_chunks)
    blocks = [((tm, aw), BF16), ((tm, sw), BF16)] + [((tm, tn), BF16)] * 3
    return pl.pallas_call(
        functools.partial(_merge_kernel, n_blocks=nb, n_chunks=n_chunks),
        grid=(nb, n_chunks),
        in_specs=[
            pl.BlockSpec((tm, aw), lambda j, i: (i, 0)),
            pl.BlockSpec((tm, sw), lambda j, i: (i, 0)),
            pl.BlockSpec(memory_space=pl.ANY),
            pl.BlockSpec(memory_space=pl.ANY),
            pl.BlockSpec((tm, tn), lambda j, i: (i, gb0 + j)),
            pl.BlockSpec((tm, tn), lambda j, i: (i, gb0 + nb + j)),
        ],
        out_specs=pl.BlockSpec((tm, tn), lambda j, i: (i, j)),
        out_shape=jax.ShapeDtypeStruct((m, d), BF16),
        scratch_shapes=scratch_a + scratch_s,
        compiler_params=_params(("arbitrary", "arbitrary"), _vmem_estimate(blocks, shapes_a + shapes_s + [((tm, tn), F32)] * 4)),
        name="merge",
    )(xa, xs, w_pa, w_ps, acts, acts)


def _out_kernel(mg_ref, w_ref, x_ref, g_ref, o_ref, ssq_ref, *, n_col_blocks, tn):
    j = pl.program_id(1)
    blk = x_ref[...] + jnp.dot(mg_ref[...], w_ref[...], preferred_element_type=F32)
    o_ref[:, pl.ds(pl.multiple_of(j * tn, tn), tn)] = blk
    part = jnp.sum(blk * blk, axis=-1, keepdims=True)

    @pl.when(j == 0)
    def _():
        ssq_ref[...] = part

    @pl.when(j > 0)
    def _():
        ssq_ref[...] += part

    @pl.when(j == n_col_blocks - 1)
    def _():
        scale = lax.rsqrt(ssq_ref[...] / (n_col_blocks * tn) + NORM_EPS)
        o_ref[...] = o_ref[...] * scale * g_ref[...]


def _out(merged, row_block0, w_out, x, final_g, tm, tn):
    mx, d = x.shape
    nb = d // tn
    blocks = [((tm, d), BF16), ((d, tn), BF16), ((tm, tn), F32), ((tm, d), F32)]
    return pl.pallas_call(
        functools.partial(_out_kernel, n_col_blocks=nb, tn=tn),
        grid=(mx // tm, nb),
        in_specs=[
            pl.BlockSpec((tm, d), lambda i, j: (row_block0 + i, 0)),
            pl.BlockSpec((d, tn), lambda i, j: (0, j)),
            pl.BlockSpec((tm, tn), lambda i, j: (i, j)),
            pl.BlockSpec((1, d), lambda i, j: (0, 0)),
        ],
        out_specs=pl.BlockSpec((tm, d), lambda i, j: (i, 0)),
        out_shape=jax.ShapeDtypeStruct((mx, d), F32),
        scratch_shapes=[pltpu.VMEM((tm, 1), F32)],
        compiler_params=_params(("arbitrary", "arbitrary"), _vmem_estimate(blocks, [((tm, tn), F32)] * 3 + [((tm, LANES), F32)])),
        name="out_norm",
    )(merged, w_out, x, final_g.reshape(1, d))


def _rope_tables(positions):
    half = HEAD_DIM // 2
    inv_freq = ROPE_THETA ** (-jnp.arange(half, dtype=F32) / half)
    ang = positions.astype(F32)[:, None] * inv_freq[None, :]
    cos, sin = jnp.cos(ang), jnp.sin(ang)
    reps = LANES // HEAD_DIM
    return jnp.tile(jnp.concatenate([cos, cos], axis=1), (1, reps)), jnp.tile(jnp.concatenate([-sin, sin], axis=1), (1, reps))


def kernel(x_prompt, x_sample, cache_k, cache_v, state_ssm_re, state_ssm_im, norm_g, w_in, sink, lambda_re, lambda_im,
           log_dt, b_re, b_im, c_re, c_im, d_skip, w_glu, b_glu, w_pa, w_ps, w_out, final_g):
    depth = norm_g.shape[0]
    assert depth == 1, "one trunk layer"
    batch, seq, d = x_prompt.shape
    dec_batch, dec_seq, _ = x_sample.shape
    aw = w_pa.shape[1]
    sw = w_ps.shape[1]
    nh = aw // HEAD_DIM
    g = max(1, nh // GQA_GROUPING)
    rep = nh // g
    kvw = g * HEAD_DIM
    ng, p = lambda_re.shape[1:]
    assert dec_seq == CHUNK and cache_k.shape[2] == WINDOW and seq % (2 * CHUNK) == 0
    assert kvw % LANES == 0 and rep % 2 == 0 and sw == ng * SSM_GROUP and (1 << SSM_POW_BITS) == SSM_CHUNK
    assert 2 * p == LANES and sw % LANES == 0
    mp, ms = batch * seq, dec_batch * dec_seq
    m = mp + ms
    widths = (aw, 2 * kvw, aw, sw, sw, 2 * d)
    assert sum(widths) == w_in.shape[2]
    c_q, c_kv, c_za, c_u, c_zs, c_gate = (sum(widths[:n]) for n in range(len(widths)))
    tm = _pick(math.gcd(mp, ms), (512, 256, 128))
    tmp = max(t for t in range(16, 1153, 16) if m % t == 0)
    tn = _pick(math.gcd(c_kv, c_za, c_u, c_zs, c_gate, 2 * d), (512, 256, 128))

    xp = x_prompt.reshape(mp, d)
    xs = x_sample.reshape(ms, d)
    w_in2 = w_in.reshape(d, w_in.shape[2])
    positions = jnp.concatenate([jnp.tile(jnp.arange(seq, dtype=jnp.int32), batch),
                                 jnp.tile(PAST_LEN + jnp.arange(dec_seq, dtype=jnp.int32), dec_batch)])
    cos, sin = _rope_tables(positions)

    h, k_f, v_f, k_hm, v_hm = _norm_kv(xp, xs, norm_g[0], w_in2, cos, sin, c_kv, kvw, tm)
    tna = _pick(math.gcd(aw, sw), (1024, 512, 256))
    q_hm = _proj_q(h, w_in2, cos, sin, aw, tmp, tna)
    acts = _proj_act(h, w_in2, [(c_za, aw, "silu"), (c_zs, sw, "silu"), (c_gate, 2 * d, "sigmoid")], tmp, tna, "proj_acts")
    u = _proj_act(h, w_in2, [(c_u, sw, "none")], tmp, tna, "proj_u", out_dtype=F32)

    to_heads = lambda c: jnp.transpose(c[0], (2, 0, 1, 3)).astype(BF16)
    new_rows = lambda a: a[:, mp:].reshape(g, dec_batch, dec_seq, HEAD_DIM)
    ks = jnp.concatenate([to_heads(cache_k), new_rows(k_hm)], axis=2).reshape(g, dec_batch * (WINDOW + CHUNK), HEAD_DIM)
    vs = jnp.concatenate([to_heads(cache_v), new_rows(v_hm)], axis=2).reshape(g, dec_batch * (WINDOW + CHUNK), HEAD_DIM)
    sink_rows = jnp.repeat(sink[0].reshape(g, rep), CHUNK, axis=1).reshape(g, 1, rep * CHUNK)
    xa, w_out_b = _attention(q_hm, k_hm, v_hm, ks, vs, sink_rows, acts, w_out.reshape(d, d), batch, seq, dec_batch, aw)

    met, ft, a1, a2, a2s = _ssm_params(lambda_re[0], lambda_im[0], log_dt[0], b_re[0], b_im[0], c_re[0], c_im[0], d_skip[0])
    h0 = jnp.concatenate([state_ssm_re[0], state_ssm_im[0]], axis=-1).reshape(dec_batch, ng * 2 * p)
    y, fin = _ssm(u, met, ft, a1, a2, a2s, h0, batch, seq // SSM_CHUNK, dec_batch, dec_seq // SSM_CHUNK)

    tmm = tmp
    x_ssm = _glu(y, w_glu.reshape(sw, 2 * sw), b_glu.reshape(1, 2 * sw), acts, aw, tmm, tna)
    merged = _merge(xa, x_ssm, w_pa.reshape(aw, d), w_ps.reshape(sw, d), acts, aw + sw, tmm, tna)
    tno = _pick(d, (1024, 512, 256))
    y_prompt = _out(merged, 0, w_out_b, xp, final_g, tm, tno).reshape(batch, seq, d)
    y_sample = _out(merged, mp // tm, w_out_b, xs, final_g, tm, tno).reshape(dec_batch, dec_seq, d)

    keep = min(WINDOW, seq)
    last_rows = lambda a: jnp.stack([a[(b + 1) * seq - keep:(b + 1) * seq] for b in range(batch)]).reshape(1, batch, keep, g, HEAD_DIM)
    dec_rows = lambda a: a[mp:].reshape(dec_batch, dec_seq, g, HEAD_DIM)[None]
    fin = fin.reshape(batch + dec_batch, ng, 2, p)
    return (y_prompt, y_sample, last_rows(k_f), last_rows(v_f), fin[:batch, :, 0][None], fin[:batch, :, 1][None],
            dec_rows(k_f), dec_rows(v_f), fin[batch:, :, 0][None], fin[batch:, :, 1][None])
```

```python
import functools
import math

import jax
import jax.numpy as jnp
from jax import lax
from jax.experimental import pallas as pl
from jax.experimental.pallas import tpu as pltpu

CHUNK = 64
WINDOW = 128
HEAD_DIM = 64
GQA_GROUPING = 8
SSM_GROUP = 16
PAST_LEN = 1024
ROPE_THETA = 10000.0
NORM_EPS = 1e-5
LAMBDA_RE_MAX = -1e-4
LOG2_E = math.log2(math.e)

SSM_CHUNK = 16
SSM_POW_BITS = 4
LANES = 128
V7X_VMEM_BYTES = 64 * 1024 * 1024
V7X_VMEM_RESERVED_BYTES = 4 * 1024 * 1024
BF16 = jnp.bfloat16
F32 = jnp.float32


def _pick(n, prefs):
    for p in prefs:
        if n % p == 0:
            return p
    raise ValueError(f"no tile in {prefs} divides {n}")


def _params(sem, vmem_bytes):
    limit = min(int(vmem_bytes), V7X_VMEM_BYTES - V7X_VMEM_RESERVED_BYTES)
    return pltpu.CompilerParams(dimension_semantics=sem, vmem_limit_bytes=limit)


def _sigmoid(x):
    return 0.5 * jnp.tanh(0.5 * x) + 0.5


def _nbytes(shape, dtype):
    return math.prod(shape) * jnp.dtype(dtype).itemsize


def _vmem_estimate(blocks, temps=()):
    return 2 * (2 * sum(_nbytes(s, d) for s, d in blocks) + sum(_nbytes(s, d) for s, d in temps))


def _norm_kv_kernel(xp_ref, xs_ref, g_ref, w_ref, cos_ref, sin_ref, h_ref, kf_ref, vf_ref, kh_ref, vh_ref, wb_ref, *, n_prompt_blocks, kvw):
    i = pl.program_id(0)

    @pl.when(i == 0)
    def _():
        wb_ref[...] = w_ref[...].astype(wb_ref.dtype)

    tm = h_ref.shape[0]
    n_parts = 2 if tm % 32 == 0 else 1
    for r in range(n_parts):
        rows = slice(r * (tm // n_parts), (r + 1) * (tm // n_parts))
        x = jnp.where(i < n_prompt_blocks, xp_ref[rows, :], xs_ref[rows, :])
        y = x * lax.rsqrt(jnp.mean(x * x, axis=-1, keepdims=True) + NORM_EPS)
        h = (y * g_ref[...]).astype(h_ref.dtype)
        h_ref[rows, :] = h
        acc = jnp.dot(h, wb_ref[...], preferred_element_type=F32)
        k = _rope(acc[:, :kvw], cos_ref.at[rows, :], sin_ref.at[rows, :])
        v = acc[:, kvw:]
        kf_ref[rows, :] = k
        vf_ref[rows, :] = v
        _store_heads(kh_ref, rows, k)
        _store_heads(vh_ref, rows, v)


def _norm_kv(xp, xs, norm_g, w, cos, sin, col0, kvw, tm):
    mp, d = xp.shape
    ms = xs.shape[0]
    npb, nsb = mp // tm, ms // tm
    m = mp + ms
    g = kvw // HEAD_DIM
    tn = 2 * kvw
    assert col0 % tn == 0
    once = pl.Buffered(1)
    tab = pl.BlockSpec((tm, LANES), lambda i: (i, 0))
    flat = pl.BlockSpec((tm, kvw), lambda i: (i, 0))
    heads = pl.BlockSpec((g, tm, HEAD_DIM), lambda i: (0, i, 0))
    xs_spec = (pl.BlockSpec((tm, d), lambda i: (0, 0), pipeline_mode=once) if nsb == 1
               else pl.BlockSpec((tm, d), lambda i: (jnp.maximum(i - npb, 0), 0)))
    blocks = [((tm, d), F32), ((tm, d), BF16), ((tm, 2 * tn), F32), ((tm, 2 * tn), BF16), ((tm, 2 * LANES), F32)]
    temps = [((tm, d), F32)] * 2 + [((d, tn), F32), ((d, tn), BF16)]
    return pl.pallas_call(
        functools.partial(_norm_kv_kernel, n_prompt_blocks=npb, kvw=kvw),
        grid=(npb + nsb,),
        in_specs=[
            pl.BlockSpec((tm, d), lambda i: (jnp.minimum(i, npb - 1), 0)),
            xs_spec,
            pl.BlockSpec((1, d), lambda i: (0, 0)),
            pl.BlockSpec((d, tn), lambda i: (0, col0 // tn), pipeline_mode=once),
            tab, tab,
        ],
        out_specs=[pl.BlockSpec((tm, d), lambda i: (i, 0)), flat, flat, heads, heads],
        out_shape=[jax.ShapeDtypeStruct((m, d), BF16)] + [jax.ShapeDtypeStruct((m, kvw), F32)] * 2
        + [jax.ShapeDtypeStruct((g, m, HEAD_DIM), BF16)] * 2,
        scratch_shapes=[pltpu.VMEM((d, tn), BF16)],
        compiler_params=_params(("arbitrary",), _vmem_estimate(blocks, temps)),
        name="norm_kv",
    )(xp, xs, norm_g.reshape(1, d), w, cos, sin)


def _rope(acc, cos_ref, sin_ref):
    tm, tn = acc.shape
    reps = tn // LANES
    cos = jnp.tile(cos_ref[...], (1, reps))
    sin = jnp.tile(sin_ref[...], (1, reps))
    lane = lax.broadcasted_iota(jnp.int32, (tm, tn), 1)
    low = (lane % HEAD_DIM) < (HEAD_DIM // 2)
    partner = jnp.where(low, pltpu.roll(acc, tn - HEAD_DIM // 2, 1), pltpu.roll(acc, HEAD_DIM // 2, 1))
    return acc * cos + partner * sin


def _store_heads(o_ref, rows, val):
    for h in range(val.shape[1] // HEAD_DIM):
        o_ref[h, rows, :] = val[:, h * HEAD_DIM:(h + 1) * HEAD_DIM].astype(o_ref.dtype)


def _stream_weights(w_hbm, wbuf_ref, stage_ref, sem_ref, *, col_of_block, n_blocks, n_chunks):
    d, tn = wbuf_ref.shape[1:]
    ck = d // n_chunks
    total = n_blocks * n_chunks
    j = pl.program_id(0)
    t = j * n_chunks + pl.program_id(1)

    def aligned(x, a):
        return x if isinstance(x, int) else pl.multiple_of(x, a)

    def chunk_copy(blk, chunk, slot):
        src = w_hbm.at[pl.ds(aligned(chunk * ck, ck), ck), pl.ds(aligned(col_of_block(blk), LANES), tn)]
        return pltpu.make_async_copy(src, stage_ref.at[slot], sem_ref.at[slot])

    def generation(s):
        s = jnp.asarray(s, jnp.int32)
        k = s + n_chunks - 1
        past = (k > total - 1).astype(jnp.int32)
        k = jnp.minimum(k, total - 1)
        blk, chunk = k // n_chunks, k % n_chunks
        return chunk_copy(blk, chunk, s % 2), (blk + past) % 2, chunk

    def land(copy, half, chunk, slot):
        copy.wait()
        wbuf_ref[half, pl.ds(aligned(chunk * ck, ck), ck), :] = stage_ref[slot].astype(wbuf_ref.dtype)

    @pl.when(t == 0)
    def _():
        slot_of = lambda c: (c + n_chunks - 1) % 2
        first = [chunk_copy(0, c, slot_of(c)) for c in range(n_chunks)]
        first[0].start()
        for c in range(n_chunks - 1):
            first[c + 1].start()
            land(first[c], 0, c, slot_of(c))

    @pl.when(t + 1 < total)
    def _():
        generation(t + 1)[0].start()

    copy, half, chunk = generation(t)
    land(copy, half, chunk, t % 2)
    return wbuf_ref.at[j % 2]


def _weight_stream_scratch(k, tn, n_chunks):
    assert k % (8 * n_chunks) == 0
    shapes = [((2, k, tn), BF16), ((2, k // n_chunks, tn), F32)]
    return shapes, [pltpu.VMEM(s, t) for s, t in shapes] + [pltpu.SemaphoreType.DMA((2,))]


def _row_parts(tm, n=2):
    n = n if tm % (16 * n) == 0 and tm >= 1024 else 1
    return [slice(r * (tm // n), (r + 1) * (tm // n)) for r in range(n)]


def _proj_q_kernel(h_ref, w_hbm, cos_ref, sin_ref, q_ref, wbuf_ref, stage_ref, sem_ref, *, n_blocks, n_chunks):
    tn = wbuf_ref.shape[2]
    wb_ref = _stream_weights(w_hbm, wbuf_ref, stage_ref, sem_ref, col_of_block=lambda blk: blk * tn, n_blocks=n_blocks, n_chunks=n_chunks)
    for rows in _row_parts(h_ref.shape[0], 4):
        acc = jnp.dot(h_ref[rows, :], wb_ref[...], preferred_element_type=F32)
        _store_heads(q_ref, rows, _rope(acc, cos_ref.at[rows, :], sin_ref.at[rows, :]) * (HEAD_DIM ** -0.5 * LOG2_E))


def _proj_act_kernel(h_ref, w_hbm, o_ref, wbuf_ref, stage_ref, sem_ref, *, segments, n_chunks):
    tn = o_ref.shape[1]
    n_blocks = sum(s[1] for s in segments)
    in_seg = lambda blk, s: (blk >= s[0]) & (blk < s[0] + s[1])

    def col_of_block(blk):
        if isinstance(blk, int):
            return next(s[2] + (blk - s[0]) * tn for s in segments if s[0] <= blk < s[0] + s[1])
        return sum(jnp.where(in_seg(blk, s), s[2] + (blk - s[0]) * tn, 0) for s in segments)

    wb_ref = _stream_weights(w_hbm, wbuf_ref, stage_ref, sem_ref, col_of_block=col_of_block, n_blocks=n_blocks, n_chunks=n_chunks)
    acts = {s[3] for s in segments}
    assert acts in ({"none"}, {"silu"}, {"sigmoid"}, {"silu", "sigmoid"})
    j = pl.program_id(0)
    for rows in _row_parts(h_ref.shape[0]):
        acc = jnp.dot(h_ref[rows, :], wb_ref[...], preferred_element_type=F32)
        if acts != {"none"}:
            sig = _sigmoid(acc)
            if acts == {"silu"}:
                acc = acc * sig
            elif acts == {"sigmoid"}:
                acc = sig
            else:
                is_silu = functools.reduce(lambda a, b: a | b, [in_seg(j, s) for s in segments if s[3] == "silu"])
                acc = jnp.where(is_silu, acc * sig, sig)
        o_ref[rows, :] = acc.astype(o_ref.dtype)


def _proj_q(h, w, cos, sin, aw, tm, tn):
    m, d = h.shape
    nh = aw // HEAD_DIM
    tab = pl.BlockSpec((tm, LANES), lambda j, i: (i, 0))
    n_blocks, n_chunks = aw // tn, m // tm
    scratch, scratch_shapes = _weight_stream_scratch(d, tn, n_chunks)
    return pl.pallas_call(
        functools.partial(_proj_q_kernel, n_blocks=n_blocks, n_chunks=n_chunks),
        grid=(n_blocks, n_chunks),
        in_specs=[pl.BlockSpec((tm, d), lambda j, i: (i, 0)), pl.BlockSpec(memory_space=pl.ANY), tab, tab],
        out_specs=pl.BlockSpec((tn // HEAD_DIM, tm, HEAD_DIM), lambda j, i: (j, i, 0)),
        out_shape=jax.ShapeDtypeStruct((nh, m, HEAD_DIM), BF16),
        scratch_shapes=scratch_shapes,
        compiler_params=_params(("arbitrary", "arbitrary"),
                                _vmem_estimate([((tm, d), BF16), ((tm, 2 * tn), BF16), ((tm, 2 * LANES), F32)], scratch + [((tm // 4, tn), F32)] * 4)),
        name="proj_q",
    )(h, w, cos, sin)


def _proj_act(h, w, runs, tm, tn, name, out_dtype=BF16):
    m, d = h.shape
    n_chunks = m // tm
    segments, b0 = [], 0
    for col0, width, act in runs:
        assert width % tn == 0 and col0 % LANES == 0
        segments.append((b0, width // tn, col0, act))
        b0 += width // tn
    n_blocks, ncols = b0, b0 * tn
    scratch, scratch_shapes = _weight_stream_scratch(d, tn, n_chunks)
    return pl.pallas_call(
        functools.partial(_proj_act_kernel, segments=tuple(segments), n_chunks=n_chunks),
        grid=(n_blocks, n_chunks),
        in_specs=[pl.BlockSpec((tm, d), lambda j, i: (i, 0)), pl.BlockSpec(memory_space=pl.ANY)],
        out_specs=pl.BlockSpec((tm, tn), lambda j, i: (i, j)),
        out_shape=jax.ShapeDtypeStruct((m, ncols), out_dtype),
        scratch_shapes=scratch_shapes,
        compiler_params=_params(("arbitrary", "arbitrary"),
                                _vmem_estimate([((tm, d), BF16), ((tm, tn), out_dtype)], scratch + [((tm // len(_row_parts(tm)), tn), F32)] * 3)),
        name=name,
    )(h, w)


def _attn_weights(q, kw, sink, n_valid):
    st = lax.dot_general(kw, q, (((1,), (1,)), ((), ())), preferred_element_type=F32)
    if n_valid is not None:
        row = lax.broadcasted_iota(jnp.int32, st.shape, 0)
        st = jnp.where(row < n_valid, st, -jnp.inf)
    m = jnp.maximum(jnp.max(st, axis=0, keepdims=True), sink)
    e = jnp.exp2(st - m)
    return e.astype(BF16), jnp.sum(e, axis=0, keepdims=True) + jnp.exp2(sink - m)


def _attn_values(vw, e, denom):
    ot = jnp.dot(vw.astype(F32).T.astype(BF16), e, preferred_element_type=F32) / denom
    return ot.T


def _store_unit(o_ref, za_ref, row0, col0, o, rep):
    for r in range(0, rep, 2):
        pair = jnp.concatenate([o[r * CHUNK:(r + 1) * CHUNK], o[(r + 1) * CHUNK:(r + 2) * CHUNK]], axis=1)
        cols = slice(col0 + r * HEAD_DIM, col0 + (r + 2) * HEAD_DIM)
        gate = za_ref[row0:row0 + CHUNK, cols].astype(F32)
        o_ref[row0:row0 + CHUNK, cols] = (pair * gate).astype(o_ref.dtype)


def _attn_kernel(q_ref, k_ref, v_ref, ks_ref, vs_ref, sink_ref, za_ref, wo_ref, o_ref, wob_ref,
                 *, n_units, n_heads, rep, steps_per_stream, n_prompt_steps, n_cast_steps):
    step = pl.program_id(1)
    win = WINDOW + CHUNK

    @pl.when(pl.program_id(0) * (n_prompt_steps + 1) + step < n_cast_steps)
    def _():
        wob_ref[...] = wo_ref[...].astype(wob_ref.dtype)

    def run(windows):
        staged = []
        for hd, c, kw, vw, n_valid in windows:
            q = q_ref[hd * rep:(hd + 1) * rep, c * CHUNK:(c + 1) * CHUNK, :].reshape(rep * CHUNK, HEAD_DIM)
            staged.append((hd, c, vw) + _attn_weights(q, kw, sink_ref[hd] * LOG2_E, n_valid))
        for hd, c, vw, e, denom in staged:
            _store_unit(o_ref, za_ref, c * CHUNK, hd * rep * HEAD_DIM, _attn_values(vw, e, denom), rep)

    @pl.when(step < n_prompt_steps)
    def _():
        cb = step % steps_per_stream
        windows = []
        for hd in range(n_heads):
            for c in range(n_units):
                start = pl.multiple_of(jnp.maximum(cb * n_units + c - WINDOW // CHUNK, 0) * CHUNK, CHUNK)
                n_valid = jnp.where(cb == 0, (c + 1) * CHUNK, win) if c < WINDOW // CHUNK else None
                windows.append((hd, c, k_ref[hd, pl.ds(start, win), :], v_ref[hd, pl.ds(start, win), :], n_valid))
        run(windows)

    @pl.when(step == n_prompt_steps)
    def _():
        run([(hd, b, ks_ref[hd, b * win:(b + 1) * win, :], vs_ref[hd, b * win:(b + 1) * win, :], None)
             for hd in range(n_heads) for b in range(n_units)])


def _attention(q_hm, k_hm, v_hm, ks, vs, sink_rows, za, w_out, batch, seq, dec_batch, aw):
    nh, m, _ = q_hm.shape
    g = k_hm.shape[0]
    rep = nh // g
    win = WINDOW + CHUNK
    n_units = dec_batch
    rows = n_units * CHUNK
    assert seq % rows == 0 and n_units >= WINDOW // CHUNK
    steps_per_stream = seq // rows
    n_prompt_steps = batch * steps_per_stream
    nhd = 2 if g % 2 == 0 else 1
    blocks = ([((nhd * rep, rows, LANES), BF16)] + [((nhd * seq, LANES), BF16)] * 2 + [((nhd * dec_batch * win, LANES), BF16)] * 2
              + [((rows, nhd * rep * HEAD_DIM), BF16)] * 2)
    temps = [((rep * CHUNK, 2 * LANES), F32)] * (4 * n_units * nhd)
    stream = lambda gi, s: (gi, jnp.minimum(s // steps_per_stream, batch - 1), 0)
    dk, dn = w_out.shape
    n_steps = (g // nhd) * (n_prompt_steps + 1)
    n_cast_steps = max(c for c in (32, 16, 8) if c <= n_steps)
    assert dk % (16 * n_cast_steps) == 0
    cast_rows = dk // n_cast_steps
    cast_block = pl.BlockSpec((cast_rows, dn), lambda gi, s: (jnp.minimum(gi * (n_prompt_steps + 1) + s, n_cast_steps - 1), 0))
    blocks += [((cast_rows, dn), F32), ((cast_rows, dn), BF16)]
    return pl.pallas_call(
        functools.partial(_attn_kernel, n_units=n_units, n_heads=nhd, rep=rep, steps_per_stream=steps_per_stream,
                          n_prompt_steps=n_prompt_steps, n_cast_steps=n_cast_steps),
        grid=(g // nhd, n_prompt_steps + 1),
        in_specs=[
            pl.BlockSpec((nhd * rep, rows, HEAD_DIM), lambda gi, s: (gi, s, 0)),
            pl.BlockSpec((nhd, seq, HEAD_DIM), stream),
            pl.BlockSpec((nhd, seq, HEAD_DIM), stream),
            pl.BlockSpec((nhd, dec_batch * win, HEAD_DIM), lambda gi, s: (gi, 0, 0)),
            pl.BlockSpec((nhd, dec_batch * win, HEAD_DIM), lambda gi, s: (gi, 0, 0)),
            pl.BlockSpec((nhd, 1, rep * CHUNK), lambda gi, s: (gi, 0, 0)),
            pl.BlockSpec((rows, nhd * rep * HEAD_DIM), lambda gi, s: (s, gi)),
            cast_block,
        ],
        out_specs=[pl.BlockSpec((rows, nhd * rep * HEAD_DIM), lambda gi, s: (s, gi)), cast_block],
        out_shape=[jax.ShapeDtypeStruct((m, aw), BF16), jax.ShapeDtypeStruct((dk, dn), BF16)],
        compiler_params=_params(("arbitrary",) * 2, _vmem_estimate(blocks, temps)),
        name="attention",
    )(q_hm, k_hm, v_hm, ks, vs, sink_rows, za, w_out)


def _ssm_disc_kernel(lre_ref, lim_ref, ldt_ref, are_ref, aim_ref, dre_ref, dim_ref, fre_ref, fim_ref):
    lr = jnp.minimum(lre_ref[...], LAMBDA_RE_MAX)
    li = lim_ref[...]
    dt = jnp.exp(ldt_ref[...])
    mag = jnp.exp(lr * dt)
    a_re = mag * jnp.cos(li * dt)
    a_im = mag * jnp.sin(li * dt)
    den = lr * lr + li * li
    nr = a_re - 1.0
    fre_ref[...] = (nr * lr + a_im * li) / den
    fim_ref[...] = (a_im * lr - nr * li) / den
    are_ref[...] = a_re
    aim_ref[...] = a_im
    for _ in range(SSM_POW_BITS):
        a_re, a_im = a_re * a_re - a_im * a_im, 2.0 * a_re * a_im
    dre_ref[...] = a_re
    dim_ref[...] = a_im


def _cmul(ar, ai, br, bi):
    return ar * br - ai * bi, ar * bi + ai * br


def _ssm_build_kernel(are_ref, aim_ref, dre_ref, dim_ref, fre_ref, fim_ref, btr_ref, bti_ref, cr_ref, ci_ref, dv_ref,
                      met_ref, ft_ref, a1_ref, a2_ref, a2s_ref, *, groups):
    lc = SSM_CHUNK * SSM_GROUP
    sub = lax.broadcasted_iota(jnp.int32, (SSM_GROUP, lc), 0)
    lane = lax.broadcasted_iota(jnp.int32, (SSM_GROUP, lc), 1)
    for gi in range(groups):
        row = slice(gi, gi + 1)
        a_re, a_im = are_ref[row, :], aim_ref[row, :]
        pw = [(jnp.ones_like(a_re), jnp.zeros_like(a_re))]
        for _ in range(SSM_CHUNK):
            pw.append(_cmul(pw[-1][0], pw[-1][1], a_re, a_im))
        c_re, c_im = cr_ref[gi], ci_ref[gi]
        wt = [_cmul(pr, pi, c_re, c_im) for pr, pi in pw]
        wt_re = jnp.concatenate([w[0] for w in wt[:SSM_CHUNK]], axis=0)
        wt_im = jnp.concatenate([w[1] for w in wt[:SSM_CHUNK]], axis=0)
        et_re = jnp.concatenate([w[0] for w in wt[1:]], axis=0)
        et_im = jnp.concatenate([w[1] for w in wt[1:]], axis=0)
        bb_re, bb_im = _cmul(fre_ref[row, :], fim_ref[row, :], btr_ref[gi], bti_ref[gi])
        r0 = lax.dot_general(jnp.concatenate([bb_re, bb_im], axis=1), jnp.concatenate([wt_re, -wt_im], axis=1),
                             (((1,), (1,)), ((), ())), preferred_element_type=F32, precision=lax.Precision.HIGHEST)
        r0 = r0 + jnp.where(sub == lane, dv_ref[gi], 0.0)
        rows = [r0] + [jnp.where(lane >= s * SSM_GROUP, pltpu.roll(r0, s * SSM_GROUP, 1), 0.0) for s in range(1, SSM_CHUNK)]
        mt = jnp.concatenate(rows, axis=0).T
        met_ref[gi] = jnp.concatenate([mt, et_re, -et_im], axis=1).astype(met_ref.dtype)
        fb = [_cmul(pw[SSM_CHUNK - 1 - s][0], pw[SSM_CHUNK - 1 - s][1], bb_re, bb_im) for s in range(SSM_CHUNK)]
        f_all = jnp.concatenate([jnp.concatenate([x[0] for x in fb], axis=0), jnp.concatenate([x[1] for x in fb], axis=0)], axis=1)
        ft_ref[gi] = f_all.T.astype(ft_ref.dtype)
        d_re, d_im = dre_ref[row, :], dim_ref[row, :]
        a1_ref[gi] = jnp.concatenate([d_re, d_re], axis=1)
        a2_ref[gi] = jnp.concatenate([-d_im, d_im], axis=1)
        a2s_ref[gi] = jnp.concatenate([d_im, -d_im], axis=1)


def _ssm_params(lambda_re, lambda_im, log_dt, b_re, b_im, c_re, c_im, d_skip):
    ng, p = lambda_re.shape
    lc = SSM_CHUNK * SSM_GROUP
    full = pl.BlockSpec((ng, p), lambda: (0, 0))
    disc = pl.pallas_call(
        _ssm_disc_kernel,
        in_specs=[full, full, pl.BlockSpec((ng, 1), lambda: (0, 0))],
        out_specs=[full] * 6,
        out_shape=[jax.ShapeDtypeStruct((ng, p), F32)] * 6,
        name="ssm_disc",
    )(lambda_re, lambda_im, log_dt.reshape(ng, 1))
    gb = _pick(ng, (8,))
    bt_re = jnp.swapaxes(b_re, 1, 2)
    bt_im = jnp.swapaxes(b_im, 1, 2)
    dvec = jnp.pad(d_skip, ((0, 0), (0, lc - SSM_GROUP))).reshape(ng, 1, lc)
    rows = pl.BlockSpec((gb, p), lambda i: (i, 0))
    mats = pl.BlockSpec((gb, SSM_GROUP, p), lambda i: (i, 0, 0))
    dec = pl.BlockSpec((gb, 1, 2 * p), lambda i: (i, 0, 0))
    blocks = ([((gb, LANES), F32)] * 6 + [((gb, SSM_GROUP, LANES), F32)] * 4 + [((gb, 8, lc), F32)]
              + [((gb, lc, lc + 2 * p), BF16), ((gb, 2 * p, lc), BF16)] + [((gb, 8, LANES), F32)] * 3)
    met, ft, a1, a2, a2s = pl.pallas_call(
        functools.partial(_ssm_build_kernel, groups=gb),
        grid=(ng // gb,),
        in_specs=[rows] * 6 + [mats] * 4 + [pl.BlockSpec((gb, 1, lc), lambda i: (i, 0, 0))],
        out_specs=[pl.BlockSpec((gb, lc, lc + 2 * p), lambda i: (i, 0, 0)), pl.BlockSpec((gb, 2 * p, lc), lambda i: (i, 0, 0)), dec, dec, dec],
        out_shape=[jax.ShapeDtypeStruct((ng, lc, lc + 2 * p), BF16), jax.ShapeDtypeStruct((ng, 2 * p, lc), BF16)]
        + [jax.ShapeDtypeStruct((ng, 1, 2 * p), F32)] * 3,
        compiler_params=_params(("arbitrary",), V7X_VMEM_BYTES),
        name="ssm_build",
    )(*disc, bt_re, bt_im, c_re, c_im, dvec)
    flat = lambda a: a.reshape(1, ng * 2 * p)
    return met, ft, flat(a1), flat(a2), flat(a2s)


def _chunk_blocks(nk):
    return [(k0, min(LANES, nk - k0)) for k0 in range(0, nk, LANES)]


def _pad_rows(x):
    n = x.shape[0]
    return x if n == LANES else jnp.concatenate([x, jnp.zeros((LANES - n, x.shape[1]), x.dtype)], axis=0)


def _ssm_inputs_phase(u_ref, ft_ref, rhs_ref, s_ref, *, nk, p):
    groups = LANES // SSM_GROUP
    blocks = _chunk_blocks(nk)
    xt = [[_pad_rows(u_ref[pl.ds(k0 * SSM_CHUNK + s, n, stride=SSM_CHUNK), :].astype(BF16)).T for k0, n in blocks]
          for s in range(SSM_CHUNK)]
    for gi in range(groups):
        rhs = jnp.concatenate([jnp.concatenate([xt[s][kb][gi * SSM_GROUP:(gi + 1) * SSM_GROUP, :] for kb in range(len(blocks))], axis=1)
                               for s in range(SSM_CHUNK)], axis=0)
        rhs_ref[gi] = rhs
        st = jnp.dot(ft_ref[gi], rhs, preferred_element_type=F32)
        for kb, (k0, n) in enumerate(blocks):
            s_ref[k0:k0 + n, gi * 2 * p:(gi + 1) * 2 * p] = st[:, kb * LANES:(kb + 1) * LANES].T[:n, :]


def _ssm_outputs_phase(rhs_ref, h_ref, met_ref, y_ref, *, nk, p):
    groups = LANES // SSM_GROUP
    blocks = _chunk_blocks(nk)
    zt = [[None] * groups for _ in range(SSM_CHUNK)]
    for gi in range(groups):
        ht = jnp.concatenate([_pad_rows(h_ref[k0:k0 + n, gi * 2 * p:(gi + 1) * 2 * p]).T for k0, n in blocks], axis=1)
        rhs = jnp.concatenate([rhs_ref[gi], ht.astype(BF16)], axis=0)
        yt = jnp.dot(met_ref[gi], rhs, preferred_element_type=F32)
        for t in range(SSM_CHUNK):
            zt[t][gi] = yt[t * SSM_GROUP:(t + 1) * SSM_GROUP, :]
    for t in range(SSM_CHUNK):
        z = jnp.concatenate(zt[t], axis=0)
        for kb, (k0, n) in enumerate(blocks):
            y_ref[pl.ds(k0 * SSM_CHUNK + t, n, stride=SSM_CHUNK), :] = z[:, kb * LANES:(kb + 1) * LANES].T[:n, :]


def _ssm_scan_phase(s_ref, h0_ref, a1_ref, a2_ref, a2s_ref, h_ref, fin_ref, ss_ref, *, batch, kp, dec_batch, ks, p):
    def swap_halves(x):
        lane = lax.broadcasted_iota(jnp.int32, x.shape, 1)
        return jnp.where(lane % (2 * p) < p, pltpu.roll(x, x.shape[1] - p, 1), pltpu.roll(x, p, 1))

    ss_ref[...] = swap_halves(s_ref[...])
    a1, a2, a2s = a1_ref[...], a2_ref[...], a2s_ref[...]
    zero = jnp.zeros_like(a1)

    def step(k, h, hs):
        h_ref[pl.ds(k, 1), :] = h
        s = s_ref[pl.ds(k, 1), :]
        ss = ss_ref[pl.ds(k, 1), :]
        return a1 * h + a2 * hs + s, a1 * hs + a2s * h + ss

    def body(k, carry):
        out = []
        for b in range(batch):
            out.extend(step(b * kp + k, carry[2 * b], carry[2 * b + 1]))
        return tuple(out)

    fin = lax.fori_loop(0, kp, body, (zero,) * (2 * batch), unroll=4 if kp % 4 == 0 else 1)
    for b in range(batch):
        fin_ref[b:b + 1, :] = fin[2 * b]
    h0s_all = swap_halves(h0_ref[...])
    for b in range(dec_batch):
        h, hs = h0_ref[b:b + 1, :], h0s_all[b:b + 1, :]
        for k in range(ks):
            h, hs = step(batch * kp + b * ks + k, h, hs)
        fin_ref[batch + b:batch + b + 1, :] = h


def _ssm_kernel(u_ref, ft_ref, met_ref, h0_ref, a1_ref, a2_ref, a2s_ref, y_ref, fin_ref, rhs_ref, s_ref, ss_ref, h_ref,
                *, batch, kp, dec_batch, ks, p):
    nk = batch * kp + dec_batch * ks
    _ssm_inputs_phase(u_ref, ft_ref, rhs_ref, s_ref, nk=nk, p=p)
    _ssm_scan_phase(s_ref, h0_ref, a1_ref, a2_ref, a2s_ref, h_ref, fin_ref, ss_ref, batch=batch, kp=kp, dec_batch=dec_batch, ks=ks, p=p)
    _ssm_outputs_phase(rhs_ref, h_ref, met_ref, y_ref, nk=nk, p=p)


def _ssm(u, met, ft, a1, a2, a2s, h0, batch, kp, dec_batch, ks):
    m, sw = u.shape
    ng, p2, lc = ft.shape
    p = p2 // 2
    nk = m // SSM_CHUNK
    assert nk == batch * kp + dec_batch * ks and (nk % LANES) % 8 == 0
    nk_lanes = -(-nk // LANES) * LANES
    groups = LANES // SSM_GROUP
    lb = groups * p2
    nseq = batch + dec_batch
    u_spec = pl.BlockSpec((m, LANES), lambda i: (0, i))
    rowb = pl.BlockSpec((1, lb), lambda i: (0, i))
    blocks = [((m, LANES), F32)] * 2 + [((groups, p2, lc), BF16), ((groups, lc, lc + p2), BF16), ((nseq, lb), F32)]
    scratch = [((groups, lc, nk_lanes), BF16)] + [((nk, lb), F32)] * 3
    return pl.pallas_call(
        functools.partial(_ssm_kernel, batch=batch, kp=kp, dec_batch=dec_batch, ks=ks, p=p),
        grid=(ng // groups,),
        in_specs=[u_spec, pl.BlockSpec((groups, p2, lc), lambda i: (i, 0, 0)), pl.BlockSpec((groups, lc, lc + p2), lambda i: (i, 0, 0)),
                  pl.BlockSpec((dec_batch, lb), lambda i: (0, i)), rowb, rowb, rowb],
        out_specs=[u_spec, pl.BlockSpec((nseq, lb), lambda i: (0, i))],
        out_shape=[jax.ShapeDtypeStruct((m, sw), F32), jax.ShapeDtypeStruct((nseq, ng * p2), F32)],
        scratch_shapes=[pltpu.VMEM(s, d) for s, d in scratch],
        compiler_params=_params(("arbitrary",), _vmem_estimate(blocks, scratch + [((lc + p2, nk_lanes), BF16), ((lc, nk_lanes), F32)])),
        name="ssm",
    )(u, ft, met, h0, a1, a2, a2s)


def _glu_kernel(y_ref, w_hbm, ba_ref, bg_ref, zs_ref, o_ref, wa_buf, wa_stage, wa_sem, wg_buf, wg_stage, wg_sem, *, n_blocks, n_chunks):
    tn = o_ref.shape[1]
    stream = functools.partial(_stream_weights, w_hbm, n_blocks=n_blocks, n_chunks=n_chunks)
    wa_ref = stream(wa_buf, wa_stage, wa_sem, col_of_block=lambda blk: blk * tn)
    wg_ref = stream(wg_buf, wg_stage, wg_sem, col_of_block=lambda blk: (n_blocks + blk) * tn)
    for rows in _row_parts(y_ref.shape[0]):
        y = y_ref[rows, :].astype(BF16)
        a = jnp.dot(y, wa_ref[...], preferred_element_type=F32) + ba_ref[...]
        g = jnp.dot(y, wg_ref[...], preferred_element_type=F32) + bg_ref[...]
        o_ref[rows, :] = (a * _sigmoid(g) * zs_ref[rows, :].astype(F32)).astype(o_ref.dtype)


def _glu(y, w_glu, b_glu, acts, zs_col0, tm, tn):
    m, sw = y.shape
    nb, n_chunks = sw // tn, m // tm
    zb0 = zs_col0 // tn
    shapes, scratch = _weight_stream_scratch(sw, tn, n_chunks)
    blocks = [((tm, sw), F32), ((tm, tn), BF16), ((tm, tn), BF16)]
    return pl.pallas_call(
        functools.partial(_glu_kernel, n_blocks=nb, n_chunks=n_chunks),
        grid=(nb, n_chunks),
        in_specs=[
            pl.BlockSpec((tm, sw), lambda j, i: (i, 0)),
            pl.BlockSpec(memory_space=pl.ANY),
            pl.BlockSpec((1, tn), lambda j, i: (0, j)),
            pl.BlockSpec((1, tn), lambda j, i: (0, nb + j)),
            pl.BlockSpec((tm, tn), lambda j, i: (i, zb0 + j)),
        ],
        out_specs=pl.BlockSpec((tm, tn), lambda j, i: (i, j)),
        out_shape=jax.ShapeDtypeStruct((m, sw), BF16),
        scratch_shapes=scratch * 2,
        compiler_params=_params(("arbitrary", "arbitrary"), _vmem_estimate(blocks, shapes * 2 + [((tm, tn), F32)] * 4)),
        name="glu",
    )(y, w_glu, b_glu, b_glu, acts)


def _merge_kernel(xa_ref, xs_ref, wpa_hbm, wps_hbm, ga_ref, gs_ref, o_ref, wa_buf, wa_stage, wa_sem, ws_buf, ws_stage, ws_sem,
                  *, n_blocks, n_chunks):
    tn = o_ref.shape[1]
    col = lambda blk: blk * tn
    wpa_ref = _stream_weights(wpa_hbm, wa_buf, wa_stage, wa_sem, col_of_block=col, n_blocks=n_blocks, n_chunks=n_chunks)
    wps_ref = _stream_weights(wps_hbm, ws_buf, ws_stage, ws_sem, col_of_block=col, n_blocks=n_blocks, n_chunks=n_chunks)
    for rows in _row_parts(xa_ref.shape[0]):
        br_a = jnp.dot(xa_ref[rows, :], wpa_ref[...], preferred_element_type=F32)
        br_s = jnp.dot(xs_ref[rows, :], wps_ref[...], preferred_element_type=F32)
        o_ref[rows, :] = (ga_ref[rows, :].astype(F32) * br_a + gs_ref[rows, :].astype(F32) * br_s).astype(o_ref.dtype)


def _merge(xa, xs, w_pa, w_ps, acts, gate_col0, tm, tn):
    m, aw = xa.shape
    sw = xs.shape[1]
    d = w_pa.shape[1]
    nb, n_chunks = d // tn, m // tm
    gb0 = gate_col0 // tn
    shapes_a, scratch_a = _weight_stream_scratch(aw, tn, n_chunks)
    shapes_s, scratch_s = _weight_stream_scratch(sw, tn, n_chunks)
    blocks = [((tm, aw), BF16), ((tm, sw), BF16)] + [((tm, tn), BF16)] * 3
    return pl.pallas_call(
        functools.partial(_merge_kernel, n_blocks=nb, n_chunks=n_chunks),
        grid=(nb, n_chunks),
        in_specs=[
            pl.BlockSpec((tm, aw), lambda j, i: (i, 0)),
            pl.BlockSpec((tm, sw), lambda j, i: (i, 0)),
            pl.BlockSpec(memory_space=pl.ANY),
            pl.BlockSpec(memory_space=pl.ANY),
            pl.BlockSpec((tm, tn), lambda j, i: (i, gb0 + j)),
            pl.BlockSpec((tm, tn), lambda j, i: (i, gb0 + nb + j)),
        ],
        out_specs=pl.BlockSpec((tm, tn), lambda j, i: (i, j)),
        out_shape=jax.ShapeDtypeStruct((m, d), BF16),
        scratch_shapes=scratch_a + scratch_s,
        compiler_params=_params(("arbitrary", "arbitrary"), _vmem_estimate(blocks, shapes_a + shapes_s + [((tm, tn), F32)] * 4)),
        name="merge",
    )(xa, xs, w_pa, w_ps, acts, acts)


def _out_kernel(mg_ref, w_ref, x_ref, g_ref, o_ref, ssq_ref, *, n_col_blocks, tn):
    j = pl.program_id(1)
    blk = x_ref[...] + jnp.dot(mg_ref[...], w_ref[...], preferred_element_type=F32)
    o_ref[:, pl.ds(pl.multiple_of(j * tn, tn), tn)] = blk
    part = jnp.sum(blk * blk, axis=-1, keepdims=True)

    @pl.when(j == 0)
    def _():
        ssq_ref[...] = part

    @pl.when(j > 0)
    def _():
        ssq_ref[...] += part

    @pl.when(j == n_col_blocks - 1)
    def _():
        scale = lax.rsqrt(ssq_ref[...] / (n_col_blocks * tn) + NORM_EPS)
        o_ref[...] = o_ref[...] * scale * g_ref[...]


def _out(merged, row_block0, w_out, x, final_g, tm, tn):
    mx, d = x.shape
    nb = d // tn
    blocks = [((tm, d), BF16), ((d, tn), BF16), ((tm, tn), F32), ((tm, d), F32)]
    return pl.pallas_call(
        functools.partial(_out_kernel, n_col_blocks=nb, tn=tn),
        grid=(mx // tm, nb),
        in_specs=[
            pl.BlockSpec((tm, d), lambda i, j: (row_block0 + i, 0)),
            pl.BlockSpec((d, tn), lambda i, j: (0, j)),
            pl.BlockSpec((tm, tn), lambda i, j: (i, j)),
            pl.BlockSpec((1, d), lambda i, j: (0, 0)),
        ],
        out_specs=pl.BlockSpec((tm, d), lambda i, j: (i, 0)),
        out_shape=jax.ShapeDtypeStruct((mx, d), F32),
        scratch_shapes=[pltpu.VMEM((tm, 1), F32)],
        compiler_params=_params(("arbitrary", "arbitrary"), _vmem_estimate(blocks, [((tm, tn), F32)] * 3 + [((tm, LANES), F32)])),
        name="out_norm",
    )(merged, w_out, x, final_g.reshape(1, d))


def _rope_tables(positions):
    half = HEAD_DIM // 2
    inv_freq = ROPE_THETA ** (-jnp.arange(half, dtype=F32) / half)
    ang = positions.astype(F32)[:, None] * inv_freq[None, :]
    cos, sin = jnp.cos(ang), jnp.sin(ang)
    reps = LANES // HEAD_DIM
    return jnp.tile(jnp.concatenate([cos, cos], axis=1), (1, reps)), jnp.tile(jnp.concatenate([-sin, sin], axis=1), (1, reps))


def kernel(x_prompt, x_sample, cache_k, cache_v, state_ssm_re, state_ssm_im, norm_g, w_in, sink, lambda_re, lambda_im,
           log_dt, b_re, b_im, c_re, c_im, d_skip, w_glu, b_glu, w_pa, w_ps, w_out, final_g):
    depth = norm_g.shape[0]
    assert depth == 1, "one trunk layer"
    batch, seq, d = x_prompt.shape
    dec_batch, dec_seq, _ = x_sample.shape
    aw = w_pa.shape[1]
    sw = w_ps.shape[1]
    nh = aw // HEAD_DIM
    g = max(1, nh // GQA_GROUPING)
    rep = nh // g
    kvw = g * HEAD_DIM
    ng, p = lambda_re.shape[1:]
    assert dec_seq == CHUNK and cache_k.shape[2] == WINDOW and seq % (2 * CHUNK) == 0
    assert kvw % LANES == 0 and rep % 2 == 0 and sw == ng * SSM_GROUP and (1 << SSM_POW_BITS) == SSM_CHUNK
    assert 2 * p == LANES and sw % LANES == 0
    mp, ms = batch * seq, dec_batch * dec_seq
    m = mp + ms
    widths = (aw, 2 * kvw, aw, sw, sw, 2 * d)
    assert sum(widths) == w_in.shape[2]
    c_q, c_kv, c_za, c_u, c_zs, c_gate = (sum(widths[:n]) for n in range(len(widths)))
    tm = _pick(math.gcd(mp, ms), (512, 256, 128))
    tmp = max(t for t in range(16, 1153, 16) if m % t == 0)
    tn = _pick(math.gcd(c_kv, c_za, c_u, c_zs, c_gate, 2 * d), (512, 256, 128))

    xp = x_prompt.reshape(mp, d)
    xs = x_sample.reshape(ms, d)
    w_in2 = w_in.reshape(d, w_in.shape[2])
    positions = jnp.concatenate([jnp.tile(jnp.arange(seq, dtype=jnp.int32), batch),
                                 jnp.tile(PAST_LEN + jnp.arange(dec_seq, dtype=jnp.int32), dec_batch)])
    cos, sin = _rope_tables(positions)

    h, k_f, v_f, k_hm, v_hm = _norm_kv(xp, xs, norm_g[0], w_in2, cos, sin, c_kv, kvw, tm)
    tna = _pick(math.gcd(aw, sw), (1024, 512, 256))
    q_hm = _proj_q(h, w_in2, cos, sin, aw, tmp, tna)
    acts = _proj_act(h, w_in2, [(c_za, aw, "silu"), (c_zs, sw, "silu"), (c_gate, 2 * d, "sigmoid")], tmp, tna, "proj_acts")
    u = _proj_act(h, w_in2, [(c_u, sw, "none")], tmp, tna, "proj_u", out_dtype=F32)

    to_heads = lambda c: jnp.transpose(c[0], (2, 0, 1, 3)).astype(BF16)
    new_rows = lambda a: a[:, mp:].reshape(g, dec_batch, dec_seq, HEAD_DIM)
    ks = jnp.concatenate([to_heads(cache_k), new_rows(k_hm)], axis=2).reshape(g, dec_batch * (WINDOW + CHUNK), HEAD_DIM)
    vs = jnp.concatenate([to_heads(cache_v), new_rows(v_hm)], axis=2).reshape(g, dec_batch * (WINDOW + CHUNK), HEAD_DIM)
    sink_rows = jnp.repeat(sink[0].reshape(g, rep), CHUNK, axis=1).reshape(g, 1, rep * CHUNK)
    xa, w_out_b = _attention(q_hm, k_hm, v_hm, ks, vs, sink_rows, acts, w_out.reshape(d, d), batch, seq, dec_batch, aw)

    met, ft, a1, a2, a2s = _ssm_params(lambda_re[0], lambda_im[0], log_dt[0], b_re[0], b_im[0], c_re[0], c_im[0], d_skip[0])
    h0 = jnp.concatenate([state_ssm_re[0], state_ssm_im[0]], axis=-1).reshape(dec_batch, ng * 2 * p)
    y, fin = _ssm(u, met, ft, a1, a2, a2s, h0, batch, seq // SSM_CHUNK, dec_batch, dec_seq // SSM_CHUNK)

    tmm = tmp
    x_ssm = _glu(y, w_glu.reshape(sw, 2 * sw), b_glu.reshape(1, 2 * sw), acts, aw, tmm, tna)
    merged = _merge(xa, x_ssm, w_pa.reshape(aw, d), w_ps.reshape(sw, d), acts, aw + sw, tmm, tna)
    tno = _pick(d, (1024, 512, 256))
    y_prompt = _out(merged, 0, w_out_b, xp, final_g, tm, tno).reshape(batch, seq, d)
    y_sample = _out(merged, mp // tm, w_out_b, xs, final_g, tm, tno).reshape(dec_batch, dec_seq, d)

    keep = min(WINDOW, seq)
    last_rows = lambda a: jnp.stack([a[(b + 1) * seq - keep:(b + 1) * seq] for b in range(batch)]).reshape(1, batch, keep, g, HEAD_DIM)
    dec_rows = lambda a: a[mp:].reshape(dec_batch, dec_seq, g, HEAD_DIM)[None]
    fin = fin.reshape(batch + dec_batch, ng, 2, p)
    return (y_prompt, y_sample, last_rows(k_f), last_rows(v_f), fin[:batch, :, 0][None], fin[:batch, :, 1][None],
            dec_rows(k_f), dec_rows(v_f), fin[batch:, :, 0][None], fin[batch:, :, 1][None])
```
